```python
import math
import jax, jax.numpy as jnp
from jax import lax
import numpy as np

D_MODEL = 1024
BATCH = 8
SEQ = 2048
DEPTH = 2
DEC_BATCH = 128
DEC_SEQ = 1
PAST_LEN = 8192
PAGE_SIZE = 128

GM_GROUPS = 4
GM_GROUP_DIM = D_MODEL // 16
W_A = GM_GROUPS * GM_GROUP_DIM
GM_CHUNK = 128
MLA_HEADS = 4
MLA_NOPE = 64
MLA_ROPE = 32
MLA_V = 64
MLA_Q_RANK = D_MODEL // 4
MLA_KV_RANK = D_MODEL // 8
W_B = MLA_HEADS * MLA_V
MLA_SCALE = (MLA_NOPE + MLA_ROPE) ** -0.5
ROPE_THETA = 10000.0
Q_BLOCK = 128
GDN_HEADS = 4
GDN_DK = 64
GDN_DV = 64
W_C = GDN_HEADS * GDN_DV
QKV_DIM = GDN_HEADS * (2 * GDN_DK + GDN_DV)
CONV_W = 4
GDN_CHUNK = 64
MEM_TOKENS = 256
MEM_HEADS = 4
MEM_DIM = 64
W_M = MEM_HEADS * MEM_DIM
N_BRANCH = 4
BR_WIDTH = 256
N_GROUPS = 4
EXPERTS_PER_GROUP = 8
N_EXPERTS = N_GROUPS * EXPERTS_PER_GROUP
TOP_K = 2
D_EXPERT = D_MODEL // 4
EPS = 1e-6

IN_SPLITS = (2 * W_A, MLA_Q_RANK, MLA_KV_RANK, MLA_ROPE, QKV_DIM, W_C, GDN_HEADS, GDN_HEADS, W_M, N_BRANCH * D_MODEL)
IN_DIM = sum(IN_SPLITS)
IN_OFFSETS = tuple(int(o) for o in np.cumsum(IN_SPLITS)[:-1])

kernel_name = 'hybrid_gmlp_mla_gdn_mem_hiermoe_step'


def rmsnorm(x, g):
    xf = x.astype(jnp.float32)
    y = xf * lax.rsqrt(jnp.mean(xf * xf, axis=-1, keepdims=True) + EPS)
    return (y * g.astype(jnp.float32)).astype(x.dtype)


def l2norm(x):
    xf = x.astype(jnp.float32)
    return xf * lax.rsqrt(jnp.sum(xf * xf, axis=-1, keepdims=True) + EPS)


def rope(x, pos):
    half = x.shape[-1] // 2
    inv = ROPE_THETA ** (-jnp.arange(half, dtype=jnp.float32) / half)
    ang = pos.astype(jnp.float32)[:, None] * inv[None, :]
    shape = (1, pos.shape[0]) + (1,) * (x.ndim - 3) + (half,)
    cos = jnp.cos(ang).reshape(shape)
    sin = jnp.sin(ang).reshape(shape)
    xf = x.astype(jnp.float32)
    x1, x2 = xf[..., :half], xf[..., half:]
    return jnp.concatenate([x1 * cos - x2 * sin, x1 * sin + x2 * cos], axis=-1).astype(x.dtype)


def causal_conv(x, buf, w):
    xp = jnp.concatenate([buf.astype(x.dtype), x], axis=1)
    y = lax.conv_general_dilated(xp, w[:, None, :].astype(x.dtype), window_strides=(1,), padding='VALID',
                                 dimension_numbers=('NWC', 'WIO', 'NWC'), feature_group_count=x.shape[-1])
    return y, xp[:, -(CONV_W - 1):]


def gmlp_spatial(v, ws, b):
    n, t, _ = v.shape
    nc = -(-t // GM_CHUNK)
    pad = nc * GM_CHUNK - t
    vp = jnp.pad(v, ((0, 0), (0, pad), (0, 0))).reshape(n, nc, GM_CHUNK, GM_GROUPS, GM_GROUP_DIM)
    w = jnp.where(jnp.tril(jnp.ones((GM_CHUNK, GM_CHUNK), bool)), ws, 0)
    s = jnp.einsum('gij,ncjgd->ncigd', w, vp) + b.T[None, None, :, :, None]
    return s.reshape(n, nc * GM_CHUNK, W_A)[:, :t]


def mla_attention(qn, qr, kn, kr, ckv, w_uv, q_pos, k_pos):
    n, tq, h, _ = qn.shape
    qb = min(Q_BLOCK, tq)
    nb = -(-tq // qb)
    pad = nb * qb - tq
    qn = jnp.pad(qn, ((0, 0), (0, pad), (0, 0), (0, 0)))
    qr = jnp.pad(qr, ((0, 0), (0, pad), (0, 0), (0, 0)))
    q_pos = jnp.pad(q_pos, (0, pad), mode='edge')

    def to_blocks(a):
        return jnp.moveaxis(a.reshape((n, nb, qb) + a.shape[2:]), 1, 0)

    def one_block(args):
        qn_b, qr_b, pos_b = args
        s = jnp.einsum('nthe,nshe->nhts', qn_b, kn) + jnp.einsum('nthe,nse->nhts', qr_b, kr)
        s = s.astype(jnp.float32) * MLA_SCALE
        s = jnp.where((k_pos[None, :] <= pos_b[:, None])[None, None], s, -jnp.inf)
        p = jax.nn.softmax(s, axis=-1).astype(ckv.dtype)
        return jnp.einsum('nhts,nsr->nthr', p, ckv)

    lat = lax.map(one_block, (to_blocks(qn), to_blocks(qr), q_pos.reshape(nb, qb)))
    lat = jnp.moveaxis(lat, 0, 1).reshape(n, nb * qb, h, ckv.shape[-1])[:, :tq]
    return jnp.einsum('nthr,rhe->nthe', lat, w_uv)


def mem_attention(q, k, v):
    s = jnp.einsum('nthe,nshe->nhts', q, k).astype(jnp.float32) * MEM_DIM ** -0.5
    p = jax.nn.softmax(s, axis=-1).astype(v.dtype)
    return jnp.einsum('nhts,nshe->nthe', p, v)


def gdn_chunked(q, k, v, beta, g, s0):
    n, t, h, _ = q.shape
    c = min(GDN_CHUNK, t)
    nc = -(-t // c)
    pad = nc * c - t

    def blk4(a):
        a = jnp.pad(a, ((0, 0), (0, pad), (0, 0), (0, 0)))
        return a.reshape(n, nc, c, a.shape[2], a.shape[3]).transpose(1, 0, 3, 2, 4)

    def blk3(a):
        a = jnp.pad(a, ((0, 0), (0, pad), (0, 0)))
        return a.reshape(n, nc, c, a.shape[2]).transpose(1, 0, 3, 2)

    incl = jnp.tril(jnp.ones((c, c), bool))
    strict = jnp.tril(jnp.ones((c, c), bool), k=-1)
    eye = jnp.eye(c, dtype=jnp.float32)

    def step(s, xs):
        qc, kc, vc, bc, gc = xs
        gcum = jnp.cumsum(gc, axis=-1)
        decay = jnp.exp(jnp.where(incl, gcum[..., :, None] - gcum[..., None, :], -jnp.inf))
        a = jnp.where(strict, bc[..., :, None] * jnp.einsum('nhtd,nhsd->nhts', kc, kc) * decay, 0.0)
        lmat = a + eye
        u_v = lax.linalg.triangular_solve(lmat, bc[..., None] * vc, left_side=True, lower=True, unit_diagonal=True)
        w_k = lax.linalg.triangular_solve(lmat, (bc * jnp.exp(gcum))[..., None] * kc,
                                          left_side=True, lower=True, unit_diagonal=True)
        u = u_v - jnp.einsum('nhck,nhvk->nhcv', w_k, s)
        qk = jnp.einsum('nhtd,nhsd->nhts', qc, kc) * decay
        o = jnp.exp(gcum)[..., None] * jnp.einsum('nhtk,nhvk->nhtv', qc, s) + jnp.einsum('nhts,nhsv->nhtv', qk, u)
        g_last = gcum[..., -1:]
        s_new = jnp.exp(g_last)[..., None] * s + jnp.einsum('nhs,nhsv,nhsk->nhvk', jnp.exp(g_last - gcum), u, kc)
        return s_new, o

    s_fin, o = lax.scan(step, s0, (blk4(q), blk4(k), blk4(v), blk3(beta), blk3(g)))
    o = o.transpose(1, 0, 3, 2, 4).reshape(n, nc * c, h, v.shape[-1])[:, :t]
    return o, s_fin


def hier_moe(x, lp):
    n, t, d = x.shape
    m = n * t
    xf = x.reshape(m, d)
    pg = jax.nn.softmax((xf @ lp['moe_wg']).astype(jnp.float32) + lp['moe_bg'].astype(jnp.float32), axis=-1)
    g_w, g_idx = lax.top_k(pg, 1)
    le = ((xf @ lp['moe_we']).astype(jnp.float32) + lp['moe_be'].astype(jnp.float32)).reshape(m, N_GROUPS, EXPERTS_PER_GROUP)
    le_sel = jnp.take_along_axis(le, jnp.broadcast_to(g_idx[:, :, None], (m, 1, EXPERTS_PER_GROUP)), axis=1)[:, 0]
    e_w, e_idx = lax.top_k(jax.nn.softmax(le_sel, axis=-1), TOP_K)
    e_w = e_w / jnp.sum(e_w, axis=-1, keepdims=True) * g_w
    e_glob = g_idx * EXPERTS_PER_GROUP + e_idx
    comb = jnp.einsum('mk,mke->me', e_w, jax.nn.one_hot(e_glob, N_EXPERTS, dtype=jnp.float32)).astype(x.dtype)
    act = jax.nn.silu(jnp.einsum('md,edf->mef', xf, lp['moe_w_gate'])) * jnp.einsum('md,edf->mef', xf, lp['moe_w_up'])
    y = jnp.einsum('mef,efd->md', act * comb[:, :, None], lp['moe_w_down'])
    return y.reshape(n, t, d)


def mem_kv(mem, lp):
    n, m, _ = mem.shape
    kv = rmsnorm(mem, lp['mem_norm_g']) @ lp['mem_w_kv']
    k, v = jnp.split(kv, 2, axis=-1)
    k = rmsnorm(k.reshape(n, m, MEM_HEADS, MEM_DIM), lp['mem_kn_g'])
    return k, v.reshape(n, m, MEM_HEADS, MEM_DIM)


def block(x, pos, lp, mla_past, gdn_s0, conv0, mem_k, mem_v):
    n, t, _ = x.shape
    h = rmsnorm(x, lp['norm1_g'])
    z = h @ lp['w_in']
    (a_raw, cq_raw, ckv_raw, kr_raw, qkv_raw, z_gate, b_raw, alpha_raw, mq_raw, gate_raw) = jnp.split(z, IN_OFFSETS, axis=-1)

    u, v = jnp.split(jax.nn.gelu(a_raw, approximate=False), 2, axis=-1)
    v = rmsnorm(v, lp['gm_norm_g'])
    a_out = u * gmlp_spatial(v, lp['gm_ws'], lp['gm_b'])

    cq = rmsnorm(cq_raw, lp['mla_cq_g'])
    q = jnp.einsum('ntr,rhe->nthe', cq, lp['mla_w_uq'])
    qn = rmsnorm(q[..., :MLA_NOPE], lp['mla_qn_g'])
    qr = rope(rmsnorm(q[..., MLA_NOPE:], lp['mla_qr_g']), pos)
    ckv = rmsnorm(ckv_raw, lp['mla_ckv_g'])
    kr = rope(rmsnorm(kr_raw, lp['mla_kr_g']), pos)
    if mla_past is None:
        ckv_all, kr_all, k_pos = ckv, kr, pos
    else:
        ckv_all = jnp.concatenate([mla_past[0].astype(ckv.dtype), ckv], axis=1)
        kr_all = jnp.concatenate([mla_past[1].astype(kr.dtype), kr], axis=1)
        k_pos = jnp.arange(ckv_all.shape[1], dtype=jnp.int32)
    kn = rmsnorm(jnp.einsum('nsr,rhe->nshe', ckv_all, lp['mla_w_uk']), lp['mla_kn_g'])
    b_out = mla_attention(qn, qr, kn, kr_all, ckv_all, lp['mla_w_uv'], pos, k_pos).reshape(n, t, W_B)

    qkv, conv_new = causal_conv(qkv_raw, conv0, lp['gdn_conv_w'])
    qkv = jax.nn.silu(qkv).astype(jnp.float32)
    gq, gk, gv = jnp.split(qkv, [GDN_HEADS * GDN_DK, 2 * GDN_HEADS * GDN_DK], axis=-1)
    gq = l2norm(gq.reshape(n, t, GDN_HEADS, GDN_DK)) * GDN_DK ** -0.5
    gk = l2norm(gk.reshape(n, t, GDN_HEADS, GDN_DK))
    gv = gv.reshape(n, t, GDN_HEADS, GDN_DV)
    beta = jax.nn.sigmoid(b_raw.astype(jnp.float32))
    gdec = -jnp.exp(lp['gdn_a_log'].astype(jnp.float32)) * jax.nn.softplus(
        alpha_raw.astype(jnp.float32) + lp['gdn_dt_bias'].astype(jnp.float32))
    o, s_new = gdn_chunked(gq, gk, gv, beta, gdec, gdn_s0.astype(jnp.float32))
    o = rmsnorm(o.astype(x.dtype), lp['gdn_out_g']) * jax.nn.silu(z_gate.reshape(n, t, GDN_HEADS, GDN_DV))
    c_out = o.reshape(n, t, W_C)

    mq = rmsnorm(mq_raw.reshape(n, t, MEM_HEADS, MEM_DIM), lp['mem_qn_g'])
    m_out = mem_attention(mq, mem_k.astype(mq.dtype), mem_v.astype(mq.dtype)).reshape(n, t, W_M)

    branches = jnp.stack([a_out, b_out, c_out, m_out], axis=2)
    proj = jnp.einsum('ntbw,bwd->ntbd', branches, lp['w_branch'])
    gates = jax.nn.sigmoid(gate_raw.reshape(n, t, N_BRANCH, D_MODEL))
    x = x + jnp.einsum('ntbd,ntbd->ntd', gates, proj) @ lp['w_out']

    x = x + hier_moe(rmsnorm(x, lp['norm2_g']), lp)
    return x, ckv, kr, s_new.astype(x.dtype), conv_new, v


def setup_inputs(seed: int = 0) -> dict:
    key = jax.random.key(seed)
    ks = iter(jax.random.split(key, 64))

    def nrm(shape, scale=1.0):
        return jax.random.normal(next(ks), shape, jnp.float32) * scale

    def gain(shape):
        return 1.0 + nrm(shape, 0.02)

    n_pages = PAST_LEN // PAGE_SIZE
    n_used = DEC_BATCH * n_pages
    n_pool = n_used + n_used // 4
    page_table = jax.random.permutation(next(ks), n_pool)[:n_used].reshape(DEC_BATCH, n_pages).astype(jnp.int32)
    dt = jnp.exp(jax.random.uniform(next(ks), (DEPTH, GDN_HEADS), jnp.float32, math.log(1e-3), math.log(1e-1)))
    gm_row_scale = (jnp.arange(1, GM_CHUNK + 1, dtype=jnp.float32) ** -0.5)[:, None]
    return {
        'x_prompt': nrm((BATCH, SEQ, D_MODEL)),
        'mem_prompt': nrm((BATCH, MEM_TOKENS, D_MODEL)),
        'x_sample': nrm((DEC_BATCH, DEC_SEQ, D_MODEL)),
        'cache_mla_ckv': nrm((DEPTH, n_pool, PAGE_SIZE, MLA_KV_RANK)),
        'cache_mla_kr': nrm((DEPTH, n_pool, PAGE_SIZE, MLA_ROPE)),
        'cache_mem_k': nrm((DEPTH, DEC_BATCH, MEM_TOKENS, MEM_HEADS, MEM_DIM)),
        'cache_mem_v': nrm((DEPTH, DEC_BATCH, MEM_TOKENS, MEM_HEADS, MEM_DIM)),
        'state_gdn': nrm((DEPTH, DEC_BATCH, GDN_HEADS, GDN_DV, GDN_DK), 0.1),
        'state_conv': nrm((DEPTH, DEC_BATCH, CONV_W - 1, QKV_DIM)),
        'page_table': page_table,
        'norm1_g': gain((DEPTH, D_MODEL)),
        'w_in': nrm((DEPTH, D_MODEL, IN_DIM), D_MODEL ** -0.5),
        'gm_norm_g': gain((DEPTH, W_A)),
        'gm_ws': nrm((DEPTH, GM_GROUPS, GM_CHUNK, GM_CHUNK)) * gm_row_scale,
        'gm_b': nrm((DEPTH, GM_GROUPS, GM_CHUNK), 0.02),
        'mla_cq_g': gain((DEPTH, MLA_Q_RANK)),
        'mla_w_uq': nrm((DEPTH, MLA_Q_RANK, MLA_HEADS, MLA_NOPE + MLA_ROPE), MLA_Q_RANK ** -0.5),
        'mla_qn_g': gain((DEPTH, MLA_NOPE)),
        'mla_qr_g': gain((DEPTH, MLA_ROPE)),
        'mla_ckv_g': gain((DEPTH, MLA_KV_RANK)),
        'mla_kr_g': gain((DEPTH, MLA_ROPE)),
        'mla_w_uk': nrm((DEPTH, MLA_KV_RANK, MLA_HEADS, MLA_NOPE), MLA_KV_RANK ** -0.5),
        'mla_kn_g': gain((DEPTH, MLA_NOPE)),
        'mla_w_uv': nrm((DEPTH, MLA_KV_RANK, MLA_HEADS, MLA_V), MLA_KV_RANK ** -0.5),
        'gdn_conv_w': nrm((DEPTH, CONV_W, QKV_DIM), CONV_W ** -0.5),
        'gdn_a_log': jnp.log(jax.random.uniform(next(ks), (DEPTH, GDN_HEADS), jnp.float32, 1.0, 16.0)),
        'gdn_dt_bias': dt + jnp.log(-jnp.expm1(-dt)),
        'gdn_out_g': gain((DEPTH, GDN_DV)),
        'mem_norm_g': gain((DEPTH, D_MODEL)),
        'mem_w_kv': nrm((DEPTH, D_MODEL, 2 * W_M), D_MODEL ** -0.5),
        'mem_qn_g': gain((DEPTH, MEM_DIM)),
        'mem_kn_g': gain((DEPTH, MEM_DIM)),
        'w_branch': nrm((DEPTH, N_BRANCH, BR_WIDTH, D_MODEL), BR_WIDTH ** -0.5),
        'w_out': nrm((DEPTH, D_MODEL, D_MODEL), D_MODEL ** -0.5),
        'norm2_g': gain((DEPTH, D_MODEL)),
        'moe_wg': nrm((DEPTH, D_MODEL, N_GROUPS), D_MODEL ** -0.5),
        'moe_bg': nrm((DEPTH, N_GROUPS), 0.01),
        'moe_we': nrm((DEPTH, D_MODEL, N_EXPERTS), D_MODEL ** -0.5),
        'moe_be': nrm((DEPTH, N_EXPERTS), 0.01),
        'moe_w_gate': nrm((DEPTH, N_EXPERTS, D_MODEL, D_EXPERT), D_MODEL ** -0.5),
        'moe_w_up': nrm((DEPTH, N_EXPERTS, D_MODEL, D_EXPERT), D_MODEL ** -0.5),
        'moe_w_down': nrm((DEPTH, N_EXPERTS, D_EXPERT, D_MODEL), D_EXPERT ** -0.5),
    }


def reference(x_prompt, mem_prompt, x_sample, cache_mla_ckv, cache_mla_kr, cache_mem_k, cache_mem_v,
              state_gdn, state_conv, page_table, norm1_g, w_in, gm_norm_g, gm_ws, gm_b, mla_cq_g, mla_w_uq,
              mla_qn_g, mla_qr_g, mla_ckv_g, mla_kr_g, mla_w_uk, mla_kn_g, mla_w_uv, gdn_conv_w, gdn_a_log,
              gdn_dt_bias, gdn_out_g, mem_norm_g, mem_w_kv, mem_qn_g, mem_kn_g, w_branch, w_out, norm2_g,
              moe_wg, moe_bg, moe_we, moe_be, moe_w_gate, moe_w_up, moe_w_down):
    bp, tp, _ = x_prompt.shape
    bs, ts, _ = x_sample.shape
    n_pages = page_table.shape[1]
    past_len = n_pages * cache_mla_ckv.shape[2]
    pos_p = jnp.arange(tp, dtype=jnp.int32)
    pos_s = past_len + jnp.arange(ts, dtype=jnp.int32)
    xp, xs = x_prompt, x_sample
    rows_p, rows_s = [], []
    for l in range(DEPTH):
        lp = {
            'norm1_g': norm1_g[l], 'w_in': w_in[l], 'gm_norm_g': gm_norm_g[l], 'gm_ws': gm_ws[l], 'gm_b': gm_b[l],
            'mla_cq_g': mla_cq_g[l], 'mla_w_uq': mla_w_uq[l], 'mla_qn_g': mla_qn_g[l], 'mla_qr_g': mla_qr_g[l],
            'mla_ckv_g': mla_ckv_g[l], 'mla_kr_g': mla_kr_g[l], 'mla_w_uk': mla_w_uk[l], 'mla_kn_g': mla_kn_g[l],
            'mla_w_uv': mla_w_uv[l], 'gdn_conv_w': gdn_conv_w[l], 'gdn_a_log': gdn_a_log[l],
            'gdn_dt_bias': gdn_dt_bias[l], 'gdn_out_g': gdn_out_g[l], 'mem_norm_g': mem_norm_g[l],
            'mem_w_kv': mem_w_kv[l], 'mem_qn_g': mem_qn_g[l], 'mem_kn_g': mem_kn_g[l], 'w_branch': w_branch[l],
            'w_out': w_out[l], 'norm2_g': norm2_g[l], 'moe_wg': moe_wg[l], 'moe_bg': moe_bg[l], 'moe_we': moe_we[l],
            'moe_be': moe_be[l], 'moe_w_gate': moe_w_gate[l], 'moe_w_up': moe_w_up[l], 'moe_w_down': moe_w_down[l],
        }
        mk_p, mv_p = mem_kv(mem_prompt, lp)
        zero_s = jnp.zeros((bp, GDN_HEADS, GDN_DV, GDN_DK), jnp.float32)
        zero_c = jnp.zeros((bp, CONV_W - 1, QKV_DIM), xp.dtype)
        xp, ckv_p, kr_p, s_p, c_p, _ = block(xp, pos_p, lp, None, zero_s, zero_c, mk_p, mv_p)
        rows_p.append((ckv_p, kr_p, s_p, c_p, mk_p, mv_p))
        ckv_past = cache_mla_ckv[l][page_table].reshape(bs, past_len, MLA_KV_RANK)
        kr_past = cache_mla_kr[l][page_table].reshape(bs, past_len, MLA_ROPE)
        xs, ckv_s, kr_s, s_s, c_s, v_s = block(xs, pos_s, lp, (ckv_past, kr_past), state_gdn[l], state_conv[l],
                                               cache_mem_k[l], cache_mem_v[l])
        rows_s.append((ckv_s, kr_s, s_s, c_s, v_s))
    p_ckv, p_kr, p_gdn, p_conv, p_mk, p_mv = [jnp.stack(a) for a in zip(*rows_p)]
    s_ckv, s_kr, s_gdn, s_conv, s_v = [jnp.stack(a) for a in zip(*rows_s)]
    return (xp, xs, p_ckv, p_kr, p_gdn, p_conv, p_mk, p_mv, s_ckv, s_kr, s_gdn, s_conv, s_v)
```

```python
import functools

import numpy as np
import jax
import jax.numpy as jnp
from jax import lax
from jax.experimental import pallas as pl
from jax.experimental.pallas import tpu as pltpu

F32 = jnp.float32
BF16 = jnp.bfloat16
HI = lax.Precision.HIGHEST
EPS = 1e-6
NEG = float("-inf")

D_MODEL = 1024
HEADS = 4
HD = 64
BW = 256
MLA_ROPE = 32
MLA_KV_RANK = 128
MLA_SCALE = 96.0 ** -0.5
ROPE_THETA = 10000.0
GM_CHUNK = 128
GDN_CHUNK = 64
QKV_DIM = 768
N_GROUPS = 4
EPG = 8
N_EXPERTS = 32
D_EXPERT = 256
PAGE = 128
VMEM_LIMIT = 56 * 1024 * 1024

NT = (((1,), (1,)), ((), ()))
TN = (((0,), (0,)), ((), ()))


def _cp(*sem):
    return pltpu.CompilerParams(dimension_semantics=sem, vmem_limit_bytes=VMEM_LIMIT)


def _rms(x, g):
    ms = jnp.sum(x * x, axis=-1, keepdims=True) * (1.0 / x.shape[-1])
    return x * lax.rsqrt(ms + EPS) * g


def _bdot(a, b):
    return jnp.dot(a.astype(BF16), b.astype(BF16), preferred_element_type=F32)


def _bdot_nt(a, b):
    return lax.dot_general(a.astype(BF16), b.astype(BF16), NT, preferred_element_type=F32)


def _hdot(a, b):
    return jnp.dot(a, b, precision=HI, preferred_element_type=F32)


def _silu(x):
    return x * jax.nn.sigmoid(x)


def _gelu(x):
    return 0.5 * x * (1.0 + lax.erf(x * 0.7071067811865476))


def _lane_head(width=BW):
    return lax.broadcasted_iota(jnp.int32, (1, width), 1) // HD


def _full(shape):
    n = len(shape)
    return pl.BlockSpec(shape, lambda *_: (0,) * n)


def _block_mean(width, segs):
    m = np.zeros((width, width), np.float32)
    for a, b in segs:
        m[a:b, a:b] = 1.0 / (b - a)
    return m


_BMEAN64 = _block_mean(BW, [(64 * h, 64 * h + 64) for h in range(HEADS)])
_BONES64 = _BMEAN64 * 64.0
_BQ = _block_mean(512, [(128 * h, 128 * h + 64) for h in range(HEADS)] + [(128 * h + 64, 128 * h + 96) for h in range(HEADS)])
_BK = _block_mean(512, [(128 * h, 128 * h + 64) for h in range(HEADS)])
_IND8 = np.zeros((8, BW), np.float32)
for _h in range(HEADS):
    _IND8[_h, 64 * _h:64 * _h + 64] = 1.0
_EXPB = np.zeros((128, BW), np.float32)
_EXPG = np.zeros((128, BW), np.float32)
for _h in range(HEADS):
    _EXPB[_h, 64 * _h:64 * _h + 64] = 1.0
    _EXPG[4 + _h, 64 * _h:64 * _h + 64] = 1.0
_BDMASK = (_BONES64 > 0).astype(np.float32)


def _in_proj_kernel(x_ref, g_ref, w_ref, wbat_ref, oa_ref, ob_ref, oc_ref, om_ref, oba_ref, obat_ref):
    hb = _rms(x_ref[...], g_ref[...]).astype(BF16)
    oa_ref[...] = jnp.dot(hb, w_ref[:, 0:512], preferred_element_type=F32)
    ob_ref[...] = jnp.dot(hb, w_ref[:, 512:1024], preferred_element_type=F32)
    oc_ref[...] = jnp.dot(hb, w_ref[:, 1024:2048], preferred_element_type=F32)
    om_ref[...] = jnp.dot(hb, w_ref[:, 2048:2304], preferred_element_type=F32)
    oba_ref[...] = jnp.dot(hb, w_ref[:, 2304:2432], preferred_element_type=F32)
    obat_ref[...] = lax.dot_general(wbat_ref[...], hb, NT, preferred_element_type=F32)


def _in_proj(x, g, w, wbat, tm):
    m = x.shape[0]
    widths = (512, 512, 1024, 256, 128)
    return pl.pallas_call(
        _in_proj_kernel,
        grid=(m // tm,),
        in_specs=[pl.BlockSpec((tm, D_MODEL), lambda i: (i, 0)), _full((1, D_MODEL)), _full(w.shape), _full(wbat.shape)],
        out_specs=[pl.BlockSpec((tm, n), lambda i: (i, 0)) for n in widths] + [pl.BlockSpec((8, tm), lambda i: (0, i))],
        out_shape=[jax.ShapeDtypeStruct((m, n), F32) for n in widths] + [jax.ShapeDtypeStruct((8, m), F32)],
        compiler_params=_cp("parallel"),
        name="in_proj",
    )(x, g, w, wbat)


def _gmlp_kernel(z_ref, g_ref, ws_ref, b_ref, o_ref, *, ta):
    row = lax.broadcasted_iota(jnp.int32, (GM_CHUNK, GM_CHUNK), 0)
    col = lax.broadcasted_iota(jnp.int32, (GM_CHUNK, GM_CHUNK), 1)
    tril = col <= row
    lh = _lane_head()
    wts = [jnp.where(tril, ws_ref[g], 0.0).astype(BF16) for g in range(HEADS)]
    for c in range(ta // GM_CHUNK):
        sl = slice(c * GM_CHUNK, (c + 1) * GM_CHUNK)
        ge = _gelu(z_ref[sl, :])
        u = ge[:, :BW]
        vb = _rms(ge[:, BW:], g_ref[...]).astype(BF16)
        s = b_ref[...]
        for g in range(HEADS):
            s = s + jnp.where(lh == g, jnp.dot(wts[g], vb, preferred_element_type=F32), 0.0)
        o_ref[sl, :] = u * s


def _gmlp(za, m, g, ws, bfull, ta):
    return pl.pallas_call(
        functools.partial(_gmlp_kernel, ta=ta),
        grid=(m // ta,),
        in_specs=[pl.BlockSpec((ta, 512), lambda i: (i, 0)), _full((1, BW)), _full(ws.shape), _full(bfull.shape)],
        out_specs=pl.BlockSpec((ta, BW), lambda i: (i, 0)),
        out_shape=jax.ShapeDtypeStruct((m, BW), F32),
        compiler_params=_cp("parallel"),
        name="gmlp",
    )(za, g, ws, bfull)


def _mla_pre_kernel(z_ref, c_ref, s1_ref, s2_ref, gcq_ref, wuq_ref, qg_ref, bq_ref, gckv_ref, gkr_ref, wuk_ref, kg_ref,
                    bk_ref, oq_ref, ok_ref, ockv_ref, okr_ref, ockv16_ref, oq32_ref):
    cs, s1, s2 = c_ref[...], s1_ref[...], s2_ref[...]

    def rope(x):
        return x * cs + pltpu.roll(x, 112, 1) * s1 + pltpu.roll(x, 16, 1) * s2

    z = z_ref[...]
    cq = _rms(z[:, 0:256], gcq_ref[...])
    q = _bdot(cq, wuq_ref[...])
    qn = q * lax.rsqrt(_bdot(q * q, bq_ref[...]) + EPS) * qg_ref[...]
    ckv = _rms(z[:, 256:384], gckv_ref[...])
    ockv_ref[...] = ckv
    ockv16_ref[...] = ckv.astype(BF16)
    krb = z[:, 384:512]
    kr = rope(krb * lax.rsqrt(jnp.sum(krb * krb, axis=-1, keepdims=True) * (1.0 / MLA_ROPE) + EPS) * gkr_ref[...])
    okr_ref[...] = kr[:, 64:96]
    k = _bdot(ckv, wuk_ref[...])
    kn = k * lax.rsqrt(_bdot(k * k, bk_ref[...]) + EPS) * kg_ref[...]
    for h in range(HEADS):
        sl = slice(128 * h, 128 * h + 128)
        qh = rope(qn[:, sl])
        oq_ref[h] = qh.astype(BF16)
        oq32_ref[:, sl] = qh
        ok_ref[h] = (kn[:, sl] + kr).astype(BF16)


def _mla_pre(zb, m, tabs, t_blocks, p, tm):
    cs, s1, s2 = tabs
    tab_spec = pl.BlockSpec((tm, 128), lambda i: (i % t_blocks, 0))
    consts = [p["mla_cq_g"], p["w_uq"], p["qg"], jnp.asarray(_BQ, BF16), p["mla_ckv_g"], p["kr_g"], p["w_uk"], p["kg"],
              jnp.asarray(_BK, BF16)]
    return pl.pallas_call(
        _mla_pre_kernel,
        grid=(m // tm,),
        in_specs=[pl.BlockSpec((tm, 512), lambda i: (i, 0)), tab_spec, tab_spec, tab_spec] + [_full(c.shape) for c in consts],
        out_specs=[pl.BlockSpec((HEADS, tm, 128), lambda i: (0, i, 0)), pl.BlockSpec((HEADS, tm, 128), lambda i: (0, i, 0)),
                   pl.BlockSpec((tm, 128), lambda i: (i, 0)), pl.BlockSpec((tm, MLA_ROPE), lambda i: (i, 0)),
                   pl.BlockSpec((tm, 128), lambda i: (i, 0)), pl.BlockSpec((tm, 512), lambda i: (i, 0))],
        out_shape=[jax.ShapeDtypeStruct((HEADS, m, 128), BF16), jax.ShapeDtypeStruct((HEADS, m, 128), BF16),
                   jax.ShapeDtypeStruct((m, 128), F32), jax.ShapeDtypeStruct((m, MLA_ROPE), F32),
                   jax.ShapeDtypeStruct((m, 128), BF16), jax.ShapeDtypeStruct((m, 512), F32)],
        compiler_params=_cp("parallel"),
        name="mla_pre",
    )(zb, cs, s1, s2, *consts)


def _mla_attn_kernel(q_ref, k_ref, v_ref, wuv_ref, o_ref, *, tq):
    i = pl.program_id(1)
    row = lax.broadcasted_iota(jnp.int32, (tq, tq), 0)
    col = lax.broadcasted_iota(jnp.int32, (tq, tq), 1)
    causal = col <= row
    out = jnp.zeros((tq, BW), F32)
    for h in range(HEADS):
        qh = q_ref[h]

        def step(off, carry, mask):
            m, l, acc = carry
            kj = k_ref[h, pl.ds(off, tq), :]
            vj = v_ref[pl.ds(off, tq), :]
            s = lax.dot_general(qh, kj, NT, preferred_element_type=F32) * MLA_SCALE
            if mask:
                s = jnp.where(causal, s, NEG)
            mn = jnp.maximum(m, jnp.max(s, axis=-1, keepdims=True))
            pr = jnp.exp(s - mn)
            al = jnp.exp(m - mn)
            return mn, al * l + jnp.sum(pr, axis=-1, keepdims=True), al * acc + jnp.dot(pr.astype(BF16), vj, preferred_element_type=F32)

        init = (jnp.full((tq, 1), NEG, F32), jnp.zeros((tq, 1), F32), jnp.zeros((tq, 128), F32))
        carry = lax.fori_loop(0, i, lambda j, c: step(pl.multiple_of(j * tq, tq), c, False), init)
        m, l, acc = step(pl.multiple_of(i * tq, tq), carry, True)
        out = out + jnp.dot((acc / l).astype(BF16), wuv_ref[h], preferred_element_type=F32)
    o_ref[...] = out


def _mla_attn(q4, k4, ckv16, wuv, n, t, tq):
    nq = t // tq
    return pl.pallas_call(
        functools.partial(_mla_attn_kernel, tq=tq),
        grid=(n, nq),
        in_specs=[pl.BlockSpec((HEADS, tq, 128), lambda b, i: (0, b * nq + i, 0)),
                  pl.BlockSpec((HEADS, t, 128), lambda b, i: (0, b, 0)),
                  pl.BlockSpec((t, 128), lambda b, i: (b, 0)), _full(wuv.shape)],
        out_specs=pl.BlockSpec((tq, BW), lambda b, i: (b * nq + i, 0)),
        out_shape=jax.ShapeDtypeStruct((n * t, BW), F32),
        compiler_params=_cp("parallel", "arbitrary"),
        name="mla_attn",
    )(q4, k4, ckv16, wuv)


def _mla_decode_kernel(pt_ref, *refs, pp):
    del pt_ref
    ckv_refs, kr_refs = refs[:pp], refs[pp:2 * pp]
    qk_ref, qr_ref, cnew_ref, krnew_ref, kng_ref, wuk_ref, wuv_ref, ind_ref, o_ref, m_sc, l_sc, acc_sc = refs[2 * pp:]
    j = pl.program_id(1)

    @pl.when(j == 0)
    def _():
        m_sc[...] = jnp.full(m_sc.shape, NEG, F32)
        l_sc[...] = jnp.zeros(l_sc.shape, F32)
        acc_sc[...] = jnp.zeros(acc_sc.shape, F32)

    ind = ind_ref[...]
    qbd = (ind * (qk_ref[...] * kng_ref[...])).astype(BF16)
    ind64 = (ind * (1.0 / HD)).astype(BF16)
    qr = qr_ref[...].astype(BF16)
    wuk = wuk_ref[...]

    def block(c, krr, mask):
        cb = c.astype(BF16)
        k = jnp.dot(cb, wuk, preferred_element_type=F32)
        num = lax.dot_general(qbd, k.astype(BF16), NT, preferred_element_type=F32)
        ms = lax.dot_general(ind64, (k * k).astype(BF16), NT, preferred_element_type=F32)
        rp = lax.dot_general(qr, krr.astype(BF16), NT, preferred_element_type=F32)
        s = (num * lax.rsqrt(ms + EPS) + rp) * MLA_SCALE
        if mask is not None:
            s = jnp.where(mask, s, NEG)
        m = m_sc[...]
        mn = jnp.maximum(m, jnp.max(s, axis=-1, keepdims=True))
        pr = jnp.exp(s - mn)
        al = jnp.exp(m - mn)
        l_sc[...] = al * l_sc[...] + jnp.sum(pr, axis=-1, keepdims=True)
        acc_sc[...] = al * acc_sc[...] + jnp.dot(pr.astype(BF16), cb, preferred_element_type=F32)
        m_sc[...] = mn

    for i in range(pp):
        block(ckv_refs[i][...], kr_refs[i][...], None)

    @pl.when(j == pl.num_programs(1) - 1)
    def _():
        c8 = jnp.broadcast_to(cnew_ref[...], (8, MLA_KV_RANK))
        kr8 = jnp.broadcast_to(krnew_ref[...], (8, MLA_ROPE))
        block(c8, kr8, lax.broadcasted_iota(jnp.int32, (8, 8), 1) == 0)
        lat = acc_sc[...] / l_sc[...]
        o8 = jnp.dot(lat.astype(BF16), wuv_ref[...], preferred_element_type=F32)
        o_ref[...] = jnp.sum(o8 * ind, axis=0, keepdims=True)


def _mla_decode(layer, page_table, cache_ckv, cache_kr, qk, qr8, cnew, krnew, kng, wuk, wuv, pp):
    ns, npages = page_table.shape
    pt = page_table.reshape(-1)

    def page_spec(i, width):
        return pl.BlockSpec((None, None, PAGE, width), lambda n, j, pt_ref, i=i: (layer, pt_ref[n * npages + j * pp + i], 0, 0))

    def per_sample(shape):
        return pl.BlockSpec((None,) + shape, lambda n, j, pt_ref: (n, 0, 0))

    def const(a):
        nd = a.ndim
        return pl.BlockSpec(a.shape, lambda n, j, pt_ref: (0,) * nd)

    ind = jnp.asarray(_IND8)
    grid_spec = pltpu.PrefetchScalarGridSpec(
        num_scalar_prefetch=1,
        grid=(ns, npages // pp),
        in_specs=[page_spec(i, MLA_KV_RANK) for i in range(pp)] + [page_spec(i, MLA_ROPE) for i in range(pp)]
        + [per_sample((1, BW)), per_sample((8, MLA_ROPE)), per_sample((1, MLA_KV_RANK)), per_sample((1, MLA_ROPE)),
           const(kng), const(wuk), const(wuv), const(ind)],
        out_specs=per_sample((1, BW)),
        scratch_shapes=[pltpu.VMEM((8, 1), F32), pltpu.VMEM((8, 1), F32), pltpu.VMEM((8, MLA_KV_RANK), F32)],
    )
    return pl.pallas_call(
        functools.partial(_mla_decode_kernel, pp=pp),
        grid_spec=grid_spec,
        out_shape=jax.ShapeDtypeStruct((ns, 1, BW), F32),
        compiler_params=_cp("parallel", "arbitrary"),
        name="mla_decode",
    )(pt, *([cache_ckv] * pp), *([cache_kr] * pp), qk, qr8, cnew, krnew, kng, wuk, wuv, ind)


def _gdn_kernel(zc_ref, zba_ref, bat_ref, cw_ref, parr_ref, parc_ref, gout_ref, bones_ref, expb_ref, expg_ref, bdm_ref,
                o_ref, sfin_ref, xbuf, s_sc, *, tg):
    t = pl.program_id(1)
    c = GDN_CHUNK

    @pl.when(t == 0)
    def _():
        xbuf[0:8, :] = jnp.zeros((8, QKV_DIM), F32)
        s_sc[...] = jnp.zeros(s_sc.shape, F32)

    @pl.when(t > 0)
    def _():
        xbuf[5:8, :] = xbuf[tg + 5:tg + 8, :]

    xbuf[8:8 + tg, :] = zc_ref[:, 0:QKV_DIM]
    y = cw_ref[0:1, :] * xbuf[5:5 + tg, :]
    for i in range(1, 4):
        y = y + cw_ref[i:i + 1, :] * xbuf[5 + i:5 + i + tg, :]
    y = _silu(y)

    zba = zba_ref[...]
    beta_col = jax.nn.sigmoid(zba)
    g_col = -jnp.exp(parr_ref[0:1, :]) * jax.nn.softplus(zba + parr_ref[1:2, :])
    g_row = -jnp.exp(parc_ref[:, 0:1]) * jax.nn.softplus(bat_ref[...] + parc_ref[:, 1:2])

    row = lax.broadcasted_iota(jnp.int32, (c, c), 0)
    col = lax.broadcasted_iota(jnp.int32, (c, c), 1)
    incl = col <= row
    strict = col < row
    eye = (col == row).astype(F32)
    lt = incl.astype(F32)
    ut = (col >= row).astype(F32)
    lh = _lane_head()
    bones = bones_ref[...]
    bdm = bdm_ref[...]

    for ci in range(tg // c):
        sl = slice(ci * c, (ci + 1) * c)
        q, k, v = y[sl, 0:256], y[sl, 256:512], y[sl, 512:768]
        qn = q * lax.rsqrt(_bdot(q * q, bones) + EPS) * (HD ** -0.5)
        kn = k * lax.rsqrt(_bdot(k * k, bones) + EPS)
        bcol = beta_col[sl, :]
        gcum_c = _hdot(lt, g_col[sl, :])
        gcum_r = _hdot(g_row[:, sl], ut)
        gx = _hdot(gcum_c, expg_ref[...])
        bx = _hdot(bcol, expb_ref[...])
        egx = jnp.exp(gx)
        rhs = jnp.concatenate([bx * v, bx * egx * kn], axis=1)
        kb = kn.astype(BF16)
        uv = jnp.zeros((c, BW), F32)
        wk = jnp.zeros((c, BW), F32)
        qks = []
        for h in range(HEADS):
            mh = lh == h
            dm = jnp.exp(jnp.where(incl, gcum_c[:, 4 + h:5 + h] - gcum_r[4 + h:5 + h, :], NEG))
            kk = lax.dot_general(jnp.where(mh, kn, 0.0).astype(BF16), kb, NT, preferred_element_type=F32)
            a = jnp.where(strict, bcol[:, h:h + 1] * kk * dm, 0.0)
            qks.append(lax.dot_general(jnp.where(mh, qn, 0.0).astype(BF16), kb, NT, preferred_element_type=F32) * dm)
            pw = -a
            tinv = eye + pw
            for _ in range(5):
                pw = _hdot(pw, pw)
                tinv = tinv + _hdot(tinv, pw)
            x = _hdot(tinv, rhs)
            uv = uv + jnp.where(mh, x[:, :BW], 0.0)
            wk = wk + jnp.where(mh, x[:, BW:], 0.0)
        s = s_sc[...]
        sb = s.astype(BF16)
        u = uv - jnp.dot(wk.astype(BF16), sb, preferred_element_type=F32)
        ub = u.astype(BF16)
        o = egx * jnp.dot(qn.astype(BF16), sb, preferred_element_type=F32)
        for h in range(HEADS):
            o = o + jnp.where(lh == h, jnp.dot(qks[h].astype(BF16), ub, preferred_element_type=F32), 0.0)
        glast = gx[c - 1:c, :]
        kf = (kn * jnp.exp(glast - gx)).astype(BF16)
        s_new = jnp.exp(glast) * s + lax.dot_general(kf, ub, TN, preferred_element_type=F32)
        s_sc[...] = s_new * bdm
        on = o * lax.rsqrt(_bdot(o * o, bones) * (1.0 / HD) + EPS) * gout_ref[...]
        o_ref[sl, :] = on * _silu(zc_ref[sl, QKV_DIM:QKV_DIM + BW])

    @pl.when(t == pl.num_programs(1) - 1)
    def _():
        sfin_ref[...] = s_sc[...]


def _gdn_prompt(zc, zba, bat, n, t, p, tg):
    nt = t // tg
    consts = [p["conv_w"], p["gdn_par_r"], p["gdn_par_c"], p["gdn_out_gx"], jnp.asarray(_BONES64, BF16), jnp.asarray(_EXPB),
              jnp.asarray(_EXPG), jnp.asarray(_BDMASK)]
    return pl.pallas_call(
        functools.partial(_gdn_kernel, tg=tg),
        grid=(n, nt),
        in_specs=[pl.BlockSpec((tg, 1024), lambda b, i: (b * nt + i, 0)), pl.BlockSpec((tg, 128), lambda b, i: (b * nt + i, 0)),
                  pl.BlockSpec((8, tg), lambda b, i: (0, b * nt + i))] + [_full(c.shape) for c in consts],
        out_specs=[pl.BlockSpec((tg, BW), lambda b, i: (b * nt + i, 0)), pl.BlockSpec((None, BW, BW), lambda b, i: (b, 0, 0))],
        out_shape=[jax.ShapeDtypeStruct((n * t, BW), F32), jax.ShapeDtypeStruct((n, BW, BW), F32)],
        scratch_shapes=[pltpu.VMEM((8 + tg, QKV_DIM), F32), pltpu.VMEM((BW, BW), F32)],
        compiler_params=_cp("parallel", "arbitrary"),
        name="gdn_prompt",
    )(zc, zba, bat, *consts)


def _sample_tok_kernel(za_ref, zc_ref, zba_ref, zm_ref, sconv_ref, cw_ref, gmg_ref, gmw_ref, gmb_ref, parr_ref, bones_ref,
                       memg_ref, bmean_ref, oa_ref, ov_ref, oconv_ref, oq_ref, ok_ref, ovv_ref, obeta_ref, og_ref, omq_ref):
    ge = _gelu(za_ref[...])
    v = _rms(ge[:, BW:], gmg_ref[...])
    ov_ref[...] = v
    oa_ref[...] = ge[:, :BW] * (gmw_ref[...] * v + gmb_ref[...])
    sc = sconv_ref[...]
    x = zc_ref[:, 0:QKV_DIM]
    y = (cw_ref[0:1, :] * sc[:, 0:768] + cw_ref[1:2, :] * sc[:, 768:1536] + cw_ref[2:3, :] * sc[:, 1536:2304]
         + cw_ref[3:4, :] * x)
    oconv_ref[:, 0:1536] = sc[:, 768:2304]
    oconv_ref[:, 1536:2304] = x
    y = _silu(y)
    q, k = y[:, 0:256], y[:, 256:512]
    bones = bones_ref[...]
    oq_ref[...] = q * lax.rsqrt(_bdot(q * q, bones) + EPS) * (HD ** -0.5)
    ok_ref[...] = k * lax.rsqrt(_bdot(k * k, bones) + EPS)
    ovv_ref[...] = y[:, 512:768]
    zba = zba_ref[...]
    obeta_ref[...] = jax.nn.sigmoid(zba)
    og_ref[...] = -jnp.exp(parr_ref[0:1, :]) * jax.nn.softplus(zba + parr_ref[1:2, :])
    mq = zm_ref[...]
    omq_ref[...] = mq * lax.rsqrt(_bdot(mq * mq, bmean_ref[...]) + EPS) * memg_ref[...]


def _sample_tok(za, zc, zba, zm, sconv, p):
    ns = za.shape[0]
    args = [za, zc, zba, zm, sconv, p["conv_w"], p["gm_norm_g"], p["gm_w0"], p["gm_b0"], p["gdn_par_r"],
            jnp.asarray(_BONES64, BF16), p["mem_qn_gx"], jnp.asarray(_BMEAN64, BF16)]
    widths = (BW, BW, 2304, BW, BW, BW, 128, 128, BW)
    return pl.pallas_call(
        _sample_tok_kernel,
        in_specs=[_full(a.shape) for a in args],
        out_specs=[_full((ns, w)) for w in widths],
        out_shape=[jax.ShapeDtypeStruct((ns, w), F32) for w in widths],
        grid=(1,),
        compiler_params=_cp("arbitrary"),
        name="sample_tok",
    )(*args)


def _gdn_step_kernel(s_ref, q_ref, k_ref, v_ref, beta_ref, g_ref, zg_ref, gout_ref, so_ref, o_ref):
    s = s_ref[...]
    q, k = q_ref[...], k_ref[...]
    eg = jnp.exp(g_ref[...])
    beta = beta_ref[...]
    sk = jnp.sum(s * k, axis=-1, keepdims=True)
    sq = jnp.sum(s * q, axis=-1, keepdims=True)
    qk = jnp.sum(q * k, axis=-1, keepdims=True)
    u = beta * v_ref[...] - beta * eg * sk
    o = eg * sq + qk * u
    so_ref[...] = eg * s + u * k
    ms = jnp.sum(o * o, axis=1, keepdims=True) * (1.0 / HD)
    o_ref[...] = o * lax.rsqrt(ms + EPS) * gout_ref[...] * _silu(zg_ref[...])


def _gdn_step(s, q, k, v, beta, g, zg, gout_col, bn):
    nb = s.shape[0]
    row = pl.BlockSpec((bn, 1, HD), lambda i: (i, 0, 0))
    colv = pl.BlockSpec((bn, HD, 1), lambda i: (i, 0, 0))
    sca = pl.BlockSpec((bn, 1, 1), lambda i: (i, 0, 0))
    mat = pl.BlockSpec((bn, HD, HD), lambda i: (i, 0, 0))
    return pl.pallas_call(
        _gdn_step_kernel,
        grid=(nb // bn,),
        in_specs=[mat, row, row, colv, sca, sca, colv, _full((HD, 1))],
        out_specs=[mat, colv],
        out_shape=[jax.ShapeDtypeStruct((nb, HD, HD), F32), jax.ShapeDtypeStruct((nb, HD, 1), F32)],
        compiler_params=_cp("parallel"),
        name="gdn_step",
    )(s, q, k, v, beta, g, zg, gout_col)


def _mem_kv_kernel(x_ref, g_ref, w_ref, kg_ref, bmean_ref, ok_ref, ov_ref):
    kv = _bdot(_rms(x_ref[...], g_ref[...]), w_ref[...])
    k = kv[:, 0:BW]
    ok_ref[...] = k * lax.rsqrt(_bdot(k * k, bmean_ref[...]) + EPS) * kg_ref[...]
    ov_ref[...] = kv[:, BW:]


def _mem_kv(mem, g, w, kgx, tm):
    m = mem.shape[0]
    bmean = jnp.asarray(_BMEAN64, BF16)
    return pl.pallas_call(
        _mem_kv_kernel,
        grid=(m // tm,),
        in_specs=[pl.BlockSpec((tm, D_MODEL), lambda i: (i, 0)), _full((1, D_MODEL)), _full(w.shape), _full((1, BW)), _full((BW, BW))],
        out_specs=[pl.BlockSpec((tm, BW), lambda i: (i, 0))] * 2,
        out_shape=[jax.ShapeDtypeStruct((m, BW), F32)] * 2,
        compiler_params=_cp("parallel"),
        name="mem_kv",
    )(mem, g, w, kgx, bmean)


def _mem_attn_kernel(q_ref, k_ref, v_ref, gq_ref, bmean_ref, o_ref):
    q = q_ref[...]
    qn = q * lax.rsqrt(_bdot(q * q, bmean_ref[...]) + EPS) * gq_ref[...]
    kb = k_ref[...].astype(BF16)
    vb = v_ref[...].astype(BF16)
    lh = _lane_head()
    out = jnp.zeros(q.shape, F32)
    for h in range(HEADS):
        mh = lh == h
        s = lax.dot_general(jnp.where(mh, qn, 0.0).astype(BF16), kb, NT, preferred_element_type=F32) * (HD ** -0.5)
        e = jnp.exp(s - jnp.max(s, axis=-1, keepdims=True))
        pr = e / jnp.sum(e, axis=-1, keepdims=True)
        out = out + jnp.where(mh, jnp.dot(pr.astype(BF16), vb, preferred_element_type=F32), 0.0)
    o_ref[...] = out


def _mem_attn(zm, mk, mv, gqx, n, t, mt, tq):
    nq = t // tq
    bmean = jnp.asarray(_BMEAN64, BF16)
    return pl.pallas_call(
        _mem_attn_kernel,
        grid=(n, nq),
        in_specs=[pl.BlockSpec((tq, BW), lambda b, i: (b * nq + i, 0)), pl.BlockSpec((mt, BW), lambda b, i: (b, 0)),
                  pl.BlockSpec((mt, BW), lambda b, i: (b, 0)), _full((1, BW)), _full((BW, BW))],
        out_specs=pl.BlockSpec((tq, BW), lambda b, i: (b * nq + i, 0)),
        out_shape=jax.ShapeDtypeStruct((n * t, BW), F32),
        compiler_params=_cp("parallel", "parallel"),
        name="mem_attn",
    )(zm, mk, mv, gqx, bmean)


def _mem_attn_s_kernel(q_ref, k_ref, v_ref, ind_ref, o_ref, *, bn):
    ind = ind_ref[...]
    for i in range(bn):
        qbd = (ind * q_ref[i:i + 1, :]).astype(BF16)
        s = lax.dot_general(qbd, k_ref[i].astype(BF16), NT, preferred_element_type=F32) * (HD ** -0.5)
        e = jnp.exp(s - jnp.max(s, axis=-1, keepdims=True))
        pr = e / jnp.sum(e, axis=-1, keepdims=True)
        o8 = jnp.dot(pr.astype(BF16), v_ref[i].astype(BF16), preferred_element_type=F32)
        o_ref[i:i + 1, :] = jnp.sum(o8 * ind, axis=0, keepdims=True)


def _mem_attn_s(layer, mqn, cache_k, cache_v, bn):
    ns = mqn.shape[0]
    mt = cache_k.shape[2]
    kv_spec = pl.BlockSpec((None, bn, mt, BW), lambda i: (layer, i, 0, 0))
    return pl.pallas_call(
        functools.partial(_mem_attn_s_kernel, bn=bn),
        grid=(ns // bn,),
        in_specs=[pl.BlockSpec((bn, BW), lambda i: (i, 0)), kv_spec, kv_spec, _full((8, BW))],
        out_specs=pl.BlockSpec((bn, BW), lambda i: (i, 0)),
        out_shape=jax.ShapeDtypeStruct((ns, BW), F32),
        compiler_params=_cp("parallel"),
        name="mem_attn_s",
    )(mqn, cache_k, cache_v, jnp.asarray(_IND8))


def _merge_kernel(x_ref, a_ref, b_ref, c_ref, m_ref, g1_ref, wg_ref, wb_ref, wo_ref, g2_ref, wr_ref, br_ref,
                  x1_ref, h2_ref, ei_ref, ew_ref):
    x = x_ref[...]
    hb = _rms(x, g1_ref[...]).astype(BF16)
    acc = jnp.zeros(x.shape, F32)
    for b, br in enumerate((a_ref, b_ref, c_ref, m_ref)):
        gate = jax.nn.sigmoid(jnp.dot(hb, wg_ref[:, b * D_MODEL:(b + 1) * D_MODEL], preferred_element_type=F32))
        acc = acc + gate * jnp.dot(br[...].astype(BF16), wb_ref[b], preferred_element_type=F32)
    x1 = x + jnp.dot(acc.astype(BF16), wo_ref[...], preferred_element_type=F32)
    x1_ref[...] = x1
    h2 = _rms(x1, g2_ref[...])
    h2_ref[...] = h2
    logits = _hdot(h2, wr_ref[...]) + br_ref[...]
    lane = lax.broadcasted_iota(jnp.int32, (1, 128), 1).astype(F32)
    big = 1e9
    lg = jnp.where(lane < N_GROUPS, logits, NEG)
    mg = jnp.max(lg, axis=-1, keepdims=True)
    g_w = 1.0 / jnp.sum(jnp.exp(lg - mg), axis=-1, keepdims=True)
    gi = jnp.min(jnp.where(lg == mg, lane, big), axis=-1, keepdims=True)
    sel = (lane >= N_GROUPS) & (lane < N_GROUPS + N_EXPERTS) & (jnp.floor((lane - N_GROUPS) * (1.0 / EPG)) == gi)
    le = jnp.where(sel, logits, NEG)
    m1 = jnp.max(le, axis=-1, keepdims=True)
    i1 = jnp.min(jnp.where(le == m1, lane, big), axis=-1, keepdims=True)
    le2 = jnp.where(lane == i1, NEG, le)
    m2 = jnp.max(le2, axis=-1, keepdims=True)
    i2 = jnp.min(jnp.where(le2 == m2, lane, big), axis=-1, keepdims=True)
    z = jnp.sum(jnp.exp(le - m1), axis=-1, keepdims=True)
    p1 = 1.0 / z
    p2 = jnp.exp(m2 - m1) / z
    w1 = p1 / (p1 + p2) * g_w
    w2 = p2 / (p1 + p2) * g_w
    ei_ref[...] = jnp.where(lane == 0, i1 - N_GROUPS, jnp.where(lane == 1, i2 - N_GROUPS, 0.0)).astype(jnp.int32)
    ew_ref[...] = jnp.where(lane == 0, w1, jnp.where(lane == 1, w2, 0.0))


def _merge(x, branches, p, tm):
    m = x.shape[0]
    consts = [p["norm1_g"], p["w_gate"], p["w_branch"], p["w_out"], p["norm2_g"], p["w_router"], p["b_router"]]
    tile = lambda w: pl.BlockSpec((tm, w), lambda i: (i, 0))
    return pl.pallas_call(
        _merge_kernel,
        grid=(m // tm,),
        in_specs=[tile(D_MODEL)] + [tile(BW)] * 4 + [_full(c.shape) for c in consts],
        out_specs=[tile(D_MODEL), tile(D_MODEL), tile(128), tile(128)],
        out_shape=[jax.ShapeDtypeStruct((m, D_MODEL), F32), jax.ShapeDtypeStruct((m, D_MODEL), F32),
                   jax.ShapeDtypeStruct((m, 128), jnp.int32), jax.ShapeDtypeStruct((m, 128), F32)],
        compiler_params=_cp("parallel"),
        name="merge",
    )(x, *branches, *consts)


def _dispatch_kernel(pos_ref, h_hbm, xs_in, xs_out, sem, *, tmd):
    del xs_in
    base = pl.program_id(0) * tmd

    def issue(i, carry):
        for s in range(2):
            pltpu.make_async_copy(h_hbm.at[pl.ds(base + i, 1)], xs_out.at[pl.ds(pos_ref[2 * i + s], 1)], sem).start()
        return carry

    lax.fori_loop(0, tmd, issue, 0)

    def drain(i, carry):
        for s in range(2):
            pltpu.make_async_copy(h_hbm.at[pl.ds(0, 1)], xs_out.at[pl.ds(0, 1)], sem).wait()
        return carry

    lax.fori_loop(0, tmd, drain, 0)


def _dispatch(pos, h2, rows, tmd):
    m = h2.shape[0]
    xs0 = jnp.zeros((rows, D_MODEL), F32)
    return pl.pallas_call(
        functools.partial(_dispatch_kernel, tmd=tmd),
        grid=(m // tmd,),
        in_specs=[pl.BlockSpec((2 * tmd,), lambda i: (i,), memory_space=pltpu.SMEM),
                  pl.BlockSpec(memory_space=pl.ANY), pl.BlockSpec(memory_space=pl.ANY)],
        out_specs=pl.BlockSpec(memory_space=pl.ANY),
        out_shape=jax.ShapeDtypeStruct((rows, D_MODEL), F32),
        scratch_shapes=[pltpu.SemaphoreType.DMA(())],
        input_output_aliases={2: 0},
        compiler_params=_cp("arbitrary"),
        name="moe_dispatch",
    )(pos, h2, xs0)


def _expert_kernel(te_ref, nv_ref, x_ref, wg_ref, wu_ref, wd_ref, o_ref):
    del te_ref

    @pl.when(pl.program_id(0) < nv_ref[0])
    def _():
        xb = x_ref[...].astype(BF16)
        gt = jnp.dot(xb, wg_ref[...].astype(BF16), preferred_element_type=F32)
        up = jnp.dot(xb, wu_ref[...].astype(BF16), preferred_element_type=F32)
        o_ref[...] = jnp.dot((_silu(gt) * up).astype(BF16), wd_ref[...].astype(BF16), preferred_element_type=F32)

    @pl.when(pl.program_id(0) >= nv_ref[0])
    def _():
        o_ref[...] = jnp.zeros(o_ref.shape, F32)


def _experts(layer, tile_expert, n_valid, xs, w_gate, w_up, w_down, te):
    rows = xs.shape[0]

    def xmap(i, te_ref, nv_ref):
        return (jnp.minimum(i, nv_ref[0] - 1), 0)

    def wmap(i, te_ref, nv_ref):
        return (layer, te_ref[i], 0, 0)

    grid_spec = pltpu.PrefetchScalarGridSpec(
        num_scalar_prefetch=2,
        grid=(rows // te,),
        in_specs=[pl.BlockSpec((te, D_MODEL), xmap), pl.BlockSpec((None, None, D_MODEL, D_EXPERT), wmap),
                  pl.BlockSpec((None, None, D_MODEL, D_EXPERT), wmap), pl.BlockSpec((None, None, D_EXPERT, D_MODEL), wmap)],
        out_specs=pl.BlockSpec((te, D_MODEL), lambda i, te_ref, nv_ref: (i, 0)),
    )
    return pl.pallas_call(
        _expert_kernel,
        grid_spec=grid_spec,
        out_shape=jax.ShapeDtypeStruct((rows, D_MODEL), F32),
        compiler_params=_cp("arbitrary"),
        name="moe_experts",
    )(tile_expert, n_valid, xs, w_gate, w_up, w_down)


def _combine_kernel(pos_ref, x1_ref, ew_ref, ys_hbm, o_ref, r0, r1, sem, *, tmc):
    bufs = (r0, r1)

    def issue(i, carry):
        for s in range(2):
            pltpu.make_async_copy(ys_hbm.at[pl.ds(pos_ref[2 * i + s], 1)], bufs[s].at[pl.ds(i, 1)], sem).start()
        return carry

    lax.fori_loop(0, tmc, issue, 0)

    def drain(i, carry):
        for s in range(2):
            pltpu.make_async_copy(ys_hbm.at[pl.ds(0, 1)], bufs[s].at[pl.ds(0, 1)], sem).wait()
        return carry

    lax.fori_loop(0, tmc, drain, 0)
    ew = ew_ref[...]
    o_ref[...] = x1_ref[...] + ew[:, 0:1] * r0[...] + ew[:, 1:2] * r1[...]


def _combine(pos, x1, ew, ys, tmc):
    m = x1.shape[0]
    return pl.pallas_call(
        functools.partial(_combine_kernel, tmc=tmc),
        grid=(m // tmc,),
        in_specs=[pl.BlockSpec((2 * tmc,), lambda i: (i,), memory_space=pltpu.SMEM),
                  pl.BlockSpec((tmc, D_MODEL), lambda i: (i, 0)), pl.BlockSpec((tmc, 128), lambda i: (i, 0)),
                  pl.BlockSpec(memory_space=pl.ANY)],
        out_specs=pl.BlockSpec((tmc, D_MODEL), lambda i: (i, 0)),
        out_shape=jax.ShapeDtypeStruct((m, D_MODEL), F32),
        scratch_shapes=[pltpu.VMEM((tmc, D_MODEL), F32), pltpu.VMEM((tmc, D_MODEL), F32), pltpu.SemaphoreType.DMA(())],
        compiler_params=_cp("arbitrary"),
        name="moe_combine",
    )(pos, x1, ew, ys)


def _moe(layer, x1, h2, ei, ew, w_gate, w_up, w_down, te, tmd):
    m = x1.shape[0]
    flat_e = ei[:, 0:2].reshape(-1)
    onehot = (flat_e[:, None] == jnp.arange(N_EXPERTS, dtype=jnp.int32)[None, :]).astype(jnp.int32)
    csum = jnp.cumsum(onehot, axis=0)
    rank = jnp.sum(csum * onehot, axis=1) - 1
    counts = csum[-1]
    padded = ((counts + te - 1) // te) * te
    pend = jnp.cumsum(padded)
    pstart = pend - padded
    pos = (jnp.sum(onehot * pstart[None, :], axis=1) + rank).astype(jnp.int32)
    rows = ((2 * m + N_EXPERTS * (te - 1)) // te) * te
    n_tiles = rows // te
    n_valid = (pend[-1] // te).astype(jnp.int32).reshape(1)
    tile_start = jnp.arange(n_tiles, dtype=jnp.int32) * te
    tile_expert = jnp.minimum(jnp.sum((tile_start[:, None] >= pend[None, :]).astype(jnp.int32), axis=1), N_EXPERTS - 1)
    last_e = jnp.take(tile_expert, jnp.maximum(n_valid[0] - 1, 0))
    tile_expert = jnp.where(jnp.arange(n_tiles) < n_valid[0], tile_expert, last_e).astype(jnp.int32)
    xs = _dispatch(pos, h2, rows, tmd)
    ys = _experts(layer, tile_expert, n_valid, xs, w_gate, w_up, w_down, te)
    return _combine(pos, x1, ew, ys, tmd)


def _tile4(v):
    return jnp.tile(v, HEADS).reshape(1, BW)


def _prep_layer(l, w):
    w_in = w["w_in"][l]
    z = lambda n: jnp.zeros((D_MODEL, n), F32)
    b_al = w_in[:, 1952:1960]
    w_small = jnp.concatenate(
        [w_in[:, 0:512], w_in[:, 512:768], w_in[:, 768:896], z(64), w_in[:, 896:928], z(32), w_in[:, 928:1696],
         w_in[:, 1696:1952], w_in[:, 1960:2216], b_al, z(120)], axis=1).astype(BF16)
    uq = w["mla_w_uq"][l]
    w_uq = jnp.pad(uq, ((0, 0), (0, 0), (0, 32))).reshape(256, 512).astype(BF16)
    qg = jnp.tile(jnp.concatenate([w["mla_qn_g"][l], w["mla_qr_g"][l], jnp.zeros((32,), F32)]), HEADS).reshape(1, 512)
    uk = w["mla_w_uk"][l]
    w_uk_p = jnp.pad(uk, ((0, 0), (0, 0), (0, 64))).reshape(128, 512).astype(BF16)
    kg = jnp.tile(jnp.concatenate([w["mla_kn_g"][l], jnp.zeros((64,), F32)]), HEADS).reshape(1, 512)
    kr_g = jnp.concatenate([jnp.zeros((64,), F32), w["mla_kr_g"][l], jnp.zeros((32,), F32)]).reshape(1, 128)
    uv = w["mla_w_uv"][l]
    w_uv_p = jnp.stack([jnp.pad(uv[:, h, :], ((0, 0), (64 * h, BW - 64 * h - 64))) for h in range(HEADS)]).astype(BF16)
    par_r = jnp.zeros((8, 128), F32).at[0, 4:8].set(w["gdn_a_log"][l]).at[1, 4:8].set(w["gdn_dt_bias"][l])
    par_c = jnp.zeros((8, 128), F32).at[4:8, 0].set(w["gdn_a_log"][l]).at[4:8, 1].set(w["gdn_dt_bias"][l])
    w_router = jnp.concatenate([w["moe_wg"][l], w["moe_we"][l], jnp.zeros((D_MODEL, 128 - 36), F32)], axis=1)
    b_router = jnp.concatenate([w["moe_bg"][l], w["moe_be"][l], jnp.zeros((128 - 36,), F32)]).reshape(1, 128)
    return {
        "norm1_g": w["norm1_g"][l].reshape(1, D_MODEL), "w_small": w_small, "w_bat": b_al.T.astype(BF16),
        "w_gate": w_in[:, 2216:].astype(BF16),
        "gm_norm_g": w["gm_norm_g"][l].reshape(1, BW), "gm_ws": w["gm_ws"][l],
        "gm_bfull": jnp.repeat(w["gm_b"][l].T, HD, axis=1),
        "gm_w0": jnp.repeat(w["gm_ws"][l][:, 0, 0], HD).reshape(1, BW), "gm_b0": jnp.repeat(w["gm_b"][l][:, 0], HD).reshape(1, BW),
        "mla_cq_g": w["mla_cq_g"][l].reshape(1, 256), "w_uq": w_uq, "qg": qg, "mla_ckv_g": w["mla_ckv_g"][l].reshape(1, 128),
        "kr_g": kr_g, "w_uk": w_uk_p, "kg": kg, "w_uv": w_uv_p,
        "w_uk_c": uk.reshape(128, BW).astype(BF16), "w_uv_c": uv.reshape(128, BW).astype(BF16), "kn_gx": _tile4(w["mla_kn_g"][l]),
        "conv_w": w["gdn_conv_w"][l], "gdn_par_r": par_r, "gdn_par_c": par_c, "gdn_out_gx": _tile4(w["gdn_out_g"][l]),
        "gdn_out_gc": w["gdn_out_g"][l].reshape(HD, 1),
        "mem_norm_g": w["mem_norm_g"][l].reshape(1, D_MODEL), "mem_w_kv": w["mem_w_kv"][l].astype(BF16),
        "mem_qn_gx": _tile4(w["mem_qn_g"][l]), "mem_kn_gx": _tile4(w["mem_kn_g"][l]),
        "w_branch": w["w_branch"][l].astype(BF16), "w_out": w["w_out"][l].astype(BF16),
        "norm2_g": w["norm2_g"][l].reshape(1, D_MODEL), "w_router": w_router, "b_router": b_router,
    }


def _rope_tables(pos):
    half = MLA_ROPE // 2
    inv = ROPE_THETA ** (-jnp.arange(half, dtype=F32) / half)
    ang = pos.astype(F32)[:, None] * inv[None, :]
    cos, sin = jnp.cos(ang), jnp.sin(ang)
    t = pos.shape[0]
    one, zero = jnp.ones((t, 64), F32), jnp.zeros((t, 64), F32)
    z16, z32 = jnp.zeros((t, 16), F32), jnp.zeros((t, 32), F32)
    return (jnp.concatenate([one, cos, cos, jnp.ones((t, 32), F32)], axis=1),
            jnp.concatenate([zero, -sin, z16, z32], axis=1),
            jnp.concatenate([zero, z16, sin, z32], axis=1))


def kernel(x_prompt, mem_prompt, x_sample, cache_mla_ckv, cache_mla_kr, cache_mem_k, cache_mem_v, state_gdn, state_conv,
           page_table, norm1_g, w_in, gm_norm_g, gm_ws, gm_b, mla_cq_g, mla_w_uq, mla_qn_g, mla_qr_g, mla_ckv_g, mla_kr_g,
           mla_w_uk, mla_kn_g, mla_w_uv, gdn_conv_w, gdn_a_log, gdn_dt_bias, gdn_out_g, mem_norm_g, mem_w_kv, mem_qn_g,
           mem_kn_g, w_branch, w_out, norm2_g, moe_wg, moe_bg, moe_we, moe_be, moe_w_gate, moe_w_up, moe_w_down):
    w = dict(norm1_g=norm1_g, w_in=w_in, gm_norm_g=gm_norm_g, gm_ws=gm_ws, gm_b=gm_b, mla_cq_g=mla_cq_g, mla_w_uq=mla_w_uq,
             mla_qn_g=mla_qn_g, mla_qr_g=mla_qr_g, mla_ckv_g=mla_ckv_g, mla_kr_g=mla_kr_g, mla_w_uk=mla_w_uk, mla_kn_g=mla_kn_g,
             mla_w_uv=mla_w_uv, gdn_conv_w=gdn_conv_w, gdn_a_log=gdn_a_log, gdn_dt_bias=gdn_dt_bias, gdn_out_g=gdn_out_g,
             mem_norm_g=mem_norm_g, mem_w_kv=mem_w_kv, mem_qn_g=mem_qn_g, mem_kn_g=mem_kn_g, w_branch=w_branch, w_out=w_out,
             norm2_g=norm2_g, moe_wg=moe_wg, moe_bg=moe_bg, moe_we=moe_we, moe_be=moe_be)
    depth = w_in.shape[0]
    bp, tp, _ = x_prompt.shape
    bs = x_sample.shape[0]
    mt = mem_prompt.shape[1]
    n_pages = page_table.shape[1]
    past_len = n_pages * cache_mla_ckv.shape[2]
    mp = bp * tp

    tm_p = min(512, mp)
    tq = min(256, tp)
    tg = min(128, tp)
    pp = min(8, n_pages)
    tabs_p = _rope_tables(jnp.arange(tp, dtype=jnp.int32))
    tabs_s = _rope_tables(jnp.full((bs,), past_len, jnp.int32))

    xp = x_prompt.reshape(mp, D_MODEL)
    xs = x_sample.reshape(bs, D_MODEL)
    mem = mem_prompt.reshape(bp * mt, D_MODEL)
    cache_k = cache_mem_k.reshape(depth, bs, mt, BW)
    cache_v = cache_mem_v.reshape(depth, bs, mt, BW)
    rows_p, rows_s = [], []
    for l in range(depth):
        p = _prep_layer(l, w)
        mk, mv = _mem_kv(mem, p["mem_norm_g"], p["mem_w_kv"], p["mem_kn_gx"], min(512, bp * mt))
        za, zb, zc, zm, zba, bat = _in_proj(xp, p["norm1_g"], p["w_small"], p["w_bat"], tm_p)
        a_out = _gmlp(za, mp, p["gm_norm_g"], p["gm_ws"], p["gm_bfull"], tm_p)
        q4, k4, ckv, kr, ckv16, _ = _mla_pre(zb, mp, tabs_p, tp // min(tm_p, tp), p, min(tm_p, tp))
        b_out = _mla_attn(q4, k4, ckv16, p["w_uv"], bp, tp, tq)
        c_out, sfin = _gdn_prompt(zc, zba, bat, bp, tp, p, tg)
        m_out = _mem_attn(zm, mk, mv, p["mem_qn_gx"], bp, tp, mt, tq)
        x1, h2, ei, ew = _merge(xp, (a_out, b_out, c_out, m_out), p, min(256, mp))
        xp = _moe(l, x1, h2, ei, ew, moe_w_gate, moe_w_up, moe_w_down, min(256, mp), min(512, mp))
        s_p = jnp.stack([sfin[:, 64 * h:64 * h + 64, 64 * h:64 * h + 64] for h in range(HEADS)], axis=1).transpose(0, 1, 3, 2)
        conv_p = zc.reshape(bp, tp, 1024)[:, tp - 3:, 0:QKV_DIM]
        rows_p.append((ckv.reshape(bp, tp, 128), kr.reshape(bp, tp, MLA_ROPE), s_p, conv_p,
                       mk.reshape(bp, mt, HEADS, HD), mv.reshape(bp, mt, HEADS, HD)))
        za, zb, zc, zm, zba, _ = _in_proj(xs, p["norm1_g"], p["w_small"], p["w_bat"], bs)
        a_s, v_s, conv_s, gq, gk, gv, beta, gdec, mqn = _sample_tok(za, zc, zba, zm, state_conv[l].reshape(bs, 3 * QKV_DIM), p)
        _, _, ckv_s, kr_s, _, q32 = _mla_pre(zb, bs, tabs_s, 1, p, bs)
        q3 = q32.reshape(bs, HEADS, 128)
        qk = q3[:, :, 0:64].reshape(bs, 1, BW)
        qr8 = jnp.pad(q3[:, :, 64:96], ((0, 0), (0, 4), (0, 0)))
        b_s = _mla_decode(l, page_table, cache_mla_ckv, cache_mla_kr, qk, qr8, ckv_s.reshape(bs, 1, 128),
                          kr_s.reshape(bs, 1, MLA_ROPE), p["kn_gx"], p["w_uk_c"], p["w_uv_c"], pp).reshape(bs, BW)
        nb = bs * HEADS
        s_new, c_col = _gdn_step(state_gdn[l].reshape(nb, HD, HD), gq.reshape(nb, 1, HD), gk.reshape(nb, 1, HD),
                                 gv.reshape(nb, HD, 1), beta[:, 0:4].reshape(nb, 1, 1), gdec[:, 4:8].reshape(nb, 1, 1),
                                 zc[:, QKV_DIM:QKV_DIM + BW].reshape(nb, HD, 1), p["gdn_out_gc"], min(64, nb))
        m_s = _mem_attn_s(l, mqn, cache_k, cache_v, min(8, bs))
        x1, h2, ei, ew = _merge(xs, (a_s, b_s, c_col.reshape(bs, BW), m_s), p, bs)
        xs = _moe(l, x1, h2, ei, ew, moe_w_gate, moe_w_up, moe_w_down, min(32, bs), bs)
        rows_s.append((ckv_s.reshape(bs, 1, 128), kr_s.reshape(bs, 1, MLA_ROPE), s_new.reshape(bs, HEADS, HD, HD),
                       conv_s.reshape(bs, 3, QKV_DIM), v_s.reshape(bs, 1, BW)))
    p_out = [jnp.stack(a) for a in zip(*rows_p)]
    s_out = [jnp.stack(a) for a in zip(*rows_s)]
    return (xp.reshape(bp, tp, D_MODEL), xs.reshape(bs, 1, D_MODEL), *p_out, *s_out)
```

```python
import functools

import numpy as np
import jax
import jax.numpy as jnp
from jax import lax
from jax.experimental import pallas as pl
from jax.experimental.pallas import tpu as pltpu

F32 = jnp.float32
BF16 = jnp.bfloat16
HI = lax.Precision.HIGHEST
EPS = 1e-6
NEG = float("-inf")

D_MODEL = 1024
HEADS = 4
HD = 64
BW = 256
MLA_ROPE = 32
MLA_KV_RANK = 128
MLA_SCALE = 96.0 ** -0.5
ROPE_THETA = 10000.0
GM_CHUNK = 128
GDN_CHUNK = 64
QKV_DIM = 768
N_GROUPS = 4
EPG = 8
N_EXPERTS = 32
D_EXPERT = 256
PAGE = 128
VMEM_LIMIT = 56 * 1024 * 1024

NT = (((1,), (1,)), ((), ()))
TN = (((0,), (0,)), ((), ()))


def _cp(*sem):
    return pltpu.CompilerParams(dimension_semantics=sem, vmem_limit_bytes=VMEM_LIMIT)


def _rms(x, g):
    ms = jnp.sum(x * x, axis=-1, keepdims=True) * (1.0 / x.shape[-1])
    return x * lax.rsqrt(ms + EPS) * g


def _bdot(a, b):
    return jnp.dot(a.astype(BF16), b.astype(BF16), preferred_element_type=F32)


def _bdot_nt(a, b):
    return lax.dot_general(a.astype(BF16), b.astype(BF16), NT, preferred_element_type=F32)


def _hdot(a, b):
    return jnp.dot(a, b, precision=HI, preferred_element_type=F32)


def _split2(a):
    hi = a.astype(BF16)
    return hi, (a - hi.astype(F32)).astype(BF16)


def _dot3(a, b):
    d = lambda x, y: jnp.dot(x, y, preferred_element_type=F32)
    return d(a[0], b[0]) + (d(a[0], b[1]) + d(a[1], b[0]))


def _sel_dot(w01, x, left):
    x0 = x.astype(BF16)
    r1 = x - x0.astype(F32)
    x1 = r1.astype(BF16)
    x2 = (r1 - x1.astype(F32)).astype(BF16)
    d = (lambda p: jnp.dot(w01, p, preferred_element_type=F32)) if left else (lambda p: jnp.dot(p, w01, preferred_element_type=F32))
    return d(x0) + (d(x1) + d(x2))


def _silu(x):
    return x * jax.nn.sigmoid(x)


def _gelu(x):
    return 0.5 * x * (1.0 + lax.erf(x * 0.7071067811865476))


def _lane_head(width=BW):
    return lax.broadcasted_iota(jnp.int32, (1, width), 1) // HD


def _full(shape):
    n = len(shape)
    return pl.BlockSpec(shape, lambda *_: (0,) * n)


def _block_mean(width, segs):
    m = np.zeros((width, width), np.float32)
    for a, b in segs:
        m[a:b, a:b] = 1.0 / (b - a)
    return m


_BMEAN64 = _block_mean(BW, [(64 * h, 64 * h + 64) for h in range(HEADS)])
_BONES64 = _BMEAN64 * 64.0
_BQ = _block_mean(512, [(128 * h, 128 * h + 64) for h in range(HEADS)] + [(128 * h + 64, 128 * h + 96) for h in range(HEADS)])
_BK = _block_mean(512, [(128 * h, 128 * h + 64) for h in range(HEADS)])
_IND8 = np.zeros((8, BW), np.float32)
for _h in range(HEADS):
    _IND8[_h, 64 * _h:64 * _h + 64] = 1.0
_EXPB = np.zeros((128, BW), np.float32)
_EXPG = np.zeros((128, BW), np.float32)
for _h in range(HEADS):
    _EXPB[_h, 64 * _h:64 * _h + 64] = 1.0
    _EXPG[4 + _h, 64 * _h:64 * _h + 64] = 1.0
_BDMASK = (_BONES64 > 0).astype(np.float32)


def _in_proj_kernel(x_ref, g_ref, w_ref, wbat_ref, oa_ref, ob_ref, oc_ref, om_ref, oba_ref, obat_ref):
    hb = _rms(x_ref[...], g_ref[...]).astype(BF16)
    oa_ref[...] = jnp.dot(hb, w_ref[:, 0:512], preferred_element_type=F32)
    ob_ref[...] = jnp.dot(hb, w_ref[:, 512:1024], preferred_element_type=F32)
    oc_ref[...] = jnp.dot(hb, w_ref[:, 1024:2048], preferred_element_type=F32)
    om_ref[...] = jnp.dot(hb, w_ref[:, 2048:2304], preferred_element_type=F32)
    oba_ref[...] = jnp.dot(hb, w_ref[:, 2304:2432], preferred_element_type=F32)
    obat_ref[...] = lax.dot_general(wbat_ref[...], hb, NT, preferred_element_type=F32)


def _in_proj(x, g, w, wbat, tm):
    m = x.shape[0]
    widths = (512, 512, 1024, 256, 128)
    return pl.pallas_call(
        _in_proj_kernel,
        grid=(m // tm,),
        in_specs=[pl.BlockSpec((tm, D_MODEL), lambda i: (i, 0)), _full((1, D_MODEL)), _full(w.shape), _full(wbat.shape)],
        out_specs=[pl.BlockSpec((tm, n), lambda i: (i, 0)) for n in widths] + [pl.BlockSpec((8, tm), lambda i: (0, i))],
        out_shape=[jax.ShapeDtypeStruct((m, n), F32) for n in widths] + [jax.ShapeDtypeStruct((8, m), F32)],
        compiler_params=_cp("parallel"),
        name="in_proj",
    )(x, g, w, wbat)


def _gmlp_kernel(z_ref, g_ref, ws_ref, b_ref, o_ref, *, ta):
    row = lax.broadcasted_iota(jnp.int32, (GM_CHUNK, GM_CHUNK), 0)
    col = lax.broadcasted_iota(jnp.int32, (GM_CHUNK, GM_CHUNK), 1)
    tril = col <= row
    lh = _lane_head()
    wts = [jnp.where(tril, ws_ref[g], 0.0).astype(BF16) for g in range(HEADS)]
    for c in range(ta // GM_CHUNK):
        sl = slice(c * GM_CHUNK, (c + 1) * GM_CHUNK)
        ge = _gelu(z_ref[sl, :])
        u = ge[:, :BW]
        vb = _rms(ge[:, BW:], g_ref[...]).astype(BF16)
        s = b_ref[...]
        for g in range(HEADS):
            s = s + jnp.where(lh == g, jnp.dot(wts[g], vb, preferred_element_type=F32), 0.0)
        o_ref[sl, :] = u * s


def _gmlp(za, m, g, ws, bfull, ta):
    return pl.pallas_call(
        functools.partial(_gmlp_kernel, ta=ta),
        grid=(m // ta,),
        in_specs=[pl.BlockSpec((ta, 512), lambda i: (i, 0)), _full((1, BW)), _full(ws.shape), _full(bfull.shape)],
        out_specs=pl.BlockSpec((ta, BW), lambda i: (i, 0)),
        out_shape=jax.ShapeDtypeStruct((m, BW), F32),
        compiler_params=_cp("parallel"),
        name="gmlp",
    )(za, g, ws, bfull)


def _mla_pre_kernel(z_ref, c_ref, s1_ref, s2_ref, gcq_ref, wuq_ref, qg_ref, bq_ref, gckv_ref, gkr_ref, wuk_ref, kg_ref,
                    bk_ref, oq_ref, ok_ref, ockv_ref, okr_ref, ockv16_ref, oq32_ref):
    cs, s1, s2 = c_ref[...], s1_ref[...], s2_ref[...]

    def rope(x):
        return x * cs + pltpu.roll(x, 112, 1) * s1 + pltpu.roll(x, 16, 1) * s2

    z = z_ref[...]
    cq = _rms(z[:, 0:256], gcq_ref[...])
    q = _bdot(cq, wuq_ref[...])
    qn = q * lax.rsqrt(_bdot(q * q, bq_ref[...]) + EPS) * qg_ref[...]
    ckv = _rms(z[:, 256:384], gckv_ref[...])
    ockv_ref[...] = ckv
    ockv16_ref[...] = ckv.astype(BF16)
    krb = z[:, 384:512]
    kr = rope(krb * lax.rsqrt(jnp.sum(krb * krb, axis=-1, keepdims=True) * (1.0 / MLA_ROPE) + EPS) * gkr_ref[...])
    okr_ref[...] = kr[:, 64:96]
    k = _bdot(ckv, wuk_ref[...])
    kn = k * lax.rsqrt(_bdot(k * k, bk_ref[...]) + EPS) * kg_ref[...]
    for h in range(HEADS):
        sl = slice(128 * h, 128 * h + 128)
        qh = rope(qn[:, sl])
        oq_ref[h] = qh.astype(BF16)
        oq32_ref[:, sl] = qh
        ok_ref[h] = (kn[:, sl] + kr).astype(BF16)


def _mla_pre(zb, m, tabs, t_blocks, p, tm):
    cs, s1, s2 = tabs
    tab_spec = pl.BlockSpec((tm, 128), lambda i: (i % t_blocks, 0))
    consts = [p["mla_cq_g"], p["w_uq"], p["qg"], jnp.asarray(_BQ, BF16), p["mla_ckv_g"], p["kr_g"], p["w_uk"], p["kg"],
              jnp.asarray(_BK, BF16)]
    return pl.pallas_call(
        _mla_pre_kernel,
        grid=(m // tm,),
        in_specs=[pl.BlockSpec((tm, 512), lambda i: (i, 0)), tab_spec, tab_spec, tab_spec] + [_full(c.shape) for c in consts],
        out_specs=[pl.BlockSpec((HEADS, tm, 128), lambda i: (0, i, 0)), pl.BlockSpec((HEADS, tm, 128), lambda i: (0, i, 0)),
                   pl.BlockSpec((tm, 128), lambda i: (i, 0)), pl.BlockSpec((tm, MLA_ROPE), lambda i: (i, 0)),
                   pl.BlockSpec((tm, 128), lambda i: (i, 0)), pl.BlockSpec((tm, 512), lambda i: (i, 0))],
        out_shape=[jax.ShapeDtypeStruct((HEADS, m, 128), BF16), jax.ShapeDtypeStruct((HEADS, m, 128), BF16),
                   jax.ShapeDtypeStruct((m, 128), F32), jax.ShapeDtypeStruct((m, MLA_ROPE), F32),
                   jax.ShapeDtypeStruct((m, 128), BF16), jax.ShapeDtypeStruct((m, 512), F32)],
        compiler_params=_cp("parallel"),
        name="mla_pre",
    )(zb, cs, s1, s2, *consts)


def _mla_attn_kernel(q_ref, k_ref, v_ref, wuv_ref, o_ref, *, tq):
    i = pl.program_id(1)
    row = lax.broadcasted_iota(jnp.int32, (tq, tq), 0)
    col = lax.broadcasted_iota(jnp.int32, (tq, tq), 1)
    causal = col <= row

    def step(off, carry, mask):
        m, l, acc = carry
        vj = v_ref[pl.ds(off, tq), :]
        ss = []
        for h in range(HEADS):
            s = lax.dot_general(q_ref[h], k_ref[h, pl.ds(off, tq), :], NT, preferred_element_type=F32) * MLA_SCALE
            ss.append(jnp.where(causal, s, NEG) if mask else s)
        s = jnp.concatenate(ss, axis=0)
        mn = jnp.maximum(m, jnp.max(s, axis=-1, keepdims=True))
        pr = jnp.exp(s - mn)
        al = jnp.exp(m - mn)
        return mn, al * l + jnp.sum(pr, axis=-1, keepdims=True), al * acc + jnp.dot(pr.astype(BF16), vj, preferred_element_type=F32)

    init = (jnp.full((HEADS * tq, 1), NEG, F32), jnp.zeros((HEADS * tq, 1), F32), jnp.zeros((HEADS * tq, 128), F32))
    carry = lax.fori_loop(0, i, lambda j, c: step(pl.multiple_of(j * tq, tq), c, False), init)
    m, l, acc = step(pl.multiple_of(i * tq, tq), carry, True)
    lat = (acc / l).astype(BF16)
    out = jnp.zeros((tq, BW), F32)
    for h in range(HEADS):
        out = out + jnp.dot(lat[h * tq:(h + 1) * tq, :], wuv_ref[h], preferred_element_type=F32)
    o_ref[...] = out


def _mla_attn(q4, k4, ckv16, wuv, n, t, tq):
    nq = t // tq
    return pl.pallas_call(
        functools.partial(_mla_attn_kernel, tq=tq),
        grid=(n, nq),
        in_specs=[pl.BlockSpec((HEADS, tq, 128), lambda b, i: (0, b * nq + i, 0)),
                  pl.BlockSpec((HEADS, t, 128), lambda b, i: (0, b, 0)),
                  pl.BlockSpec((t, 128), lambda b, i: (b, 0)), _full(wuv.shape)],
        out_specs=pl.BlockSpec((tq, BW), lambda b, i: (b * nq + i, 0)),
        out_shape=jax.ShapeDtypeStruct((n * t, BW), F32),
        compiler_params=_cp("parallel", "arbitrary"),
        name="mla_attn",
    )(q4, k4, ckv16, wuv)


def _mla_decode_kernel(pt_ref, *refs, pp):
    del pt_ref
    ckv_refs, krt_refs = refs[:pp], refs[pp:2 * pp]
    qk_ref, qr_ref, cnew_ref, krnew_ref, kng_ref, wuk_ref, wuv_ref, ind_ref, bmean_ref, o_ref, m_sc, l_sc, acc_sc = refs[2 * pp:]
    j = pl.program_id(1)

    @pl.when(j == 0)
    def _():
        m_sc[...] = jnp.full(m_sc.shape, NEG, F32)
        l_sc[...] = jnp.zeros(l_sc.shape, F32)
        acc_sc[...] = jnp.zeros(acc_sc.shape, F32)

    ind = ind_ref[...]
    qbd = (ind * (qk_ref[...] * kng_ref[...])).astype(BF16)
    ind64 = (ind * (1.0 / HD)).astype(BF16)
    qr = qr_ref[...].astype(BF16)
    wuk = wuk_ref[...]

    cb = jnp.concatenate([r[...].astype(BF16) for r in ckv_refs], axis=0)
    krt = jnp.concatenate([r[...].astype(BF16) for r in krt_refs], axis=1)
    k = jnp.dot(cb, wuk, preferred_element_type=F32)
    num = lax.dot_general(qbd, k.astype(BF16), NT, preferred_element_type=F32)
    ms = lax.dot_general(ind64, (k * k).astype(BF16), NT, preferred_element_type=F32)
    s = (num * lax.rsqrt(ms + EPS) + jnp.dot(qr, krt, preferred_element_type=F32)) * MLA_SCALE
    m = m_sc[...]
    mn = jnp.maximum(m, jnp.max(s, axis=-1, keepdims=True))
    pb = jnp.exp(s - mn)
    al = jnp.exp(m - mn)
    l_sc[...] = al * l_sc[...] + jnp.sum(pb, axis=-1, keepdims=True)
    acc_sc[...] = al * acc_sc[...] + jnp.dot(pb.astype(BF16), cb, preferred_element_type=F32)
    m_sc[...] = mn

    @pl.when(j == pl.num_programs(1) - 1)
    def _():
        cb = jnp.broadcast_to(cnew_ref[...], (8, MLA_KV_RANK)).astype(BF16)
        k1 = jnp.dot(cb, wuk, preferred_element_type=F32)
        k1b = k1.astype(BF16).astype(F32)
        kk1 = (k1 * k1).astype(BF16).astype(F32)
        krn = krnew_ref[...].astype(BF16).astype(F32)
        num1 = jnp.sum(qbd.astype(F32) * k1b, axis=-1, keepdims=True)
        ms1 = jnp.sum(ind64.astype(F32) * kk1, axis=-1, keepdims=True)
        s1 = (num1 * lax.rsqrt(ms1 + EPS) + jnp.sum(qr.astype(F32) * krn, axis=-1, keepdims=True)) * MLA_SCALE
        m1 = m_sc[...]
        mn1 = jnp.maximum(m1, s1)
        p1 = jnp.exp(s1 - mn1)
        al1 = jnp.exp(m1 - mn1)
        lat = (al1 * acc_sc[...] + p1 * cb.astype(F32)) / (al1 * l_sc[...] + p1)
        o8 = jnp.dot(lat.astype(BF16), wuv_ref[...], preferred_element_type=F32)
        o_ref[...] = jnp.sum(o8 * ind, axis=0, keepdims=True)


def _mla_decode(layer, page_table, cache_ckv, cache_krt, qk, qr8, cnew, krnew, kng, wuk, wuv, pp):
    ns, npages = page_table.shape
    pt = page_table.reshape(-1)

    def page_spec(i, shape):
        return pl.BlockSpec((None, None) + shape, lambda n, j, pt_ref, i=i: (layer, pt_ref[n * npages + j * pp + i], 0, 0))

    def per_sample(shape):
        return pl.BlockSpec((None,) + shape, lambda n, j, pt_ref: (n, 0, 0))

    def const(a):
        nd = a.ndim
        return pl.BlockSpec(a.shape, lambda n, j, pt_ref: (0,) * nd)

    ind = jnp.asarray(_IND8)
    bmean = jnp.asarray(_BMEAN64, BF16)
    grid_spec = pltpu.PrefetchScalarGridSpec(
        num_scalar_prefetch=1,
        grid=(ns, npages // pp),
        in_specs=[page_spec(i, (PAGE, MLA_KV_RANK)) for i in range(pp)] + [page_spec(i, (MLA_ROPE, PAGE)) for i in range(pp)]
        + [per_sample((1, BW)), per_sample((8, MLA_ROPE)), per_sample((1, MLA_KV_RANK)), per_sample((1, MLA_ROPE)),
           const(kng), const(wuk), const(wuv), const(ind), const(bmean)],
        out_specs=per_sample((1, BW)),
        scratch_shapes=[pltpu.VMEM((8, 1), F32), pltpu.VMEM((8, 1), F32), pltpu.VMEM((8, MLA_KV_RANK), F32)],
    )
    return pl.pallas_call(
        functools.partial(_mla_decode_kernel, pp=pp),
        grid_spec=grid_spec,
        out_shape=jax.ShapeDtypeStruct((ns, 1, BW), F32),
        compiler_params=_cp("parallel", "arbitrary"),
        name="mla_decode",
    )(pt, *([cache_ckv] * pp), *([cache_krt] * pp), qk, qr8, cnew, krnew, kng, wuk, wuv, ind, bmean)


def _gdn_kernel(zc_ref, zba_ref, bat_ref, cw_ref, parr_ref, parc_ref, gout_ref, bones_ref, expb_ref, expg_ref, bdm_ref,
                o_ref, sfin_ref, xbuf, s_sc, *, tg):
    t = pl.program_id(1)
    c = GDN_CHUNK

    @pl.when(t == 0)
    def _():
        xbuf[0:8, :] = jnp.zeros((8, QKV_DIM), F32)
        s_sc[...] = jnp.zeros(s_sc.shape, F32)

    @pl.when(t > 0)
    def _():
        xbuf[5:8, :] = xbuf[tg + 5:tg + 8, :]

    xbuf[8:8 + tg, :] = zc_ref[:, 0:QKV_DIM]
    y = cw_ref[0:1, :] * xbuf[5:5 + tg, :]
    for i in range(1, 4):
        y = y + cw_ref[i:i + 1, :] * xbuf[5 + i:5 + i + tg, :]
    y = _silu(y)

    zba = zba_ref[...]
    beta_col = jax.nn.sigmoid(zba)
    g_col = -jnp.exp(parr_ref[0:1, :]) * jax.nn.softplus(zba + parr_ref[1:2, :])
    g_row = -jnp.exp(parc_ref[:, 0:1]) * jax.nn.softplus(bat_ref[...] + parc_ref[:, 1:2])

    row = lax.broadcasted_iota(jnp.int32, (c, c), 0)
    col = lax.broadcasted_iota(jnp.int32, (c, c), 1)
    incl = col <= row
    strict = col < row
    lt = incl.astype(BF16)
    ut = (col >= row).astype(BF16)
    bones = bones_ref[...]
    bdm = bdm_ref[...]
    expb, expg = expb_ref[...], expg_ref[...]
    brow = lax.broadcasted_iota(jnp.int32, (BW, BW), 0)
    bcl = lax.broadcasted_iota(jnp.int32, (BW, BW), 1)
    same_head = (brow // HD) == (bcl // HD)
    incl_bd = same_head & ((bcl % HD) <= (brow % HD))
    strict_bd = same_head & ((bcl % HD) < (brow % HD))
    eye_bd = (brow == bcl).astype(F32)
    same_head2 = jnp.concatenate([same_head, same_head], axis=1)

    def stack4(a):
        return jnp.concatenate([a, a, a, a], axis=0)

    def fold4(a):
        return (a[0:c] + a[c:2 * c]) + (a[2 * c:3 * c] + a[3 * c:4 * c])

    for ci in range(tg // c):
        sl = slice(ci * c, (ci + 1) * c)
        q, k, v = y[sl, 0:256], y[sl, 256:512], y[sl, 512:768]
        qn = q * lax.rsqrt(_bdot(q * q, bones) + EPS) * (HD ** -0.5)
        kn = k * lax.rsqrt(_bdot(k * k, bones) + EPS)
        bcol = beta_col[sl, :]
        gcum_c = _sel_dot(lt, g_col[sl, :], True)
        gcum_r = _sel_dot(ut, g_row[:, sl], False)
        gx = _sel_dot(expg, gcum_c, False)
        bx = _sel_dot(expb, bcol, False)
        egx = jnp.exp(gx)
        rhs = _split2(jnp.concatenate([bx * v, bx * egx * kn], axis=1))
        ks = jnp.where(same_head, stack4(kn), 0.0).astype(BF16)
        qs = jnp.where(same_head, stack4(qn), 0.0).astype(BF16)
        gc_s = jnp.concatenate([gcum_c[:, 4 + h:5 + h] for h in range(HEADS)], axis=0)
        gr_s = jnp.concatenate([gcum_r[4 + h:5 + h, :] for h in range(HEADS)], axis=1)
        beta_s = jnp.concatenate([bcol[:, h:h + 1] for h in range(HEADS)], axis=0)
        dm = jnp.exp(jnp.where(incl_bd, gc_s - gr_s, NEG))
        a = jnp.where(strict_bd, beta_s * lax.dot_general(ks, ks, NT, preferred_element_type=F32) * dm, 0.0)
        qk = lax.dot_general(qs, ks, NT, preferred_element_type=F32) * dm
        ps = _split2(-a)
        tinv = eye_bd - a
        for _ in range(5):
            ps = _split2(_dot3(ps, ps))
            tinv = tinv + _dot3(_split2(tinv), ps)
        x = fold4(jnp.where(same_head2, _dot3(_split2(tinv), (stack4(rhs[0]), stack4(rhs[1]))), 0.0))
        s = s_sc[...]
        sb = s.astype(BF16)
        u = x[:, :BW] - jnp.dot(x[:, BW:].astype(BF16), sb, preferred_element_type=F32)
        ub = u.astype(BF16)
        o = egx * jnp.dot(qn.astype(BF16), sb, preferred_element_type=F32)
        o = o + fold4(jnp.where(same_head, jnp.dot(qk.astype(BF16), stack4(ub), preferred_element_type=F32), 0.0))
        glast = gx[c - 1:c, :]
        kf = (kn * jnp.exp(glast - gx)).astype(BF16)
        s_new = jnp.exp(glast) * s + lax.dot_general(kf, ub, TN, preferred_element_type=F32)
        s_sc[...] = s_new * bdm
        on = o * lax.rsqrt(_bdot(o * o, bones) * (1.0 / HD) + EPS) * gout_ref[...]
        o_ref[sl, :] = on * _silu(zc_ref[sl, QKV_DIM:QKV_DIM + BW])

    @pl.when(t == pl.num_programs(1) - 1)
    def _():
        sfin_ref[...] = s_sc[...]


def _gdn_prompt(zc, zba, bat, n, t, p, tg):
    nt = t // tg
    consts = [p["conv_w"], p["gdn_par_r"], p["gdn_par_c"], p["gdn_out_gx"], jnp.asarray(_BONES64, BF16), jnp.asarray(_EXPB, BF16),
              jnp.asarray(_EXPG, BF16), jnp.asarray(_BDMASK)]
    return pl.pallas_call(
        functools.partial(_gdn_kernel, tg=tg),
        grid=(n, nt),
        in_specs=[pl.BlockSpec((tg, 1024), lambda b, i: (b * nt + i, 0)), pl.BlockSpec((tg, 128), lambda b, i: (b * nt + i, 0)),
                  pl.BlockSpec((8, tg), lambda b, i: (0, b * nt + i))] + [_full(c.shape) for c in consts],
        out_specs=[pl.BlockSpec((tg, BW), lambda b, i: (b * nt + i, 0)), pl.BlockSpec((None, BW, BW), lambda b, i: (b, 0, 0))],
        out_shape=[jax.ShapeDtypeStruct((n * t, BW), F32), jax.ShapeDtypeStruct((n, BW, BW), F32)],
        scratch_shapes=[pltpu.VMEM((8 + tg, QKV_DIM), F32), pltpu.VMEM((BW, BW), F32)],
        compiler_params=_cp("parallel", "arbitrary"),
        name="gdn_prompt",
    )(zc, zba, bat, *consts)


def _sample_tok_kernel(za_ref, zc_ref, zba_ref, zm_ref, sconv_ref, cw_ref, gmg_ref, gmw_ref, gmb_ref, parr_ref, bones_ref,
                       memg_ref, bmean_ref, oa_ref, ov_ref, oconv_ref, oq_ref, ok_ref, ovv_ref, obeta_ref, og_ref, omq_ref):
    ge = _gelu(za_ref[...])
    v = _rms(ge[:, BW:], gmg_ref[...])
    ov_ref[...] = v
    oa_ref[...] = ge[:, :BW] * (gmw_ref[...] * v + gmb_ref[...])
    sc = sconv_ref[...]
    x = zc_ref[:, 0:QKV_DIM]
    y = (cw_ref[0:1, :] * sc[:, 0:768] + cw_ref[1:2, :] * sc[:, 768:1536] + cw_ref[2:3, :] * sc[:, 1536:2304]
         + cw_ref[3:4, :] * x)
    oconv_ref[:, 0:1536] = sc[:, 768:2304]
    oconv_ref[:, 1536:2304] = x
    y = _silu(y)
    q, k = y[:, 0:256], y[:, 256:512]
    bones = bones_ref[...]
    oq_ref[...] = q * lax.rsqrt(_bdot(q * q, bones) + EPS) * (HD ** -0.5)
    ok_ref[...] = k * lax.rsqrt(_bdot(k * k, bones) + EPS)
    ovv_ref[...] = y[:, 512:768]
    zba = zba_ref[...]
    obeta_ref[...] = jax.nn.sigmoid(zba)
    og_ref[...] = -jnp.exp(parr_ref[0:1, :]) * jax.nn.softplus(zba + parr_ref[1:2, :])
    mq = zm_ref[...]
    omq_ref[...] = mq * lax.rsqrt(_bdot(mq * mq, bmean_ref[...]) + EPS) * memg_ref[...]


def _sample_tok(za, zc, zba, zm, sconv, p):
    ns = za.shape[0]
    args = [za, zc, zba, zm, sconv, p["conv_w"], p["gm_norm_g"], p["gm_w0"], p["gm_b0"], p["gdn_par_r"],
            jnp.asarray(_BONES64, BF16), p["mem_qn_gx"], jnp.asarray(_BMEAN64, BF16)]
    widths = (BW, BW, 2304, BW, BW, BW, 128, 128, BW)
    return pl.pallas_call(
        _sample_tok_kernel,
        in_specs=[_full(a.shape) for a in args],
        out_specs=[_full((ns, w)) for w in widths],
        out_shape=[jax.ShapeDtypeStruct((ns, w), F32) for w in widths],
        grid=(1,),
        compiler_params=_cp("arbitrary"),
        name="sample_tok",
    )(*args)


def _gdn_step_kernel(s_ref, q_ref, k_ref, v_ref, beta_ref, g_ref, zg_ref, gout_ref, so_ref, o_ref):
    s = s_ref[...]
    q, k = q_ref[...], k_ref[...]
    eg = jnp.exp(g_ref[...])
    beta = beta_ref[...]
    sk = jnp.sum(s * k, axis=-1, keepdims=True)
    sq = jnp.sum(s * q, axis=-1, keepdims=True)
    qk = jnp.sum(q * k, axis=-1, keepdims=True)
    u = beta * v_ref[...] - beta * eg * sk
    o = eg * sq + qk * u
    so_ref[...] = eg * s + u * k
    ms = jnp.sum(o * o, axis=1, keepdims=True) * (1.0 / HD)
    o_ref[...] = o * lax.rsqrt(ms + EPS) * gout_ref[...] * _silu(zg_ref[...])


def _gdn_step(s, q, k, v, beta, g, zg, gout_col, bn):
    nb = s.shape[0]
    row = pl.BlockSpec((bn, 1, HD), lambda i: (i, 0, 0))
    colv = pl.BlockSpec((bn, HD, 1), lambda i: (i, 0, 0))
    sca = pl.BlockSpec((bn, 1, 1), lambda i: (i, 0, 0))
    mat = pl.BlockSpec((bn, HD, HD), lambda i: (i, 0, 0))
    return pl.pallas_call(
        _gdn_step_kernel,
        grid=(nb // bn,),
        in_specs=[mat, row, row, colv, sca, sca, colv, _full((HD, 1))],
        out_specs=[mat, colv],
        out_shape=[jax.ShapeDtypeStruct((nb, HD, HD), F32), jax.ShapeDtypeStruct((nb, HD, 1), F32)],
        compiler_params=_cp("parallel"),
        name="gdn_step",
    )(s, q, k, v, beta, g, zg, gout_col)


def _mem_kv_kernel(x_ref, g_ref, w_ref, kg_ref, bmean_ref, ok_ref, ov_ref):
    kv = _bdot(_rms(x_ref[...], g_ref[...]), w_ref[...])
    k = kv[:, 0:BW]
    ok_ref[...] = k * lax.rsqrt(_bdot(k * k, bmean_ref[...]) + EPS) * kg_ref[...]
    ov_ref[...] = kv[:, BW:]


def _mem_kv(mem, g, w, kgx, tm):
    m = mem.shape[0]
    bmean = jnp.asarray(_BMEAN64, BF16)
    return pl.pallas_call(
        _mem_kv_kernel,
        grid=(m // tm,),
        in_specs=[pl.BlockSpec((tm, D_MODEL), lambda i: (i, 0)), _full((1, D_MODEL)), _full(w.shape), _full((1, BW)), _full((BW, BW))],
        out_specs=[pl.BlockSpec((tm, BW), lambda i: (i, 0))] * 2,
        out_shape=[jax.ShapeDtypeStruct((m, BW), F32)] * 2,
        compiler_params=_cp("parallel"),
        name="mem_kv",
    )(mem, g, w, kgx, bmean)


def _mem_attn_kernel(q_ref, k_ref, v_ref, gq_ref, bmean_ref, o_ref):
    q = q_ref[...]
    qn = q * lax.rsqrt(_bdot(q * q, bmean_ref[...]) + EPS) * gq_ref[...]
    kb = k_ref[...].astype(BF16)
    vb = v_ref[...].astype(BF16)
    lh = _lane_head()
    out = jnp.zeros(q.shape, F32)
    for h in range(HEADS):
        mh = lh == h
        s = lax.dot_general(jnp.where(mh, qn, 0.0).astype(BF16), kb, NT, preferred_element_type=F32) * (HD ** -0.5)
        e = jnp.exp(s - jnp.max(s, axis=-1, keepdims=True))
        pr = e / jnp.sum(e, axis=-1, keepdims=True)
        out = out + jnp.where(mh, jnp.dot(pr.astype(BF16), vb, preferred_element_type=F32), 0.0)
    o_ref[...] = out


def _mem_attn(zm, mk, mv, gqx, n, t, mt, tq):
    nq = t // tq
    bmean = jnp.asarray(_BMEAN64, BF16)
    return pl.pallas_call(
        _mem_attn_kernel,
        grid=(n, nq),
        in_specs=[pl.BlockSpec((tq, BW), lambda b, i: (b * nq + i, 0)), pl.BlockSpec((mt, BW), lambda b, i: (b, 0)),
                  pl.BlockSpec((mt, BW), lambda b, i: (b, 0)), _full((1, BW)), _full((BW, BW))],
        out_specs=pl.BlockSpec((tq, BW), lambda b, i: (b * nq + i, 0)),
        out_shape=jax.ShapeDtypeStruct((n * t, BW), F32),
        compiler_params=_cp("parallel", "parallel"),
        name="mem_attn",
    )(zm, mk, mv, gqx, bmean)


def _mem_attn_s_kernel(q_ref, k_ref, v_ref, ind_ref, o_ref, *, bn):
    ind = ind_ref[...]
    for i in range(bn):
        qbd = (ind * q_ref[i:i + 1, :]).astype(BF16)
        s = lax.dot_general(qbd, k_ref[i].astype(BF16), NT, preferred_element_type=F32) * (HD ** -0.5)
        e = jnp.exp(s - jnp.max(s, axis=-1, keepdims=True))
        pr = e / jnp.sum(e, axis=-1, keepdims=True)
        o8 = jnp.dot(pr.astype(BF16), v_ref[i].astype(BF16), preferred_element_type=F32)
        o_ref[i:i + 1, :] = jnp.sum(o8 * ind, axis=0, keepdims=True)


def _mem_attn_s(layer, mqn, cache_k, cache_v, bn):
    ns = mqn.shape[0]
    mt = cache_k.shape[2]
    kv_spec = pl.BlockSpec((None, bn, mt, BW), lambda i: (layer, i, 0, 0))
    return pl.pallas_call(
        functools.partial(_mem_attn_s_kernel, bn=bn),
        grid=(ns // bn,),
        in_specs=[pl.BlockSpec((bn, BW), lambda i: (i, 0)), kv_spec, kv_spec, _full((8, BW))],
        out_specs=pl.BlockSpec((bn, BW), lambda i: (i, 0)),
        out_shape=jax.ShapeDtypeStruct((ns, BW), F32),
        compiler_params=_cp("parallel"),
        name="mem_attn_s",
    )(mqn, cache_k, cache_v, jnp.asarray(_IND8))


def _merge_kernel(x_ref, a_ref, b_ref, c_ref, m_ref, g1_ref, wg_ref, wb_ref, wo_ref, g2_ref, wr_ref, br_ref,
                  x1_ref, h2_ref, ei_ref, ew_ref):
    x = x_ref[...]
    hb = _rms(x, g1_ref[...]).astype(BF16)
    acc = jnp.zeros(x.shape, F32)
    for b, br in enumerate((a_ref, b_ref, c_ref, m_ref)):
        gate = jax.nn.sigmoid(jnp.dot(hb, wg_ref[:, b * D_MODEL:(b + 1) * D_MODEL], preferred_element_type=F32))
        acc = acc + gate * jnp.dot(br[...].astype(BF16), wb_ref[b], preferred_element_type=F32)
    x1 = x + jnp.dot(acc.astype(BF16), wo_ref[...], preferred_element_type=F32)
    x1_ref[...] = x1
    h2 = _rms(x1, g2_ref[...])
    h2_ref[...] = h2
    logits = _hdot(h2, wr_ref[...]) + br_ref[...]
    lane = lax.broadcasted_iota(jnp.int32, (1, 128), 1).astype(F32)
    big = 1e9
    lg = jnp.where(lane < N_GROUPS, logits, NEG)
    mg = jnp.max(lg, axis=-1, keepdims=True)
    g_w = 1.0 / jnp.sum(jnp.exp(lg - mg), axis=-1, keepdims=True)
    gi = jnp.min(jnp.where(lg == mg, lane, big), axis=-1, keepdims=True)
    sel = (lane >= N_GROUPS) & (lane < N_GROUPS + N_EXPERTS) & (jnp.floor((lane - N_GROUPS) * (1.0 / EPG)) == gi)
    le = jnp.where(sel, logits, NEG)
    m1 = jnp.max(le, axis=-1, keepdims=True)
    i1 = jnp.min(jnp.where(le == m1, lane, big), axis=-1, keepdims=True)
    le2 = jnp.where(lane == i1, NEG, le)
    m2 = jnp.max(le2, axis=-1, keepdims=True)
    i2 = jnp.min(jnp.where(le2 == m2, lane, big), axis=-1, keepdims=True)
    z = jnp.sum(jnp.exp(le - m1), axis=-1, keepdims=True)
    p1 = 1.0 / z
    p2 = jnp.exp(m2 - m1) / z
    w1 = p1 / (p1 + p2) * g_w
    w2 = p2 / (p1 + p2) * g_w
    ei_ref[...] = jnp.where(lane == 0, i1 - N_GROUPS, jnp.where(lane == 1, i2 - N_GROUPS, 0.0)).astype(jnp.int32)
    ew_ref[...] = jnp.where(lane == 0, w1, jnp.where(lane == 1, w2, 0.0))


def _merge(x, branches, p, tm):
    m = x.shape[0]
    consts = [p["norm1_g"], p["w_gate"], p["w_branch"], p["w_out"], p["norm2_g"], p["w_router"], p["b_router"]]
    tile = lambda w: pl.BlockSpec((tm, w), lambda i: (i, 0))
    return pl.pallas_call(
        _merge_kernel,
        grid=(m // tm,),
        in_specs=[tile(D_MODEL)] + [tile(BW)] * 4 + [_full(c.shape) for c in consts],
        out_specs=[tile(D_MODEL), tile(D_MODEL), tile(128), tile(128)],
        out_shape=[jax.ShapeDtypeStruct((m, D_MODEL), F32), jax.ShapeDtypeStruct((m, D_MODEL), F32),
                   jax.ShapeDtypeStruct((m, 128), jnp.int32), jax.ShapeDtypeStruct((m, 128), F32)],
        compiler_params=_cp("parallel"),
        name="merge",
    )(x, *branches, *consts)


def _dispatch_kernel(pos_ref, h_ref, xs_in, xs_out, sem, *, tmd):
    del xs_in

    def issue(i, carry):
        for s in range(2):
            pltpu.make_async_copy(h_ref.at[pl.ds(i, 1)], xs_out.at[pl.ds(pos_ref[2 * i + s], 1)], sem).start()
        return carry

    lax.fori_loop(0, tmd, issue, 0)

    def drain(i, carry):
        for s in range(2):
            pltpu.make_async_copy(h_ref.at[pl.ds(0, 1)], xs_out.at[pl.ds(0, 1)], sem).wait()
        return carry

    lax.fori_loop(0, tmd, drain, 0)


def _dispatch(pos, h2, rows, tmd):
    m = h2.shape[0]
    xs0 = jnp.zeros((rows, D_MODEL), F32)
    return pl.pallas_call(
        functools.partial(_dispatch_kernel, tmd=tmd),
        grid=(m // tmd,),
        in_specs=[pl.BlockSpec((2 * tmd,), lambda i: (i,), memory_space=pltpu.SMEM),
                  pl.BlockSpec((tmd, D_MODEL), lambda i: (i, 0)), pl.BlockSpec(memory_space=pl.ANY)],
        out_specs=pl.BlockSpec(memory_space=pl.ANY),
        out_shape=jax.ShapeDtypeStruct((rows, D_MODEL), F32),
        scratch_shapes=[pltpu.SemaphoreType.DMA(())],
        input_output_aliases={2: 0},
        compiler_params=_cp("arbitrary"),
        name="moe_dispatch",
    )(pos, h2, xs0)


def _expert_kernel(te_ref, nv_ref, x_ref, wg_ref, wu_ref, wd_ref, o_ref):
    del te_ref

    @pl.when(pl.program_id(0) < nv_ref[0])
    def _():
        xb = x_ref[...].astype(BF16)
        gt = jnp.dot(xb, wg_ref[...].astype(BF16), preferred_element_type=F32)
        up = jnp.dot(xb, wu_ref[...].astype(BF16), preferred_element_type=F32)
        o_ref[...] = jnp.dot((_silu(gt) * up).astype(BF16), wd_ref[...].astype(BF16), preferred_element_type=F32)

    @pl.when(pl.program_id(0) >= nv_ref[0])
    def _():
        o_ref[...] = jnp.zeros(o_ref.shape, F32)


def _experts(layer, tile_expert, n_valid, xs, w_gate, w_up, w_down, te):
    rows = xs.shape[0]

    def xmap(i, te_ref, nv_ref):
        return (jnp.minimum(i, nv_ref[0] - 1), 0)

    def wmap(i, te_ref, nv_ref):
        return (layer, te_ref[i], 0, 0)

    grid_spec = pltpu.PrefetchScalarGridSpec(
        num_scalar_prefetch=2,
        grid=(rows // te,),
        in_specs=[pl.BlockSpec((te, D_MODEL), xmap), pl.BlockSpec((None, None, D_MODEL, D_EXPERT), wmap),
                  pl.BlockSpec((None, None, D_MODEL, D_EXPERT), wmap), pl.BlockSpec((None, None, D_EXPERT, D_MODEL), wmap)],
        out_specs=pl.BlockSpec((te, D_MODEL), lambda i, te_ref, nv_ref: (i, 0)),
    )
    return pl.pallas_call(
        _expert_kernel,
        grid_spec=grid_spec,
        out_shape=jax.ShapeDtypeStruct((rows, D_MODEL), F32),
        compiler_params=_cp("arbitrary"),
        name="moe_experts",
    )(tile_expert, n_valid, xs, w_gate, w_up, w_down)


def _combine_kernel(pos_ref, x1_ref, ew_ref, ys_hbm, o_ref, r0, r1, sem, *, tmc):
    bufs = (r0, r1)

    def issue(i, carry):
        for s in range(2):
            pltpu.make_async_copy(ys_hbm.at[pl.ds(pos_ref[2 * i + s], 1)], bufs[s].at[pl.ds(i, 1)], sem).start()
        return carry

    lax.fori_loop(0, tmc, issue, 0)

    def drain(i, carry):
        for s in range(2):
            pltpu.make_async_copy(ys_hbm.at[pl.ds(0, 1)], bufs[s].at[pl.ds(0, 1)], sem).wait()
        return carry

    lax.fori_loop(0, tmc, drain, 0)
    ew = ew_ref[...]
    o_ref[...] = x1_ref[...] + ew[:, 0:1] * r0[...] + ew[:, 1:2] * r1[...]


def _combine(pos, x1, ew, ys, tmc):
    m = x1.shape[0]
    return pl.pallas_call(
        functools.partial(_combine_kernel, tmc=tmc),
        grid=(m // tmc,),
        in_specs=[pl.BlockSpec((2 * tmc,), lambda i: (i,), memory_space=pltpu.SMEM),
                  pl.BlockSpec((tmc, D_MODEL), lambda i: (i, 0)), pl.BlockSpec((tmc, 128), lambda i: (i, 0)),
                  pl.BlockSpec(memory_space=pl.ANY)],
        out_specs=pl.BlockSpec((tmc, D_MODEL), lambda i: (i, 0)),
        out_shape=jax.ShapeDtypeStruct((m, D_MODEL), F32),
        scratch_shapes=[pltpu.VMEM((tmc, D_MODEL), F32), pltpu.VMEM((tmc, D_MODEL), F32), pltpu.SemaphoreType.DMA(())],
        compiler_params=_cp("arbitrary"),
        name="moe_combine",
    )(pos, x1, ew, ys)


def _moe(layer, x1, h2, ei, ew, w_gate, w_up, w_down, te, tmd):
    m = x1.shape[0]
    flat_e = ei[:, 0:2].reshape(-1)
    onehot = (flat_e[:, None] == jnp.arange(N_EXPERTS, dtype=jnp.int32)[None, :]).astype(jnp.int32)
    csum = jnp.cumsum(onehot, axis=0)
    rank = jnp.sum(csum * onehot, axis=1) - 1
    counts = csum[-1]
    padded = ((counts + te - 1) // te) * te
    pend = jnp.cumsum(padded)
    pstart = pend - padded
    pos = (jnp.sum(onehot * pstart[None, :], axis=1) + rank).astype(jnp.int32)
    rows = ((2 * m + N_EXPERTS * (te - 1)) // te) * te
    n_tiles = rows // te
    n_valid = (pend[-1] // te).astype(jnp.int32).reshape(1)
    tile_start = jnp.arange(n_tiles, dtype=jnp.int32) * te
    tile_expert = jnp.minimum(jnp.sum((tile_start[:, None] >= pend[None, :]).astype(jnp.int32), axis=1), N_EXPERTS - 1)
    last_e = jnp.take(tile_expert, jnp.maximum(n_valid[0] - 1, 0))
    tile_expert = jnp.where(jnp.arange(n_tiles) < n_valid[0], tile_expert, last_e).astype(jnp.int32)
    xs = _dispatch(pos, h2, rows, tmd)
    ys = _experts(layer, tile_expert, n_valid, xs, w_gate, w_up, w_down, te)
    return _combine(pos, x1, ew, ys, tmd)


def _tile4(v):
    return jnp.tile(v, HEADS).reshape(1, BW)


def _prep_layer(l, w):
    w_in = w["w_in"][l]
    z = lambda n: jnp.zeros((D_MODEL, n), F32)
    b_al = w_in[:, 1952:1960]
    w_small = jnp.concatenate(
        [w_in[:, 0:512], w_in[:, 512:768], w_in[:, 768:896], z(64), w_in[:, 896:928], z(32), w_in[:, 928:1696],
         w_in[:, 1696:1952], w_in[:, 1960:2216], b_al, z(120)], axis=1).astype(BF16)
    uq = w["mla_w_uq"][l]
    w_uq = jnp.pad(uq, ((0, 0), (0, 0), (0, 32))).reshape(256, 512).astype(BF16)
    qg = jnp.tile(jnp.concatenate([w["mla_qn_g"][l], w["mla_qr_g"][l], jnp.zeros((32,), F32)]), HEADS).reshape(1, 512)
    uk = w["mla_w_uk"][l]
    w_uk_p = jnp.pad(uk, ((0, 0), (0, 0), (0, 64))).reshape(128, 512).astype(BF16)
    kg = jnp.tile(jnp.concatenate([w["mla_kn_g"][l], jnp.zeros((64,), F32)]), HEADS).reshape(1, 512)
    kr_g = jnp.concatenate([jnp.zeros((64,), F32), w["mla_kr_g"][l], jnp.zeros((32,), F32)]).reshape(1, 128)
    uv = w["mla_w_uv"][l]
    w_uv_p = jnp.stack([jnp.pad(uv[:, h, :], ((0, 0), (64 * h, BW - 64 * h - 64))) for h in range(HEADS)]).astype(BF16)
    par_r = jnp.zeros((8, 128), F32).at[0, 4:8].set(w["gdn_a_log"][l]).at[1, 4:8].set(w["gdn_dt_bias"][l])
    par_c = jnp.zeros((8, 128), F32).at[4:8, 0].set(w["gdn_a_log"][l]).at[4:8, 1].set(w["gdn_dt_bias"][l])
    w_router = jnp.concatenate([w["moe_wg"][l], w["moe_we"][l], jnp.zeros((D_MODEL, 128 - 36), F32)], axis=1)
    b_router = jnp.concatenate([w["moe_bg"][l], w["moe_be"][l], jnp.zeros((128 - 36,), F32)]).reshape(1, 128)
    return {
        "norm1_g": w["norm1_g"][l].reshape(1, D_MODEL), "w_small": w_small, "w_bat": b_al.T.astype(BF16),
        "w_gate": w_in[:, 2216:].astype(BF16),
        "gm_norm_g": w["gm_norm_g"][l].reshape(1, BW), "gm_ws": w["gm_ws"][l],
        "gm_bfull": jnp.repeat(w["gm_b"][l].T, HD, axis=1),
        "gm_w0": jnp.repeat(w["gm_ws"][l][:, 0, 0], HD).reshape(1, BW), "gm_b0": jnp.repeat(w["gm_b"][l][:, 0], HD).reshape(1, BW),
        "mla_cq_g": w["mla_cq_g"][l].reshape(1, 256), "w_uq": w_uq, "qg": qg, "mla_ckv_g": w["mla_ckv_g"][l].reshape(1, 128),
        "kr_g": kr_g, "w_uk": w_uk_p, "kg": kg, "w_uv": w_uv_p,
        "w_uk_c": uk.reshape(128, BW).astype(BF16), "w_uv_c": uv.reshape(128, BW).astype(BF16), "kn_gx": _tile4(w["mla_kn_g"][l]),
        "conv_w": w["gdn_conv_w"][l], "gdn_par_r": par_r, "gdn_par_c": par_c, "gdn_out_gx": _tile4(w["gdn_out_g"][l]),
        "gdn_out_gc": w["gdn_out_g"][l].reshape(HD, 1),
        "mem_norm_g": w["mem_norm_g"][l].reshape(1, D_MODEL), "mem_w_kv": w["mem_w_kv"][l].astype(BF16),
        "mem_qn_gx": _tile4(w["mem_qn_g"][l]), "mem_kn_gx": _tile4(w["mem_kn_g"][l]),
        "w_branch": w["w_branch"][l].astype(BF16), "w_out": w["w_out"][l].astype(BF16),
        "norm2_g": w["norm2_g"][l].reshape(1, D_MODEL), "w_router": w_router, "b_router": b_router,
    }


def _rope_tables(pos):
    half = MLA_ROPE // 2
    inv = ROPE_THETA ** (-jnp.arange(half, dtype=F32) / half)
    ang = pos.astype(F32)[:, None] * inv[None, :]
    cos, sin = jnp.cos(ang), jnp.sin(ang)
    t = pos.shape[0]
    one, zero = jnp.ones((t, 64), F32), jnp.zeros((t, 64), F32)
    z16, z32 = jnp.zeros((t, 16), F32), jnp.zeros((t, 32), F32)
    return (jnp.concatenate([one, cos, cos, jnp.ones((t, 32), F32)], axis=1),
            jnp.concatenate([zero, -sin, z16, z32], axis=1),
            jnp.concatenate([zero, z16, sin, z32], axis=1))


def kernel(x_prompt, mem_prompt, x_sample, cache_mla_ckv, cache_mla_kr, cache_mem_k, cache_mem_v, state_gdn, state_conv,
           page_table, norm1_g, w_in, gm_norm_g, gm_ws, gm_b, mla_cq_g, mla_w_uq, mla_qn_g, mla_qr_g, mla_ckv_g, mla_kr_g,
           mla_w_uk, mla_kn_g, mla_w_uv, gdn_conv_w, gdn_a_log, gdn_dt_bias, gdn_out_g, mem_norm_g, mem_w_kv, mem_qn_g,
           mem_kn_g, w_branch, w_out, norm2_g, moe_wg, moe_bg, moe_we, moe_be, moe_w_gate, moe_w_up, moe_w_down):
    w = dict(norm1_g=norm1_g, w_in=w_in, gm_norm_g=gm_norm_g, gm_ws=gm_ws, gm_b=gm_b, mla_cq_g=mla_cq_g, mla_w_uq=mla_w_uq,
             mla_qn_g=mla_qn_g, mla_qr_g=mla_qr_g, mla_ckv_g=mla_ckv_g, mla_kr_g=mla_kr_g, mla_w_uk=mla_w_uk, mla_kn_g=mla_kn_g,
             mla_w_uv=mla_w_uv, gdn_conv_w=gdn_conv_w, gdn_a_log=gdn_a_log, gdn_dt_bias=gdn_dt_bias, gdn_out_g=gdn_out_g,
             mem_norm_g=mem_norm_g, mem_w_kv=mem_w_kv, mem_qn_g=mem_qn_g, mem_kn_g=mem_kn_g, w_branch=w_branch, w_out=w_out,
             norm2_g=norm2_g, moe_wg=moe_wg, moe_bg=moe_bg, moe_we=moe_we, moe_be=moe_be)
    depth = w_in.shape[0]
    bp, tp, _ = x_prompt.shape
    bs = x_sample.shape[0]
    mt = mem_prompt.shape[1]
    n_pages = page_table.shape[1]
    past_len = n_pages * cache_mla_ckv.shape[2]
    mp = bp * tp

    tm_p = min(512, mp)
    tq = min(256, tp)
    tg = min(256, tp)
    pp = min(16, n_pages)
    cache_krt = jnp.swapaxes(cache_mla_kr, 2, 3)
    tabs_p = _rope_tables(jnp.arange(tp, dtype=jnp.int32))
    tabs_s = _rope_tables(jnp.full((bs,), past_len, jnp.int32))

    xp = x_prompt.reshape(mp, D_MODEL)
    xs = x_sample.reshape(bs, D_MODEL)
    mem = mem_prompt.reshape(bp * mt, D_MODEL)
    cache_k = cache_mem_k.reshape(depth, bs, mt, BW)
    cache_v = cache_mem_v.reshape(depth, bs, mt, BW)
    rows_p, rows_s = [], []
    for l in range(depth):
        p = _prep_layer(l, w)
        mk, mv = _mem_kv(mem, p["mem_norm_g"], p["mem_w_kv"], p["mem_kn_gx"], min(512, bp * mt))
        za, zb, zc, zm, zba, bat = _in_proj(xp, p["norm1_g"], p["w_small"], p["w_bat"], tm_p)
        a_out = _gmlp(za, mp, p["gm_norm_g"], p["gm_ws"], p["gm_bfull"], tm_p)
        q4, k4, ckv, kr, ckv16, _ = _mla_pre(zb, mp, tabs_p, tp // min(tm_p, tp), p, min(tm_p, tp))
        b_out = _mla_attn(q4, k4, ckv16, p["w_uv"], bp, tp, tq)
        c_out, sfin = _gdn_prompt(zc, zba, bat, bp, tp, p, tg)
        m_out = _mem_attn(zm, mk, mv, p["mem_qn_gx"], bp, tp, mt, tq)
        x1, h2, ei, ew = _merge(xp, (a_out, b_out, c_out, m_out), p, min(256, mp))
        xp = _moe(l, x1, h2, ei, ew, moe_w_gate, moe_w_up, moe_w_down, min(256, mp), min(512, mp))
        s_p = jnp.stack([sfin[:, 64 * h:64 * h + 64, 64 * h:64 * h + 64] for h in range(HEADS)], axis=1).transpose(0, 1, 3, 2)
        conv_p = zc.reshape(bp, tp, 1024)[:, tp - 3:, 0:QKV_DIM]
        rows_p.append((ckv.reshape(bp, tp, 128), kr.reshape(bp, tp, MLA_ROPE), s_p, conv_p,
                       mk.reshape(bp, mt, HEADS, HD), mv.reshape(bp, mt, HEADS, HD)))
        za, zb, zc, zm, zba, _ = _in_proj(xs, p["norm1_g"], p["w_small"], p["w_bat"], bs)
        a_s, v_s, conv_s, gq, gk, gv, beta, gdec, mqn = _sample_tok(za, zc, zba, zm, state_conv[l].reshape(bs, 3 * QKV_DIM), p)
        _, _, ckv_s, kr_s, _, q32 = _mla_pre(zb, bs, tabs_s, 1, p, bs)
        q3 = q32.reshape(bs, HEADS, 128)
        qk = q3[:, :, 0:64].reshape(bs, 1, BW)
        qr8 = jnp.pad(q3[:, :, 64:96], ((0, 0), (0, 4), (0, 0)))
        b_s = _mla_decode(l, page_table, cache_mla_ckv, cache_krt, qk, qr8, ckv_s.reshape(bs, 1, 128),
                          kr_s.reshape(bs, 1, MLA_ROPE), p["kn_gx"], p["w_uk_c"], p["w_uv_c"], pp).reshape(bs, BW)
        nb = bs * HEADS
        s_new, c_col = _gdn_step(state_gdn[l].reshape(nb, HD, HD), gq.reshape(nb, 1, HD), gk.reshape(nb, 1, HD),
                                 gv.reshape(nb, HD, 1), beta[:, 0:4].reshape(nb, 1, 1), gdec[:, 4:8].reshape(nb, 1, 1),
                                 zc[:, QKV_DIM:QKV_DIM + BW].reshape(nb, HD, 1), p["gdn_out_gc"], min(64, nb))
        m_s = _mem_attn_s(l, mqn, cache_k, cache_v, min(8, bs))
        x1, h2, ei, ew = _merge(xs, (a_s, b_s, c_col.reshape(bs, BW), m_s), p, bs)
        xs = _moe(l, x1, h2, ei, ew, moe_w_gate, moe_w_up, moe_w_down, min(32, bs), bs)
        rows_s.append((ckv_s.reshape(bs, 1, 128), kr_s.reshape(bs, 1, MLA_ROPE), s_new.reshape(bs, HEADS, HD, HD),
                       conv_s.reshape(bs, 3, QKV_DIM), v_s.reshape(bs, 1, BW)))
    p_out = [jnp.stack(a) for a in zip(*rows_p)]
    s_out = [jnp.stack(a) for a in zip(*rows_s)]
    return (xp.reshape(bp, tp, D_MODEL), xs.reshape(bs, 1, D_MODEL), *p_out, *s_out)
```

```python
import functools

import numpy as np
import jax
import jax.numpy as jnp
from jax import lax
from jax.experimental import pallas as pl
from jax.experimental.pallas import tpu as pltpu

F32 = jnp.float32
BF16 = jnp.bfloat16
HI = lax.Precision.HIGHEST
EPS = 1e-6
NEG = float("-inf")

D_MODEL = 1024
HEADS = 4
HD = 64
BW = 256
MLA_ROPE = 32
MLA_KV_RANK = 128
MLA_SCALE = 96.0 ** -0.5
ROPE_THETA = 10000.0
GM_CHUNK = 128
GDN_CHUNK = 64
QKV_DIM = 768
N_GROUPS = 4
EPG = 8
N_EXPERTS = 32
D_EXPERT = 256
PAGE = 128
VMEM_LIMIT = 56 * 1024 * 1024

NT = (((1,), (1,)), ((), ()))
TN = (((0,), (0,)), ((), ()))


def _cp(*sem):
    return pltpu.CompilerParams(dimension_semantics=sem, vmem_limit_bytes=VMEM_LIMIT)


def _rms(x, g):
    ms = jnp.sum(x * x, axis=-1, keepdims=True) * (1.0 / x.shape[-1])
    return x * lax.rsqrt(ms + EPS) * g


def _bdot(a, b):
    return jnp.dot(a.astype(BF16), b.astype(BF16), preferred_element_type=F32)


def _bdot_nt(a, b):
    return lax.dot_general(a.astype(BF16), b.astype(BF16), NT, preferred_element_type=F32)


def _hdot(a, b):
    return jnp.dot(a, b, precision=HI, preferred_element_type=F32)


def _split2(a):
    hi = a.astype(BF16)
    return hi, (a - hi.astype(F32)).astype(BF16)


def _dot3(a, b):
    d = lambda x, y: jnp.dot(x, y, preferred_element_type=F32)
    return d(a[0], b[0]) + (d(a[0], b[1]) + d(a[1], b[0]))


def _sel_dot(w01, x, left):
    x0 = x.astype(BF16)
    r1 = x - x0.astype(F32)
    x1 = r1.astype(BF16)
    x2 = (r1 - x1.astype(F32)).astype(BF16)
    d = (lambda p: jnp.dot(w01, p, preferred_element_type=F32)) if left else (lambda p: jnp.dot(p, w01, preferred_element_type=F32))
    return d(x0) + (d(x1) + d(x2))


def _silu(x):
    return x * jax.nn.sigmoid(x)


def _gelu(x):
    return 0.5 * x * (1.0 + lax.erf(x * 0.7071067811865476))


def _lane_head(width=BW):
    return lax.broadcasted_iota(jnp.int32, (1, width), 1) // HD


def _full(shape):
    n = len(shape)
    return pl.BlockSpec(shape, lambda *_: (0,) * n)


def _block_mean(width, segs):
    m = np.zeros((width, width), np.float32)
    for a, b in segs:
        m[a:b, a:b] = 1.0 / (b - a)
    return m


_BMEAN64 = _block_mean(BW, [(64 * h, 64 * h + 64) for h in range(HEADS)])
_BONES64 = _BMEAN64 * 64.0
_BQ = _block_mean(512, [(128 * h, 128 * h + 64) for h in range(HEADS)] + [(128 * h + 64, 128 * h + 96) for h in range(HEADS)])
_BK = _block_mean(512, [(128 * h, 128 * h + 64) for h in range(HEADS)])
_IND8 = np.zeros((8, BW), np.float32)
for _h in range(HEADS):
    _IND8[_h, 64 * _h:64 * _h + 64] = 1.0
_EXPB = np.zeros((128, BW), np.float32)
_EXPG = np.zeros((128, BW), np.float32)
for _h in range(HEADS):
    _EXPB[_h, 64 * _h:64 * _h + 64] = 1.0
    _EXPG[4 + _h, 64 * _h:64 * _h + 64] = 1.0
_BDMASK = (_BONES64 > 0).astype(np.float32)


def _in_proj_kernel(x_ref, g_ref, w_ref, wbat_ref, oa_ref, ob_ref, oc_ref, om_ref, oba_ref, obat_ref):
    hb = _rms(x_ref[...], g_ref[...]).astype(BF16)
    oa_ref[...] = jnp.dot(hb, w_ref[:, 0:512], preferred_element_type=F32)
    ob_ref[...] = jnp.dot(hb, w_ref[:, 512:1024], preferred_element_type=F32)
    oc_ref[...] = jnp.dot(hb, w_ref[:, 1024:2048], preferred_element_type=F32)
    om_ref[...] = jnp.dot(hb, w_ref[:, 2048:2304], preferred_element_type=F32)
    oba_ref[...] = jnp.dot(hb, w_ref[:, 2304:2432], preferred_element_type=F32)
    obat_ref[...] = lax.dot_general(wbat_ref[...], hb, NT, preferred_element_type=F32)


def _in_proj(x, g, w, wbat, tm):
    m = x.shape[0]
    widths = (512, 512, 1024, 256, 128)
    return pl.pallas_call(
        _in_proj_kernel,
        grid=(m // tm,),
        in_specs=[pl.BlockSpec((tm, D_MODEL), lambda i: (i, 0)), _full((1, D_MODEL)), _full(w.shape), _full(wbat.shape)],
        out_specs=[pl.BlockSpec((tm, n), lambda i: (i, 0)) for n in widths] + [pl.BlockSpec((8, tm), lambda i: (0, i))],
        out_shape=[jax.ShapeDtypeStruct((m, n), F32) for n in widths] + [jax.ShapeDtypeStruct((8, m), F32)],
        compiler_params=_cp("parallel"),
        name="in_proj",
    )(x, g, w, wbat)


def _gmlp_kernel(z_ref, g_ref, ws_ref, b_ref, o_ref, *, ta):
    row = lax.broadcasted_iota(jnp.int32, (GM_CHUNK, GM_CHUNK), 0)
    col = lax.broadcasted_iota(jnp.int32, (GM_CHUNK, GM_CHUNK), 1)
    tril = col <= row
    lh = _lane_head()
    wts = [jnp.where(tril, ws_ref[g], 0.0).astype(BF16) for g in range(HEADS)]
    for c in range(ta // GM_CHUNK):
        sl = slice(c * GM_CHUNK, (c + 1) * GM_CHUNK)
        ge = _gelu(z_ref[sl, :])
        u = ge[:, :BW]
        vb = _rms(ge[:, BW:], g_ref[...]).astype(BF16)
        s = b_ref[...]
        for g in range(HEADS):
            s = s + jnp.where(lh == g, jnp.dot(wts[g], vb, preferred_element_type=F32), 0.0)
        o_ref[sl, :] = u * s


def _gmlp(za, m, g, ws, bfull, ta):
    return pl.pallas_call(
        functools.partial(_gmlp_kernel, ta=ta),
        grid=(m // ta,),
        in_specs=[pl.BlockSpec((ta, 512), lambda i: (i, 0)), _full((1, BW)), _full(ws.shape), _full(bfull.shape)],
        out_specs=pl.BlockSpec((ta, BW), lambda i: (i, 0)),
        out_shape=jax.ShapeDtypeStruct((m, BW), F32),
        compiler_params=_cp("parallel"),
        name="gmlp",
    )(za, g, ws, bfull)


def _mla_pre_kernel(z_ref, c_ref, s1_ref, s2_ref, gcq_ref, wuq_ref, qg_ref, bq_ref, gckv_ref, gkr_ref, wuk_ref, kg_ref,
                    bk_ref, oq_ref, ok_ref, ockv_ref, okr_ref, ockv16_ref, oq32_ref):
    cs, s1, s2 = c_ref[...], s1_ref[...], s2_ref[...]

    def rope(x):
        return x * cs + pltpu.roll(x, 112, 1) * s1 + pltpu.roll(x, 16, 1) * s2

    z = z_ref[...]
    cq = _rms(z[:, 0:256], gcq_ref[...])
    q = _bdot(cq, wuq_ref[...])
    qn = q * lax.rsqrt(_bdot(q * q, bq_ref[...]) + EPS) * qg_ref[...]
    ckv = _rms(z[:, 256:384], gckv_ref[...])
    ockv_ref[...] = ckv
    ockv16_ref[...] = ckv.astype(BF16)
    krb = z[:, 384:512]
    kr = rope(krb * lax.rsqrt(jnp.sum(krb * krb, axis=-1, keepdims=True) * (1.0 / MLA_ROPE) + EPS) * gkr_ref[...])
    okr_ref[...] = kr[:, 64:96]
    k = _bdot(ckv, wuk_ref[...])
    kn = k * lax.rsqrt(_bdot(k * k, bk_ref[...]) + EPS) * kg_ref[...]
    for h in range(HEADS):
        sl = slice(128 * h, 128 * h + 128)
        qh = rope(qn[:, sl])
        oq_ref[h] = qh.astype(BF16)
        oq32_ref[:, sl] = qh
        ok_ref[h] = (kn[:, sl] + kr).astype(BF16)


def _mla_pre(zb, m, tabs, t_blocks, p, tm):
    cs, s1, s2 = tabs
    tab_spec = pl.BlockSpec((tm, 128), lambda i: (i % t_blocks, 0))
    consts = [p["mla_cq_g"], p["w_uq"], p["qg"], jnp.asarray(_BQ, BF16), p["mla_ckv_g"], p["kr_g"], p["w_uk"], p["kg"],
              jnp.asarray(_BK, BF16)]
    return pl.pallas_call(
        _mla_pre_kernel,
        grid=(m // tm,),
        in_specs=[pl.BlockSpec((tm, 512), lambda i: (i, 0)), tab_spec, tab_spec, tab_spec] + [_full(c.shape) for c in consts],
        out_specs=[pl.BlockSpec((HEADS, tm, 128), lambda i: (0, i, 0)), pl.BlockSpec((HEADS, tm, 128), lambda i: (0, i, 0)),
                   pl.BlockSpec((tm, 128), lambda i: (i, 0)), pl.BlockSpec((tm, MLA_ROPE), lambda i: (i, 0)),
                   pl.BlockSpec((tm, 128), lambda i: (i, 0)), pl.BlockSpec((tm, 512), lambda i: (i, 0))],
        out_shape=[jax.ShapeDtypeStruct((HEADS, m, 128), BF16), jax.ShapeDtypeStruct((HEADS, m, 128), BF16),
                   jax.ShapeDtypeStruct((m, 128), F32), jax.ShapeDtypeStruct((m, MLA_ROPE), F32),
                   jax.ShapeDtypeStruct((m, 128), BF16), jax.ShapeDtypeStruct((m, 512), F32)],
        compiler_params=_cp("parallel"),
        name="mla_pre",
    )(zb, cs, s1, s2, *consts)


def _mla_attn_kernel(q_ref, k_ref, v_ref, wuv_ref, o_ref, *, tq):
    i = pl.program_id(1)
    row = lax.broadcasted_iota(jnp.int32, (tq, tq), 0)
    col = lax.broadcasted_iota(jnp.int32, (tq, tq), 1)
    causal = col <= row

    def step(off, carry, mask):
        m, l, acc = carry
        vj = v_ref[pl.ds(off, tq), :]
        ss = []
        for h in range(HEADS):
            s = lax.dot_general(q_ref[h], k_ref[h, pl.ds(off, tq), :], NT, preferred_element_type=F32) * MLA_SCALE
            ss.append(jnp.where(causal, s, NEG) if mask else s)
        s = jnp.concatenate(ss, axis=0)
        mn = jnp.maximum(m, jnp.max(s, axis=-1, keepdims=True))
        pr = jnp.exp(s - mn)
        al = jnp.exp(m - mn)
        return mn, al * l + jnp.sum(pr, axis=-1, keepdims=True), al * acc + jnp.dot(pr.astype(BF16), vj, preferred_element_type=F32)

    init = (jnp.full((HEADS * tq, 1), NEG, F32), jnp.zeros((HEADS * tq, 1), F32), jnp.zeros((HEADS * tq, 128), F32))
    carry = lax.fori_loop(0, i, lambda j, c: step(pl.multiple_of(j * tq, tq), c, False), init)
    m, l, acc = step(pl.multiple_of(i * tq, tq), carry, True)
    lat = (acc / l).astype(BF16)
    out = jnp.zeros((tq, BW), F32)
    for h in range(HEADS):
        out = out + jnp.dot(lat[h * tq:(h + 1) * tq, :], wuv_ref[h], preferred_element_type=F32)
    o_ref[...] = out


def _mla_attn(q4, k4, ckv16, wuv, n, t, tq):
    nq = t // tq
    return pl.pallas_call(
        functools.partial(_mla_attn_kernel, tq=tq),
        grid=(n, nq),
        in_specs=[pl.BlockSpec((HEADS, tq, 128), lambda b, i: (0, b * nq + i, 0)),
                  pl.BlockSpec((HEADS, t, 128), lambda b, i: (0, b, 0)),
                  pl.BlockSpec((t, 128), lambda b, i: (b, 0)), _full(wuv.shape)],
        out_specs=pl.BlockSpec((tq, BW), lambda b, i: (b * nq + i, 0)),
        out_shape=jax.ShapeDtypeStruct((n * t, BW), F32),
        compiler_params=_cp("parallel", "arbitrary"),
        name="mla_attn",
    )(q4, k4, ckv16, wuv)


def _mla_decode_kernel(pt_ref, qk_ref, qr_ref, cnew_ref, krnew_ref, kng_ref, wuk_ref, wuv_ref, ind_ref, ckv_hbm, krt_hbm, o_ref,
                       cbuf, kbuf, sem, *, layer, npages, pp):
    n = pl.program_id(0)
    ngroups = npages // pp

    def page_copies(sample, grp, slot):
        base = sample * npages + grp * pp
        out = []
        for i in range(pp):
            page = pt_ref[base + i]
            out.append(pltpu.make_async_copy(ckv_hbm.at[layer, page], cbuf.at[slot, pl.ds(i * PAGE, PAGE)], sem.at[slot]))
            out.append(pltpu.make_async_copy(krt_hbm.at[layer, page], kbuf.at[slot, :, pl.ds(i * PAGE, PAGE)], sem.at[slot]))
        return out

    @pl.when(n == 0)
    def _():
        for cp in page_copies(0, 0, 0):
            cp.start()

    ind = ind_ref[...]
    qbd = (ind * (qk_ref[...] * kng_ref[...])).astype(BF16)
    ind64 = (ind * (1.0 / HD)).astype(BF16)
    qr = qr_ref[...].astype(BF16)
    wuk = wuk_ref[...]
    qabs = lax.dot_general(qbd, wuk, NT, preferred_element_type=F32).astype(BF16)

    m = jnp.full((8, 1), NEG, F32)
    l = jnp.zeros((8, 1), F32)
    acc = jnp.zeros((8, MLA_KV_RANK), F32)
    for grp in range(ngroups):
        slot = (n * ngroups + grp) % 2
        if grp + 1 < ngroups:
            for cp in page_copies(n, grp + 1, 1 - slot):
                cp.start()
        else:
            @pl.when(n + 1 < pl.num_programs(0))
            def _():
                for cp in page_copies(n + 1, 0, 1 - slot):
                    cp.start()
        for cp in page_copies(n, grp, slot):
            cp.wait()
        cb = cbuf[slot].astype(BF16)
        krt = kbuf[slot].astype(BF16)
        k = jnp.dot(cb, wuk, preferred_element_type=F32)
        num = lax.dot_general(qabs, cb, NT, preferred_element_type=F32)
        ms = lax.dot_general(ind64, (k * k).astype(BF16), NT, preferred_element_type=F32)
        s = (num * lax.rsqrt(ms + EPS) + jnp.dot(qr, krt, preferred_element_type=F32)) * MLA_SCALE
        mn = jnp.maximum(m, jnp.max(s, axis=-1, keepdims=True))
        pb = jnp.exp(s - mn)
        al = jnp.exp(m - mn)
        l = al * l + jnp.sum(pb, axis=-1, keepdims=True)
        acc = al * acc + jnp.dot(pb.astype(BF16), cb, preferred_element_type=F32)
        m = mn

    cb = jnp.broadcast_to(cnew_ref[...], (8, MLA_KV_RANK)).astype(BF16)
    k1 = jnp.dot(cb, wuk, preferred_element_type=F32)
    kk1 = (k1 * k1).astype(BF16).astype(F32)
    krn = krnew_ref[...].astype(BF16).astype(F32)
    num1 = jnp.sum(qabs.astype(F32) * cb.astype(F32), axis=-1, keepdims=True)
    ms1 = jnp.sum(ind64.astype(F32) * kk1, axis=-1, keepdims=True)
    s1 = (num1 * lax.rsqrt(ms1 + EPS) + jnp.sum(qr.astype(F32) * krn, axis=-1, keepdims=True)) * MLA_SCALE
    mn1 = jnp.maximum(m, s1)
    p1 = jnp.exp(s1 - mn1)
    al1 = jnp.exp(m - mn1)
    lat = (al1 * acc + p1 * cb.astype(F32)) / (al1 * l + p1)
    o8 = jnp.dot(lat.astype(BF16), wuv_ref[...], preferred_element_type=F32)
    o_ref[...] = jnp.sum(o8 * ind, axis=0, keepdims=True)


def _mla_decode(layer, page_table, cache_ckv, cache_krt, qk, qr8, cnew, krnew, kng, wuk, wuv, pp):
    ns, npages = page_table.shape
    pt = page_table.reshape(-1)

    def per_sample(shape):
        return pl.BlockSpec((None,) + shape, lambda n, pt_ref: (n, 0, 0))

    def const(a):
        nd = a.ndim
        return pl.BlockSpec(a.shape, lambda n, pt_ref: (0,) * nd)

    ind = jnp.asarray(_IND8)
    hbm = pl.BlockSpec(memory_space=pl.ANY)
    grid_spec = pltpu.PrefetchScalarGridSpec(
        num_scalar_prefetch=1,
        grid=(ns,),
        in_specs=[per_sample((1, BW)), per_sample((8, MLA_ROPE)), per_sample((1, MLA_KV_RANK)), per_sample((1, MLA_ROPE)),
                  const(kng), const(wuk), const(wuv), const(ind), hbm, hbm],
        out_specs=per_sample((1, BW)),
        scratch_shapes=[pltpu.VMEM((2, pp * PAGE, MLA_KV_RANK), F32), pltpu.VMEM((2, MLA_ROPE, pp * PAGE), F32),
                        pltpu.SemaphoreType.DMA((2,))],
    )
    return pl.pallas_call(
        functools.partial(_mla_decode_kernel, layer=layer, npages=npages, pp=pp),
        grid_spec=grid_spec,
        out_shape=jax.ShapeDtypeStruct((ns, 1, BW), F32),
        compiler_params=_cp("arbitrary"),
        name="mla_decode",
    )(pt, qk, qr8, cnew, krnew, kng, wuk, wuv, ind, cache_ckv, cache_krt)


def _gdn_kernel(zc_ref, zba_ref, bat_ref, cw_ref, parr_ref, parc_ref, gout_ref, bones_ref, expb_ref, expg_ref, bdm_ref,
                o_ref, sfin_ref, xbuf, s_sc, *, tg):
    t = pl.program_id(1)
    c = GDN_CHUNK

    @pl.when(t == 0)
    def _():
        xbuf[0:8, :] = jnp.zeros((8, QKV_DIM), F32)
        s_sc[...] = jnp.zeros(s_sc.shape, F32)

    @pl.when(t > 0)
    def _():
        xbuf[5:8, :] = xbuf[tg + 5:tg + 8, :]

    xbuf[8:8 + tg, :] = zc_ref[:, 0:QKV_DIM]
    y = cw_ref[0:1, :] * xbuf[5:5 + tg, :]
    for i in range(1, 4):
        y = y + cw_ref[i:i + 1, :] * xbuf[5 + i:5 + i + tg, :]
    y = _silu(y)

    zba = zba_ref[...]
    beta_col = jax.nn.sigmoid(zba)
    g_col = -jnp.exp(parr_ref[0:1, :]) * jax.nn.softplus(zba + parr_ref[1:2, :])
    g_row = -jnp.exp(parc_ref[:, 0:1]) * jax.nn.softplus(bat_ref[...] + parc_ref[:, 1:2])

    row = lax.broadcasted_iota(jnp.int32, (c, c), 0)
    col = lax.broadcasted_iota(jnp.int32, (c, c), 1)
    incl = col <= row
    strict = col < row
    lt = incl.astype(BF16)
    ut = (col >= row).astype(BF16)
    bones = bones_ref[...]
    bdm = bdm_ref[...]
    expb, expg = expb_ref[...], expg_ref[...]
    brow = lax.broadcasted_iota(jnp.int32, (BW, BW), 0)
    bcl = lax.broadcasted_iota(jnp.int32, (BW, BW), 1)
    same_head = (brow // HD) == (bcl // HD)
    incl_bd = same_head & ((bcl % HD) <= (brow % HD))
    strict_bd = same_head & ((bcl % HD) < (brow % HD))
    eye_bd = (brow == bcl).astype(F32)
    same_head2 = jnp.concatenate([same_head, same_head], axis=1)

    def stack4(a):
        return jnp.concatenate([a, a, a, a], axis=0)

    def fold4(a):
        return (a[0:c] + a[c:2 * c]) + (a[2 * c:3 * c] + a[3 * c:4 * c])

    nchunk = tg // c
    pre = []
    for ci in range(nchunk):
        sl = slice(ci * c, (ci + 1) * c)
        q, k, v = y[sl, 0:256], y[sl, 256:512], y[sl, 512:768]
        qn = q * lax.rsqrt(_bdot(q * q, bones) + EPS) * (HD ** -0.5)
        kn = k * lax.rsqrt(_bdot(k * k, bones) + EPS)
        bcol = beta_col[sl, :]
        gcum_c = _sel_dot(lt, g_col[sl, :], True)
        gcum_r = _sel_dot(ut, g_row[:, sl], False)
        gx = _sel_dot(expg, gcum_c, False)
        bx = _sel_dot(expb, bcol, False)
        egx = jnp.exp(gx)
        rhs = _split2(jnp.concatenate([bx * v, bx * egx * kn], axis=1))
        ks = jnp.where(same_head, stack4(kn), 0.0).astype(BF16)
        qs = jnp.where(same_head, stack4(qn), 0.0).astype(BF16)
        gc_s = jnp.concatenate([gcum_c[:, 4 + h:5 + h] for h in range(HEADS)], axis=0)
        gr_s = jnp.concatenate([gcum_r[4 + h:5 + h, :] for h in range(HEADS)], axis=1)
        beta_s = jnp.concatenate([bcol[:, h:h + 1] for h in range(HEADS)], axis=0)
        dm = jnp.exp(jnp.where(incl_bd, gc_s - gr_s, NEG))
        a = jnp.where(strict_bd, beta_s * lax.dot_general(ks, ks, NT, preferred_element_type=F32) * dm, 0.0)
        qk = (lax.dot_general(qs, ks, NT, preferred_element_type=F32) * dm).astype(BF16)
        pre.append(dict(sl=sl, qn=qn, kn=kn, gx=gx, egx=egx, rhs=rhs, qk=qk, ps=_split2(-a), tinv=eye_bd - a))

    for _ in range(5):
        for d in pre:
            d["ps"] = _split2(_dot3(d["ps"], d["ps"]))
        for d in pre:
            d["tinv"] = d["tinv"] + _dot3(_split2(d["tinv"]), d["ps"])
    for d in pre:
        rhs = d["rhs"]
        d["x"] = fold4(jnp.where(same_head2, _dot3(_split2(d["tinv"]), (stack4(rhs[0]), stack4(rhs[1]))), 0.0))

    for d in pre:
        sl, qn, kn, gx, egx, qk, x = d["sl"], d["qn"], d["kn"], d["gx"], d["egx"], d["qk"], d["x"]
        s = s_sc[...]
        sb = s.astype(BF16)
        u = x[:, :BW] - jnp.dot(x[:, BW:].astype(BF16), sb, preferred_element_type=F32)
        ub = u.astype(BF16)
        o = egx * jnp.dot(qn.astype(BF16), sb, preferred_element_type=F32)
        o = o + fold4(jnp.where(same_head, jnp.dot(qk, stack4(ub), preferred_element_type=F32), 0.0))
        glast = gx[c - 1:c, :]
        kf = (kn * jnp.exp(glast - gx)).astype(BF16)
        s_new = jnp.exp(glast) * s + lax.dot_general(kf, ub, TN, preferred_element_type=F32)
        s_sc[...] = s_new * bdm
        on = o * lax.rsqrt(_bdot(o * o, bones) * (1.0 / HD) + EPS) * gout_ref[...]
        o_ref[sl, :] = on * _silu(zc_ref[sl, QKV_DIM:QKV_DIM + BW])

    @pl.when(t == pl.num_programs(1) - 1)
    def _():
        sfin_ref[...] = s_sc[...]


def _gdn_prompt(zc, zba, bat, n, t, p, tg):
    nt = t // tg
    consts = [p["conv_w"], p["gdn_par_r"], p["gdn_par_c"], p["gdn_out_gx"], jnp.asarray(_BONES64, BF16), jnp.asarray(_EXPB, BF16),
              jnp.asarray(_EXPG, BF16), jnp.asarray(_BDMASK)]
    return pl.pallas_call(
        functools.partial(_gdn_kernel, tg=tg),
        grid=(n, nt),
        in_specs=[pl.BlockSpec((tg, 1024), lambda b, i: (b * nt + i, 0)), pl.BlockSpec((tg, 128), lambda b, i: (b * nt + i, 0)),
                  pl.BlockSpec((8, tg), lambda b, i: (0, b * nt + i))] + [_full(c.shape) for c in consts],
        out_specs=[pl.BlockSpec((tg, BW), lambda b, i: (b * nt + i, 0)), pl.BlockSpec((None, BW, BW), lambda b, i: (b, 0, 0))],
        out_shape=[jax.ShapeDtypeStruct((n * t, BW), F32), jax.ShapeDtypeStruct((n, BW, BW), F32)],
        scratch_shapes=[pltpu.VMEM((8 + tg, QKV_DIM), F32), pltpu.VMEM((BW, BW), F32)],
        compiler_params=_cp("parallel", "arbitrary"),
        name="gdn_prompt",
    )(zc, zba, bat, *consts)


def _sample_tok_kernel(za_ref, zc_ref, zba_ref, zm_ref, sconv_ref, cw_ref, gmg_ref, gmw_ref, gmb_ref, parr_ref, bones_ref,
                       memg_ref, bmean_ref, oa_ref, ov_ref, oconv_ref, oq_ref, ok_ref, ovv_ref, obeta_ref, og_ref, omq_ref):
    ge = _gelu(za_ref[...])
    v = _rms(ge[:, BW:], gmg_ref[...])
    ov_ref[...] = v
    oa_ref[...] = ge[:, :BW] * (gmw_ref[...] * v + gmb_ref[...])
    sc = sconv_ref[...]
    x = zc_ref[:, 0:QKV_DIM]
    y = (cw_ref[0:1, :] * sc[:, 0:768] + cw_ref[1:2, :] * sc[:, 768:1536] + cw_ref[2:3, :] * sc[:, 1536:2304]
         + cw_ref[3:4, :] * x)
    oconv_ref[:, 0:1536] = sc[:, 768:2304]
    oconv_ref[:, 1536:2304] = x
    y = _silu(y)
    q, k = y[:, 0:256], y[:, 256:512]
    bones = bones_ref[...]
    oq_ref[...] = q * lax.rsqrt(_bdot(q * q, bones) + EPS) * (HD ** -0.5)
    ok_ref[...] = k * lax.rsqrt(_bdot(k * k, bones) + EPS)
    ovv_ref[...] = y[:, 512:768]
    zba = zba_ref[...]
    obeta_ref[...] = jax.nn.sigmoid(zba)
    og_ref[...] = -jnp.exp(parr_ref[0:1, :]) * jax.nn.softplus(zba + parr_ref[1:2, :])
    mq = zm_ref[...]
    omq_ref[...] = mq * lax.rsqrt(_bdot(mq * mq, bmean_ref[...]) + EPS) * memg_ref[...]


def _sample_tok(za, zc, zba, zm, sconv, p):
    ns = za.shape[0]
    args = [za, zc, zba, zm, sconv, p["conv_w"], p["gm_norm_g"], p["gm_w0"], p["gm_b0"], p["gdn_par_r"],
            jnp.asarray(_BONES64, BF16), p["mem_qn_gx"], jnp.asarray(_BMEAN64, BF16)]
    widths = (BW, BW, 2304, BW, BW, BW, 128, 128, BW)
    return pl.pallas_call(
        _sample_tok_kernel,
        in_specs=[_full(a.shape) for a in args],
        out_specs=[_full((ns, w)) for w in widths],
        out_shape=[jax.ShapeDtypeStruct((ns, w), F32) for w in widths],
        grid=(1,),
        compiler_params=_cp("arbitrary"),
        name="sample_tok",
    )(*args)


def _gdn_step_kernel(s_ref, q_ref, k_ref, v_ref, beta_ref, g_ref, zg_ref, gout_ref, so_ref, o_ref):
    s = s_ref[...]
    q, k = q_ref[...], k_ref[...]
    eg = jnp.exp(g_ref[...])
    beta = beta_ref[...]
    sk = jnp.sum(s * k, axis=-1, keepdims=True)
    sq = jnp.sum(s * q, axis=-1, keepdims=True)
    qk = jnp.sum(q * k, axis=-1, keepdims=True)
    u = beta * v_ref[...] - beta * eg * sk
    o = eg * sq + qk * u
    so_ref[...] = eg * s + u * k
    ms = jnp.sum(o * o, axis=1, keepdims=True) * (1.0 / HD)
    o_ref[...] = o * lax.rsqrt(ms + EPS) * gout_ref[...] * _silu(zg_ref[...])


def _gdn_step(s, q, k, v, beta, g, zg, gout_col, bn):
    nb = s.shape[0]
    row = pl.BlockSpec((bn, 1, HD), lambda i: (i, 0, 0))
    colv = pl.BlockSpec((bn, HD, 1), lambda i: (i, 0, 0))
    sca = pl.BlockSpec((bn, 1, 1), lambda i: (i, 0, 0))
    mat = pl.BlockSpec((bn, HD, HD), lambda i: (i, 0, 0))
    return pl.pallas_call(
        _gdn_step_kernel,
        grid=(nb // bn,),
        in_specs=[mat, row, row, colv, sca, sca, colv, _full((HD, 1))],
        out_specs=[mat, colv],
        out_shape=[jax.ShapeDtypeStruct((nb, HD, HD), F32), jax.ShapeDtypeStruct((nb, HD, 1), F32)],
        compiler_params=_cp("parallel"),
        name="gdn_step",
    )(s, q, k, v, beta, g, zg, gout_col)


def _mem_kv_kernel(x_ref, g_ref, w_ref, kg_ref, bmean_ref, ok_ref, ov_ref):
    kv = _bdot(_rms(x_ref[...], g_ref[...]), w_ref[...])
    k = kv[:, 0:BW]
    ok_ref[...] = k * lax.rsqrt(_bdot(k * k, bmean_ref[...]) + EPS) * kg_ref[...]
    ov_ref[...] = kv[:, BW:]


def _mem_kv(mem, g, w, kgx, tm):
    m = mem.shape[0]
    bmean = jnp.asarray(_BMEAN64, BF16)
    return pl.pallas_call(
        _mem_kv_kernel,
        grid=(m // tm,),
        in_specs=[pl.BlockSpec((tm, D_MODEL), lambda i: (i, 0)), _full((1, D_MODEL)), _full(w.shape), _full((1, BW)), _full((BW, BW))],
        out_specs=[pl.BlockSpec((tm, BW), lambda i: (i, 0))] * 2,
        out_shape=[jax.ShapeDtypeStruct((m, BW), F32)] * 2,
        compiler_params=_cp("parallel"),
        name="mem_kv",
    )(mem, g, w, kgx, bmean)


def _mem_attn_kernel(q_ref, k_ref, v_ref, gq_ref, bmean_ref, o_ref):
    q = q_ref[...]
    qn = q * lax.rsqrt(_bdot(q * q, bmean_ref[...]) + EPS) * gq_ref[...]
    kb = k_ref[...].astype(BF16)
    vb = v_ref[...].astype(BF16)
    lh = _lane_head()
    out = jnp.zeros(q.shape, F32)
    for h in range(HEADS):
        mh = lh == h
        s = lax.dot_general(jnp.where(mh, qn, 0.0).astype(BF16), kb, NT, preferred_element_type=F32) * (HD ** -0.5)
        e = jnp.exp(s - jnp.max(s, axis=-1, keepdims=True))
        pr = e / jnp.sum(e, axis=-1, keepdims=True)
        out = out + jnp.where(mh, jnp.dot(pr.astype(BF16), vb, preferred_element_type=F32), 0.0)
    o_ref[...] = out


def _mem_attn(zm, mk, mv, gqx, n, t, mt, tq):
    nq = t // tq
    bmean = jnp.asarray(_BMEAN64, BF16)
    return pl.pallas_call(
        _mem_attn_kernel,
        grid=(n, nq),
        in_specs=[pl.BlockSpec((tq, BW), lambda b, i: (b * nq + i, 0)), pl.BlockSpec((mt, BW), lambda b, i: (b, 0)),
                  pl.BlockSpec((mt, BW), lambda b, i: (b, 0)), _full((1, BW)), _full((BW, BW))],
        out_specs=pl.BlockSpec((tq, BW), lambda b, i: (b * nq + i, 0)),
        out_shape=jax.ShapeDtypeStruct((n * t, BW), F32),
        compiler_params=_cp("parallel", "parallel"),
        name="mem_attn",
    )(zm, mk, mv, gqx, bmean)


def _mem_attn_s_kernel(q_ref, k_ref, v_ref, ind_ref, o_ref, *, bn):
    ind = ind_ref[...]
    for i in range(bn):
        qbd = (ind * q_ref[i:i + 1, :]).astype(BF16)
        s = lax.dot_general(qbd, k_ref[i].astype(BF16), NT, preferred_element_type=F32) * (HD ** -0.5)
        e = jnp.exp(s - jnp.max(s, axis=-1, keepdims=True))
        pr = e / jnp.sum(e, axis=-1, keepdims=True)
        o8 = jnp.dot(pr.astype(BF16), v_ref[i].astype(BF16), preferred_element_type=F32)
        o_ref[i:i + 1, :] = jnp.sum(o8 * ind, axis=0, keepdims=True)


def _mem_attn_s(layer, mqn, cache_k, cache_v, bn):
    ns = mqn.shape[0]
    mt = cache_k.shape[2]
    kv_spec = pl.BlockSpec((None, bn, mt, BW), lambda i: (layer, i, 0, 0))
    return pl.pallas_call(
        functools.partial(_mem_attn_s_kernel, bn=bn),
        grid=(ns // bn,),
        in_specs=[pl.BlockSpec((bn, BW), lambda i: (i, 0)), kv_spec, kv_spec, _full((8, BW))],
        out_specs=pl.BlockSpec((bn, BW), lambda i: (i, 0)),
        out_shape=jax.ShapeDtypeStruct((ns, BW), F32),
        compiler_params=_cp("parallel"),
        name="mem_attn_s",
    )(mqn, cache_k, cache_v, jnp.asarray(_IND8))


def _merge_kernel(x_ref, a_ref, b_ref, c_ref, m_ref, g1_ref, wg_ref, wb_ref, wo_ref, g2_ref, wrh_ref, wrl_ref, br_ref,
                  x1_ref, h2_ref, ei_ref, ew_ref):
    x = x_ref[...]
    hb = _rms(x, g1_ref[...]).astype(BF16)
    acc = jnp.zeros(x.shape, F32)
    for b, br in enumerate((a_ref, b_ref, c_ref, m_ref)):
        gate = jax.nn.sigmoid(jnp.dot(hb, wg_ref[:, b * D_MODEL:(b + 1) * D_MODEL], preferred_element_type=F32))
        acc = acc + gate * jnp.dot(br[...].astype(BF16), wb_ref[b], preferred_element_type=F32)
    x1 = x + jnp.dot(acc.astype(BF16), wo_ref[...], preferred_element_type=F32)
    x1_ref[...] = x1
    h2 = _rms(x1, g2_ref[...])
    h2_ref[...] = h2
    logits = _dot3(_split2(h2), (wrh_ref[...], wrl_ref[...])) + br_ref[...]
    lane = lax.broadcasted_iota(jnp.int32, (1, 128), 1).astype(F32)
    big = 1e9
    lg = jnp.where(lane < N_GROUPS, logits, NEG)
    mg = jnp.max(lg, axis=-1, keepdims=True)
    g_w = 1.0 / jnp.sum(jnp.exp(lg - mg), axis=-1, keepdims=True)
    gi = jnp.min(jnp.where(lg == mg, lane, big), axis=-1, keepdims=True)
    sel = (lane >= N_GROUPS) & (lane < N_GROUPS + N_EXPERTS) & (jnp.floor((lane - N_GROUPS) * (1.0 / EPG)) == gi)
    le = jnp.where(sel, logits, NEG)
    m1 = jnp.max(le, axis=-1, keepdims=True)
    i1 = jnp.min(jnp.where(le == m1, lane, big), axis=-1, keepdims=True)
    le2 = jnp.where(lane == i1, NEG, le)
    m2 = jnp.max(le2, axis=-1, keepdims=True)
    i2 = jnp.min(jnp.where(le2 == m2, lane, big), axis=-1, keepdims=True)
    z = jnp.sum(jnp.exp(le - m1), axis=-1, keepdims=True)
    p1 = 1.0 / z
    p2 = jnp.exp(m2 - m1) / z
    w1 = p1 / (p1 + p2) * g_w
    w2 = p2 / (p1 + p2) * g_w
    ei_ref[...] = jnp.where(lane == 0, i1 - N_GROUPS, jnp.where(lane == 1, i2 - N_GROUPS, 0.0)).astype(jnp.int32)
    ew_ref[...] = jnp.where(lane == 0, w1, jnp.where(lane == 1, w2, 0.0))


def _merge(x, branches, p, tm):
    m = x.shape[0]
    consts = [p["norm1_g"], p["w_gate"], p["w_branch"], p["w_out"], p["norm2_g"], *_split2(p["w_router"]), p["b_router"]]
    tile = lambda w: pl.BlockSpec((tm, w), lambda i: (i, 0))
    return pl.pallas_call(
        _merge_kernel,
        grid=(m // tm,),
        in_specs=[tile(D_MODEL)] + [tile(BW)] * 4 + [_full(c.shape) for c in consts],
        out_specs=[tile(D_MODEL), tile(D_MODEL), tile(128), tile(128)],
        out_shape=[jax.ShapeDtypeStruct((m, D_MODEL), F32), jax.ShapeDtypeStruct((m, D_MODEL), F32),
                   jax.ShapeDtypeStruct((m, 128), jnp.int32), jax.ShapeDtypeStruct((m, 128), F32)],
        compiler_params=_cp("parallel"),
        name="merge",
    )(x, *branches, *consts)


def _dispatch_kernel(pos_ref, h_ref, xs_in, xs_out, sem, *, tmd):
    del xs_in

    def issue(i, carry):
        for s in range(2):
            pltpu.make_async_copy(h_ref.at[pl.ds(i, 1)], xs_out.at[pl.ds(pos_ref[2 * i + s], 1)], sem).start()
        return carry

    lax.fori_loop(0, tmd, issue, 0, unroll=8)
    for s in range(2):
        pltpu.make_async_copy(h_ref, xs_out.at[pl.ds(0, tmd)], sem).wait()


def _dispatch(pos, h2, rows, tmd):
    m = h2.shape[0]
    xs0 = jnp.zeros((rows, D_MODEL), F32)
    return pl.pallas_call(
        functools.partial(_dispatch_kernel, tmd=tmd),
        grid=(m // tmd,),
        in_specs=[pl.BlockSpec((2 * tmd,), lambda i: (i,), memory_space=pltpu.SMEM),
                  pl.BlockSpec((tmd, D_MODEL), lambda i: (i, 0)), pl.BlockSpec(memory_space=pl.ANY)],
        out_specs=pl.BlockSpec(memory_space=pl.ANY),
        out_shape=jax.ShapeDtypeStruct((rows, D_MODEL), F32),
        scratch_shapes=[pltpu.SemaphoreType.DMA(())],
        input_output_aliases={2: 0},
        compiler_params=_cp("arbitrary"),
        name="moe_dispatch",
    )(pos, h2, xs0)


def _expert_kernel(te_ref, nv_ref, x_ref, wg_ref, wu_ref, wd_ref, o_ref):
    del te_ref

    @pl.when(pl.program_id(0) < nv_ref[0])
    def _():
        xb = x_ref[...].astype(BF16)
        gt = jnp.dot(xb, wg_ref[...].astype(BF16), preferred_element_type=F32)
        up = jnp.dot(xb, wu_ref[...].astype(BF16), preferred_element_type=F32)
        o_ref[...] = jnp.dot((_silu(gt) * up).astype(BF16), wd_ref[...].astype(BF16), preferred_element_type=F32)

    @pl.when(pl.program_id(0) >= nv_ref[0])
    def _():
        o_ref[...] = jnp.zeros(o_ref.shape, F32)


def _experts(layer, tile_expert, n_valid, xs, w_gate, w_up, w_down, te):
    rows = xs.shape[0]

    def xmap(i, te_ref, nv_ref):
        return (jnp.minimum(i, nv_ref[0] - 1), 0)

    def wmap(i, te_ref, nv_ref):
        return (layer, te_ref[i], 0, 0)

    grid_spec = pltpu.PrefetchScalarGridSpec(
        num_scalar_prefetch=2,
        grid=(rows // te,),
        in_specs=[pl.BlockSpec((te, D_MODEL), xmap), pl.BlockSpec((None, None, D_MODEL, D_EXPERT), wmap),
                  pl.BlockSpec((None, None, D_MODEL, D_EXPERT), wmap), pl.BlockSpec((None, None, D_EXPERT, D_MODEL), wmap)],
        out_specs=pl.BlockSpec((te, D_MODEL), lambda i, te_ref, nv_ref: (i, 0)),
    )
    return pl.pallas_call(
        _expert_kernel,
        grid_spec=grid_spec,
        out_shape=jax.ShapeDtypeStruct((rows, D_MODEL), F32),
        compiler_params=_cp("arbitrary"),
        name="moe_experts",
    )(tile_expert, n_valid, xs, w_gate, w_up, w_down)


def _combine_kernel(pos_ref, x1_ref, ew_ref, ys_hbm, o_ref, r0, r1, sem, *, tmc):
    bufs = (r0, r1)

    def issue(i, carry):
        for s in range(2):
            pltpu.make_async_copy(ys_hbm.at[pl.ds(pos_ref[2 * i + s], 1)], bufs[s].at[pl.ds(i, 1)], sem).start()
        return carry

    lax.fori_loop(0, tmc, issue, 0, unroll=8)
    for s in range(2):
        pltpu.make_async_copy(ys_hbm.at[pl.ds(0, tmc)], bufs[s], sem).wait()
    ew = ew_ref[...]
    o_ref[...] = x1_ref[...] + ew[:, 0:1] * r0[...] + ew[:, 1:2] * r1[...]


def _combine(pos, x1, ew, ys, tmc):
    m = x1.shape[0]
    return pl.pallas_call(
        functools.partial(_combine_kernel, tmc=tmc),
        grid=(m // tmc,),
        in_specs=[pl.BlockSpec((2 * tmc,), lambda i: (i,), memory_space=pltpu.SMEM),
                  pl.BlockSpec((tmc, D_MODEL), lambda i: (i, 0)), pl.BlockSpec((tmc, 128), lambda i: (i, 0)),
                  pl.BlockSpec(memory_space=pl.ANY)],
        out_specs=pl.BlockSpec((tmc, D_MODEL), lambda i: (i, 0)),
        out_shape=jax.ShapeDtypeStruct((m, D_MODEL), F32),
        scratch_shapes=[pltpu.VMEM((tmc, D_MODEL), F32), pltpu.VMEM((tmc, D_MODEL), F32), pltpu.SemaphoreType.DMA(())],
        compiler_params=_cp("arbitrary"),
        name="moe_combine",
    )(pos, x1, ew, ys)


def _moe(layer, x1, h2, ei, ew, w_gate, w_up, w_down, te, tmd):
    m = x1.shape[0]
    flat_e = ei[:, 0:2].reshape(-1)
    onehot = (flat_e[:, None] == jnp.arange(N_EXPERTS, dtype=jnp.int32)[None, :]).astype(jnp.int32)
    csum = jnp.cumsum(onehot, axis=0)
    rank = jnp.sum(csum * onehot, axis=1) - 1
    counts = csum[-1]
    padded = ((counts + te - 1) // te) * te
    pend = jnp.cumsum(padded)
    pstart = pend - padded
    pos = (jnp.sum(onehot * pstart[None, :], axis=1) + rank).astype(jnp.int32)
    rows = ((2 * m + N_EXPERTS * (te - 1)) // te) * te
    n_tiles = rows // te
    n_valid = (pend[-1] // te).astype(jnp.int32).reshape(1)
    tile_start = jnp.arange(n_tiles, dtype=jnp.int32) * te
    tile_expert = jnp.minimum(jnp.sum((tile_start[:, None] >= pend[None, :]).astype(jnp.int32), axis=1), N_EXPERTS - 1)
    last_e = jnp.take(tile_expert, jnp.maximum(n_valid[0] - 1, 0))
    tile_expert = jnp.where(jnp.arange(n_tiles) < n_valid[0], tile_expert, last_e).astype(jnp.int32)
    xs = _dispatch(pos, h2, rows, tmd)
    ys = _experts(layer, tile_expert, n_valid, xs, w_gate, w_up, w_down, te)
    return _combine(pos, x1, ew, ys, tmd)


def _tile4(v):
    return jnp.tile(v, HEADS).reshape(1, BW)


def _prep_layer(l, w):
    w_in = w["w_in"][l]
    z = lambda n: jnp.zeros((D_MODEL, n), F32)
    b_al = w_in[:, 1952:1960]
    w_small = jnp.concatenate(
        [w_in[:, 0:512], w_in[:, 512:768], w_in[:, 768:896], z(64), w_in[:, 896:928], z(32), w_in[:, 928:1696],
         w_in[:, 1696:1952], w_in[:, 1960:2216], b_al, z(120)], axis=1).astype(BF16)
    uq = w["mla_w_uq"][l]
    w_uq = jnp.pad(uq, ((0, 0), (0, 0), (0, 32))).reshape(256, 512).astype(BF16)
    qg = jnp.tile(jnp.concatenate([w["mla_qn_g"][l], w["mla_qr_g"][l], jnp.zeros((32,), F32)]), HEADS).reshape(1, 512)
    uk = w["mla_w_uk"][l]
    w_uk_p = jnp.pad(uk, ((0, 0), (0, 0), (0, 64))).reshape(128, 512).astype(BF16)
    kg = jnp.tile(jnp.concatenate([w["mla_kn_g"][l], jnp.zeros((64,), F32)]), HEADS).reshape(1, 512)
    kr_g = jnp.concatenate([jnp.zeros((64,), F32), w["mla_kr_g"][l], jnp.zeros((32,), F32)]).reshape(1, 128)
    uv = w["mla_w_uv"][l]
    w_uv_p = jnp.stack([jnp.pad(uv[:, h, :], ((0, 0), (64 * h, BW - 64 * h - 64))) for h in range(HEADS)]).astype(BF16)
    par_r = jnp.zeros((8, 128), F32).at[0, 4:8].set(w["gdn_a_log"][l]).at[1, 4:8].set(w["gdn_dt_bias"][l])
    par_c = jnp.zeros((8, 128), F32).at[4:8, 0].set(w["gdn_a_log"][l]).at[4:8, 1].set(w["gdn_dt_bias"][l])
    w_router = jnp.concatenate([w["moe_wg"][l], w["moe_we"][l], jnp.zeros((D_MODEL, 128 - 36), F32)], axis=1)
    b_router = jnp.concatenate([w["moe_bg"][l], w["moe_be"][l], jnp.zeros((128 - 36,), F32)]).reshape(1, 128)
    return {
        "norm1_g": w["norm1_g"][l].reshape(1, D_MODEL), "w_small": w_small, "w_bat": b_al.T.astype(BF16),
        "w_gate": w_in[:, 2216:].astype(BF16),
        "gm_norm_g": w["gm_norm_g"][l].reshape(1, BW), "gm_ws": w["gm_ws"][l],
        "gm_bfull": jnp.repeat(w["gm_b"][l].T, HD, axis=1),
        "gm_w0": jnp.repeat(w["gm_ws"][l][:, 0, 0], HD).reshape(1, BW), "gm_b0": jnp.repeat(w["gm_b"][l][:, 0], HD).reshape(1, BW),
        "mla_cq_g": w["mla_cq_g"][l].reshape(1, 256), "w_uq": w_uq, "qg": qg, "mla_ckv_g": w["mla_ckv_g"][l].reshape(1, 128),
        "kr_g": kr_g, "w_uk": w_uk_p, "kg": kg, "w_uv": w_uv_p,
        "w_uk_c": uk.reshape(128, BW).astype(BF16), "w_uv_c": uv.reshape(128, BW).astype(BF16), "kn_gx": _tile4(w["mla_kn_g"][l]),
        "conv_w": w["gdn_conv_w"][l], "gdn_par_r": par_r, "gdn_par_c": par_c, "gdn_out_gx": _tile4(w["gdn_out_g"][l]),
        "gdn_out_gc": w["gdn_out_g"][l].reshape(HD, 1),
        "mem_norm_g": w["mem_norm_g"][l].reshape(1, D_MODEL), "mem_w_kv": w["mem_w_kv"][l].astype(BF16),
        "mem_qn_gx": _tile4(w["mem_qn_g"][l]), "mem_kn_gx": _tile4(w["mem_kn_g"][l]),
        "w_branch": w["w_branch"][l].astype(BF16), "w_out": w["w_out"][l].astype(BF16),
        "norm2_g": w["norm2_g"][l].reshape(1, D_MODEL), "w_router": w_router, "b_router": b_router,
    }


def _rope_tables(pos):
    half = MLA_ROPE // 2
    inv = ROPE_THETA ** (-jnp.arange(half, dtype=F32) / half)
    ang = pos.astype(F32)[:, None] * inv[None, :]
    cos, sin = jnp.cos(ang), jnp.sin(ang)
    t = pos.shape[0]
    one, zero = jnp.ones((t, 64), F32), jnp.zeros((t, 64), F32)
    z16, z32 = jnp.zeros((t, 16), F32), jnp.zeros((t, 32), F32)
    return (jnp.concatenate([one, cos, cos, jnp.ones((t, 32), F32)], axis=1),
            jnp.concatenate([zero, -sin, z16, z32], axis=1),
            jnp.concatenate([zero, z16, sin, z32], axis=1))


def kernel(x_prompt, mem_prompt, x_sample, cache_mla_ckv, cache_mla_kr, cache_mem_k, cache_mem_v, state_gdn, state_conv,
           page_table, norm1_g, w_in, gm_norm_g, gm_ws, gm_b, mla_cq_g, mla_w_uq, mla_qn_g, mla_qr_g, mla_ckv_g, mla_kr_g,
           mla_w_uk, mla_kn_g, mla_w_uv, gdn_conv_w, gdn_a_log, gdn_dt_bias, gdn_out_g, mem_norm_g, mem_w_kv, mem_qn_g,
           mem_kn_g, w_branch, w_out, norm2_g, moe_wg, moe_bg, moe_we, moe_be, moe_w_gate, moe_w_up, moe_w_down):
    w = dict(norm1_g=norm1_g, w_in=w_in, gm_norm_g=gm_norm_g, gm_ws=gm_ws, gm_b=gm_b, mla_cq_g=mla_cq_g, mla_w_uq=mla_w_uq,
             mla_qn_g=mla_qn_g, mla_qr_g=mla_qr_g, mla_ckv_g=mla_ckv_g, mla_kr_g=mla_kr_g, mla_w_uk=mla_w_uk, mla_kn_g=mla_kn_g,
             mla_w_uv=mla_w_uv, gdn_conv_w=gdn_conv_w, gdn_a_log=gdn_a_log, gdn_dt_bias=gdn_dt_bias, gdn_out_g=gdn_out_g,
             mem_norm_g=mem_norm_g, mem_w_kv=mem_w_kv, mem_qn_g=mem_qn_g, mem_kn_g=mem_kn_g, w_branch=w_branch, w_out=w_out,
             norm2_g=norm2_g, moe_wg=moe_wg, moe_bg=moe_bg, moe_we=moe_we, moe_be=moe_be)
    depth = w_in.shape[0]
    bp, tp, _ = x_prompt.shape
    bs = x_sample.shape[0]
    mt = mem_prompt.shape[1]
    n_pages = page_table.shape[1]
    past_len = n_pages * cache_mla_ckv.shape[2]
    mp = bp * tp

    tm_p = min(512, mp)
    tq = min(256, tp)
    tg = min(256, tp)
    pp = min(32, n_pages)
    cache_krt = jnp.swapaxes(cache_mla_kr, 2, 3)
    tabs_p = _rope_tables(jnp.arange(tp, dtype=jnp.int32))
    tabs_s = _rope_tables(jnp.full((bs,), past_len, jnp.int32))

    xp = x_prompt.reshape(mp, D_MODEL)
    xs = x_sample.reshape(bs, D_MODEL)
    mem = mem_prompt.reshape(bp * mt, D_MODEL)
    cache_k = cache_mem_k.reshape(depth, bs, mt, BW)
    cache_v = cache_mem_v.reshape(depth, bs, mt, BW)
    rows_p, rows_s = [], []
    for l in range(depth):
        p = _prep_layer(l, w)
        mk, mv = _mem_kv(mem, p["mem_norm_g"], p["mem_w_kv"], p["mem_kn_gx"], min(512, bp * mt))
        za, zb, zc, zm, zba, bat = _in_proj(xp, p["norm1_g"], p["w_small"], p["w_bat"], tm_p)
        a_out = _gmlp(za, mp, p["gm_norm_g"], p["gm_ws"], p["gm_bfull"], tm_p)
        q4, k4, ckv, kr, ckv16, _ = _mla_pre(zb, mp, tabs_p, tp // min(tm_p, tp), p, min(tm_p, tp))
        b_out = _mla_attn(q4, k4, ckv16, p["w_uv"], bp, tp, tq)
        c_out, sfin = _gdn_prompt(zc, zba, bat, bp, tp, p, tg)
        m_out = _mem_attn(zm, mk, mv, p["mem_qn_gx"], bp, tp, mt, tq)
        x1, h2, ei, ew = _merge(xp, (a_out, b_out, c_out, m_out), p, tm_p)
        xp = _moe(l, x1, h2, ei, ew, moe_w_gate, moe_w_up, moe_w_down, min(256, mp), min(512, mp))
        s_p = jnp.stack([sfin[:, 64 * h:64 * h + 64, 64 * h:64 * h + 64] for h in range(HEADS)], axis=1).transpose(0, 1, 3, 2)
        conv_p = zc.reshape(bp, tp, 1024)[:, tp - 3:, 0:QKV_DIM]
        rows_p.append((ckv.reshape(bp, tp, 128), kr.reshape(bp, tp, MLA_ROPE), s_p, conv_p,
                       mk.reshape(bp, mt, HEADS, HD), mv.reshape(bp, mt, HEADS, HD)))
        za, zb, zc, zm, zba, _ = _in_proj(xs, p["norm1_g"], p["w_small"], p["w_bat"], bs)
        a_s, v_s, conv_s, gq, gk, gv, beta, gdec, mqn = _sample_tok(za, zc, zba, zm, state_conv[l].reshape(bs, 3 * QKV_DIM), p)
        _, _, ckv_s, kr_s, _, q32 = _mla_pre(zb, bs, tabs_s, 1, p, bs)
        q3 = q32.reshape(bs, HEADS, 128)
        qk = q3[:, :, 0:64].reshape(bs, 1, BW)
        qr8 = jnp.pad(q3[:, :, 64:96], ((0, 0), (0, 4), (0, 0)))
        b_s = _mla_decode(l, page_table, cache_mla_ckv, cache_krt, qk, qr8, ckv_s.reshape(bs, 1, 128),
                          kr_s.reshape(bs, 1, MLA_ROPE), p["kn_gx"], p["w_uk_c"], p["w_uv_c"], pp).reshape(bs, BW)
        nb = bs * HEADS
        s_new, c_col = _gdn_step(state_gdn[l].reshape(nb, HD, HD), gq.reshape(nb, 1, HD), gk.reshape(nb, 1, HD),
                                 gv.reshape(nb, HD, 1), beta[:, 0:4].reshape(nb, 1, 1), gdec[:, 4:8].reshape(nb, 1, 1),
                                 zc[:, QKV_DIM:QKV_DIM + BW].reshape(nb, HD, 1), p["gdn_out_gc"], min(64, nb))
        m_s = _mem_attn_s(l, mqn, cache_k, cache_v, min(8, bs))
        x1, h2, ei, ew = _merge(xs, (a_s, b_s, c_col.reshape(bs, BW), m_s), p, bs)
        xs = _moe(l, x1, h2, ei, ew, moe_w_gate, moe_w_up, moe_w_down, min(32, bs), bs)
        rows_s.append((ckv_s.reshape(bs, 1, 128), kr_s.reshape(bs, 1, MLA_ROPE), s_new.reshape(bs, HEADS, HD, HD),
                       conv_s.reshape(bs, 3, QKV_DIM), v_s.reshape(bs, 1, BW)))
    p_out = [jnp.stack(a) for a in zip(*rows_p)]
    s_out = [jnp.stack(a) for a in zip(*rows_s)]
    return (xp.reshape(bp, tp, D_MODEL), xs.reshape(bs, 1, D_MODEL), *p_out, *s_out)
```

```python
import functools

import numpy as np
import jax
import jax.numpy as jnp
from jax import lax
from jax.experimental import pallas as pl
from jax.experimental.pallas import tpu as pltpu

F32 = jnp.float32
BF16 = jnp.bfloat16
HI = lax.Precision.HIGHEST
EPS = 1e-6
NEG = float("-inf")

D_MODEL = 1024
HEADS = 4
HD = 64
BW = 256
MLA_ROPE = 32
MLA_KV_RANK = 128
MLA_SCALE = 96.0 ** -0.5
ROPE_THETA = 10000.0
GM_CHUNK = 128
GDN_CHUNK = 64
QKV_DIM = 768
N_GROUPS = 4
EPG = 8
N_EXPERTS = 32
D_EXPERT = 256
PAGE = 128
VMEM_LIMIT = 56 * 1024 * 1024

NT = (((1,), (1,)), ((), ()))
TN = (((0,), (0,)), ((), ()))


def _cp(*sem):
    return pltpu.CompilerParams(dimension_semantics=sem, vmem_limit_bytes=VMEM_LIMIT)


def _rms(x, g):
    ms = jnp.sum(x * x, axis=-1, keepdims=True) * (1.0 / x.shape[-1])
    return x * lax.rsqrt(ms + EPS) * g


def _bdot(a, b):
    return jnp.dot(a.astype(BF16), b.astype(BF16), preferred_element_type=F32)


def _bdot_nt(a, b):
    return lax.dot_general(a.astype(BF16), b.astype(BF16), NT, preferred_element_type=F32)


def _hdot(a, b):
    return jnp.dot(a, b, precision=HI, preferred_element_type=F32)


def _split2(a):
    hi = a.astype(BF16)
    return hi, (a - hi.astype(F32)).astype(BF16)


def _dot3(a, b):
    d = lambda x, y: jnp.dot(x, y, preferred_element_type=F32)
    return d(a[0], b[0]) + (d(a[0], b[1]) + d(a[1], b[0]))


def _sel_dot(w01, x, left):
    x0 = x.astype(BF16)
    r1 = x - x0.astype(F32)
    x1 = r1.astype(BF16)
    x2 = (r1 - x1.astype(F32)).astype(BF16)
    d = (lambda p: jnp.dot(w01, p, preferred_element_type=F32)) if left else (lambda p: jnp.dot(p, w01, preferred_element_type=F32))
    return d(x0) + (d(x1) + d(x2))


def _silu(x):
    return x * jax.nn.sigmoid(x)


def _gelu(x):
    return 0.5 * x * (1.0 + lax.erf(x * 0.7071067811865476))


def _lane_head(width=BW):
    return lax.broadcasted_iota(jnp.int32, (1, width), 1) // HD


def _full(shape):
    n = len(shape)
    return pl.BlockSpec(shape, lambda *_: (0,) * n)


def _block_mean(width, segs):
    m = np.zeros((width, width), np.float32)
    for a, b in segs:
        m[a:b, a:b] = 1.0 / (b - a)
    return m


_BMEAN64 = _block_mean(BW, [(64 * h, 64 * h + 64) for h in range(HEADS)])
_BONES64 = _BMEAN64 * 64.0
_BQ = _block_mean(512, [(128 * h, 128 * h + 64) for h in range(HEADS)] + [(128 * h + 64, 128 * h + 96) for h in range(HEADS)])
_BK = _block_mean(512, [(128 * h, 128 * h + 64) for h in range(HEADS)])
_IND8 = np.zeros((8, BW), np.float32)
for _h in range(HEADS):
    _IND8[_h, 64 * _h:64 * _h + 64] = 1.0
_EXPB = np.zeros((128, BW), np.float32)
_EXPG = np.zeros((128, BW), np.float32)
for _h in range(HEADS):
    _EXPB[_h, 64 * _h:64 * _h + 64] = 1.0
    _EXPG[4 + _h, 64 * _h:64 * _h + 64] = 1.0
_BDMASK = (_BONES64 > 0).astype(np.float32)


def _in_proj_kernel(x_ref, g_ref, w_ref, wbat_ref, oa_ref, ob_ref, oc_ref, om_ref, oba_ref, obat_ref):
    hb = _rms(x_ref[...], g_ref[...]).astype(BF16)
    oa_ref[...] = jnp.dot(hb, w_ref[:, 0:512], preferred_element_type=F32)
    ob_ref[...] = jnp.dot(hb, w_ref[:, 512:1024], preferred_element_type=F32)
    oc_ref[...] = jnp.dot(hb, w_ref[:, 1024:2048], preferred_element_type=F32)
    om_ref[...] = jnp.dot(hb, w_ref[:, 2048:2304], preferred_element_type=F32)
    oba_ref[...] = jnp.dot(hb, w_ref[:, 2304:2432], preferred_element_type=F32)
    obat_ref[...] = lax.dot_general(wbat_ref[...], hb, NT, preferred_element_type=F32)


def _in_proj(x, g, w, wbat, tm):
    m = x.shape[0]
    widths = (512, 512, 1024, 256, 128)
    return pl.pallas_call(
        _in_proj_kernel,
        grid=(m // tm,),
        in_specs=[pl.BlockSpec((tm, D_MODEL), lambda i: (i, 0)), _full((1, D_MODEL)), _full(w.shape), _full(wbat.shape)],
        out_specs=[pl.BlockSpec((tm, n), lambda i: (i, 0)) for n in widths] + [pl.BlockSpec((8, tm), lambda i: (0, i))],
        out_shape=[jax.ShapeDtypeStruct((m, n), F32) for n in widths] + [jax.ShapeDtypeStruct((8, m), F32)],
        compiler_params=_cp("parallel"),
        name="in_proj",
    )(x, g, w, wbat)


def _gmlp_kernel(z_ref, g_ref, ws_ref, b_ref, o_ref, *, ta):
    row = lax.broadcasted_iota(jnp.int32, (GM_CHUNK, GM_CHUNK), 0)
    col = lax.broadcasted_iota(jnp.int32, (GM_CHUNK, GM_CHUNK), 1)
    tril = col <= row
    lh = _lane_head()
    wts = [jnp.where(tril, ws_ref[g], 0.0).astype(BF16) for g in range(HEADS)]
    for c in range(ta // GM_CHUNK):
        sl = slice(c * GM_CHUNK, (c + 1) * GM_CHUNK)
        ge = _gelu(z_ref[sl, :])
        u = ge[:, :BW]
        vb = _rms(ge[:, BW:], g_ref[...]).astype(BF16)
        s = b_ref[...]
        for g in range(HEADS):
            s = s + jnp.where(lh == g, jnp.dot(wts[g], vb, preferred_element_type=F32), 0.0)
        o_ref[sl, :] = u * s


def _gmlp(za, m, g, ws, bfull, ta):
    return pl.pallas_call(
        functools.partial(_gmlp_kernel, ta=ta),
        grid=(m // ta,),
        in_specs=[pl.BlockSpec((ta, 512), lambda i: (i, 0)), _full((1, BW)), _full(ws.shape), _full(bfull.shape)],
        out_specs=pl.BlockSpec((ta, BW), lambda i: (i, 0)),
        out_shape=jax.ShapeDtypeStruct((m, BW), F32),
        compiler_params=_cp("parallel"),
        name="gmlp",
    )(za, g, ws, bfull)


def _mla_pre_kernel(z_ref, c_ref, s1_ref, s2_ref, gcq_ref, wuq_ref, qg_ref, bq_ref, gckv_ref, gkr_ref, wuk_ref, kg_ref,
                    bk_ref, oq_ref, ok_ref, ockv_ref, okr_ref, ockv16_ref, oq32_ref):
    cs, s1, s2 = c_ref[...], s1_ref[...], s2_ref[...]

    def rope(x):
        return x * cs + pltpu.roll(x, 112, 1) * s1 + pltpu.roll(x, 16, 1) * s2

    z = z_ref[...]
    cq = _rms(z[:, 0:256], gcq_ref[...])
    q = _bdot(cq, wuq_ref[...])
    qn = q * lax.rsqrt(_bdot(q * q, bq_ref[...]) + EPS) * qg_ref[...]
    ckv = _rms(z[:, 256:384], gckv_ref[...])
    ockv_ref[...] = ckv
    ockv16_ref[...] = ckv.astype(BF16)
    krb = z[:, 384:512]
    kr = rope(krb * lax.rsqrt(jnp.sum(krb * krb, axis=-1, keepdims=True) * (1.0 / MLA_ROPE) + EPS) * gkr_ref[...])
    okr_ref[...] = kr[:, 64:96]
    k = _bdot(ckv, wuk_ref[...])
    kn = k * lax.rsqrt(_bdot(k * k, bk_ref[...]) + EPS) * kg_ref[...]
    for h in range(HEADS):
        sl = slice(128 * h, 128 * h + 128)
        qh = rope(qn[:, sl])
        oq_ref[h] = qh.astype(BF16)
        oq32_ref[:, sl] = qh
        ok_ref[h] = (kn[:, sl] + kr).astype(BF16)


def _mla_pre(zb, m, tabs, t_blocks, p, tm):
    cs, s1, s2 = tabs
    tab_spec = pl.BlockSpec((tm, 128), lambda i: (i % t_blocks, 0))
    consts = [p["mla_cq_g"], p["w_uq"], p["qg"], jnp.asarray(_BQ, BF16), p["mla_ckv_g"], p["kr_g"], p["w_uk"], p["kg"],
              jnp.asarray(_BK, BF16)]
    return pl.pallas_call(
        _mla_pre_kernel,
        grid=(m // tm,),
        in_specs=[pl.BlockSpec((tm, 512), lambda i: (i, 0)), tab_spec, tab_spec, tab_spec] + [_full(c.shape) for c in consts],
        out_specs=[pl.BlockSpec((HEADS, tm, 128), lambda i: (0, i, 0)), pl.BlockSpec((HEADS, tm, 128), lambda i: (0, i, 0)),
                   pl.BlockSpec((tm, 128), lambda i: (i, 0)), pl.BlockSpec((tm, MLA_ROPE), lambda i: (i, 0)),
                   pl.BlockSpec((tm, 128), lambda i: (i, 0)), pl.BlockSpec((tm, 512), lambda i: (i, 0))],
        out_shape=[jax.ShapeDtypeStruct((HEADS, m, 128), BF16), jax.ShapeDtypeStruct((HEADS, m, 128), BF16),
                   jax.ShapeDtypeStruct((m, 128), F32), jax.ShapeDtypeStruct((m, MLA_ROPE), F32),
                   jax.ShapeDtypeStruct((m, 128), BF16), jax.ShapeDtypeStruct((m, 512), F32)],
        compiler_params=_cp("parallel"),
        name="mla_pre",
    )(zb, cs, s1, s2, *consts)


def _mla_attn_kernel(q_ref, k_ref, v_ref, wuv_ref, o_ref, *, tq):
    i = pl.program_id(1)
    row = lax.broadcasted_iota(jnp.int32, (tq, tq), 0)
    col = lax.broadcasted_iota(jnp.int32, (tq, tq), 1)
    causal = col <= row

    def step(off, carry, mask):
        m, l, acc = carry
        vj = v_ref[pl.ds(off, tq), :]
        ss = []
        for h in range(HEADS):
            s = lax.dot_general(q_ref[h], k_ref[h, pl.ds(off, tq), :], NT, preferred_element_type=F32) * MLA_SCALE
            ss.append(jnp.where(causal, s, NEG) if mask else s)
        s = jnp.concatenate(ss, axis=0)
        mn = jnp.maximum(m, jnp.max(s, axis=-1, keepdims=True))
        pr = jnp.exp(s - mn)
        al = jnp.exp(m - mn)
        return mn, al * l + jnp.sum(pr, axis=-1, keepdims=True), al * acc + jnp.dot(pr.astype(BF16), vj, preferred_element_type=F32)

    init = (jnp.full((HEADS * tq, 1), NEG, F32), jnp.zeros((HEADS * tq, 1), F32), jnp.zeros((HEADS * tq, 128), F32))
    carry = lax.fori_loop(0, i, lambda j, c: step(pl.multiple_of(j * tq, tq), c, False), init)
    m, l, acc = step(pl.multiple_of(i * tq, tq), carry, True)
    lat = (acc / l).astype(BF16)
    out = jnp.zeros((tq, BW), F32)
    for h in range(HEADS):
        out = out + jnp.dot(lat[h * tq:(h + 1) * tq, :], wuv_ref[h], preferred_element_type=F32)
    o_ref[...] = out


def _mla_attn(q4, k4, ckv16, wuv, n, t, tq):
    nq = t // tq
    return pl.pallas_call(
        functools.partial(_mla_attn_kernel, tq=tq),
        grid=(n, nq),
        in_specs=[pl.BlockSpec((HEADS, tq, 128), lambda b, i: (0, b * nq + i, 0)),
                  pl.BlockSpec((HEADS, t, 128), lambda b, i: (0, b, 0)),
                  pl.BlockSpec((t, 128), lambda b, i: (b, 0)), _full(wuv.shape)],
        out_specs=pl.BlockSpec((tq, BW), lambda b, i: (b * nq + i, 0)),
        out_shape=jax.ShapeDtypeStruct((n * t, BW), F32),
        compiler_params=_cp("parallel", "arbitrary"),
        name="mla_attn",
    )(q4, k4, ckv16, wuv)


def _mla_decode_kernel(pt_ref, qk_ref, qr_ref, cnew_ref, krnew_ref, kng_ref, wuk_ref, wuv_ref, ind_ref, ckv_hbm, krt_hbm, o_ref,
                       cbuf, kbuf, sem, *, layer, npages, pp):
    n = pl.program_id(0)
    ngroups = npages // pp

    def page_copies(sample, grp, slot):
        base = sample * npages + grp * pp
        out = []
        for i in range(pp):
            page = pt_ref[base + i]
            out.append(pltpu.make_async_copy(ckv_hbm.at[layer, page], cbuf.at[slot, pl.ds(i * PAGE, PAGE)], sem.at[slot]))
            out.append(pltpu.make_async_copy(krt_hbm.at[layer, page], kbuf.at[slot, :, pl.ds(i * PAGE, PAGE)], sem.at[slot]))
        return out

    @pl.when(n == 0)
    def _():
        for cp in page_copies(0, 0, 0):
            cp.start()

    ind = ind_ref[...]
    qbd = (ind * (qk_ref[...] * kng_ref[...])).astype(BF16)
    ind64 = (ind * (1.0 / HD)).astype(BF16)
    qr = qr_ref[...].astype(BF16)
    wuk = wuk_ref[...]
    qabs = lax.dot_general(qbd, wuk, NT, preferred_element_type=F32).astype(BF16)

    m = jnp.full((8, 1), NEG, F32)
    l = jnp.zeros((8, 1), F32)
    acc = jnp.zeros((8, MLA_KV_RANK), F32)
    for grp in range(ngroups):
        slot = (n * ngroups + grp) % 2
        if grp + 1 < ngroups:
            for cp in page_copies(n, grp + 1, 1 - slot):
                cp.start()
        else:
            @pl.when(n + 1 < pl.num_programs(0))
            def _():
                for cp in page_copies(n + 1, 0, 1 - slot):
                    cp.start()
        for cp in page_copies(n, grp, slot):
            cp.wait()
        cb = cbuf[slot].astype(BF16)
        krt = kbuf[slot].astype(BF16)
        k = jnp.dot(cb, wuk, preferred_element_type=F32)
        num = lax.dot_general(qabs, cb, NT, preferred_element_type=F32)
        ms = lax.dot_general(ind64, (k * k).astype(BF16), NT, preferred_element_type=F32)
        s = (num * lax.rsqrt(ms + EPS) + jnp.dot(qr, krt, preferred_element_type=F32)) * MLA_SCALE
        mn = jnp.maximum(m, jnp.max(s, axis=-1, keepdims=True))
        pb = jnp.exp(s - mn)
        al = jnp.exp(m - mn)
        l = al * l + jnp.sum(pb, axis=-1, keepdims=True)
        acc = al * acc + jnp.dot(pb.astype(BF16), cb, preferred_element_type=F32)
        m = mn

    cb = jnp.broadcast_to(cnew_ref[...], (8, MLA_KV_RANK)).astype(BF16)
    k1 = jnp.dot(cb, wuk, preferred_element_type=F32)
    kk1 = (k1 * k1).astype(BF16).astype(F32)
    krn = krnew_ref[...].astype(BF16).astype(F32)
    num1 = jnp.sum(qabs.astype(F32) * cb.astype(F32), axis=-1, keepdims=True)
    ms1 = jnp.sum(ind64.astype(F32) * kk1, axis=-1, keepdims=True)
    s1 = (num1 * lax.rsqrt(ms1 + EPS) + jnp.sum(qr.astype(F32) * krn, axis=-1, keepdims=True)) * MLA_SCALE
    mn1 = jnp.maximum(m, s1)
    p1 = jnp.exp(s1 - mn1)
    al1 = jnp.exp(m - mn1)
    lat = (al1 * acc + p1 * cb.astype(F32)) / (al1 * l + p1)
    o8 = jnp.dot(lat.astype(BF16), wuv_ref[...], preferred_element_type=F32)
    o_ref[...] = jnp.sum(o8 * ind, axis=0, keepdims=True)


def _mla_decode(layer, page_table, cache_ckv, cache_krt, qk, qr8, cnew, krnew, kng, wuk, wuv, pp):
    ns, npages = page_table.shape
    pt = page_table.reshape(-1)

    def per_sample(shape):
        return pl.BlockSpec((None,) + shape, lambda n, pt_ref: (n, 0, 0))

    def const(a):
        nd = a.ndim
        return pl.BlockSpec(a.shape, lambda n, pt_ref: (0,) * nd)

    ind = jnp.asarray(_IND8)
    hbm = pl.BlockSpec(memory_space=pl.ANY)
    grid_spec = pltpu.PrefetchScalarGridSpec(
        num_scalar_prefetch=1,
        grid=(ns,),
        in_specs=[per_sample((1, BW)), per_sample((8, MLA_ROPE)), per_sample((1, MLA_KV_RANK)), per_sample((1, MLA_ROPE)),
                  const(kng), const(wuk), const(wuv), const(ind), hbm, hbm],
        out_specs=per_sample((1, BW)),
        scratch_shapes=[pltpu.VMEM((2, pp * PAGE, MLA_KV_RANK), F32), pltpu.VMEM((2, MLA_ROPE, pp * PAGE), F32),
                        pltpu.SemaphoreType.DMA((2,))],
    )
    return pl.pallas_call(
        functools.partial(_mla_decode_kernel, layer=layer, npages=npages, pp=pp),
        grid_spec=grid_spec,
        out_shape=jax.ShapeDtypeStruct((ns, 1, BW), F32),
        compiler_params=_cp("arbitrary"),
        name="mla_decode",
    )(pt, qk, qr8, cnew, krnew, kng, wuk, wuv, ind, cache_ckv, cache_krt)


def _gdn_kernel(zc_ref, zba_ref, bat_ref, cw_ref, parr_ref, parc_ref, gout_ref, bones_ref, expb_ref, expg_ref, bdm_ref,
                o_ref, sfin_ref, xbuf, s_sc, *, tg):
    t = pl.program_id(1)
    c = GDN_CHUNK

    @pl.when(t == 0)
    def _():
        xbuf[0:8, :] = jnp.zeros((8, QKV_DIM), F32)
        s_sc[...] = jnp.zeros(s_sc.shape, F32)

    @pl.when(t > 0)
    def _():
        xbuf[5:8, :] = xbuf[tg + 5:tg + 8, :]

    xbuf[8:8 + tg, :] = zc_ref[:, 0:QKV_DIM]
    y = cw_ref[0:1, :] * xbuf[5:5 + tg, :]
    for i in range(1, 4):
        y = y + cw_ref[i:i + 1, :] * xbuf[5 + i:5 + i + tg, :]
    y = _silu(y)

    zba = zba_ref[...]
    beta_col = jax.nn.sigmoid(zba)
    g_col = -jnp.exp(parr_ref[0:1, :]) * jax.nn.softplus(zba + parr_ref[1:2, :])
    g_row = -jnp.exp(parc_ref[:, 0:1]) * jax.nn.softplus(bat_ref[...] + parc_ref[:, 1:2])

    row = lax.broadcasted_iota(jnp.int32, (c, c), 0)
    col = lax.broadcasted_iota(jnp.int32, (c, c), 1)
    incl = col <= row
    strict = col < row
    lt = incl.astype(BF16)
    ut = (col >= row).astype(BF16)
    bones = bones_ref[...]
    bdm = bdm_ref[...]
    expb, expg = expb_ref[...], expg_ref[...]
    brow = lax.broadcasted_iota(jnp.int32, (BW, BW), 0)
    bcl = lax.broadcasted_iota(jnp.int32, (BW, BW), 1)
    same_head = (brow // HD) == (bcl // HD)
    incl_bd = same_head & ((bcl % HD) <= (brow % HD))
    strict_bd = same_head & ((bcl % HD) < (brow % HD))
    eye_bd = (brow == bcl).astype(F32)
    same_head2 = jnp.concatenate([same_head, same_head], axis=1)

    def stack4(a):
        return jnp.concatenate([a, a, a, a], axis=0)

    def fold4(a):
        return (a[0:c] + a[c:2 * c]) + (a[2 * c:3 * c] + a[3 * c:4 * c])

    nchunk = tg // c
    pre = []
    for ci in range(nchunk):
        sl = slice(ci * c, (ci + 1) * c)
        q, k, v = y[sl, 0:256], y[sl, 256:512], y[sl, 512:768]
        qn = q * lax.rsqrt(_bdot(q * q, bones) + EPS) * (HD ** -0.5)
        kn = k * lax.rsqrt(_bdot(k * k, bones) + EPS)
        bcol = beta_col[sl, :]
        gcum_c = _sel_dot(lt, g_col[sl, :], True)
        gcum_r = _sel_dot(ut, g_row[:, sl], False)
        gx = _sel_dot(expg, gcum_c, False)
        bx = _sel_dot(expb, bcol, False)
        egx = jnp.exp(gx)
        rhs = _split2(jnp.concatenate([bx * v, bx * egx * kn], axis=1))
        ks = jnp.where(same_head, stack4(kn), 0.0).astype(BF16)
        qs = jnp.where(same_head, stack4(qn), 0.0).astype(BF16)
        gc_s = jnp.concatenate([gcum_c[:, 4 + h:5 + h] for h in range(HEADS)], axis=0)
        gr_s = jnp.concatenate([gcum_r[4 + h:5 + h, :] for h in range(HEADS)], axis=1)
        beta_s = jnp.concatenate([bcol[:, h:h + 1] for h in range(HEADS)], axis=0)
        dm = jnp.exp(jnp.where(incl_bd, gc_s - gr_s, NEG))
        a = jnp.where(strict_bd, beta_s * lax.dot_general(ks, ks, NT, preferred_element_type=F32) * dm, 0.0)
        qk = (lax.dot_general(qs, ks, NT, preferred_element_type=F32) * dm).astype(BF16)
        pre.append(dict(sl=sl, qn=qn, kn=kn, gx=gx, egx=egx, rhs=rhs, qk=qk, ps=_split2(-a), tinv=eye_bd - a))

    for _ in range(5):
        for d in pre:
            d["ps"] = _split2(_dot3(d["ps"], d["ps"]))
        for d in pre:
            d["tinv"] = d["tinv"] + _dot3(_split2(d["tinv"]), d["ps"])
    for d in pre:
        rhs = d["rhs"]
        d["x"] = fold4(jnp.where(same_head2, _dot3(_split2(d["tinv"]), (stack4(rhs[0]), stack4(rhs[1]))), 0.0))

    for d in pre:
        sl, qn, kn, gx, egx, qk, x = d["sl"], d["qn"], d["kn"], d["gx"], d["egx"], d["qk"], d["x"]
        s = s_sc[...]
        sb = s.astype(BF16)
        u = x[:, :BW] - jnp.dot(x[:, BW:].astype(BF16), sb, preferred_element_type=F32)
        ub = u.astype(BF16)
        o = egx * jnp.dot(qn.astype(BF16), sb, preferred_element_type=F32)
        o = o + fold4(jnp.where(same_head, jnp.dot(qk, stack4(ub), preferred_element_type=F32), 0.0))
        glast = gx[c - 1:c, :]
        kf = (kn * jnp.exp(glast - gx)).astype(BF16)
        s_new = jnp.exp(glast) * s + lax.dot_general(kf, ub, TN, preferred_element_type=F32)
        s_sc[...] = s_new * bdm
        on = o * lax.rsqrt(_bdot(o * o, bones) * (1.0 / HD) + EPS) * gout_ref[...]
        o_ref[sl, :] = on * _silu(zc_ref[sl, QKV_DIM:QKV_DIM + BW])

    @pl.when(t == pl.num_programs(1) - 1)
    def _():
        sfin_ref[...] = s_sc[...]


def _gdn_prompt(zc, zba, bat, n, t, p, tg):
    nt = t // tg
    consts = [p["conv_w"], p["gdn_par_r"], p["gdn_par_c"], p["gdn_out_gx"], jnp.asarray(_BONES64, BF16), jnp.asarray(_EXPB, BF16),
              jnp.asarray(_EXPG, BF16), jnp.asarray(_BDMASK)]
    return pl.pallas_call(
        functools.partial(_gdn_kernel, tg=tg),
        grid=(n, nt),
        in_specs=[pl.BlockSpec((tg, 1024), lambda b, i: (b * nt + i, 0)), pl.BlockSpec((tg, 128), lambda b, i: (b * nt + i, 0)),
                  pl.BlockSpec((8, tg), lambda b, i: (0, b * nt + i))] + [_full(c.shape) for c in consts],
        out_specs=[pl.BlockSpec((tg, BW), lambda b, i: (b * nt + i, 0)), pl.BlockSpec((None, BW, BW), lambda b, i: (b, 0, 0))],
        out_shape=[jax.ShapeDtypeStruct((n * t, BW), F32), jax.ShapeDtypeStruct((n, BW, BW), F32)],
        scratch_shapes=[pltpu.VMEM((8 + tg, QKV_DIM), F32), pltpu.VMEM((BW, BW), F32)],
        compiler_params=_cp("parallel", "arbitrary"),
        name="gdn_prompt",
    )(zc, zba, bat, *consts)


def _sample_tok_kernel(za_ref, zc_ref, zba_ref, zm_ref, sconv_ref, cw_ref, gmg_ref, gmw_ref, gmb_ref, parr_ref, bones_ref,
                       memg_ref, bmean_ref, oa_ref, ov_ref, oconv_ref, oq_ref, ok_ref, ovv_ref, ozg_ref, obeta_ref, og_ref, omq_ref):
    ge = _gelu(za_ref[...])
    v = _rms(ge[:, BW:], gmg_ref[...])
    ov_ref[...] = v
    oa_ref[...] = ge[:, :BW] * (gmw_ref[...] * v + gmb_ref[...])
    sc = sconv_ref[...]
    x = zc_ref[:, 0:QKV_DIM]
    y = (cw_ref[0:1, :] * sc[:, 0:768] + cw_ref[1:2, :] * sc[:, 768:1536] + cw_ref[2:3, :] * sc[:, 1536:2304]
         + cw_ref[3:4, :] * x)
    oconv_ref[:, 0:1536] = sc[:, 768:2304]
    oconv_ref[:, 1536:2304] = x
    y = _silu(y)
    q, k = y[:, 0:256], y[:, 256:512]
    bones = bones_ref[...]
    oq_ref[...] = (q * lax.rsqrt(_bdot(q * q, bones) + EPS) * (HD ** -0.5)).T
    ok_ref[...] = (k * lax.rsqrt(_bdot(k * k, bones) + EPS)).T
    ovv_ref[...] = y[:, 512:768].T
    ozg_ref[...] = zc_ref[:, QKV_DIM:QKV_DIM + BW].T
    zba = zba_ref[...]
    obeta_ref[...] = jax.nn.sigmoid(zba).T
    og_ref[...] = (-jnp.exp(parr_ref[0:1, :]) * jax.nn.softplus(zba + parr_ref[1:2, :])).T
    mq = zm_ref[...]
    omq_ref[...] = mq * lax.rsqrt(_bdot(mq * mq, bmean_ref[...]) + EPS) * memg_ref[...]


def _sample_tok(za, zc, zba, zm, sconv, p):
    ns = za.shape[0]
    args = [za, zc, zba, zm, sconv, p["conv_w"], p["gm_norm_g"], p["gm_w0"], p["gm_b0"], p["gdn_par_r"],
            jnp.asarray(_BONES64, BF16), p["mem_qn_gx"], jnp.asarray(_BMEAN64, BF16)]
    shapes = [(ns, BW), (ns, BW), (ns, 2304), (BW, ns), (BW, ns), (BW, ns), (BW, ns), (128, ns), (128, ns), (ns, BW)]
    return pl.pallas_call(
        _sample_tok_kernel,
        in_specs=[_full(a.shape) for a in args],
        out_specs=[_full(s) for s in shapes],
        out_shape=[jax.ShapeDtypeStruct(s, F32) for s in shapes],
        grid=(1,),
        compiler_params=_cp("arbitrary"),
        name="sample_tok",
    )(*args)


def _gdn_step_kernel(s_ref, q_ref, k_ref, v_ref, beta_ref, g_ref, zg_ref, gout_ref, so_ref, o_ref, o_sc):
    q, k = q_ref[...], k_ref[...]
    eg = jnp.exp(g_ref[...])
    beta = beta_ref[...]
    qk = jnp.sum(q * k, axis=0, keepdims=True)
    ssq = jnp.zeros(qk.shape, F32)
    for v in range(HD):
        sv = s_ref[v]
        sk = jnp.sum(sv * k, axis=0, keepdims=True)
        sq = jnp.sum(sv * q, axis=0, keepdims=True)
        u = beta * (v_ref[v:v + 1, :] - eg * sk)
        o = eg * sq + qk * u
        so_ref[v] = eg * sv + u * k
        o_sc[v:v + 1, :] = o
        ssq = ssq + o * o
    o_ref[...] = o_sc[...] * lax.rsqrt(ssq * (1.0 / HD) + EPS) * gout_ref[...] * _silu(zg_ref[...])


def _gdn_step(layer, state_t, q, k, v, beta, g, zg, gout_col):
    ns = state_t.shape[-1]
    vec = pl.BlockSpec((None, HD, ns), lambda h: (h, 0, 0))
    sca = pl.BlockSpec((None, 1, ns), lambda h: (h, 0, 0))
    return pl.pallas_call(
        _gdn_step_kernel,
        grid=(HEADS,),
        in_specs=[pl.BlockSpec((None, None, HD, HD, ns), lambda h: (layer, h, 0, 0, 0)), vec, vec, vec, sca, sca, vec, _full((HD, 1))],
        out_specs=[pl.BlockSpec((None, HD, HD, ns), lambda h: (h, 0, 0, 0)), vec],
        out_shape=[jax.ShapeDtypeStruct((HEADS, HD, HD, ns), F32), jax.ShapeDtypeStruct((HEADS, HD, ns), F32)],
        scratch_shapes=[pltpu.VMEM((HD, ns), F32)],
        compiler_params=_cp("parallel"),
        name="gdn_step",
    )(state_t, q, k, v, beta, g, zg, gout_col)


def _mem_kv_kernel(x_ref, g_ref, w_ref, kg_ref, bmean_ref, ok_ref, ov_ref):
    kv = _bdot(_rms(x_ref[...], g_ref[...]), w_ref[...])
    k = kv[:, 0:BW]
    ok_ref[...] = k * lax.rsqrt(_bdot(k * k, bmean_ref[...]) + EPS) * kg_ref[...]
    ov_ref[...] = kv[:, BW:]


def _mem_kv(mem, g, w, kgx, tm):
    m = mem.shape[0]
    bmean = jnp.asarray(_BMEAN64, BF16)
    return pl.pallas_call(
        _mem_kv_kernel,
        grid=(m // tm,),
        in_specs=[pl.BlockSpec((tm, D_MODEL), lambda i: (i, 0)), _full((1, D_MODEL)), _full(w.shape), _full((1, BW)), _full((BW, BW))],
        out_specs=[pl.BlockSpec((tm, BW), lambda i: (i, 0))] * 2,
        out_shape=[jax.ShapeDtypeStruct((m, BW), F32)] * 2,
        compiler_params=_cp("parallel"),
        name="mem_kv",
    )(mem, g, w, kgx, bmean)


def _mem_attn_kernel(q_ref, k_ref, v_ref, gq_ref, bmean_ref, o_ref):
    q = q_ref[...]
    qn = q * lax.rsqrt(_bdot(q * q, bmean_ref[...]) + EPS) * gq_ref[...]
    kb = k_ref[...].astype(BF16)
    vb = v_ref[...].astype(BF16)
    lh = _lane_head()
    out = jnp.zeros(q.shape, F32)
    for h in range(HEADS):
        mh = lh == h
        s = lax.dot_general(jnp.where(mh, qn, 0.0).astype(BF16), kb, NT, preferred_element_type=F32) * (HD ** -0.5)
        e = jnp.exp(s - jnp.max(s, axis=-1, keepdims=True))
        pr = e / jnp.sum(e, axis=-1, keepdims=True)
        out = out + jnp.where(mh, jnp.dot(pr.astype(BF16), vb, preferred_element_type=F32), 0.0)
    o_ref[...] = out


def _mem_attn(zm, mk, mv, gqx, n, t, mt, tq):
    nq = t // tq
    bmean = jnp.asarray(_BMEAN64, BF16)
    return pl.pallas_call(
        _mem_attn_kernel,
        grid=(n, nq),
        in_specs=[pl.BlockSpec((tq, BW), lambda b, i: (b * nq + i, 0)), pl.BlockSpec((mt, BW), lambda b, i: (b, 0)),
                  pl.BlockSpec((mt, BW), lambda b, i: (b, 0)), _full((1, BW)), _full((BW, BW))],
        out_specs=pl.BlockSpec((tq, BW), lambda b, i: (b * nq + i, 0)),
        out_shape=jax.ShapeDtypeStruct((n * t, BW), F32),
        compiler_params=_cp("parallel", "parallel"),
        name="mem_attn",
    )(zm, mk, mv, gqx, bmean)


def _mem_attn_s_kernel(q_ref, k_ref, v_ref, ind_ref, o_ref, *, bn):
    ind = ind_ref[...]
    for i in range(bn):
        qbd = (ind * q_ref[i:i + 1, :]).astype(BF16)
        s = jnp.dot(qbd, k_ref[i].astype(BF16), preferred_element_type=F32) * (HD ** -0.5)
        e = jnp.exp(s - jnp.max(s, axis=-1, keepdims=True))
        pr = e / jnp.sum(e, axis=-1, keepdims=True)
        o8 = lax.dot_general(pr.astype(BF16), v_ref[i].astype(BF16), NT, preferred_element_type=F32)
        o_ref[i:i + 1, :] = jnp.sum(o8 * ind, axis=0, keepdims=True)


def _mem_attn_s(layer, mqn, cache_k, cache_v, bn):
    ns = mqn.shape[0]
    mt = cache_k.shape[3]
    kv_spec = pl.BlockSpec((None, bn, BW, mt), lambda i: (layer, i, 0, 0))
    return pl.pallas_call(
        functools.partial(_mem_attn_s_kernel, bn=bn),
        grid=(ns // bn,),
        in_specs=[pl.BlockSpec((bn, BW), lambda i: (i, 0)), kv_spec, kv_spec, _full((8, BW))],
        out_specs=pl.BlockSpec((bn, BW), lambda i: (i, 0)),
        out_shape=jax.ShapeDtypeStruct((ns, BW), F32),
        compiler_params=_cp("parallel"),
        name="mem_attn_s",
    )(mqn, cache_k, cache_v, jnp.asarray(_IND8))


def _merge_kernel(x_ref, a_ref, b_ref, c_ref, m_ref, g1_ref, wg_ref, wb_ref, wo_ref, g2_ref, wrh_ref, wrl_ref, br_ref,
                  x1_ref, h2_ref, ei_ref, ew_ref):
    x = x_ref[...]
    hb = _rms(x, g1_ref[...]).astype(BF16)
    acc = jnp.zeros(x.shape, F32)
    for b, br in enumerate((a_ref, b_ref, c_ref, m_ref)):
        gate = jax.nn.sigmoid(jnp.dot(hb, wg_ref[:, b * D_MODEL:(b + 1) * D_MODEL], preferred_element_type=F32))
        acc = acc + gate * jnp.dot(br[...].astype(BF16), wb_ref[b], preferred_element_type=F32)
    x1 = x + jnp.dot(acc.astype(BF16), wo_ref[...], preferred_element_type=F32)
    x1_ref[...] = x1
    h2 = _rms(x1, g2_ref[...])
    h2_ref[...] = h2
    logits = _dot3(_split2(h2), (wrh_ref[...], wrl_ref[...])) + br_ref[...]
    lane = lax.broadcasted_iota(jnp.int32, (1, 128), 1).astype(F32)
    big = 1e9
    lg = jnp.where(lane < N_GROUPS, logits, NEG)
    mg = jnp.max(lg, axis=-1, keepdims=True)
    g_w = 1.0 / jnp.sum(jnp.exp(lg - mg), axis=-1, keepdims=True)
    gi = jnp.min(jnp.where(lg == mg, lane, big), axis=-1, keepdims=True)
    sel = (lane >= N_GROUPS) & (lane < N_GROUPS + N_EXPERTS) & (jnp.floor((lane - N_GROUPS) * (1.0 / EPG)) == gi)
    le = jnp.where(sel, logits, NEG)
    m1 = jnp.max(le, axis=-1, keepdims=True)
    i1 = jnp.min(jnp.where(le == m1, lane, big), axis=-1, keepdims=True)
    le2 = jnp.where(lane == i1, NEG, le)
    m2 = jnp.max(le2, axis=-1, keepdims=True)
    i2 = jnp.min(jnp.where(le2 == m2, lane, big), axis=-1, keepdims=True)
    z = jnp.sum(jnp.exp(le - m1), axis=-1, keepdims=True)
    p1 = 1.0 / z
    p2 = jnp.exp(m2 - m1) / z
    w1 = p1 / (p1 + p2) * g_w
    w2 = p2 / (p1 + p2) * g_w
    ei_ref[...] = jnp.where(lane == 0, i1 - N_GROUPS, jnp.where(lane == 1, i2 - N_GROUPS, 0.0)).astype(jnp.int32)
    ew_ref[...] = jnp.where(lane == 0, w1, jnp.where(lane == 1, w2, 0.0))


def _merge(x, branches, p, tm):
    m = x.shape[0]
    consts = [p["norm1_g"], p["w_gate"], p["w_branch"], p["w_out"], p["norm2_g"], *_split2(p["w_router"]), p["b_router"]]
    tile = lambda w: pl.BlockSpec((tm, w), lambda i: (i, 0))
    return pl.pallas_call(
        _merge_kernel,
        grid=(m // tm,),
        in_specs=[tile(D_MODEL)] + [tile(BW)] * 4 + [_full(c.shape) for c in consts],
        out_specs=[tile(D_MODEL), tile(D_MODEL), tile(128), tile(128)],
        out_shape=[jax.ShapeDtypeStruct((m, D_MODEL), F32), jax.ShapeDtypeStruct((m, D_MODEL), F32),
                   jax.ShapeDtypeStruct((m, 128), jnp.int32), jax.ShapeDtypeStruct((m, 128), F32)],
        compiler_params=_cp("parallel"),
        name="merge",
    )(x, *branches, *consts)


def _dispatch_kernel(pos_ref, h_ref, xs_in, xs_out, sem, *, tmd):
    del xs_in

    def issue(i, carry):
        for s in range(2):
            pltpu.make_async_copy(h_ref.at[pl.ds(i, 1)], xs_out.at[pl.ds(pos_ref[2 * i + s], 1)], sem).start()
        return carry

    lax.fori_loop(0, tmd, issue, 0, unroll=8)
    for s in range(2):
        pltpu.make_async_copy(h_ref, xs_out.at[pl.ds(0, tmd)], sem).wait()


def _dispatch(pos, h2, rows, tmd):
    m = h2.shape[0]
    xs0 = jnp.zeros((rows, D_MODEL), F32)
    return pl.pallas_call(
        functools.partial(_dispatch_kernel, tmd=tmd),
        grid=(m // tmd,),
        in_specs=[pl.BlockSpec((2 * tmd,), lambda i: (i,), memory_space=pltpu.SMEM),
                  pl.BlockSpec((tmd, D_MODEL), lambda i: (i, 0)), pl.BlockSpec(memory_space=pl.ANY)],
        out_specs=pl.BlockSpec(memory_space=pl.ANY),
        out_shape=jax.ShapeDtypeStruct((rows, D_MODEL), F32),
        scratch_shapes=[pltpu.SemaphoreType.DMA(())],
        input_output_aliases={2: 0},
        compiler_params=_cp("arbitrary"),
        name="moe_dispatch",
    )(pos, h2, xs0)


def _expert_kernel(te_ref, nv_ref, x_ref, wg_ref, wu_ref, wd_ref, o_ref):
    del te_ref

    @pl.when(pl.program_id(0) < nv_ref[0])
    def _():
        xb = x_ref[...].astype(BF16)
        gt = jnp.dot(xb, wg_ref[...].astype(BF16), preferred_element_type=F32)
        up = jnp.dot(xb, wu_ref[...].astype(BF16), preferred_element_type=F32)
        o_ref[...] = jnp.dot((_silu(gt) * up).astype(BF16), wd_ref[...].astype(BF16), preferred_element_type=F32)

    @pl.when(pl.program_id(0) >= nv_ref[0])
    def _():
        o_ref[...] = jnp.zeros(o_ref.shape, F32)


def _experts(layer, tile_expert, n_valid, xs, w_gate, w_up, w_down, te):
    rows = xs.shape[0]

    def xmap(i, te_ref, nv_ref):
        return (jnp.minimum(i, nv_ref[0] - 1), 0)

    def wmap(i, te_ref, nv_ref):
        return (layer, te_ref[i], 0, 0)

    grid_spec = pltpu.PrefetchScalarGridSpec(
        num_scalar_prefetch=2,
        grid=(rows // te,),
        in_specs=[pl.BlockSpec((te, D_MODEL), xmap), pl.BlockSpec((None, None, D_MODEL, D_EXPERT), wmap),
                  pl.BlockSpec((None, None, D_MODEL, D_EXPERT), wmap), pl.BlockSpec((None, None, D_EXPERT, D_MODEL), wmap)],
        out_specs=pl.BlockSpec((te, D_MODEL), lambda i, te_ref, nv_ref: (i, 0)),
    )
    return pl.pallas_call(
        _expert_kernel,
        grid_spec=grid_spec,
        out_shape=jax.ShapeDtypeStruct((rows, D_MODEL), F32),
        compiler_params=_cp("arbitrary"),
        name="moe_experts",
    )(tile_expert, n_valid, xs, w_gate, w_up, w_down)


def _combine_kernel(pos_ref, x1_ref, ew_ref, ys_hbm, o_ref, r0, r1, sem, *, tmc):
    bufs = (r0, r1)

    def issue(i, carry):
        for s in range(2):
            pltpu.make_async_copy(ys_hbm.at[pl.ds(pos_ref[2 * i + s], 1)], bufs[s].at[pl.ds(i, 1)], sem).start()
        return carry

    lax.fori_loop(0, tmc, issue, 0, unroll=8)
    for s in range(2):
        pltpu.make_async_copy(ys_hbm.at[pl.ds(0, tmc)], bufs[s], sem).wait()
    ew = ew_ref[...]
    o_ref[...] = x1_ref[...] + ew[:, 0:1] * r0[...] + ew[:, 1:2] * r1[...]


def _combine(pos, x1, ew, ys, tmc):
    m = x1.shape[0]
    return pl.pallas_call(
        functools.partial(_combine_kernel, tmc=tmc),
        grid=(m // tmc,),
        in_specs=[pl.BlockSpec((2 * tmc,), lambda i: (i,), memory_space=pltpu.SMEM),
                  pl.BlockSpec((tmc, D_MODEL), lambda i: (i, 0)), pl.BlockSpec((tmc, 128), lambda i: (i, 0)),
                  pl.BlockSpec(memory_space=pl.ANY)],
        out_specs=pl.BlockSpec((tmc, D_MODEL), lambda i: (i, 0)),
        out_shape=jax.ShapeDtypeStruct((m, D_MODEL), F32),
        scratch_shapes=[pltpu.VMEM((tmc, D_MODEL), F32), pltpu.VMEM((tmc, D_MODEL), F32), pltpu.SemaphoreType.DMA(())],
        compiler_params=_cp("arbitrary"),
        name="moe_combine",
    )(pos, x1, ew, ys)


def _moe(layer, x1, h2, ei, ew, w_gate, w_up, w_down, te, tmd):
    m = x1.shape[0]
    flat_e = ei[:, 0:2].reshape(-1)
    onehot = (flat_e[:, None] == jnp.arange(N_EXPERTS, dtype=jnp.int32)[None, :]).astype(jnp.int32)
    csum = jnp.cumsum(onehot, axis=0)
    rank = jnp.sum(csum * onehot, axis=1) - 1
    counts = csum[-1]
    padded = ((counts + te - 1) // te) * te
    pend = jnp.cumsum(padded)
    pstart = pend - padded
    pos = (jnp.sum(onehot * pstart[None, :], axis=1) + rank).astype(jnp.int32)
    rows = ((2 * m + N_EXPERTS * (te - 1)) // te) * te
    n_tiles = rows // te
    n_valid = (pend[-1] // te).astype(jnp.int32).reshape(1)
    tile_start = jnp.arange(n_tiles, dtype=jnp.int32) * te
    tile_expert = jnp.minimum(jnp.sum((tile_start[:, None] >= pend[None, :]).astype(jnp.int32), axis=1), N_EXPERTS - 1)
    last_e = jnp.take(tile_expert, jnp.maximum(n_valid[0] - 1, 0))
    tile_expert = jnp.where(jnp.arange(n_tiles) < n_valid[0], tile_expert, last_e).astype(jnp.int32)
    xs = _dispatch(pos, h2, rows, tmd)
    ys = _experts(layer, tile_expert, n_valid, xs, w_gate, w_up, w_down, te)
    return _combine(pos, x1, ew, ys, tmd)


def _tile4(v):
    return jnp.tile(v, HEADS).reshape(1, BW)


def _prep_layer(l, w):
    w_in = w["w_in"][l]
    z = lambda n: jnp.zeros((D_MODEL, n), F32)
    b_al = w_in[:, 1952:1960]
    w_small = jnp.concatenate(
        [w_in[:, 0:512], w_in[:, 512:768], w_in[:, 768:896], z(64), w_in[:, 896:928], z(32), w_in[:, 928:1696],
         w_in[:, 1696:1952], w_in[:, 1960:2216], b_al, z(120)], axis=1).astype(BF16)
    uq = w["mla_w_uq"][l]
    w_uq = jnp.pad(uq, ((0, 0), (0, 0), (0, 32))).reshape(256, 512).astype(BF16)
    qg = jnp.tile(jnp.concatenate([w["mla_qn_g"][l], w["mla_qr_g"][l], jnp.zeros((32,), F32)]), HEADS).reshape(1, 512)
    uk = w["mla_w_uk"][l]
    w_uk_p = jnp.pad(uk, ((0, 0), (0, 0), (0, 64))).reshape(128, 512).astype(BF16)
    kg = jnp.tile(jnp.concatenate([w["mla_kn_g"][l], jnp.zeros((64,), F32)]), HEADS).reshape(1, 512)
    kr_g = jnp.concatenate([jnp.zeros((64,), F32), w["mla_kr_g"][l], jnp.zeros((32,), F32)]).reshape(1, 128)
    uv = w["mla_w_uv"][l]
    w_uv_p = jnp.stack([jnp.pad(uv[:, h, :], ((0, 0), (64 * h, BW - 64 * h - 64))) for h in range(HEADS)]).astype(BF16)
    par_r = jnp.zeros((8, 128), F32).at[0, 4:8].set(w["gdn_a_log"][l]).at[1, 4:8].set(w["gdn_dt_bias"][l])
    par_c = jnp.zeros((8, 128), F32).at[4:8, 0].set(w["gdn_a_log"][l]).at[4:8, 1].set(w["gdn_dt_bias"][l])
    w_router = jnp.concatenate([w["moe_wg"][l], w["moe_we"][l], jnp.zeros((D_MODEL, 128 - 36), F32)], axis=1)
    b_router = jnp.concatenate([w["moe_bg"][l], w["moe_be"][l], jnp.zeros((128 - 36,), F32)]).reshape(1, 128)
    return {
        "norm1_g": w["norm1_g"][l].reshape(1, D_MODEL), "w_small": w_small, "w_bat": b_al.T.astype(BF16),
        "w_gate": w_in[:, 2216:].astype(BF16),
        "gm_norm_g": w["gm_norm_g"][l].reshape(1, BW), "gm_ws": w["gm_ws"][l],
        "gm_bfull": jnp.repeat(w["gm_b"][l].T, HD, axis=1),
        "gm_w0": jnp.repeat(w["gm_ws"][l][:, 0, 0], HD).reshape(1, BW), "gm_b0": jnp.repeat(w["gm_b"][l][:, 0], HD).reshape(1, BW),
        "mla_cq_g": w["mla_cq_g"][l].reshape(1, 256), "w_uq": w_uq, "qg": qg, "mla_ckv_g": w["mla_ckv_g"][l].reshape(1, 128),
        "kr_g": kr_g, "w_uk": w_uk_p, "kg": kg, "w_uv": w_uv_p,
        "w_uk_c": uk.reshape(128, BW).astype(BF16), "w_uv_c": uv.reshape(128, BW).astype(BF16), "kn_gx": _tile4(w["mla_kn_g"][l]),
        "conv_w": w["gdn_conv_w"][l], "gdn_par_r": par_r, "gdn_par_c": par_c, "gdn_out_gx": _tile4(w["gdn_out_g"][l]),
        "gdn_out_gc": w["gdn_out_g"][l].reshape(HD, 1),
        "mem_norm_g": w["mem_norm_g"][l].reshape(1, D_MODEL), "mem_w_kv": w["mem_w_kv"][l].astype(BF16),
        "mem_qn_gx": _tile4(w["mem_qn_g"][l]), "mem_kn_gx": _tile4(w["mem_kn_g"][l]),
        "w_branch": w["w_branch"][l].astype(BF16), "w_out": w["w_out"][l].astype(BF16),
        "norm2_g": w["norm2_g"][l].reshape(1, D_MODEL), "w_router": w_router, "b_router": b_router,
    }


def _rope_tables(pos):
    half = MLA_ROPE // 2
    inv = ROPE_THETA ** (-jnp.arange(half, dtype=F32) / half)
    ang = pos.astype(F32)[:, None] * inv[None, :]
    cos, sin = jnp.cos(ang), jnp.sin(ang)
    t = pos.shape[0]
    one, zero = jnp.ones((t, 64), F32), jnp.zeros((t, 64), F32)
    z16, z32 = jnp.zeros((t, 16), F32), jnp.zeros((t, 32), F32)
    return (jnp.concatenate([one, cos, cos, jnp.ones((t, 32), F32)], axis=1),
            jnp.concatenate([zero, -sin, z16, z32], axis=1),
            jnp.concatenate([zero, z16, sin, z32], axis=1))


def kernel(x_prompt, mem_prompt, x_sample, cache_mla_ckv, cache_mla_kr, cache_mem_k, cache_mem_v, state_gdn, state_conv,
           page_table, norm1_g, w_in, gm_norm_g, gm_ws, gm_b, mla_cq_g, mla_w_uq, mla_qn_g, mla_qr_g, mla_ckv_g, mla_kr_g,
           mla_w_uk, mla_kn_g, mla_w_uv, gdn_conv_w, gdn_a_log, gdn_dt_bias, gdn_out_g, mem_norm_g, mem_w_kv, mem_qn_g,
           mem_kn_g, w_branch, w_out, norm2_g, moe_wg, moe_bg, moe_we, moe_be, moe_w_gate, moe_w_up, moe_w_down):
    w = dict(norm1_g=norm1_g, w_in=w_in, gm_norm_g=gm_norm_g, gm_ws=gm_ws, gm_b=gm_b, mla_cq_g=mla_cq_g, mla_w_uq=mla_w_uq,
             mla_qn_g=mla_qn_g, mla_qr_g=mla_qr_g, mla_ckv_g=mla_ckv_g, mla_kr_g=mla_kr_g, mla_w_uk=mla_w_uk, mla_kn_g=mla_kn_g,
             mla_w_uv=mla_w_uv, gdn_conv_w=gdn_conv_w, gdn_a_log=gdn_a_log, gdn_dt_bias=gdn_dt_bias, gdn_out_g=gdn_out_g,
             mem_norm_g=mem_norm_g, mem_w_kv=mem_w_kv, mem_qn_g=mem_qn_g, mem_kn_g=mem_kn_g, w_branch=w_branch, w_out=w_out,
             norm2_g=norm2_g, moe_wg=moe_wg, moe_bg=moe_bg, moe_we=moe_we, moe_be=moe_be)
    depth = w_in.shape[0]
    bp, tp, _ = x_prompt.shape
    bs = x_sample.shape[0]
    mt = mem_prompt.shape[1]
    n_pages = page_table.shape[1]
    past_len = n_pages * cache_mla_ckv.shape[2]
    mp = bp * tp

    tm_p = min(512, mp)
    tq = min(256, tp)
    tg = min(256, tp)
    pp = min(32, n_pages)
    cache_krt = jnp.swapaxes(cache_mla_kr, 2, 3)
    tabs_p = _rope_tables(jnp.arange(tp, dtype=jnp.int32))
    tabs_s = _rope_tables(jnp.full((bs,), past_len, jnp.int32))

    xp = x_prompt.reshape(mp, D_MODEL)
    xs = x_sample.reshape(bs, D_MODEL)
    mem = mem_prompt.reshape(bp * mt, D_MODEL)
    cache_k = cache_mem_k.transpose(0, 1, 3, 4, 2).reshape(depth, bs, BW, mt)
    cache_v = cache_mem_v.transpose(0, 1, 3, 4, 2).reshape(depth, bs, BW, mt)
    state_t = state_gdn.transpose(0, 2, 3, 4, 1)
    rows_p, rows_s = [], []
    for l in range(depth):
        p = _prep_layer(l, w)
        mk, mv = _mem_kv(mem, p["mem_norm_g"], p["mem_w_kv"], p["mem_kn_gx"], min(512, bp * mt))
        za, zb, zc, zm, zba, bat = _in_proj(xp, p["norm1_g"], p["w_small"], p["w_bat"], tm_p)
        a_out = _gmlp(za, mp, p["gm_norm_g"], p["gm_ws"], p["gm_bfull"], tm_p)
        q4, k4, ckv, kr, ckv16, _ = _mla_pre(zb, mp, tabs_p, tp // min(tm_p, tp), p, min(tm_p, tp))
        b_out = _mla_attn(q4, k4, ckv16, p["w_uv"], bp, tp, tq)
        c_out, sfin = _gdn_prompt(zc, zba, bat, bp, tp, p, tg)
        m_out = _mem_attn(zm, mk, mv, p["mem_qn_gx"], bp, tp, mt, tq)
        x1, h2, ei, ew = _merge(xp, (a_out, b_out, c_out, m_out), p, tm_p)
        xp = _moe(l, x1, h2, ei, ew, moe_w_gate, moe_w_up, moe_w_down, min(256, mp), min(512, mp))
        s_p = jnp.stack([sfin[:, 64 * h:64 * h + 64, 64 * h:64 * h + 64] for h in range(HEADS)], axis=1).transpose(0, 1, 3, 2)
        conv_p = zc.reshape(bp, tp, 1024)[:, tp - 3:, 0:QKV_DIM]
        rows_p.append((ckv.reshape(bp, tp, 128), kr.reshape(bp, tp, MLA_ROPE), s_p, conv_p,
                       mk.reshape(bp, mt, HEADS, HD), mv.reshape(bp, mt, HEADS, HD)))
        za, zb, zc, zm, zba, _ = _in_proj(xs, p["norm1_g"], p["w_small"], p["w_bat"], bs)
        a_s, v_s, conv_s, gq, gk, gv, zg, beta, gdec, mqn = _sample_tok(za, zc, zba, zm, state_conv[l].reshape(bs, 3 * QKV_DIM), p)
        _, _, ckv_s, kr_s, _, q32 = _mla_pre(zb, bs, tabs_s, 1, p, bs)
        q3 = q32.reshape(bs, HEADS, 128)
        qk = q3[:, :, 0:64].reshape(bs, 1, BW)
        qr8 = jnp.pad(q3[:, :, 64:96], ((0, 0), (0, 4), (0, 0)))
        b_s = _mla_decode(l, page_table, cache_mla_ckv, cache_krt, qk, qr8, ckv_s.reshape(bs, 1, 128),
                          kr_s.reshape(bs, 1, MLA_ROPE), p["kn_gx"], p["w_uk_c"], p["w_uv_c"], pp).reshape(bs, BW)
        hv = lambda a: a.reshape(HEADS, HD, bs)
        s_new, c_t = _gdn_step(l, state_t, hv(gq), hv(gk), hv(gv), beta[0:4].reshape(HEADS, 1, bs),
                               gdec[4:8].reshape(HEADS, 1, bs), hv(zg), p["gdn_out_gc"])
        m_s = _mem_attn_s(l, mqn, cache_k, cache_v, min(8, bs))
        x1, h2, ei, ew = _merge(xs, (a_s, b_s, c_t.reshape(BW, bs).T, m_s), p, bs)
        xs = _moe(l, x1, h2, ei, ew, moe_w_gate, moe_w_up, moe_w_down, min(32, bs), bs)
        rows_s.append((ckv_s.reshape(bs, 1, 128), kr_s.reshape(bs, 1, MLA_ROPE), s_new.transpose(3, 0, 1, 2),
                       conv_s.reshape(bs, 3, QKV_DIM), v_s.reshape(bs, 1, BW)))
    p_out = [jnp.stack(a) for a in zip(*rows_p)]
    s_out = [jnp.stack(a) for a in zip(*rows_s)]
    return (xp.reshape(bp, tp, D_MODEL), xs.reshape(bs, 1, D_MODEL), *p_out, *s_out)
```

```python
import functools

import numpy as np
import jax
import jax.numpy as jnp
from jax import lax
from jax.experimental import pallas as pl
from jax.experimental.pallas import tpu as pltpu

F32 = jnp.float32
BF16 = jnp.bfloat16
HI = lax.Precision.HIGHEST
EPS = 1e-6
NEG = float("-inf")

D_MODEL = 1024
HEADS = 4
HD = 64
BW = 256
MLA_ROPE = 32
MLA_KV_RANK = 128
MLA_SCALE = 96.0 ** -0.5
ROPE_THETA = 10000.0
GM_CHUNK = 128
GDN_CHUNK = 64
QKV_DIM = 768
N_GROUPS = 4
EPG = 8
N_EXPERTS = 32
D_EXPERT = 256
PAGE = 128
VMEM_LIMIT = 56 * 1024 * 1024

NT = (((1,), (1,)), ((), ()))
TN = (((0,), (0,)), ((), ()))


def _cp(*sem):
    return pltpu.CompilerParams(dimension_semantics=sem, vmem_limit_bytes=VMEM_LIMIT)


def _rms(x, g):
    ms = jnp.sum(x * x, axis=-1, keepdims=True) * (1.0 / x.shape[-1])
    return x * lax.rsqrt(ms + EPS) * g


def _bdot(a, b):
    return jnp.dot(a.astype(BF16), b.astype(BF16), preferred_element_type=F32)


def _bdot_nt(a, b):
    return lax.dot_general(a.astype(BF16), b.astype(BF16), NT, preferred_element_type=F32)


def _hdot(a, b):
    return jnp.dot(a, b, precision=HI, preferred_element_type=F32)


def _split2(a):
    hi = a.astype(BF16)
    return hi, (a - hi.astype(F32)).astype(BF16)


def _dot3(a, b):
    d = lambda x, y: jnp.dot(x, y, preferred_element_type=F32)
    return d(a[0], b[0]) + (d(a[0], b[1]) + d(a[1], b[0]))


def _sel_dot(w01, x, left):
    x0 = x.astype(BF16)
    r1 = x - x0.astype(F32)
    x1 = r1.astype(BF16)
    x2 = (r1 - x1.astype(F32)).astype(BF16)
    d = (lambda p: jnp.dot(w01, p, preferred_element_type=F32)) if left else (lambda p: jnp.dot(p, w01, preferred_element_type=F32))
    return d(x0) + (d(x1) + d(x2))


def _silu(x):
    return x * jax.nn.sigmoid(x)


def _gelu(x):
    return 0.5 * x * (1.0 + lax.erf(x * 0.7071067811865476))


def _lane_head(width=BW):
    return lax.broadcasted_iota(jnp.int32, (1, width), 1) // HD


def _full(shape):
    n = len(shape)
    return pl.BlockSpec(shape, lambda *_: (0,) * n)


def _block_mean(width, segs):
    m = np.zeros((width, width), np.float32)
    for a, b in segs:
        m[a:b, a:b] = 1.0 / (b - a)
    return m


_BMEAN64 = _block_mean(BW, [(64 * h, 64 * h + 64) for h in range(HEADS)])
_BONES64 = _BMEAN64 * 64.0
_BQ = _block_mean(512, [(128 * h, 128 * h + 64) for h in range(HEADS)] + [(128 * h + 64, 128 * h + 96) for h in range(HEADS)])
_BK = _block_mean(512, [(128 * h, 128 * h + 64) for h in range(HEADS)])
_IND8 = np.zeros((8, BW), np.float32)
for _h in range(HEADS):
    _IND8[_h, 64 * _h:64 * _h + 64] = 1.0
_EXPB = np.zeros((128, BW), np.float32)
_EXPG = np.zeros((128, BW), np.float32)
for _h in range(HEADS):
    _EXPB[_h, 64 * _h:64 * _h + 64] = 1.0
    _EXPG[4 + _h, 64 * _h:64 * _h + 64] = 1.0
_BDMASK = (_BONES64 > 0).astype(np.float32)


def _in_proj_kernel(x_ref, g_ref, w_ref, wbat_ref, oa_ref, ob_ref, oc_ref, om_ref, oba_ref, obat_ref):
    hb = _rms(x_ref[...], g_ref[...]).astype(BF16)
    oa_ref[...] = jnp.dot(hb, w_ref[:, 0:512], preferred_element_type=F32)
    ob_ref[...] = jnp.dot(hb, w_ref[:, 512:1024], preferred_element_type=F32)
    oc_ref[...] = jnp.dot(hb, w_ref[:, 1024:2048], preferred_element_type=F32)
    om_ref[...] = jnp.dot(hb, w_ref[:, 2048:2304], preferred_element_type=F32)
    oba_ref[...] = jnp.dot(hb, w_ref[:, 2304:2432], preferred_element_type=F32)
    obat_ref[...] = lax.dot_general(wbat_ref[...], hb, NT, preferred_element_type=F32)


def _in_proj(x, g, w, wbat, tm):
    m = x.shape[0]
    widths = (512, 512, 1024, 256, 128)
    return pl.pallas_call(
        _in_proj_kernel,
        grid=(m // tm,),
        in_specs=[pl.BlockSpec((tm, D_MODEL), lambda i: (i, 0)), _full((1, D_MODEL)), _full(w.shape), _full(wbat.shape)],
        out_specs=[pl.BlockSpec((tm, n), lambda i: (i, 0)) for n in widths] + [pl.BlockSpec((8, tm), lambda i: (0, i))],
        out_shape=[jax.ShapeDtypeStruct((m, n), F32) for n in widths] + [jax.ShapeDtypeStruct((8, m), F32)],
        compiler_params=_cp("parallel"),
        name="in_proj",
    )(x, g, w, wbat)


def _gmlp_kernel(z_ref, g_ref, ws_ref, b_ref, o_ref, *, ta):
    row = lax.broadcasted_iota(jnp.int32, (GM_CHUNK, GM_CHUNK), 0)
    col = lax.broadcasted_iota(jnp.int32, (GM_CHUNK, GM_CHUNK), 1)
    tril = col <= row
    lh = _lane_head()
    wts = [jnp.where(tril, ws_ref[g], 0.0).astype(BF16) for g in range(HEADS)]
    for c in range(ta // GM_CHUNK):
        sl = slice(c * GM_CHUNK, (c + 1) * GM_CHUNK)
        ge = _gelu(z_ref[sl, :])
        u = ge[:, :BW]
        vb = _rms(ge[:, BW:], g_ref[...]).astype(BF16)
        s = b_ref[...]
        for g in range(HEADS):
            s = s + jnp.where(lh == g, jnp.dot(wts[g], vb, preferred_element_type=F32), 0.0)
        o_ref[sl, :] = u * s


def _gmlp(za, m, g, ws, bfull, ta):
    return pl.pallas_call(
        functools.partial(_gmlp_kernel, ta=ta),
        grid=(m // ta,),
        in_specs=[pl.BlockSpec((ta, 512), lambda i: (i, 0)), _full((1, BW)), _full(ws.shape), _full(bfull.shape)],
        out_specs=pl.BlockSpec((ta, BW), lambda i: (i, 0)),
        out_shape=jax.ShapeDtypeStruct((m, BW), F32),
        compiler_params=_cp("parallel"),
        name="gmlp",
    )(za, g, ws, bfull)


def _mla_pre_kernel(z_ref, c_ref, s1_ref, s2_ref, gcq_ref, wuq_ref, qg_ref, bq_ref, gckv_ref, gkr_ref, wuk_ref, kg_ref,
                    bk_ref, oq_ref, ok_ref, ockv_ref, okr_ref, ockvt_ref, oq32_ref, *, tb):
    cs, s1, s2 = c_ref[...], s1_ref[...], s2_ref[...]

    def rope(x):
        return x * cs + pltpu.roll(x, 112, 1) * s1 + pltpu.roll(x, 16, 1) * s2

    z = z_ref[...]
    cq = _rms(z[:, 0:256], gcq_ref[...])
    q = _bdot(cq, wuq_ref[...])
    qn = q * lax.rsqrt(_bdot(q * q, bq_ref[...]) + EPS) * qg_ref[...]
    ckv = _rms(z[:, 256:384], gckv_ref[...])
    ockv_ref[...] = ckv
    for c in range(ckv.shape[0] // tb):
        ockvt_ref[c] = ckv[c * tb:(c + 1) * tb, :].T.astype(BF16)
    krb = z[:, 384:512]
    kr = rope(krb * lax.rsqrt(jnp.sum(krb * krb, axis=-1, keepdims=True) * (1.0 / MLA_ROPE) + EPS) * gkr_ref[...])
    okr_ref[...] = kr[:, 64:96]
    k = _bdot(ckv, wuk_ref[...])
    kn = k * lax.rsqrt(_bdot(k * k, bk_ref[...]) + EPS) * kg_ref[...]
    for h in range(HEADS):
        sl = slice(128 * h, 128 * h + 128)
        qh = rope(qn[:, sl])
        oq_ref[h] = qh.astype(BF16)
        oq32_ref[:, sl] = qh
        ok_ref[h] = (kn[:, sl] + kr).astype(BF16)


def _mla_pre(zb, m, tabs, t_blocks, p, tm, tb):
    cs, s1, s2 = tabs
    tab_spec = pl.BlockSpec((tm, 128), lambda i: (i % t_blocks, 0))
    consts = [p["mla_cq_g"], p["w_uq"], p["qg"], jnp.asarray(_BQ, BF16), p["mla_ckv_g"], p["kr_g"], p["w_uk"], p["kg"],
              jnp.asarray(_BK, BF16)]
    return pl.pallas_call(
        functools.partial(_mla_pre_kernel, tb=tb),
        grid=(m // tm,),
        in_specs=[pl.BlockSpec((tm, 512), lambda i: (i, 0)), tab_spec, tab_spec, tab_spec] + [_full(c.shape) for c in consts],
        out_specs=[pl.BlockSpec((HEADS, tm, 128), lambda i: (0, i, 0)), pl.BlockSpec((HEADS, tm, 128), lambda i: (0, i, 0)),
                   pl.BlockSpec((tm, 128), lambda i: (i, 0)), pl.BlockSpec((tm, MLA_ROPE), lambda i: (i, 0)),
                   pl.BlockSpec((tm // tb, 128, tb), lambda i: (i, 0, 0)), pl.BlockSpec((tm, 512), lambda i: (i, 0))],
        out_shape=[jax.ShapeDtypeStruct((HEADS, m, 128), BF16), jax.ShapeDtypeStruct((HEADS, m, 128), BF16),
                   jax.ShapeDtypeStruct((m, 128), F32), jax.ShapeDtypeStruct((m, MLA_ROPE), F32),
                   jax.ShapeDtypeStruct((m // tb, 128, tb), BF16), jax.ShapeDtypeStruct((m, 512), F32)],
        compiler_params=_cp("parallel"),
        name="mla_pre",
    )(zb, cs, s1, s2, *consts)


def _mla_attn_kernel(q_ref, k_ref, v_ref, wuv_ref, o_ref, *, tq):
    i = pl.program_id(1)
    row = lax.broadcasted_iota(jnp.int32, (tq, tq), 0)
    col = lax.broadcasted_iota(jnp.int32, (tq, tq), 1)
    causal = row <= col

    def step(j, carry, mask):
        off = pl.multiple_of(j * tq, tq)
        vt = v_ref[j]
        ss = [lax.dot_general(k_ref[h, pl.ds(off, tq), :], q_ref[h], NT, preferred_element_type=F32) for h in range(HEADS)]
        stats = []
        for h in range(HEADS):
            m, l, _ = carry[h]
            s = ss[h] * MLA_SCALE
            if mask:
                s = jnp.where(causal, s, NEG)
            mn = jnp.maximum(m, jnp.max(s, axis=0, keepdims=True))
            pr = jnp.exp(s - mn)
            al = jnp.exp(m - mn)
            stats.append((mn, al * l + jnp.sum(pr, axis=0, keepdims=True), al, pr.astype(BF16)))
        pvs = [jnp.dot(vt, stats[h][3], preferred_element_type=F32) for h in range(HEADS)]
        return tuple((stats[h][0], stats[h][1], stats[h][2] * carry[h][2] + pvs[h]) for h in range(HEADS))

    init = tuple((jnp.full((1, tq), NEG, F32), jnp.zeros((1, tq), F32), jnp.zeros((MLA_KV_RANK, tq), F32)) for _ in range(HEADS))
    carry = lax.fori_loop(0, i, lambda j, c: step(j, c, False), init)
    carry = step(i, carry, True)
    out = jnp.zeros((tq, BW), F32)
    for h in range(HEADS):
        m, l, acc = carry[h]
        out = out + lax.dot_general((acc / l).astype(BF16), wuv_ref[h], TN, preferred_element_type=F32)
    o_ref[...] = out


def _mla_attn(q4, k4, ckvt, wuv, n, t, tq):
    nq = t // tq
    return pl.pallas_call(
        functools.partial(_mla_attn_kernel, tq=tq),
        grid=(n, nq),
        in_specs=[pl.BlockSpec((HEADS, tq, 128), lambda b, i: (0, b * nq + i, 0)),
                  pl.BlockSpec((HEADS, t, 128), lambda b, i: (0, b, 0)),
                  pl.BlockSpec((nq, MLA_KV_RANK, tq), lambda b, i: (b, 0, 0)), _full(wuv.shape)],
        out_specs=pl.BlockSpec((tq, BW), lambda b, i: (b * nq + i, 0)),
        out_shape=jax.ShapeDtypeStruct((n * t, BW), F32),
        compiler_params=_cp("parallel", "arbitrary"),
        name="mla_attn",
    )(q4, k4, ckvt, wuv)


def _mla_decode_kernel(pt_ref, qk_ref, qr_ref, cnew_ref, krnew_ref, kng_ref, wuk_ref, wuv_ref, ind_ref, ckv_hbm, krt_hbm, o_ref,
                       cbuf, kbuf, sem, *, layer, npages, pp):
    n = pl.program_id(0)
    ngroups = npages // pp

    def page_copies(sample, grp, slot):
        base = sample * npages + grp * pp
        out = []
        for i in range(pp):
            page = pt_ref[base + i]
            out.append(pltpu.make_async_copy(ckv_hbm.at[layer, page], cbuf.at[slot, pl.ds(i * PAGE, PAGE)], sem.at[slot]))
            out.append(pltpu.make_async_copy(krt_hbm.at[layer, page], kbuf.at[slot, :, pl.ds(i * PAGE, PAGE)], sem.at[slot]))
        return out

    @pl.when(n == 0)
    def _():
        for cp in page_copies(0, 0, 0):
            cp.start()

    ind = ind_ref[...]
    qbd = (ind * (qk_ref[...] * kng_ref[...])).astype(BF16)
    ind64 = (ind * (1.0 / HD)).astype(BF16)
    qr = qr_ref[...].astype(BF16)
    wuk = wuk_ref[...]
    qabs = lax.dot_general(qbd, wuk, NT, preferred_element_type=F32).astype(BF16)

    m = jnp.full((8, 1), NEG, F32)
    l = jnp.zeros((8, 1), F32)
    acc = jnp.zeros((8, MLA_KV_RANK), F32)
    for grp in range(ngroups):
        slot = (n * ngroups + grp) % 2
        if grp + 1 < ngroups:
            for cp in page_copies(n, grp + 1, 1 - slot):
                cp.start()
        else:
            @pl.when(n + 1 < pl.num_programs(0))
            def _():
                for cp in page_copies(n + 1, 0, 1 - slot):
                    cp.start()
        for cp in page_copies(n, grp, slot):
            cp.wait()
        cb = cbuf[slot].astype(BF16)
        krt = kbuf[slot].astype(BF16)
        k = jnp.dot(cb, wuk, preferred_element_type=F32)
        num = lax.dot_general(qabs, cb, NT, preferred_element_type=F32)
        ms = lax.dot_general(ind64, (k * k).astype(BF16), NT, preferred_element_type=F32)
        s = (num * lax.rsqrt(ms + EPS) + jnp.dot(qr, krt, preferred_element_type=F32)) * MLA_SCALE
        mn = jnp.maximum(m, jnp.max(s, axis=-1, keepdims=True))
        pb = jnp.exp(s - mn)
        al = jnp.exp(m - mn)
        l = al * l + jnp.sum(pb, axis=-1, keepdims=True)
        acc = al * acc + jnp.dot(pb.astype(BF16), cb, preferred_element_type=F32)
        m = mn

    cb = jnp.broadcast_to(cnew_ref[...], (8, MLA_KV_RANK)).astype(BF16)
    k1 = jnp.dot(cb, wuk, preferred_element_type=F32)
    kk1 = (k1 * k1).astype(BF16).astype(F32)
    krn = krnew_ref[...].astype(BF16).astype(F32)
    num1 = jnp.sum(qabs.astype(F32) * cb.astype(F32), axis=-1, keepdims=True)
    ms1 = jnp.sum(ind64.astype(F32) * kk1, axis=-1, keepdims=True)
    s1 = (num1 * lax.rsqrt(ms1 + EPS) + jnp.sum(qr.astype(F32) * krn, axis=-1, keepdims=True)) * MLA_SCALE
    mn1 = jnp.maximum(m, s1)
    p1 = jnp.exp(s1 - mn1)
    al1 = jnp.exp(m - mn1)
    lat = (al1 * acc + p1 * cb.astype(F32)) / (al1 * l + p1)
    o8 = jnp.dot(lat.astype(BF16), wuv_ref[...], preferred_element_type=F32)
    o_ref[...] = jnp.sum(o8 * ind, axis=0, keepdims=True)


def _mla_decode(layer, page_table, cache_ckv, cache_krt, qk, qr8, cnew, krnew, kng, wuk, wuv, pp):
    ns, npages = page_table.shape
    pt = page_table.reshape(-1)

    def per_sample(shape):
        return pl.BlockSpec((None,) + shape, lambda n, pt_ref: (n, 0, 0))

    def const(a):
        nd = a.ndim
        return pl.BlockSpec(a.shape, lambda n, pt_ref: (0,) * nd)

    ind = jnp.asarray(_IND8)
    hbm = pl.BlockSpec(memory_space=pl.ANY)
    grid_spec = pltpu.PrefetchScalarGridSpec(
        num_scalar_prefetch=1,
        grid=(ns,),
        in_specs=[per_sample((1, BW)), per_sample((8, MLA_ROPE)), per_sample((1, MLA_KV_RANK)), per_sample((1, MLA_ROPE)),
                  const(kng), const(wuk), const(wuv), const(ind), hbm, hbm],
        out_specs=per_sample((1, BW)),
        scratch_shapes=[pltpu.VMEM((2, pp * PAGE, MLA_KV_RANK), F32), pltpu.VMEM((2, MLA_ROPE, pp * PAGE), F32),
                        pltpu.SemaphoreType.DMA((2,))],
    )
    return pl.pallas_call(
        functools.partial(_mla_decode_kernel, layer=layer, npages=npages, pp=pp),
        grid_spec=grid_spec,
        out_shape=jax.ShapeDtypeStruct((ns, 1, BW), F32),
        compiler_params=_cp("arbitrary"),
        name="mla_decode",
    )(pt, qk, qr8, cnew, krnew, kng, wuk, wuv, ind, cache_ckv, cache_krt)


def _gdn_kernel(zc_ref, zba_ref, bat_ref, cw_ref, parr_ref, parc_ref, gout_ref, bones_ref, expb_ref, expg_ref, bdm_ref,
                o_ref, sfin_ref, xbuf, s_sc, *, tg):
    t = pl.program_id(1)
    c = GDN_CHUNK

    @pl.when(t == 0)
    def _():
        xbuf[0:8, :] = jnp.zeros((8, QKV_DIM), F32)
        s_sc[...] = jnp.zeros(s_sc.shape, F32)

    @pl.when(t > 0)
    def _():
        xbuf[5:8, :] = xbuf[tg + 5:tg + 8, :]

    xbuf[8:8 + tg, :] = zc_ref[:, 0:QKV_DIM]
    y = cw_ref[0:1, :] * xbuf[5:5 + tg, :]
    for i in range(1, 4):
        y = y + cw_ref[i:i + 1, :] * xbuf[5 + i:5 + i + tg, :]
    y = _silu(y)

    zba = zba_ref[...]
    beta_col = jax.nn.sigmoid(zba)
    g_col = -jnp.exp(parr_ref[0:1, :]) * jax.nn.softplus(zba + parr_ref[1:2, :])
    g_row = -jnp.exp(parc_ref[:, 0:1]) * jax.nn.softplus(bat_ref[...] + parc_ref[:, 1:2])

    row = lax.broadcasted_iota(jnp.int32, (c, c), 0)
    col = lax.broadcasted_iota(jnp.int32, (c, c), 1)
    incl = col <= row
    strict = col < row
    lt = incl.astype(BF16)
    ut = (col >= row).astype(BF16)
    bones = bones_ref[...]
    bdm = bdm_ref[...]
    expb, expg = expb_ref[...], expg_ref[...]
    brow = lax.broadcasted_iota(jnp.int32, (BW, BW), 0)
    bcl = lax.broadcasted_iota(jnp.int32, (BW, BW), 1)
    same_head = (brow // HD) == (bcl // HD)
    incl_bd = same_head & ((bcl % HD) <= (brow % HD))
    strict_bd = same_head & ((bcl % HD) < (brow % HD))
    eye_bd = (brow == bcl).astype(F32)
    same_head2 = jnp.concatenate([same_head, same_head], axis=1)

    def stack4(a):
        return jnp.concatenate([a, a, a, a], axis=0)

    def fold4(a):
        return (a[0:c] + a[c:2 * c]) + (a[2 * c:3 * c] + a[3 * c:4 * c])

    nchunk = tg // c
    pre = []
    for ci in range(nchunk):
        sl = slice(ci * c, (ci + 1) * c)
        q, k, v = y[sl, 0:256], y[sl, 256:512], y[sl, 512:768]
        qn = q * lax.rsqrt(_bdot(q * q, bones) + EPS) * (HD ** -0.5)
        kn = k * lax.rsqrt(_bdot(k * k, bones) + EPS)
        bcol = beta_col[sl, :]
        gcum_c = _sel_dot(lt, g_col[sl, :], True)
        gcum_r = _sel_dot(ut, g_row[:, sl], False)
        gx = _sel_dot(expg, gcum_c, False)
        bx = _sel_dot(expb, bcol, False)
        egx = jnp.exp(gx)
        rhs = _split2(jnp.concatenate([bx * v, bx * egx * kn], axis=1))
        ks = jnp.where(same_head, stack4(kn), 0.0).astype(BF16)
        qs = jnp.where(same_head, stack4(qn), 0.0).astype(BF16)
        gc_s = jnp.concatenate([gcum_c[:, 4 + h:5 + h] for h in range(HEADS)], axis=0)
        gr_s = jnp.concatenate([gcum_r[4 + h:5 + h, :] for h in range(HEADS)], axis=1)
        beta_s = jnp.concatenate([bcol[:, h:h + 1] for h in range(HEADS)], axis=0)
        dm = jnp.exp(jnp.where(incl_bd, gc_s - gr_s, NEG))
        a = jnp.where(strict_bd, beta_s * lax.dot_general(ks, ks, NT, preferred_element_type=F32) * dm, 0.0)
        qk = (lax.dot_general(qs, ks, NT, preferred_element_type=F32) * dm).astype(BF16)
        pre.append(dict(sl=sl, qn=qn, kn=kn, gx=gx, egx=egx, rhs=rhs, qk=qk, ps=_split2(-a), tinv=eye_bd - a))

    for _ in range(5):
        for d in pre:
            d["ps"] = _split2(_dot3(d["ps"], d["ps"]))
        for d in pre:
            d["tinv"] = d["tinv"] + _dot3(_split2(d["tinv"]), d["ps"])
    for d in pre:
        rhs = d["rhs"]
        d["x"] = fold4(jnp.where(same_head2, _dot3(_split2(d["tinv"]), (stack4(rhs[0]), stack4(rhs[1]))), 0.0))

    for d in pre:
        sl, qn, kn, gx, egx, qk, x = d["sl"], d["qn"], d["kn"], d["gx"], d["egx"], d["qk"], d["x"]
        s = s_sc[...]
        sb = s.astype(BF16)
        u = x[:, :BW] - jnp.dot(x[:, BW:].astype(BF16), sb, preferred_element_type=F32)
        ub = u.astype(BF16)
        o = egx * jnp.dot(qn.astype(BF16), sb, preferred_element_type=F32)
        o = o + fold4(jnp.where(same_head, jnp.dot(qk, stack4(ub), preferred_element_type=F32), 0.0))
        glast = gx[c - 1:c, :]
        kf = (kn * jnp.exp(glast - gx)).astype(BF16)
        s_new = jnp.exp(glast) * s + lax.dot_general(kf, ub, TN, preferred_element_type=F32)
        s_sc[...] = s_new * bdm
        on = o * lax.rsqrt(_bdot(o * o, bones) * (1.0 / HD) + EPS) * gout_ref[...]
        o_ref[sl, :] = on * _silu(zc_ref[sl, QKV_DIM:QKV_DIM + BW])

    @pl.when(t == pl.num_programs(1) - 1)
    def _():
        sfin_ref[...] = s_sc[...]


def _gdn_prompt(zc, zba, bat, n, t, p, tg):
    nt = t // tg
    consts = [p["conv_w"], p["gdn_par_r"], p["gdn_par_c"], p["gdn_out_gx"], jnp.asarray(_BONES64, BF16), jnp.asarray(_EXPB, BF16),
              jnp.asarray(_EXPG, BF16), jnp.asarray(_BDMASK)]
    return pl.pallas_call(
        functools.partial(_gdn_kernel, tg=tg),
        grid=(n, nt),
        in_specs=[pl.BlockSpec((tg, 1024), lambda b, i: (b * nt + i, 0)), pl.BlockSpec((tg, 128), lambda b, i: (b * nt + i, 0)),
                  pl.BlockSpec((8, tg), lambda b, i: (0, b * nt + i))] + [_full(c.shape) for c in consts],
        out_specs=[pl.BlockSpec((tg, BW), lambda b, i: (b * nt + i, 0)), pl.BlockSpec((None, BW, BW), lambda b, i: (b, 0, 0))],
        out_shape=[jax.ShapeDtypeStruct((n * t, BW), F32), jax.ShapeDtypeStruct((n, BW, BW), F32)],
        scratch_shapes=[pltpu.VMEM((8 + tg, QKV_DIM), F32), pltpu.VMEM((BW, BW), F32)],
        compiler_params=_cp("parallel", "arbitrary"),
        name="gdn_prompt",
    )(zc, zba, bat, *consts)


def _sample_tok_kernel(za_ref, zc_ref, zba_ref, zm_ref, sconv_ref, cw_ref, gmg_ref, gmw_ref, gmb_ref, parr_ref, bones_ref,
                       memg_ref, bmean_ref, oa_ref, ov_ref, oconv_ref, oq_ref, ok_ref, ovv_ref, ozg_ref, obeta_ref, og_ref, omq_ref):
    ge = _gelu(za_ref[...])
    v = _rms(ge[:, BW:], gmg_ref[...])
    ov_ref[...] = v
    oa_ref[...] = ge[:, :BW] * (gmw_ref[...] * v + gmb_ref[...])
    sc = sconv_ref[...]
    x = zc_ref[:, 0:QKV_DIM]
    y = (cw_ref[0:1, :] * sc[:, 0:768] + cw_ref[1:2, :] * sc[:, 768:1536] + cw_ref[2:3, :] * sc[:, 1536:2304]
         + cw_ref[3:4, :] * x)
    oconv_ref[:, 0:1536] = sc[:, 768:2304]
    oconv_ref[:, 1536:2304] = x
    y = _silu(y)
    q, k = y[:, 0:256], y[:, 256:512]
    bones = bones_ref[...]
    oq_ref[...] = (q * lax.rsqrt(_bdot(q * q, bones) + EPS) * (HD ** -0.5)).T
    ok_ref[...] = (k * lax.rsqrt(_bdot(k * k, bones) + EPS)).T
    ovv_ref[...] = y[:, 512:768].T
    ozg_ref[...] = zc_ref[:, QKV_DIM:QKV_DIM + BW].T
    zba = zba_ref[...]
    obeta_ref[...] = jax.nn.sigmoid(zba).T
    og_ref[...] = (-jnp.exp(parr_ref[0:1, :]) * jax.nn.softplus(zba + parr_ref[1:2, :])).T
    mq = zm_ref[...]
    omq_ref[...] = mq * lax.rsqrt(_bdot(mq * mq, bmean_ref[...]) + EPS) * memg_ref[...]


def _sample_tok(za, zc, zba, zm, sconv, p):
    ns = za.shape[0]
    args = [za, zc, zba, zm, sconv, p["conv_w"], p["gm_norm_g"], p["gm_w0"], p["gm_b0"], p["gdn_par_r"],
            jnp.asarray(_BONES64, BF16), p["mem_qn_gx"], jnp.asarray(_BMEAN64, BF16)]
    shapes = [(ns, BW), (ns, BW), (ns, 2304), (BW, ns), (BW, ns), (BW, ns), (BW, ns), (128, ns), (128, ns), (ns, BW)]
    return pl.pallas_call(
        _sample_tok_kernel,
        in_specs=[_full(a.shape) for a in args],
        out_specs=[_full(s) for s in shapes],
        out_shape=[jax.ShapeDtypeStruct(s, F32) for s in shapes],
        grid=(1,),
        compiler_params=_cp("arbitrary"),
        name="sample_tok",
    )(*args)


def _gdn_step_kernel(s_ref, q_ref, k_ref, v_ref, beta_ref, g_ref, zg_ref, gout_ref, so_ref, o_ref, o_sc):
    q, k = q_ref[...], k_ref[...]
    eg = jnp.exp(g_ref[...])
    beta = beta_ref[...]
    qk = jnp.sum(q * k, axis=0, keepdims=True)
    ssq = jnp.zeros(qk.shape, F32)
    for v in range(HD):
        sv = s_ref[v]
        sk = jnp.sum(sv * k, axis=0, keepdims=True)
        sq = jnp.sum(sv * q, axis=0, keepdims=True)
        u = beta * (v_ref[v:v + 1, :] - eg * sk)
        o = eg * sq + qk * u
        so_ref[v] = eg * sv + u * k
        o_sc[v:v + 1, :] = o
        ssq = ssq + o * o
    o_ref[...] = o_sc[...] * lax.rsqrt(ssq * (1.0 / HD) + EPS) * gout_ref[...] * _silu(zg_ref[...])


def _gdn_step(layer, state_t, q, k, v, beta, g, zg, gout_col):
    ns = state_t.shape[-1]
    vec = pl.BlockSpec((None, HD, ns), lambda h: (h, 0, 0))
    sca = pl.BlockSpec((None, 1, ns), lambda h: (h, 0, 0))
    return pl.pallas_call(
        _gdn_step_kernel,
        grid=(HEADS,),
        in_specs=[pl.BlockSpec((None, None, HD, HD, ns), lambda h: (layer, h, 0, 0, 0)), vec, vec, vec, sca, sca, vec, _full((HD, 1))],
        out_specs=[pl.BlockSpec((None, HD, HD, ns), lambda h: (h, 0, 0, 0)), vec],
        out_shape=[jax.ShapeDtypeStruct((HEADS, HD, HD, ns), F32), jax.ShapeDtypeStruct((HEADS, HD, ns), F32)],
        scratch_shapes=[pltpu.VMEM((HD, ns), F32)],
        compiler_params=_cp("parallel"),
        name="gdn_step",
    )(state_t, q, k, v, beta, g, zg, gout_col)


def _mem_kv_kernel(x_ref, g_ref, w_ref, kg_ref, bmean_ref, ok_ref, ov_ref):
    kv = _bdot(_rms(x_ref[...], g_ref[...]), w_ref[...])
    k = kv[:, 0:BW]
    ok_ref[...] = k * lax.rsqrt(_bdot(k * k, bmean_ref[...]) + EPS) * kg_ref[...]
    ov_ref[...] = kv[:, BW:]


def _mem_kv(mem, g, w, kgx, tm):
    m = mem.shape[0]
    bmean = jnp.asarray(_BMEAN64, BF16)
    return pl.pallas_call(
        _mem_kv_kernel,
        grid=(m // tm,),
        in_specs=[pl.BlockSpec((tm, D_MODEL), lambda i: (i, 0)), _full((1, D_MODEL)), _full(w.shape), _full((1, BW)), _full((BW, BW))],
        out_specs=[pl.BlockSpec((tm, BW), lambda i: (i, 0))] * 2,
        out_shape=[jax.ShapeDtypeStruct((m, BW), F32)] * 2,
        compiler_params=_cp("parallel"),
        name="mem_kv",
    )(mem, g, w, kgx, bmean)


def _mem_attn_kernel(q_ref, k_ref, v_ref, gq_ref, bmean_ref, o_ref):
    q = q_ref[...]
    qn = q * lax.rsqrt(_bdot(q * q, bmean_ref[...]) + EPS) * gq_ref[...]
    kb = k_ref[...].astype(BF16)
    vb = v_ref[...].astype(BF16)
    lh = _lane_head()
    out = jnp.zeros(q.shape, F32)
    for h in range(HEADS):
        mh = lh == h
        s = lax.dot_general(jnp.where(mh, qn, 0.0).astype(BF16), kb, NT, preferred_element_type=F32) * (HD ** -0.5)
        e = jnp.exp(s - jnp.max(s, axis=-1, keepdims=True))
        pr = e / jnp.sum(e, axis=-1, keepdims=True)
        out = out + jnp.where(mh, jnp.dot(pr.astype(BF16), vb, preferred_element_type=F32), 0.0)
    o_ref[...] = out


def _mem_attn(zm, mk, mv, gqx, n, t, mt, tq):
    nq = t // tq
    bmean = jnp.asarray(_BMEAN64, BF16)
    return pl.pallas_call(
        _mem_attn_kernel,
        grid=(n, nq),
        in_specs=[pl.BlockSpec((tq, BW), lambda b, i: (b * nq + i, 0)), pl.BlockSpec((mt, BW), lambda b, i: (b, 0)),
                  pl.BlockSpec((mt, BW), lambda b, i: (b, 0)), _full((1, BW)), _full((BW, BW))],
        out_specs=pl.BlockSpec((tq, BW), lambda b, i: (b * nq + i, 0)),
        out_shape=jax.ShapeDtypeStruct((n * t, BW), F32),
        compiler_params=_cp("parallel", "parallel"),
        name="mem_attn",
    )(zm, mk, mv, gqx, bmean)


def _mem_attn_s_kernel(q_ref, k_ref, v_ref, ind_ref, o_ref, *, bn):
    ind = ind_ref[...]
    for i in range(bn):
        qbd = (ind * q_ref[i:i + 1, :]).astype(BF16)
        s = jnp.dot(qbd, k_ref[i].astype(BF16), preferred_element_type=F32) * (HD ** -0.5)
        e = jnp.exp(s - jnp.max(s, axis=-1, keepdims=True))
        pr = e / jnp.sum(e, axis=-1, keepdims=True)
        o8 = lax.dot_general(pr.astype(BF16), v_ref[i].astype(BF16), NT, preferred_element_type=F32)
        o_ref[i:i + 1, :] = jnp.sum(o8 * ind, axis=0, keepdims=True)


def _mem_attn_s(layer, mqn, cache_k, cache_v, bn):
    ns = mqn.shape[0]
    mt = cache_k.shape[3]
    kv_spec = pl.BlockSpec((None, bn, BW, mt), lambda i: (layer, i, 0, 0))
    return pl.pallas_call(
        functools.partial(_mem_attn_s_kernel, bn=bn),
        grid=(ns // bn,),
        in_specs=[pl.BlockSpec((bn, BW), lambda i: (i, 0)), kv_spec, kv_spec, _full((8, BW))],
        out_specs=pl.BlockSpec((bn, BW), lambda i: (i, 0)),
        out_shape=jax.ShapeDtypeStruct((ns, BW), F32),
        compiler_params=_cp("parallel"),
        name="mem_attn_s",
    )(mqn, cache_k, cache_v, jnp.asarray(_IND8))


def _merge_kernel(x_ref, a_ref, b_ref, c_ref, m_ref, g1_ref, wg_ref, wb_ref, wo_ref, g2_ref, wrh_ref, wrl_ref, br_ref,
                  x1_ref, h2_ref, ei_ref, ew_ref):
    x = x_ref[...]
    hb = _rms(x, g1_ref[...]).astype(BF16)
    acc = jnp.zeros(x.shape, F32)
    for b, br in enumerate((a_ref, b_ref, c_ref, m_ref)):
        gate = jax.nn.sigmoid(jnp.dot(hb, wg_ref[:, b * D_MODEL:(b + 1) * D_MODEL], preferred_element_type=F32))
        acc = acc + gate * jnp.dot(br[...].astype(BF16), wb_ref[b], preferred_element_type=F32)
    x1 = x + jnp.dot(acc.astype(BF16), wo_ref[...], preferred_element_type=F32)
    x1_ref[...] = x1
    h2 = _rms(x1, g2_ref[...])
    h2_ref[...] = h2
    logits = _dot3(_split2(h2), (wrh_ref[...], wrl_ref[...])) + br_ref[...]
    lane = lax.broadcasted_iota(jnp.int32, (1, 128), 1).astype(F32)
    big = 1e9
    lg = jnp.where(lane < N_GROUPS, logits, NEG)
    mg = jnp.max(lg, axis=-1, keepdims=True)
    g_w = 1.0 / jnp.sum(jnp.exp(lg - mg), axis=-1, keepdims=True)
    gi = jnp.min(jnp.where(lg == mg, lane, big), axis=-1, keepdims=True)
    sel = (lane >= N_GROUPS) & (lane < N_GROUPS + N_EXPERTS) & (jnp.floor((lane - N_GROUPS) * (1.0 / EPG)) == gi)
    le = jnp.where(sel, logits, NEG)
    m1 = jnp.max(le, axis=-1, keepdims=True)
    i1 = jnp.min(jnp.where(le == m1, lane, big), axis=-1, keepdims=True)
    le2 = jnp.where(lane == i1, NEG, le)
    m2 = jnp.max(le2, axis=-1, keepdims=True)
    i2 = jnp.min(jnp.where(le2 == m2, lane, big), axis=-1, keepdims=True)
    z = jnp.sum(jnp.exp(le - m1), axis=-1, keepdims=True)
    p1 = 1.0 / z
    p2 = jnp.exp(m2 - m1) / z
    w1 = p1 / (p1 + p2) * g_w
    w2 = p2 / (p1 + p2) * g_w
    ei_ref[...] = jnp.where(lane == 0, i1 - N_GROUPS, jnp.where(lane == 1, i2 - N_GROUPS, 0.0)).astype(jnp.int32)
    ew_ref[...] = jnp.where(lane == 0, w1, jnp.where(lane == 1, w2, 0.0))


def _merge(x, branches, p, tm):
    m = x.shape[0]
    consts = [p["norm1_g"], p["w_gate"], p["w_branch"], p["w_out"], p["norm2_g"], *_split2(p["w_router"]), p["b_router"]]
    tile = lambda w: pl.BlockSpec((tm, w), lambda i: (i, 0))
    return pl.pallas_call(
        _merge_kernel,
        grid=(m // tm,),
        in_specs=[tile(D_MODEL)] + [tile(BW)] * 4 + [_full(c.shape) for c in consts],
        out_specs=[tile(D_MODEL), tile(D_MODEL), tile(128), tile(128)],
        out_shape=[jax.ShapeDtypeStruct((m, D_MODEL), F32), jax.ShapeDtypeStruct((m, D_MODEL), F32),
                   jax.ShapeDtypeStruct((m, 128), jnp.int32), jax.ShapeDtypeStruct((m, 128), F32)],
        compiler_params=_cp("parallel"),
        name="merge",
    )(x, *branches, *consts)


def _dispatch_kernel(poff_ref, plen_ref, pos_ref, h_ref, xs_out, zbuf, sem, zsem, *, tmd, te):
    @pl.when(pl.program_id(0) == 0)
    def _():
        zbuf[...] = jnp.zeros(zbuf.shape, F32)

        def pad_copies(e):
            off = poff_ref[e]
            head = (-off) & 7
            body = plen_ref[e] - head
            out = [(i < head, pltpu.make_async_copy(zbuf.at[pl.ds(0, 1)], xs_out.at[pl.ds(off + i, 1)], zsem)) for i in range(7)]
            b = te // 2
            while b >= 8:
                start = pl.multiple_of(off + head + (body & ~(2 * b - 1)), 8)
                out.append(((body & b) != 0, pltpu.make_async_copy(zbuf.at[pl.ds(0, b)], xs_out.at[pl.ds(start, b)], zsem)))
                b //= 2
            return out

        def fill(e, carry):
            for cond, cp in pad_copies(e):
                pl.when(cond)(cp.start)
            for cond, cp in pad_copies(e):
                pl.when(cond)(cp.wait)
            return carry

        lax.fori_loop(0, N_EXPERTS, fill, 0)

    def issue(i, carry):
        for s in range(2):
            pltpu.make_async_copy(h_ref.at[pl.ds(i, 1)], xs_out.at[pl.ds(pos_ref[2 * i + s], 1)], sem).start()
        return carry

    lax.fori_loop(0, tmd, issue, 0, unroll=8)
    for s in range(2):
        pltpu.make_async_copy(h_ref, xs_out.at[pl.ds(0, tmd)], sem).wait()


def _dispatch(pad_off, pad_len, pos, h2, rows, tmd, te):
    m = h2.shape[0]
    grid_spec = pltpu.PrefetchScalarGridSpec(
        num_scalar_prefetch=2,
        grid=(m // tmd,),
        in_specs=[pl.BlockSpec((2 * tmd,), lambda i, po, pn: (i,), memory_space=pltpu.SMEM),
                  pl.BlockSpec((tmd, D_MODEL), lambda i, po, pn: (i, 0))],
        out_specs=pl.BlockSpec(memory_space=pl.ANY),
        scratch_shapes=[pltpu.VMEM((max(te // 2, 8), D_MODEL), F32), pltpu.SemaphoreType.DMA(()), pltpu.SemaphoreType.DMA(())],
    )
    return pl.pallas_call(
        functools.partial(_dispatch_kernel, tmd=tmd, te=te),
        grid_spec=grid_spec,
        out_shape=jax.ShapeDtypeStruct((rows, D_MODEL), F32),
        compiler_params=_cp("arbitrary"),
        name="moe_dispatch",
    )(pad_off, pad_len, pos, h2)


def _expert_kernel(te_ref, nv_ref, x_ref, wg_ref, wu_ref, wd_ref, o_ref):
    del te_ref

    @pl.when(pl.program_id(0) < nv_ref[0])
    def _():
        xb = x_ref[...].astype(BF16)
        gt = jnp.dot(xb, wg_ref[...].astype(BF16), preferred_element_type=F32)
        up = jnp.dot(xb, wu_ref[...].astype(BF16), preferred_element_type=F32)
        o_ref[...] = jnp.dot((_silu(gt) * up).astype(BF16), wd_ref[...].astype(BF16), preferred_element_type=F32)

    @pl.when(pl.program_id(0) >= nv_ref[0])
    def _():
        o_ref[...] = jnp.zeros(o_ref.shape, F32)


def _experts(layer, tile_expert, n_valid, xs, w_gate, w_up, w_down, te):
    rows = xs.shape[0]

    def xmap(i, te_ref, nv_ref):
        return (jnp.minimum(i, nv_ref[0] - 1), 0)

    def wmap(i, te_ref, nv_ref):
        return (layer, te_ref[i], 0, 0)

    grid_spec = pltpu.PrefetchScalarGridSpec(
        num_scalar_prefetch=2,
        grid=(rows // te,),
        in_specs=[pl.BlockSpec((te, D_MODEL), xmap), pl.BlockSpec((None, None, D_MODEL, D_EXPERT), wmap),
                  pl.BlockSpec((None, None, D_MODEL, D_EXPERT), wmap), pl.BlockSpec((None, None, D_EXPERT, D_MODEL), wmap)],
        out_specs=pl.BlockSpec((te, D_MODEL), lambda i, te_ref, nv_ref: (i, 0)),
    )
    return pl.pallas_call(
        _expert_kernel,
        grid_spec=grid_spec,
        out_shape=jax.ShapeDtypeStruct((rows, D_MODEL), F32),
        compiler_params=_cp("arbitrary"),
        name="moe_experts",
    )(tile_expert, n_valid, xs, w_gate, w_up, w_down)


def _combine_kernel(pos_ref, x1_ref, ew_ref, ys_hbm, o_ref, r0, r1, sem, *, tmc):
    bufs = (r0, r1)

    def issue(i, carry):
        for s in range(2):
            pltpu.make_async_copy(ys_hbm.at[pl.ds(pos_ref[2 * i + s], 1)], bufs[s].at[pl.ds(i, 1)], sem).start()
        return carry

    lax.fori_loop(0, tmc, issue, 0, unroll=8)
    for s in range(2):
        pltpu.make_async_copy(ys_hbm.at[pl.ds(0, tmc)], bufs[s], sem).wait()
    ew = ew_ref[...]
    o_ref[...] = x1_ref[...] + ew[:, 0:1] * r0[...] + ew[:, 1:2] * r1[...]


def _combine(pos, x1, ew, ys, tmc):
    m = x1.shape[0]
    return pl.pallas_call(
        functools.partial(_combine_kernel, tmc=tmc),
        grid=(m // tmc,),
        in_specs=[pl.BlockSpec((2 * tmc,), lambda i: (i,), memory_space=pltpu.SMEM),
                  pl.BlockSpec((tmc, D_MODEL), lambda i: (i, 0)), pl.BlockSpec((tmc, 128), lambda i: (i, 0)),
                  pl.BlockSpec(memory_space=pl.ANY)],
        out_specs=pl.BlockSpec((tmc, D_MODEL), lambda i: (i, 0)),
        out_shape=jax.ShapeDtypeStruct((m, D_MODEL), F32),
        scratch_shapes=[pltpu.VMEM((tmc, D_MODEL), F32), pltpu.VMEM((tmc, D_MODEL), F32), pltpu.SemaphoreType.DMA(())],
        compiler_params=_cp("arbitrary"),
        name="moe_combine",
    )(pos, x1, ew, ys)


def _moe(layer, x1, h2, ei, ew, w_gate, w_up, w_down, te, tmd):
    m = x1.shape[0]
    flat_e = ei[:, 0:2].reshape(-1)
    onehot = (flat_e[:, None] == jnp.arange(N_EXPERTS, dtype=jnp.int32)[None, :]).astype(jnp.int32)
    csum = jnp.cumsum(onehot, axis=0)
    rank = jnp.sum(csum * onehot, axis=1) - 1
    counts = csum[-1]
    padded = ((counts + te - 1) // te) * te
    pend = jnp.cumsum(padded)
    pstart = pend - padded
    pos = (jnp.sum(onehot * pstart[None, :], axis=1) + rank).astype(jnp.int32)
    rows = ((2 * m + N_EXPERTS * (te - 1)) // te) * te
    n_tiles = rows // te
    n_valid = (pend[-1] // te).astype(jnp.int32).reshape(1)
    tile_start = jnp.arange(n_tiles, dtype=jnp.int32) * te
    tile_expert = jnp.minimum(jnp.sum((tile_start[:, None] >= pend[None, :]).astype(jnp.int32), axis=1), N_EXPERTS - 1)
    last_e = jnp.take(tile_expert, jnp.maximum(n_valid[0] - 1, 0))
    tile_expert = jnp.where(jnp.arange(n_tiles) < n_valid[0], tile_expert, last_e).astype(jnp.int32)
    xs = _dispatch((pstart + counts).astype(jnp.int32), (padded - counts).astype(jnp.int32), pos, h2, rows, tmd, te)
    ys = _experts(layer, tile_expert, n_valid, xs, w_gate, w_up, w_down, te)
    return _combine(pos, x1, ew, ys, tmd)


def _tile4(v):
    return jnp.tile(v, HEADS).reshape(1, BW)


def _prep_layer(l, w):
    w_in = w["w_in"][l]
    z = lambda n: jnp.zeros((D_MODEL, n), F32)
    b_al = w_in[:, 1952:1960]
    w_small = jnp.concatenate(
        [w_in[:, 0:512], w_in[:, 512:768], w_in[:, 768:896], z(64), w_in[:, 896:928], z(32), w_in[:, 928:1696],
         w_in[:, 1696:1952], w_in[:, 1960:2216], b_al, z(120)], axis=1).astype(BF16)
    uq = w["mla_w_uq"][l]
    w_uq = jnp.pad(uq, ((0, 0), (0, 0), (0, 32))).reshape(256, 512).astype(BF16)
    qg = jnp.tile(jnp.concatenate([w["mla_qn_g"][l], w["mla_qr_g"][l], jnp.zeros((32,), F32)]), HEADS).reshape(1, 512)
    uk = w["mla_w_uk"][l]
    w_uk_p = jnp.pad(uk, ((0, 0), (0, 0), (0, 64))).reshape(128, 512).astype(BF16)
    kg = jnp.tile(jnp.concatenate([w["mla_kn_g"][l], jnp.zeros((64,), F32)]), HEADS).reshape(1, 512)
    kr_g = jnp.concatenate([jnp.zeros((64,), F32), w["mla_kr_g"][l], jnp.zeros((32,), F32)]).reshape(1, 128)
    uv = w["mla_w_uv"][l]
    w_uv_p = jnp.stack([jnp.pad(uv[:, h, :], ((0, 0), (64 * h, BW - 64 * h - 64))) for h in range(HEADS)]).astype(BF16)
    par_r = jnp.zeros((8, 128), F32).at[0, 4:8].set(w["gdn_a_log"][l]).at[1, 4:8].set(w["gdn_dt_bias"][l])
    par_c = jnp.zeros((8, 128), F32).at[4:8, 0].set(w["gdn_a_log"][l]).at[4:8, 1].set(w["gdn_dt_bias"][l])
    w_router = jnp.concatenate([w["moe_wg"][l], w["moe_we"][l], jnp.zeros((D_MODEL, 128 - 36), F32)], axis=1)
    b_router = jnp.concatenate([w["moe_bg"][l], w["moe_be"][l], jnp.zeros((128 - 36,), F32)]).reshape(1, 128)
    return {
        "norm1_g": w["norm1_g"][l].reshape(1, D_MODEL), "w_small": w_small, "w_bat": b_al.T.astype(BF16),
        "w_gate": w_in[:, 2216:].astype(BF16),
        "gm_norm_g": w["gm_norm_g"][l].reshape(1, BW), "gm_ws": w["gm_ws"][l],
        "gm_bfull": jnp.repeat(w["gm_b"][l].T, HD, axis=1),
        "gm_w0": jnp.repeat(w["gm_ws"][l][:, 0, 0], HD).reshape(1, BW), "gm_b0": jnp.repeat(w["gm_b"][l][:, 0], HD).reshape(1, BW),
        "mla_cq_g": w["mla_cq_g"][l].reshape(1, 256), "w_uq": w_uq, "qg": qg, "mla_ckv_g": w["mla_ckv_g"][l].reshape(1, 128),
        "kr_g": kr_g, "w_uk": w_uk_p, "kg": kg, "w_uv": w_uv_p,
        "w_uk_c": uk.reshape(128, BW).astype(BF16), "w_uv_c": uv.reshape(128, BW).astype(BF16), "kn_gx": _tile4(w["mla_kn_g"][l]),
        "conv_w": w["gdn_conv_w"][l], "gdn_par_r": par_r, "gdn_par_c": par_c, "gdn_out_gx": _tile4(w["gdn_out_g"][l]),
        "gdn_out_gc": w["gdn_out_g"][l].reshape(HD, 1),
        "mem_norm_g": w["mem_norm_g"][l].reshape(1, D_MODEL), "mem_w_kv": w["mem_w_kv"][l].astype(BF16),
        "mem_qn_gx": _tile4(w["mem_qn_g"][l]), "mem_kn_gx": _tile4(w["mem_kn_g"][l]),
        "w_branch": w["w_branch"][l].astype(BF16), "w_out": w["w_out"][l].astype(BF16),
        "norm2_g": w["norm2_g"][l].reshape(1, D_MODEL), "w_router": w_router, "b_router": b_router,
    }


def _rope_tables(pos):
    half = MLA_ROPE // 2
    inv = ROPE_THETA ** (-jnp.arange(half, dtype=F32) / half)
    ang = pos.astype(F32)[:, None] * inv[None, :]
    cos, sin = jnp.cos(ang), jnp.sin(ang)
    t = pos.shape[0]
    one, zero = jnp.ones((t, 64), F32), jnp.zeros((t, 64), F32)
    z16, z32 = jnp.zeros((t, 16), F32), jnp.zeros((t, 32), F32)
    return (jnp.concatenate([one, cos, cos, jnp.ones((t, 32), F32)], axis=1),
            jnp.concatenate([zero, -sin, z16, z32], axis=1),
            jnp.concatenate([zero, z16, sin, z32], axis=1))


def kernel(x_prompt, mem_prompt, x_sample, cache_mla_ckv, cache_mla_kr, cache_mem_k, cache_mem_v, state_gdn, state_conv,
           page_table, norm1_g, w_in, gm_norm_g, gm_ws, gm_b, mla_cq_g, mla_w_uq, mla_qn_g, mla_qr_g, mla_ckv_g, mla_kr_g,
           mla_w_uk, mla_kn_g, mla_w_uv, gdn_conv_w, gdn_a_log, gdn_dt_bias, gdn_out_g, mem_norm_g, mem_w_kv, mem_qn_g,
           mem_kn_g, w_branch, w_out, norm2_g, moe_wg, moe_bg, moe_we, moe_be, moe_w_gate, moe_w_up, moe_w_down):
    w = dict(norm1_g=norm1_g, w_in=w_in, gm_norm_g=gm_norm_g, gm_ws=gm_ws, gm_b=gm_b, mla_cq_g=mla_cq_g, mla_w_uq=mla_w_uq,
             mla_qn_g=mla_qn_g, mla_qr_g=mla_qr_g, mla_ckv_g=mla_ckv_g, mla_kr_g=mla_kr_g, mla_w_uk=mla_w_uk, mla_kn_g=mla_kn_g,
             mla_w_uv=mla_w_uv, gdn_conv_w=gdn_conv_w, gdn_a_log=gdn_a_log, gdn_dt_bias=gdn_dt_bias, gdn_out_g=gdn_out_g,
             mem_norm_g=mem_norm_g, mem_w_kv=mem_w_kv, mem_qn_g=mem_qn_g, mem_kn_g=mem_kn_g, w_branch=w_branch, w_out=w_out,
             norm2_g=norm2_g, moe_wg=moe_wg, moe_bg=moe_bg, moe_we=moe_we, moe_be=moe_be)
    depth = w_in.shape[0]
    bp, tp, _ = x_prompt.shape
    bs = x_sample.shape[0]
    mt = mem_prompt.shape[1]
    n_pages = page_table.shape[1]
    past_len = n_pages * cache_mla_ckv.shape[2]
    mp = bp * tp

    tm_p = min(512, mp)
    tq = min(256, tp)
    tg = min(256, tp)
    pp = min(32, n_pages)
    cache_krt = jnp.swapaxes(cache_mla_kr, 2, 3)
    tabs_p = _rope_tables(jnp.arange(tp, dtype=jnp.int32))
    tabs_s = _rope_tables(jnp.full((bs,), past_len, jnp.int32))

    xp = x_prompt.reshape(mp, D_MODEL)
    xs = x_sample.reshape(bs, D_MODEL)
    mem = mem_prompt.reshape(bp * mt, D_MODEL)
    cache_k = cache_mem_k.transpose(0, 1, 3, 4, 2).reshape(depth, bs, BW, mt)
    cache_v = cache_mem_v.transpose(0, 1, 3, 4, 2).reshape(depth, bs, BW, mt)
    state_t = state_gdn.transpose(0, 2, 3, 4, 1)
    rows_p, rows_s = [], []
    for l in range(depth):
        p = _prep_layer(l, w)
        mk, mv = _mem_kv(mem, p["mem_norm_g"], p["mem_w_kv"], p["mem_kn_gx"], min(512, bp * mt))
        za, zb, zc, zm, zba, bat = _in_proj(xp, p["norm1_g"], p["w_small"], p["w_bat"], tm_p)
        a_out = _gmlp(za, mp, p["gm_norm_g"], p["gm_ws"], p["gm_bfull"], tm_p)
        q4, k4, ckv, kr, ckvt, _ = _mla_pre(zb, mp, tabs_p, tp // min(tm_p, tp), p, min(tm_p, tp), tq)
        b_out = _mla_attn(q4, k4, ckvt, p["w_uv"], bp, tp, tq)
        c_out, sfin = _gdn_prompt(zc, zba, bat, bp, tp, p, tg)
        m_out = _mem_attn(zm, mk, mv, p["mem_qn_gx"], bp, tp, mt, tq)
        x1, h2, ei, ew = _merge(xp, (a_out, b_out, c_out, m_out), p, tm_p)
        xp = _moe(l, x1, h2, ei, ew, moe_w_gate, moe_w_up, moe_w_down, min(256, mp), min(512, mp))
        s_p = jnp.stack([sfin[:, 64 * h:64 * h + 64, 64 * h:64 * h + 64] for h in range(HEADS)], axis=1).transpose(0, 1, 3, 2)
        conv_p = zc.reshape(bp, tp, 1024)[:, tp - 3:, 0:QKV_DIM]
        rows_p.append((ckv.reshape(bp, tp, 128), kr.reshape(bp, tp, MLA_ROPE), s_p, conv_p,
                       mk.reshape(bp, mt, HEADS, HD), mv.reshape(bp, mt, HEADS, HD)))
        za, zb, zc, zm, zba, _ = _in_proj(xs, p["norm1_g"], p["w_small"], p["w_bat"], bs)
        a_s, v_s, conv_s, gq, gk, gv, zg, beta, gdec, mqn = _sample_tok(za, zc, zba, zm, state_conv[l].reshape(bs, 3 * QKV_DIM), p)
        _, _, ckv_s, kr_s, _, q32 = _mla_pre(zb, bs, tabs_s, 1, p, bs, bs)
        q3 = q32.reshape(bs, HEADS, 128)
        qk = q3[:, :, 0:64].reshape(bs, 1, BW)
        qr8 = jnp.pad(q3[:, :, 64:96], ((0, 0), (0, 4), (0, 0)))
        b_s = _mla_decode(l, page_table, cache_mla_ckv, cache_krt, qk, qr8, ckv_s.reshape(bs, 1, 128),
                          kr_s.reshape(bs, 1, MLA_ROPE), p["kn_gx"], p["w_uk_c"], p["w_uv_c"], pp).reshape(bs, BW)
        hv = lambda a: a.reshape(HEADS, HD, bs)
        s_new, c_t = _gdn_step(l, state_t, hv(gq), hv(gk), hv(gv), beta[0:4].reshape(HEADS, 1, bs),
                               gdec[4:8].reshape(HEADS, 1, bs), hv(zg), p["gdn_out_gc"])
        m_s = _mem_attn_s(l, mqn, cache_k, cache_v, min(8, bs))
        x1, h2, ei, ew = _merge(xs, (a_s, b_s, c_t.reshape(BW, bs).T, m_s), p, bs)
        xs = _moe(l, x1, h2, ei, ew, moe_w_gate, moe_w_up, moe_w_down, min(32, bs), bs)
        rows_s.append((ckv_s.reshape(bs, 1, 128), kr_s.reshape(bs, 1, MLA_ROPE), s_new.transpose(3, 0, 1, 2),
                       conv_s.reshape(bs, 3, QKV_DIM), v_s.reshape(bs, 1, BW)))
    p_out = [jnp.stack(a) for a in zip(*rows_p)]
    s_out = [jnp.stack(a) for a in zip(*rows_s)]
    return (xp.reshape(bp, tp, D_MODEL), xs.reshape(bs, 1, D_MODEL), *p_out, *s_out)
```

```python
import functools

import numpy as np
import jax
import jax.numpy as jnp
from jax import lax
from jax.experimental import pallas as pl
from jax.experimental.pallas import tpu as pltpu

F32 = jnp.float32
BF16 = jnp.bfloat16
HI = lax.Precision.HIGHEST
EPS = 1e-6
NEG = float("-inf")

D_MODEL = 1024
HEADS = 4
HD = 64
BW = 256
MLA_ROPE = 32
MLA_KV_RANK = 128
MLA_SCALE = 96.0 ** -0.5
ROPE_THETA = 10000.0
GM_CHUNK = 128
GDN_CHUNK = 64
QKV_DIM = 768
N_GROUPS = 4
EPG = 8
N_EXPERTS = 32
D_EXPERT = 256
PAGE = 128
VMEM_LIMIT = 56 * 1024 * 1024

NT = (((1,), (1,)), ((), ()))
TN = (((0,), (0,)), ((), ()))


def _cp(*sem):
    return pltpu.CompilerParams(dimension_semantics=sem, vmem_limit_bytes=VMEM_LIMIT)


def _rms(x, g):
    ms = jnp.sum(x * x, axis=-1, keepdims=True) * (1.0 / x.shape[-1])
    return x * lax.rsqrt(ms + EPS) * g


def _bdot(a, b):
    return jnp.dot(a.astype(BF16), b.astype(BF16), preferred_element_type=F32)


def _bdot_nt(a, b):
    return lax.dot_general(a.astype(BF16), b.astype(BF16), NT, preferred_element_type=F32)


def _hdot(a, b):
    return jnp.dot(a, b, precision=HI, preferred_element_type=F32)


def _split2(a):
    hi = a.astype(BF16)
    return hi, (a - hi.astype(F32)).astype(BF16)


def _dot3(a, b):
    d = lambda x, y: jnp.dot(x, y, preferred_element_type=F32)
    return d(a[0], b[0]) + (d(a[0], b[1]) + d(a[1], b[0]))


def _sel_dot(w01, x, left):
    x0 = x.astype(BF16)
    r1 = x - x0.astype(F32)
    x1 = r1.astype(BF16)
    x2 = (r1 - x1.astype(F32)).astype(BF16)
    d = (lambda p: jnp.dot(w01, p, preferred_element_type=F32)) if left else (lambda p: jnp.dot(p, w01, preferred_element_type=F32))
    return d(x0) + (d(x1) + d(x2))


def _silu(x):
    return x * jax.nn.sigmoid(x)


def _gelu(x):
    return 0.5 * x * (1.0 + lax.erf(x * 0.7071067811865476))


def _lane_head(width=BW):
    return lax.broadcasted_iota(jnp.int32, (1, width), 1) // HD


def _full(shape):
    n = len(shape)
    return pl.BlockSpec(shape, lambda *_: (0,) * n)


def _block_mean(width, segs):
    m = np.zeros((width, width), np.float32)
    for a, b in segs:
        m[a:b, a:b] = 1.0 / (b - a)
    return m


_BMEAN64 = _block_mean(BW, [(64 * h, 64 * h + 64) for h in range(HEADS)])
_BONES64 = _BMEAN64 * 64.0
_BQ = _block_mean(512, [(128 * h, 128 * h + 64) for h in range(HEADS)] + [(128 * h + 64, 128 * h + 96) for h in range(HEADS)])
_BK = _block_mean(512, [(128 * h, 128 * h + 64) for h in range(HEADS)])
_IND8 = np.zeros((8, BW), np.float32)
for _h in range(HEADS):
    _IND8[_h, 64 * _h:64 * _h + 64] = 1.0
_EXPB = np.zeros((128, BW), np.float32)
_EXPG = np.zeros((128, BW), np.float32)
for _h in range(HEADS):
    _EXPB[_h, 64 * _h:64 * _h + 64] = 1.0
    _EXPG[4 + _h, 64 * _h:64 * _h + 64] = 1.0
_BDMASK = (_BONES64 > 0).astype(np.float32)


def _in_proj_kernel(x_ref, g_ref, w_ref, wbat_ref, oa_ref, ob_ref, oc_ref, om_ref, oba_ref, obat_ref):
    hb = _rms(x_ref[...], g_ref[...]).astype(BF16)
    oa_ref[...] = jnp.dot(hb, w_ref[:, 0:512], preferred_element_type=F32)
    ob_ref[...] = jnp.dot(hb, w_ref[:, 512:1024], preferred_element_type=F32)
    oc_ref[...] = jnp.dot(hb, w_ref[:, 1024:2048], preferred_element_type=F32)
    om_ref[...] = jnp.dot(hb, w_ref[:, 2048:2304], preferred_element_type=F32)
    oba_ref[...] = jnp.dot(hb, w_ref[:, 2304:2432], preferred_element_type=F32)
    obat_ref[...] = lax.dot_general(wbat_ref[...], hb, NT, preferred_element_type=F32)


def _in_proj(x, g, w, wbat, tm):
    m = x.shape[0]
    widths = (512, 512, 1024, 256, 128)
    return pl.pallas_call(
        _in_proj_kernel,
        grid=(m // tm,),
        in_specs=[pl.BlockSpec((tm, D_MODEL), lambda i: (i, 0)), _full((1, D_MODEL)), _full(w.shape), _full(wbat.shape)],
        out_specs=[pl.BlockSpec((tm, n), lambda i: (i, 0)) for n in widths] + [pl.BlockSpec((8, tm), lambda i: (0, i))],
        out_shape=[jax.ShapeDtypeStruct((m, n), F32) for n in widths] + [jax.ShapeDtypeStruct((8, m), F32)],
        compiler_params=_cp("parallel"),
        name="in_proj",
    )(x, g, w, wbat)


def _gmlp_kernel(z_ref, g_ref, ws_ref, b_ref, o_ref, *, ta):
    row = lax.broadcasted_iota(jnp.int32, (GM_CHUNK, GM_CHUNK), 0)
    col = lax.broadcasted_iota(jnp.int32, (GM_CHUNK, GM_CHUNK), 1)
    tril = col <= row
    lh = _lane_head()
    wts = [jnp.where(tril, ws_ref[g], 0.0).astype(BF16) for g in range(HEADS)]
    for c in range(ta // GM_CHUNK):
        sl = slice(c * GM_CHUNK, (c + 1) * GM_CHUNK)
        ge = _gelu(z_ref[sl, :])
        u = ge[:, :BW]
        vb = _rms(ge[:, BW:], g_ref[...]).astype(BF16)
        s = b_ref[...]
        for g in range(HEADS):
            s = s + jnp.where(lh == g, jnp.dot(wts[g], vb, preferred_element_type=F32), 0.0)
        o_ref[sl, :] = u * s


def _gmlp(za, m, g, ws, bfull, ta):
    return pl.pallas_call(
        functools.partial(_gmlp_kernel, ta=ta),
        grid=(m // ta,),
        in_specs=[pl.BlockSpec((ta, 512), lambda i: (i, 0)), _full((1, BW)), _full(ws.shape), _full(bfull.shape)],
        out_specs=pl.BlockSpec((ta, BW), lambda i: (i, 0)),
        out_shape=jax.ShapeDtypeStruct((m, BW), F32),
        compiler_params=_cp("parallel"),
        name="gmlp",
    )(za, g, ws, bfull)


def _mla_pre_kernel(z_ref, c_ref, s1_ref, s2_ref, gcq_ref, wuq_ref, qg_ref, bq_ref, gckv_ref, gkr_ref, wuk_ref, kg_ref,
                    bk_ref, oq_ref, ok_ref, ockv_ref, okr_ref, ockvt_ref, oq32_ref, *, tb):
    cs, s1, s2 = c_ref[...], s1_ref[...], s2_ref[...]

    def rope(x):
        return x * cs + pltpu.roll(x, 112, 1) * s1 + pltpu.roll(x, 16, 1) * s2

    z = z_ref[...]
    cq = _rms(z[:, 0:256], gcq_ref[...])
    q = _bdot(cq, wuq_ref[...])
    qn = q * lax.rsqrt(_bdot(q * q, bq_ref[...]) + EPS) * qg_ref[...]
    ckv = _rms(z[:, 256:384], gckv_ref[...])
    ockv_ref[...] = ckv
    for c in range(ckv.shape[0] // tb):
        ockvt_ref[c] = ckv[c * tb:(c + 1) * tb, :].T.astype(BF16)
    krb = z[:, 384:512]
    kr = rope(krb * lax.rsqrt(jnp.sum(krb * krb, axis=-1, keepdims=True) * (1.0 / MLA_ROPE) + EPS) * gkr_ref[...])
    okr_ref[...] = kr[:, 64:96]
    k = _bdot(ckv, wuk_ref[...])
    kn = k * lax.rsqrt(_bdot(k * k, bk_ref[...]) + EPS) * kg_ref[...]
    for h in range(HEADS):
        sl = slice(128 * h, 128 * h + 128)
        qh = rope(qn[:, sl])
        oq_ref[h] = qh.astype(BF16)
        oq32_ref[:, sl] = qh
        ok_ref[h] = (kn[:, sl] + kr).astype(BF16)


def _mla_pre(zb, m, tabs, t_blocks, p, tm, tb):
    cs, s1, s2 = tabs
    tab_spec = pl.BlockSpec((tm, 128), lambda i: (i % t_blocks, 0))
    consts = [p["mla_cq_g"], p["w_uq"], p["qg"], jnp.asarray(_BQ, BF16), p["mla_ckv_g"], p["kr_g"], p["w_uk"], p["kg"],
              jnp.asarray(_BK, BF16)]
    return pl.pallas_call(
        functools.partial(_mla_pre_kernel, tb=tb),
        grid=(m // tm,),
        in_specs=[pl.BlockSpec((tm, 512), lambda i: (i, 0)), tab_spec, tab_spec, tab_spec] + [_full(c.shape) for c in consts],
        out_specs=[pl.BlockSpec((HEADS, tm, 128), lambda i: (0, i, 0)), pl.BlockSpec((HEADS, tm, 128), lambda i: (0, i, 0)),
                   pl.BlockSpec((tm, 128), lambda i: (i, 0)), pl.BlockSpec((tm, MLA_ROPE), lambda i: (i, 0)),
                   pl.BlockSpec((tm // tb, 128, tb), lambda i: (i, 0, 0)), pl.BlockSpec((tm, 512), lambda i: (i, 0))],
        out_shape=[jax.ShapeDtypeStruct((HEADS, m, 128), BF16), jax.ShapeDtypeStruct((HEADS, m, 128), BF16),
                   jax.ShapeDtypeStruct((m, 128), F32), jax.ShapeDtypeStruct((m, MLA_ROPE), F32),
                   jax.ShapeDtypeStruct((m // tb, 128, tb), BF16), jax.ShapeDtypeStruct((m, 512), F32)],
        compiler_params=_cp("parallel"),
        name="mla_pre",
    )(zb, cs, s1, s2, *consts)


def _mla_attn_kernel(q_ref, k_ref, v_ref, wuv_ref, o_ref, *, tq):
    i = pl.program_id(1)
    row = lax.broadcasted_iota(jnp.int32, (tq, tq), 0)
    col = lax.broadcasted_iota(jnp.int32, (tq, tq), 1)
    causal = row <= col

    def step(j, carry, mask):
        off = pl.multiple_of(j * tq, tq)
        vt = v_ref[j]
        ss = [lax.dot_general(k_ref[h, pl.ds(off, tq), :], q_ref[h], NT, preferred_element_type=F32) for h in range(HEADS)]
        stats = []
        for h in range(HEADS):
            m, l, _ = carry[h]
            s = ss[h] * MLA_SCALE
            if mask:
                s = jnp.where(causal, s, NEG)
            mn = jnp.maximum(m, jnp.max(s, axis=0, keepdims=True))
            pr = jnp.exp(s - mn)
            al = jnp.exp(m - mn)
            stats.append((mn, al * l + jnp.sum(pr, axis=0, keepdims=True), al, pr.astype(BF16)))
        pvs = [jnp.dot(vt, stats[h][3], preferred_element_type=F32) for h in range(HEADS)]
        return tuple((stats[h][0], stats[h][1], stats[h][2] * carry[h][2] + pvs[h]) for h in range(HEADS))

    init = tuple((jnp.full((1, tq), NEG, F32), jnp.zeros((1, tq), F32), jnp.zeros((MLA_KV_RANK, tq), F32)) for _ in range(HEADS))
    carry = lax.fori_loop(0, i, lambda j, c: step(j, c, False), init)
    carry = step(i, carry, True)
    out = jnp.zeros((tq, BW), F32)
    for h in range(HEADS):
        m, l, acc = carry[h]
        out = out + lax.dot_general((acc / l).astype(BF16), wuv_ref[h], TN, preferred_element_type=F32)
    o_ref[...] = out


def _mla_attn(q4, k4, ckvt, wuv, n, t, tq):
    nq = t // tq
    return pl.pallas_call(
        functools.partial(_mla_attn_kernel, tq=tq),
        grid=(n, nq),
        in_specs=[pl.BlockSpec((HEADS, tq, 128), lambda b, i: (0, b * nq + i, 0)),
                  pl.BlockSpec((HEADS, t, 128), lambda b, i: (0, b, 0)),
                  pl.BlockSpec((nq, MLA_KV_RANK, tq), lambda b, i: (b, 0, 0)), _full(wuv.shape)],
        out_specs=pl.BlockSpec((tq, BW), lambda b, i: (b * nq + i, 0)),
        out_shape=jax.ShapeDtypeStruct((n * t, BW), F32),
        compiler_params=_cp("parallel", "arbitrary"),
        name="mla_attn",
    )(q4, k4, ckvt, wuv)


def _mla_decode_kernel(pt_ref, qk_ref, qr_ref, cnew_ref, krnew_ref, kng_ref, wukt_ref, wuv_ref, ind_ref, ckv_hbm, krt_hbm, o_ref,
                       cbuf, kbuf, sem, *, layer, npages, pp):
    n = pl.program_id(0)
    ngroups = npages // pp

    def page_copies(sample, grp, slot):
        base = sample * npages + grp * pp
        out = []
        for i in range(pp):
            page = pt_ref[base + i]
            out.append(pltpu.make_async_copy(ckv_hbm.at[layer, page], cbuf.at[slot, pl.ds(i * PAGE, PAGE)], sem.at[slot]))
            out.append(pltpu.make_async_copy(krt_hbm.at[layer, page], kbuf.at[slot, :, pl.ds(i * PAGE, PAGE)], sem.at[slot]))
        return out

    @pl.when(n == 0)
    def _():
        for cp in page_copies(0, 0, 0):
            cp.start()

    ind = ind_ref[...]
    qbd = (ind * (qk_ref[...] * kng_ref[...])).astype(BF16)
    qr = qr_ref[...].astype(BF16)
    wukt = wukt_ref[...]
    qabs = jnp.dot(qbd, wukt, preferred_element_type=F32).astype(BF16)
    w2 = jnp.concatenate([wukt, qabs, jnp.zeros((8, MLA_KV_RANK), BF16)], axis=0)

    def head_ms(kk):
        rows = [jnp.sum(kk[HD * h:HD * (h + 1)], axis=0, keepdims=True) for h in range(HEADS)]
        return jnp.concatenate(rows + [jnp.ones((8 - HEADS, kk.shape[1]), F32)], axis=0) * (1.0 / HD)

    m = jnp.full((8, 1), NEG, F32)
    l = jnp.zeros((8, 1), F32)
    acc = jnp.zeros((8, MLA_KV_RANK), F32)
    for grp in range(ngroups):
        slot = (n * ngroups + grp) % 2
        if grp + 1 < ngroups:
            for cp in page_copies(n, grp + 1, 1 - slot):
                cp.start()
        else:
            @pl.when(n + 1 < pl.num_programs(0))
            def _():
                for cp in page_copies(n + 1, 0, 1 - slot):
                    cp.start()
        for cp in page_copies(n, grp, slot):
            cp.wait()
        cb = cbuf[slot].astype(BF16)
        krt = kbuf[slot].astype(BF16)
        kq = lax.dot_general(w2, cb, NT, preferred_element_type=F32)
        kt = kq[0:BW]
        num = kq[BW:BW + 8]
        s = (num * lax.rsqrt(head_ms(kt * kt) + EPS) + jnp.dot(qr, krt, preferred_element_type=F32)) * MLA_SCALE
        mn = jnp.maximum(m, jnp.max(s, axis=-1, keepdims=True))
        pb = jnp.exp(s - mn)
        al = jnp.exp(m - mn)
        l = al * l + jnp.sum(pb, axis=-1, keepdims=True)
        acc = al * acc + jnp.dot(pb.astype(BF16), cb, preferred_element_type=F32)
        m = mn

    cb = jnp.broadcast_to(cnew_ref[...], (8, MLA_KV_RANK)).astype(BF16)
    k1 = lax.dot_general(cb, wukt, NT, preferred_element_type=F32)
    krn = krnew_ref[...].astype(BF16).astype(F32)
    num1 = jnp.sum(qabs.astype(F32) * cb.astype(F32), axis=-1, keepdims=True)
    ms1 = jnp.sum(ind * (k1 * k1), axis=-1, keepdims=True) * (1.0 / HD)
    s1 = (num1 * lax.rsqrt(ms1 + EPS) + jnp.sum(qr.astype(F32) * krn, axis=-1, keepdims=True)) * MLA_SCALE
    mn1 = jnp.maximum(m, s1)
    p1 = jnp.exp(s1 - mn1)
    al1 = jnp.exp(m - mn1)
    lat = (al1 * acc + p1 * cb.astype(F32)) / (al1 * l + p1)
    o8 = jnp.dot(lat.astype(BF16), wuv_ref[...], preferred_element_type=F32)
    o_ref[...] = jnp.sum(o8 * ind, axis=0, keepdims=True)


def _mla_decode(layer, page_table, cache_ckv, cache_krt, qk, qr8, cnew, krnew, kng, wuk, wuv, pp):
    ns, npages = page_table.shape
    pt = page_table.reshape(-1)

    def per_sample(shape):
        return pl.BlockSpec((None,) + shape, lambda n, pt_ref: (n, 0, 0))

    def const(a):
        nd = a.ndim
        return pl.BlockSpec(a.shape, lambda n, pt_ref: (0,) * nd)

    ind = jnp.asarray(_IND8)
    hbm = pl.BlockSpec(memory_space=pl.ANY)
    grid_spec = pltpu.PrefetchScalarGridSpec(
        num_scalar_prefetch=1,
        grid=(ns,),
        in_specs=[per_sample((1, BW)), per_sample((8, MLA_ROPE)), per_sample((1, MLA_KV_RANK)), per_sample((1, MLA_ROPE)),
                  const(kng), const(wuk), const(wuv), const(ind), hbm, hbm],
        out_specs=per_sample((1, BW)),
        scratch_shapes=[pltpu.VMEM((2, pp * PAGE, MLA_KV_RANK), F32), pltpu.VMEM((2, MLA_ROPE, pp * PAGE), F32),
                        pltpu.SemaphoreType.DMA((2,))],
    )
    return pl.pallas_call(
        functools.partial(_mla_decode_kernel, layer=layer, npages=npages, pp=pp),
        grid_spec=grid_spec,
        out_shape=jax.ShapeDtypeStruct((ns, 1, BW), F32),
        compiler_params=_cp("arbitrary"),
        name="mla_decode",
    )(pt, qk, qr8, cnew, krnew, kng, wuk, wuv, ind, cache_ckv, cache_krt)


def _gdn_kernel(zc_ref, zba_ref, bat_ref, cw_ref, parr_ref, parc_ref, gout_ref, bones_ref, expb_ref, expg_ref, bdm_ref,
                o_ref, sfin_ref, xbuf, s_sc, *, tg):
    t = pl.program_id(1)
    c = GDN_CHUNK

    @pl.when(t == 0)
    def _():
        xbuf[0:8, :] = jnp.zeros((8, QKV_DIM), F32)
        s_sc[...] = jnp.zeros(s_sc.shape, F32)

    @pl.when(t > 0)
    def _():
        xbuf[5:8, :] = xbuf[tg + 5:tg + 8, :]

    xbuf[8:8 + tg, :] = zc_ref[:, 0:QKV_DIM]
    y = cw_ref[0:1, :] * xbuf[5:5 + tg, :]
    for i in range(1, 4):
        y = y + cw_ref[i:i + 1, :] * xbuf[5 + i:5 + i + tg, :]
    y = _silu(y)

    zba = zba_ref[...]
    beta_col = jax.nn.sigmoid(zba)
    g_col = -jnp.exp(parr_ref[0:1, :]) * jax.nn.softplus(zba + parr_ref[1:2, :])
    g_row = -jnp.exp(parc_ref[:, 0:1]) * jax.nn.softplus(bat_ref[...] + parc_ref[:, 1:2])

    row = lax.broadcasted_iota(jnp.int32, (c, c), 0)
    col = lax.broadcasted_iota(jnp.int32, (c, c), 1)
    incl = col <= row
    strict = col < row
    lt = incl.astype(BF16)
    ut = (col >= row).astype(BF16)
    bones = bones_ref[...]
    bdm = bdm_ref[...]
    expb, expg = expb_ref[...], expg_ref[...]
    brow = lax.broadcasted_iota(jnp.int32, (BW, BW), 0)
    bcl = lax.broadcasted_iota(jnp.int32, (BW, BW), 1)
    same_head = (brow // HD) == (bcl // HD)
    incl_bd = same_head & ((bcl % HD) <= (brow % HD))
    strict_bd = same_head & ((bcl % HD) < (brow % HD))
    eye_bd = (brow == bcl).astype(F32)
    same_head2 = jnp.concatenate([same_head, same_head], axis=1)

    def stack4(a):
        return jnp.concatenate([a, a, a, a], axis=0)

    def fold4(a):
        return (a[0:c] + a[c:2 * c]) + (a[2 * c:3 * c] + a[3 * c:4 * c])

    nchunk = tg // c
    pre = []
    for ci in range(nchunk):
        sl = slice(ci * c, (ci + 1) * c)
        q, k, v = y[sl, 0:256], y[sl, 256:512], y[sl, 512:768]
        qn = q * lax.rsqrt(_bdot(q * q, bones) + EPS) * (HD ** -0.5)
        kn = k * lax.rsqrt(_bdot(k * k, bones) + EPS)
        bcol = beta_col[sl, :]
        gcum_c = _sel_dot(lt, g_col[sl, :], True)
        gcum_r = _sel_dot(ut, g_row[:, sl], False)
        gx = _sel_dot(expg, gcum_c, False)
        bx = _sel_dot(expb, bcol, False)
        egx = jnp.exp(gx)
        rhs = _split2(jnp.concatenate([bx * v, bx * egx * kn], axis=1))
        ks = jnp.where(same_head, stack4(kn), 0.0).astype(BF16)
        qs = jnp.where(same_head, stack4(qn), 0.0).astype(BF16)
        gc_s = jnp.concatenate([gcum_c[:, 4 + h:5 + h] for h in range(HEADS)], axis=0)
        gr_s = jnp.concatenate([gcum_r[4 + h:5 + h, :] for h in range(HEADS)], axis=1)
        beta_s = jnp.concatenate([bcol[:, h:h + 1] for h in range(HEADS)], axis=0)
        dm = jnp.exp(jnp.where(incl_bd, gc_s - gr_s, NEG))
        a = jnp.where(strict_bd, beta_s * lax.dot_general(ks, ks, NT, preferred_element_type=F32) * dm, 0.0)
        qk = (lax.dot_general(qs, ks, NT, preferred_element_type=F32) * dm).astype(BF16)
        pre.append(dict(sl=sl, qn=qn, kn=kn, gx=gx, egx=egx, rhs=rhs, qk=qk, ps=_split2(-a), tinv=eye_bd - a))

    for _ in range(5):
        for d in pre:
            d["ps"] = _split2(_dot3(d["ps"], d["ps"]))
        for d in pre:
            d["tinv"] = d["tinv"] + _dot3(_split2(d["tinv"]), d["ps"])
    for d in pre:
        rhs = d["rhs"]
        d["x"] = fold4(jnp.where(same_head2, _dot3(_split2(d["tinv"]), (stack4(rhs[0]), stack4(rhs[1]))), 0.0))

    for d in pre:
        sl, qn, kn, gx, egx, qk, x = d["sl"], d["qn"], d["kn"], d["gx"], d["egx"], d["qk"], d["x"]
        s = s_sc[...]
        sb = s.astype(BF16)
        u = x[:, :BW] - jnp.dot(x[:, BW:].astype(BF16), sb, preferred_element_type=F32)
        ub = u.astype(BF16)
        o = egx * jnp.dot(qn.astype(BF16), sb, preferred_element_type=F32)
        o = o + fold4(jnp.where(same_head, jnp.dot(qk, stack4(ub), preferred_element_type=F32), 0.0))
        glast = gx[c - 1:c, :]
        kf = (kn * jnp.exp(glast - gx)).astype(BF16)
        s_new = jnp.exp(glast) * s + lax.dot_general(kf, ub, TN, preferred_element_type=F32)
        s_sc[...] = s_new * bdm
        on = o * lax.rsqrt(_bdot(o * o, bones) * (1.0 / HD) + EPS) * gout_ref[...]
        o_ref[sl, :] = on * _silu(zc_ref[sl, QKV_DIM:QKV_DIM + BW])

    @pl.when(t == pl.num_programs(1) - 1)
    def _():
        sfin_ref[...] = s_sc[...]


def _gdn_prompt(zc, zba, bat, n, t, p, tg):
    nt = t // tg
    consts = [p["conv_w"], p["gdn_par_r"], p["gdn_par_c"], p["gdn_out_gx"], jnp.asarray(_BONES64, BF16), jnp.asarray(_EXPB, BF16),
              jnp.asarray(_EXPG, BF16), jnp.asarray(_BDMASK)]
    return pl.pallas_call(
        functools.partial(_gdn_kernel, tg=tg),
        grid=(n, nt),
        in_specs=[pl.BlockSpec((tg, 1024), lambda b, i: (b * nt + i, 0)), pl.BlockSpec((tg, 128), lambda b, i: (b * nt + i, 0)),
                  pl.BlockSpec((8, tg), lambda b, i: (0, b * nt + i))] + [_full(c.shape) for c in consts],
        out_specs=[pl.BlockSpec((tg, BW), lambda b, i: (b * nt + i, 0)), pl.BlockSpec((None, BW, BW), lambda b, i: (b, 0, 0))],
        out_shape=[jax.ShapeDtypeStruct((n * t, BW), F32), jax.ShapeDtypeStruct((n, BW, BW), F32)],
        scratch_shapes=[pltpu.VMEM((8 + tg, QKV_DIM), F32), pltpu.VMEM((BW, BW), F32)],
        compiler_params=_cp("parallel", "arbitrary"),
        name="gdn_prompt",
    )(zc, zba, bat, *consts)


def _sample_tok_kernel(za_ref, zc_ref, zba_ref, zm_ref, sconv_ref, cw_ref, gmg_ref, gmw_ref, gmb_ref, parr_ref, bones_ref,
                       memg_ref, bmean_ref, oa_ref, ov_ref, oconv_ref, oq_ref, ok_ref, ovv_ref, ozg_ref, obeta_ref, og_ref, omq_ref):
    ge = _gelu(za_ref[...])
    v = _rms(ge[:, BW:], gmg_ref[...])
    ov_ref[...] = v
    oa_ref[...] = ge[:, :BW] * (gmw_ref[...] * v + gmb_ref[...])
    sc = sconv_ref[...]
    x = zc_ref[:, 0:QKV_DIM]
    y = (cw_ref[0:1, :] * sc[:, 0:768] + cw_ref[1:2, :] * sc[:, 768:1536] + cw_ref[2:3, :] * sc[:, 1536:2304]
         + cw_ref[3:4, :] * x)
    oconv_ref[:, 0:1536] = sc[:, 768:2304]
    oconv_ref[:, 1536:2304] = x
    y = _silu(y)
    q, k = y[:, 0:256], y[:, 256:512]
    bones = bones_ref[...]
    oq_ref[...] = (q * lax.rsqrt(_bdot(q * q, bones) + EPS) * (HD ** -0.5)).T
    ok_ref[...] = (k * lax.rsqrt(_bdot(k * k, bones) + EPS)).T
    ovv_ref[...] = y[:, 512:768].T
    ozg_ref[...] = zc_ref[:, QKV_DIM:QKV_DIM + BW].T
    zba = zba_ref[...]
    obeta_ref[...] = jax.nn.sigmoid(zba).T
    og_ref[...] = (-jnp.exp(parr_ref[0:1, :]) * jax.nn.softplus(zba + parr_ref[1:2, :])).T
    mq = zm_ref[...]
    omq_ref[...] = mq * lax.rsqrt(_bdot(mq * mq, bmean_ref[...]) + EPS) * memg_ref[...]


def _sample_tok(za, zc, zba, zm, sconv, p):
    ns = za.shape[0]
    args = [za, zc, zba, zm, sconv, p["conv_w"], p["gm_norm_g"], p["gm_w0"], p["gm_b0"], p["gdn_par_r"],
            jnp.asarray(_BONES64, BF16), p["mem_qn_gx"], jnp.asarray(_BMEAN64, BF16)]
    shapes = [(ns, BW), (ns, BW), (ns, 2304), (BW, ns), (BW, ns), (BW, ns), (BW, ns), (128, ns), (128, ns), (ns, BW)]
    return pl.pallas_call(
        _sample_tok_kernel,
        in_specs=[_full(a.shape) for a in args],
        out_specs=[_full(s) for s in shapes],
        out_shape=[jax.ShapeDtypeStruct(s, F32) for s in shapes],
        grid=(1,),
        compiler_params=_cp("arbitrary"),
        name="sample_tok",
    )(*args)


def _gdn_step_kernel(s_ref, q_ref, k_ref, v_ref, beta_ref, g_ref, zg_ref, gout_ref, so_ref, o_ref, o_sc):
    q, k = q_ref[...], k_ref[...]
    eg = jnp.exp(g_ref[...])
    beta = beta_ref[...]
    qk = jnp.sum(q * k, axis=0, keepdims=True)
    ssq = jnp.zeros(qk.shape, F32)
    for v in range(HD):
        sv = s_ref[v]
        sk = jnp.sum(sv * k, axis=0, keepdims=True)
        sq = jnp.sum(sv * q, axis=0, keepdims=True)
        u = beta * (v_ref[v:v + 1, :] - eg * sk)
        o = eg * sq + qk * u
        so_ref[v] = eg * sv + u * k
        o_sc[v:v + 1, :] = o
        ssq = ssq + o * o
    o_ref[...] = o_sc[...] * lax.rsqrt(ssq * (1.0 / HD) + EPS) * gout_ref[...] * _silu(zg_ref[...])


def _gdn_step(layer, state_t, q, k, v, beta, g, zg, gout_col):
    ns = state_t.shape[-1]
    vec = pl.BlockSpec((None, HD, ns), lambda h: (h, 0, 0))
    sca = pl.BlockSpec((None, 1, ns), lambda h: (h, 0, 0))
    return pl.pallas_call(
        _gdn_step_kernel,
        grid=(HEADS,),
        in_specs=[pl.BlockSpec((None, None, HD, HD, ns), lambda h: (layer, h, 0, 0, 0)), vec, vec, vec, sca, sca, vec, _full((HD, 1))],
        out_specs=[pl.BlockSpec((None, HD, HD, ns), lambda h: (h, 0, 0, 0)), vec],
        out_shape=[jax.ShapeDtypeStruct((HEADS, HD, HD, ns), F32), jax.ShapeDtypeStruct((HEADS, HD, ns), F32)],
        scratch_shapes=[pltpu.VMEM((HD, ns), F32)],
        compiler_params=_cp("parallel"),
        name="gdn_step",
    )(state_t, q, k, v, beta, g, zg, gout_col)


def _mem_kv_kernel(x_ref, g_ref, w_ref, kg_ref, bmean_ref, ok_ref, ov_ref):
    kv = _bdot(_rms(x_ref[...], g_ref[...]), w_ref[...])
    k = kv[:, 0:BW]
    ok_ref[...] = k * lax.rsqrt(_bdot(k * k, bmean_ref[...]) + EPS) * kg_ref[...]
    ov_ref[...] = kv[:, BW:]


def _mem_kv(mem, g, w, kgx, tm):
    m = mem.shape[0]
    bmean = jnp.asarray(_BMEAN64, BF16)
    return pl.pallas_call(
        _mem_kv_kernel,
        grid=(m // tm,),
        in_specs=[pl.BlockSpec((tm, D_MODEL), lambda i: (i, 0)), _full((1, D_MODEL)), _full(w.shape), _full((1, BW)), _full((BW, BW))],
        out_specs=[pl.BlockSpec((tm, BW), lambda i: (i, 0))] * 2,
        out_shape=[jax.ShapeDtypeStruct((m, BW), F32)] * 2,
        compiler_params=_cp("parallel"),
        name="mem_kv",
    )(mem, g, w, kgx, bmean)


def _mem_attn_kernel(q_ref, k_ref, v_ref, gq_ref, bmean_ref, o_ref):
    q = q_ref[...]
    qn = q * lax.rsqrt(_bdot(q * q, bmean_ref[...]) + EPS) * gq_ref[...]
    kb = k_ref[...].astype(BF16)
    vb = v_ref[...].astype(BF16)
    lh = _lane_head()
    out = jnp.zeros(q.shape, F32)
    for h in range(HEADS):
        mh = lh == h
        s = lax.dot_general(jnp.where(mh, qn, 0.0).astype(BF16), kb, NT, preferred_element_type=F32) * (HD ** -0.5)
        e = jnp.exp(s - jnp.max(s, axis=-1, keepdims=True))
        pr = e / jnp.sum(e, axis=-1, keepdims=True)
        out = out + jnp.where(mh, jnp.dot(pr.astype(BF16), vb, preferred_element_type=F32), 0.0)
    o_ref[...] = out


def _mem_attn(zm, mk, mv, gqx, n, t, mt, tq):
    nq = t // tq
    bmean = jnp.asarray(_BMEAN64, BF16)
    return pl.pallas_call(
        _mem_attn_kernel,
        grid=(n, nq),
        in_specs=[pl.BlockSpec((tq, BW), lambda b, i: (b * nq + i, 0)), pl.BlockSpec((mt, BW), lambda b, i: (b, 0)),
                  pl.BlockSpec((mt, BW), lambda b, i: (b, 0)), _full((1, BW)), _full((BW, BW))],
        out_specs=pl.BlockSpec((tq, BW), lambda b, i: (b * nq + i, 0)),
        out_shape=jax.ShapeDtypeStruct((n * t, BW), F32),
        compiler_params=_cp("parallel", "parallel"),
        name="mem_attn",
    )(zm, mk, mv, gqx, bmean)


def _mem_attn_s_kernel(q_ref, k_ref, v_ref, ind_ref, o_ref, *, bn):
    ind = ind_ref[...]
    for i in range(bn):
        qbd = (ind * q_ref[i:i + 1, :]).astype(BF16)
        s = jnp.dot(qbd, k_ref[i].astype(BF16), preferred_element_type=F32) * (HD ** -0.5)
        e = jnp.exp(s - jnp.max(s, axis=-1, keepdims=True))
        pr = e / jnp.sum(e, axis=-1, keepdims=True)
        o8 = lax.dot_general(pr.astype(BF16), v_ref[i].astype(BF16), NT, preferred_element_type=F32)
        o_ref[i:i + 1, :] = jnp.sum(o8 * ind, axis=0, keepdims=True)


def _mem_attn_s(layer, mqn, cache_k, cache_v, bn):
    ns = mqn.shape[0]
    mt = cache_k.shape[3]
    kv_spec = pl.BlockSpec((None, bn, BW, mt), lambda i: (layer, i, 0, 0))
    return pl.pallas_call(
        functools.partial(_mem_attn_s_kernel, bn=bn),
        grid=(ns // bn,),
        in_specs=[pl.BlockSpec((bn, BW), lambda i: (i, 0)), kv_spec, kv_spec, _full((8, BW))],
        out_specs=pl.BlockSpec((bn, BW), lambda i: (i, 0)),
        out_shape=jax.ShapeDtypeStruct((ns, BW), F32),
        compiler_params=_cp("parallel"),
        name="mem_attn_s",
    )(mqn, cache_k, cache_v, jnp.asarray(_IND8))


def _merge_kernel(x_ref, a_ref, b_ref, c_ref, m_ref, g1_ref, wg_ref, wb_ref, wo_ref, g2_ref, wrh_ref, wrl_ref, br_ref,
                  x1_ref, h2_ref, ei_ref, ew_ref):
    x = x_ref[...]
    hb = _rms(x, g1_ref[...]).astype(BF16)
    acc = jnp.zeros(x.shape, F32)
    for b, br in enumerate((a_ref, b_ref, c_ref, m_ref)):
        gate = jax.nn.sigmoid(jnp.dot(hb, wg_ref[:, b * D_MODEL:(b + 1) * D_MODEL], preferred_element_type=F32))
        acc = acc + gate * jnp.dot(br[...].astype(BF16), wb_ref[b], preferred_element_type=F32)
    x1 = x + jnp.dot(acc.astype(BF16), wo_ref[...], preferred_element_type=F32)
    x1_ref[...] = x1
    h2 = _rms(x1, g2_ref[...])
    h2_ref[...] = h2
    logits = _dot3(_split2(h2), (wrh_ref[...], wrl_ref[...])) + br_ref[...]
    lane = lax.broadcasted_iota(jnp.int32, (1, 128), 1).astype(F32)
    big = 1e9
    lg = jnp.where(lane < N_GROUPS, logits, NEG)
    mg = jnp.max(lg, axis=-1, keepdims=True)
    g_w = 1.0 / jnp.sum(jnp.exp(lg - mg), axis=-1, keepdims=True)
    gi = jnp.min(jnp.where(lg == mg, lane, big), axis=-1, keepdims=True)
    sel = (lane >= N_GROUPS) & (lane < N_GROUPS + N_EXPERTS) & (jnp.floor((lane - N_GROUPS) * (1.0 / EPG)) == gi)
    le = jnp.where(sel, logits, NEG)
    m1 = jnp.max(le, axis=-1, keepdims=True)
    i1 = jnp.min(jnp.where(le == m1, lane, big), axis=-1, keepdims=True)
    le2 = jnp.where(lane == i1, NEG, le)
    m2 = jnp.max(le2, axis=-1, keepdims=True)
    i2 = jnp.min(jnp.where(le2 == m2, lane, big), axis=-1, keepdims=True)
    z = jnp.sum(jnp.exp(le - m1), axis=-1, keepdims=True)
    p1 = 1.0 / z
    p2 = jnp.exp(m2 - m1) / z
    w1 = p1 / (p1 + p2) * g_w
    w2 = p2 / (p1 + p2) * g_w
    ei_ref[...] = jnp.where(lane == 0, i1 - N_GROUPS, jnp.where(lane == 1, i2 - N_GROUPS, 0.0)).astype(jnp.int32)
    ew_ref[...] = jnp.where(lane == 0, w1, jnp.where(lane == 1, w2, 0.0))


def _merge(x, branches, p, tm):
    m = x.shape[0]
    consts = [p["norm1_g"], p["w_gate"], p["w_branch"], p["w_out"], p["norm2_g"], *_split2(p["w_router"]), p["b_router"]]
    tile = lambda w: pl.BlockSpec((tm, w), lambda i: (i, 0))
    return pl.pallas_call(
        _merge_kernel,
        grid=(m // tm,),
        in_specs=[tile(D_MODEL)] + [tile(BW)] * 4 + [_full(c.shape) for c in consts],
        out_specs=[tile(D_MODEL), tile(D_MODEL), tile(128), tile(128)],
        out_shape=[jax.ShapeDtypeStruct((m, D_MODEL), F32), jax.ShapeDtypeStruct((m, D_MODEL), F32),
                   jax.ShapeDtypeStruct((m, 128), jnp.int32), jax.ShapeDtypeStruct((m, 128), F32)],
        compiler_params=_cp("parallel"),
        name="merge",
    )(x, *branches, *consts)


def _dispatch_kernel(poff_ref, plen_ref, nv_ref, pos_ref, h_ref, xs_out, zbuf, sem, zsem, *, tmd, te):
    @pl.when(pl.program_id(0) == 0)
    def _():
        zbuf[...] = jnp.zeros(zbuf.shape, F32)

        def pad_copies(e):
            off = poff_ref[e]
            head = (-off) & 7
            body = plen_ref[e] - head
            out = [(i < head, pltpu.make_async_copy(zbuf.at[pl.ds(0, 1)], xs_out.at[pl.ds(off + i, 1)], zsem)) for i in range(7)]
            b = te // 2
            while b >= 8:
                start = pl.multiple_of(off + head + (body & ~(2 * b - 1)), 8)
                out.append(((body & b) != 0, pltpu.make_async_copy(zbuf.at[pl.ds(0, b)], xs_out.at[pl.ds(start, b)], zsem)))
                b //= 2
            return out

        def tail_copies(t):
            return [pltpu.make_async_copy(zbuf, xs_out.at[pl.ds(pl.multiple_of(t * te + k * zbuf.shape[0], 8), zbuf.shape[0])], zsem)
                    for k in range(te // zbuf.shape[0])]

        def start_pad(e, carry):
            for cond, cp in pad_copies(e):
                pl.when(cond)(cp.start)
            return carry

        def wait_pad(e, carry):
            for cond, cp in pad_copies(e):
                pl.when(cond)(cp.wait)
            return carry

        def start_tail(t, carry):
            for cp in tail_copies(t):
                cp.start()
            return carry

        def wait_tail(t, carry):
            for cp in tail_copies(t):
                cp.wait()
            return carry

        n_tiles = xs_out.shape[0] // te
        lax.fori_loop(0, N_EXPERTS, start_pad, 0)
        lax.fori_loop(nv_ref[0], n_tiles, start_tail, 0)
        lax.fori_loop(0, N_EXPERTS, wait_pad, 0)
        lax.fori_loop(nv_ref[0], n_tiles, wait_tail, 0)

    def issue(i, carry):
        for s in range(2):
            pltpu.make_async_copy(h_ref.at[pl.ds(i, 1)], xs_out.at[pl.ds(pos_ref[2 * i + s], 1)], sem).start()
        return carry

    lax.fori_loop(0, tmd, issue, 0, unroll=8)
    for s in range(2):
        pltpu.make_async_copy(h_ref, xs_out.at[pl.ds(0, tmd)], sem).wait()


def _dispatch(pad_off, pad_len, n_valid, pos, h2, rows, tmd, te):
    m = h2.shape[0]
    grid_spec = pltpu.PrefetchScalarGridSpec(
        num_scalar_prefetch=3,
        grid=(m // tmd,),
        in_specs=[pl.BlockSpec((2 * tmd,), lambda i, po, pn, nv: (i,), memory_space=pltpu.SMEM),
                  pl.BlockSpec((tmd, D_MODEL), lambda i, po, pn, nv: (i, 0))],
        out_specs=pl.BlockSpec(memory_space=pl.ANY),
        scratch_shapes=[pltpu.VMEM((max(te // 2, 8), D_MODEL), F32), pltpu.SemaphoreType.DMA(()), pltpu.SemaphoreType.DMA(())],
    )
    return pl.pallas_call(
        functools.partial(_dispatch_kernel, tmd=tmd, te=te),
        grid_spec=grid_spec,
        out_shape=jax.ShapeDtypeStruct((rows, D_MODEL), F32),
        compiler_params=_cp("arbitrary"),
        name="moe_dispatch",
    )(pad_off, pad_len, n_valid, pos, h2)


def _expert_kernel(te_ref, nv_ref, x_ref, wg_ref, wu_ref, wd_ref, o_ref):
    del te_ref

    @pl.when(pl.program_id(0) < nv_ref[0])
    def _():
        xb = x_ref[...].astype(BF16)
        gt = jnp.dot(xb, wg_ref[...].astype(BF16), preferred_element_type=F32)
        up = jnp.dot(xb, wu_ref[...].astype(BF16), preferred_element_type=F32)
        o_ref[...] = jnp.dot((_silu(gt) * up).astype(BF16), wd_ref[...].astype(BF16), preferred_element_type=F32)

    @pl.when(pl.program_id(0) >= nv_ref[0])
    def _():
        o_ref[...] = jnp.zeros(o_ref.shape, F32)


def _experts(layer, tile_expert, n_valid, xs, w_gate, w_up, w_down, te):
    rows = xs.shape[0]

    def xmap(i, te_ref, nv_ref):
        return (jnp.minimum(i, nv_ref[0] - 1), 0)

    def wmap(i, te_ref, nv_ref):
        return (layer, te_ref[i], 0, 0)

    grid_spec = pltpu.PrefetchScalarGridSpec(
        num_scalar_prefetch=2,
        grid=(rows // te,),
        in_specs=[pl.BlockSpec((te, D_MODEL), xmap), pl.BlockSpec((None, None, D_MODEL, D_EXPERT), wmap),
                  pl.BlockSpec((None, None, D_MODEL, D_EXPERT), wmap), pl.BlockSpec((None, None, D_EXPERT, D_MODEL), wmap)],
        out_specs=pl.BlockSpec((te, D_MODEL), lambda i, te_ref, nv_ref: (i, 0)),
    )
    return pl.pallas_call(
        _expert_kernel,
        grid_spec=grid_spec,
        out_shape=jax.ShapeDtypeStruct((rows, D_MODEL), F32),
        compiler_params=_cp("arbitrary"),
        name="moe_experts",
    )(tile_expert, n_valid, xs, w_gate, w_up, w_down)


def _combine_kernel(pos_ref, x1_ref, ew_ref, ys_hbm, o_ref, r0, r1, sem, *, tmc):
    bufs = (r0, r1)

    def issue(i, carry):
        for s in range(2):
            pltpu.make_async_copy(ys_hbm.at[pl.ds(pos_ref[2 * i + s], 1)], bufs[s].at[pl.ds(i, 1)], sem).start()
        return carry

    lax.fori_loop(0, tmc, issue, 0, unroll=8)
    for s in range(2):
        pltpu.make_async_copy(ys_hbm.at[pl.ds(0, tmc)], bufs[s], sem).wait()
    ew = ew_ref[...]
    o_ref[...] = x1_ref[...] + ew[:, 0:1] * r0[...] + ew[:, 1:2] * r1[...]


def _combine(pos, x1, ew, ys, tmc):
    m = x1.shape[0]
    return pl.pallas_call(
        functools.partial(_combine_kernel, tmc=tmc),
        grid=(m // tmc,),
        in_specs=[pl.BlockSpec((2 * tmc,), lambda i: (i,), memory_space=pltpu.SMEM),
                  pl.BlockSpec((tmc, D_MODEL), lambda i: (i, 0)), pl.BlockSpec((tmc, 128), lambda i: (i, 0)),
                  pl.BlockSpec(memory_space=pl.ANY)],
        out_specs=pl.BlockSpec((tmc, D_MODEL), lambda i: (i, 0)),
        out_shape=jax.ShapeDtypeStruct((m, D_MODEL), F32),
        scratch_shapes=[pltpu.VMEM((tmc, D_MODEL), F32), pltpu.VMEM((tmc, D_MODEL), F32), pltpu.SemaphoreType.DMA(())],
        compiler_params=_cp("arbitrary"),
        name="moe_combine",
    )(pos, x1, ew, ys)


def _moe(layer, x1, h2, ei, ew, w_gate, w_up, w_down, te, tmd):
    m = x1.shape[0]
    flat_e = ei[:, 0:2].reshape(-1)
    onehot = (flat_e[:, None] == jnp.arange(N_EXPERTS, dtype=jnp.int32)[None, :]).astype(jnp.int32)
    csum = jnp.cumsum(onehot, axis=0)
    rank = jnp.sum(csum * onehot, axis=1) - 1
    counts = csum[-1]
    padded = ((counts + te - 1) // te) * te
    pend = jnp.cumsum(padded)
    pstart = pend - padded
    pos = (jnp.sum(onehot * pstart[None, :], axis=1) + rank).astype(jnp.int32)
    rows = ((2 * m + N_EXPERTS * (te - 1)) // te) * te
    n_tiles = rows // te
    n_valid = (pend[-1] // te).astype(jnp.int32).reshape(1)
    tile_start = jnp.arange(n_tiles, dtype=jnp.int32) * te
    tile_expert = jnp.minimum(jnp.sum((tile_start[:, None] >= pend[None, :]).astype(jnp.int32), axis=1), N_EXPERTS - 1)
    last_e = jnp.take(tile_expert, jnp.maximum(n_valid[0] - 1, 0))
    tile_expert = jnp.where(jnp.arange(n_tiles) < n_valid[0], tile_expert, last_e).astype(jnp.int32)
    xs = _dispatch((pstart + counts).astype(jnp.int32), (padded - counts).astype(jnp.int32), n_valid, pos, h2, rows, tmd, te)
    ys = _experts(layer, tile_expert, n_valid, xs, w_gate, w_up, w_down, te)
    return _combine(pos, x1, ew, ys, tmd)


def _tile4(v):
    return jnp.tile(v, HEADS).reshape(1, BW)


def _prep_layer(l, w):
    w_in = w["w_in"][l]
    z = lambda n: jnp.zeros((D_MODEL, n), F32)
    b_al = w_in[:, 1952:1960]
    w_small = jnp.concatenate(
        [w_in[:, 0:512], w_in[:, 512:768], w_in[:, 768:896], z(64), w_in[:, 896:928], z(32), w_in[:, 928:1696],
         w_in[:, 1696:1952], w_in[:, 1960:2216], b_al, z(120)], axis=1).astype(BF16)
    uq = w["mla_w_uq"][l]
    w_uq = jnp.pad(uq, ((0, 0), (0, 0), (0, 32))).reshape(256, 512).astype(BF16)
    qg = jnp.tile(jnp.concatenate([w["mla_qn_g"][l], w["mla_qr_g"][l], jnp.zeros((32,), F32)]), HEADS).reshape(1, 512)
    uk = w["mla_w_uk"][l]
    w_uk_p = jnp.pad(uk, ((0, 0), (0, 0), (0, 64))).reshape(128, 512).astype(BF16)
    kg = jnp.tile(jnp.concatenate([w["mla_kn_g"][l], jnp.zeros((64,), F32)]), HEADS).reshape(1, 512)
    kr_g = jnp.concatenate([jnp.zeros((64,), F32), w["mla_kr_g"][l], jnp.zeros((32,), F32)]).reshape(1, 128)
    uv = w["mla_w_uv"][l]
    w_uv_p = jnp.stack([jnp.pad(uv[:, h, :], ((0, 0), (64 * h, BW - 64 * h - 64))) for h in range(HEADS)]).astype(BF16)
    par_r = jnp.zeros((8, 128), F32).at[0, 4:8].set(w["gdn_a_log"][l]).at[1, 4:8].set(w["gdn_dt_bias"][l])
    par_c = jnp.zeros((8, 128), F32).at[4:8, 0].set(w["gdn_a_log"][l]).at[4:8, 1].set(w["gdn_dt_bias"][l])
    w_router = jnp.concatenate([w["moe_wg"][l], w["moe_we"][l], jnp.zeros((D_MODEL, 128 - 36), F32)], axis=1)
    b_router = jnp.concatenate([w["moe_bg"][l], w["moe_be"][l], jnp.zeros((128 - 36,), F32)]).reshape(1, 128)
    return {
        "norm1_g": w["norm1_g"][l].reshape(1, D_MODEL), "w_small": w_small, "w_bat": b_al.T.astype(BF16),
        "w_gate": w_in[:, 2216:].astype(BF16),
        "gm_norm_g": w["gm_norm_g"][l].reshape(1, BW), "gm_ws": w["gm_ws"][l],
        "gm_bfull": jnp.repeat(w["gm_b"][l].T, HD, axis=1),
        "gm_w0": jnp.repeat(w["gm_ws"][l][:, 0, 0], HD).reshape(1, BW), "gm_b0": jnp.repeat(w["gm_b"][l][:, 0], HD).reshape(1, BW),
        "mla_cq_g": w["mla_cq_g"][l].reshape(1, 256), "w_uq": w_uq, "qg": qg, "mla_ckv_g": w["mla_ckv_g"][l].reshape(1, 128),
        "kr_g": kr_g, "w_uk": w_uk_p, "kg": kg, "w_uv": w_uv_p,
        "w_uk_c": uk.reshape(128, BW).T.astype(BF16), "w_uv_c": uv.reshape(128, BW).astype(BF16), "kn_gx": _tile4(w["mla_kn_g"][l]),
        "conv_w": w["gdn_conv_w"][l], "gdn_par_r": par_r, "gdn_par_c": par_c, "gdn_out_gx": _tile4(w["gdn_out_g"][l]),
        "gdn_out_gc": w["gdn_out_g"][l].reshape(HD, 1),
        "mem_norm_g": w["mem_norm_g"][l].reshape(1, D_MODEL), "mem_w_kv": w["mem_w_kv"][l].astype(BF16),
        "mem_qn_gx": _tile4(w["mem_qn_g"][l]), "mem_kn_gx": _tile4(w["mem_kn_g"][l]),
        "w_branch": w["w_branch"][l].astype(BF16), "w_out": w["w_out"][l].astype(BF16),
        "norm2_g": w["norm2_g"][l].reshape(1, D_MODEL), "w_router": w_router, "b_router": b_router,
    }


def _rope_tables(pos):
    half = MLA_ROPE // 2
    inv = ROPE_THETA ** (-jnp.arange(half, dtype=F32) / half)
    ang = pos.astype(F32)[:, None] * inv[None, :]
    cos, sin = jnp.cos(ang), jnp.sin(ang)
    t = pos.shape[0]
    one, zero = jnp.ones((t, 64), F32), jnp.zeros((t, 64), F32)
    z16, z32 = jnp.zeros((t, 16), F32), jnp.zeros((t, 32), F32)
    return (jnp.concatenate([one, cos, cos, jnp.ones((t, 32), F32)], axis=1),
            jnp.concatenate([zero, -sin, z16, z32], axis=1),
            jnp.concatenate([zero, z16, sin, z32], axis=1))


def kernel(x_prompt, mem_prompt, x_sample, cache_mla_ckv, cache_mla_kr, cache_mem_k, cache_mem_v, state_gdn, state_conv,
           page_table, norm1_g, w_in, gm_norm_g, gm_ws, gm_b, mla_cq_g, mla_w_uq, mla_qn_g, mla_qr_g, mla_ckv_g, mla_kr_g,
           mla_w_uk, mla_kn_g, mla_w_uv, gdn_conv_w, gdn_a_log, gdn_dt_bias, gdn_out_g, mem_norm_g, mem_w_kv, mem_qn_g,
           mem_kn_g, w_branch, w_out, norm2_g, moe_wg, moe_bg, moe_we, moe_be, moe_w_gate, moe_w_up, moe_w_down):
    w = dict(norm1_g=norm1_g, w_in=w_in, gm_norm_g=gm_norm_g, gm_ws=gm_ws, gm_b=gm_b, mla_cq_g=mla_cq_g, mla_w_uq=mla_w_uq,
             mla_qn_g=mla_qn_g, mla_qr_g=mla_qr_g, mla_ckv_g=mla_ckv_g, mla_kr_g=mla_kr_g, mla_w_uk=mla_w_uk, mla_kn_g=mla_kn_g,
             mla_w_uv=mla_w_uv, gdn_conv_w=gdn_conv_w, gdn_a_log=gdn_a_log, gdn_dt_bias=gdn_dt_bias, gdn_out_g=gdn_out_g,
             mem_norm_g=mem_norm_g, mem_w_kv=mem_w_kv, mem_qn_g=mem_qn_g, mem_kn_g=mem_kn_g, w_branch=w_branch, w_out=w_out,
             norm2_g=norm2_g, moe_wg=moe_wg, moe_bg=moe_bg, moe_we=moe_we, moe_be=moe_be)
    depth = w_in.shape[0]
    bp, tp, _ = x_prompt.shape
    bs = x_sample.shape[0]
    mt = mem_prompt.shape[1]
    n_pages = page_table.shape[1]
    past_len = n_pages * cache_mla_ckv.shape[2]
    mp = bp * tp

    tm_p = min(512, mp)
    tq = min(256, tp)
    ta = min(512, tp)
    tg = min(256, tp)
    pp = min(32, n_pages)
    cache_krt = jnp.swapaxes(cache_mla_kr, 2, 3)
    tabs_p = _rope_tables(jnp.arange(tp, dtype=jnp.int32))
    tabs_s = _rope_tables(jnp.full((bs,), past_len, jnp.int32))

    xp = x_prompt.reshape(mp, D_MODEL)
    xs = x_sample.reshape(bs, D_MODEL)
    mem = mem_prompt.reshape(bp * mt, D_MODEL)
    cache_k = cache_mem_k.transpose(0, 1, 3, 4, 2).reshape(depth, bs, BW, mt)
    cache_v = cache_mem_v.transpose(0, 1, 3, 4, 2).reshape(depth, bs, BW, mt)
    state_t = state_gdn.transpose(0, 2, 3, 4, 1)
    rows_p, rows_s = [], []
    for l in range(depth):
        p = _prep_layer(l, w)
        mk, mv = _mem_kv(mem, p["mem_norm_g"], p["mem_w_kv"], p["mem_kn_gx"], min(512, bp * mt))
        za, zb, zc, zm, zba, bat = _in_proj(xp, p["norm1_g"], p["w_small"], p["w_bat"], tm_p)
        a_out = _gmlp(za, mp, p["gm_norm_g"], p["gm_ws"], p["gm_bfull"], tm_p)
        q4, k4, ckv, kr, ckvt, _ = _mla_pre(zb, mp, tabs_p, tp // min(tm_p, tp), p, min(tm_p, tp), ta)
        b_out = _mla_attn(q4, k4, ckvt, p["w_uv"], bp, tp, ta)
        c_out, sfin = _gdn_prompt(zc, zba, bat, bp, tp, p, tg)
        m_out = _mem_attn(zm, mk, mv, p["mem_qn_gx"], bp, tp, mt, tq)
        x1, h2, ei, ew = _merge(xp, (a_out, b_out, c_out, m_out), p, tm_p)
        xp = _moe(l, x1, h2, ei, ew, moe_w_gate, moe_w_up, moe_w_down, min(256, mp), min(512, mp))
        s_p = jnp.stack([sfin[:, 64 * h:64 * h + 64, 64 * h:64 * h + 64] for h in range(HEADS)], axis=1).transpose(0, 1, 3, 2)
        conv_p = zc.reshape(bp, tp, 1024)[:, tp - 3:, 0:QKV_DIM]
        rows_p.append((ckv.reshape(bp, tp, 128), kr.reshape(bp, tp, MLA_ROPE), s_p, conv_p,
                       mk.reshape(bp, mt, HEADS, HD), mv.reshape(bp, mt, HEADS, HD)))
        za, zb, zc, zm, zba, _ = _in_proj(xs, p["norm1_g"], p["w_small"], p["w_bat"], bs)
        a_s, v_s, conv_s, gq, gk, gv, zg, beta, gdec, mqn = _sample_tok(za, zc, zba, zm, state_conv[l].reshape(bs, 3 * QKV_DIM), p)
        _, _, ckv_s, kr_s, _, q32 = _mla_pre(zb, bs, tabs_s, 1, p, bs, bs)
        q3 = q32.reshape(bs, HEADS, 128)
        qk = q3[:, :, 0:64].reshape(bs, 1, BW)
        qr8 = jnp.pad(q3[:, :, 64:96], ((0, 0), (0, 4), (0, 0)))
        b_s = _mla_decode(l, page_table, cache_mla_ckv, cache_krt, qk, qr8, ckv_s.reshape(bs, 1, 128),
                          kr_s.reshape(bs, 1, MLA_ROPE), p["kn_gx"], p["w_uk_c"], p["w_uv_c"], pp).reshape(bs, BW)
        hv = lambda a: a.reshape(HEADS, HD, bs)
        s_new, c_t = _gdn_step(l, state_t, hv(gq), hv(gk), hv(gv), beta[0:4].reshape(HEADS, 1, bs),
                               gdec[4:8].reshape(HEADS, 1, bs), hv(zg), p["gdn_out_gc"])
        m_s = _mem_attn_s(l, mqn, cache_k, cache_v, min(8, bs))
        x1, h2, ei, ew = _merge(xs, (a_s, b_s, c_t.reshape(BW, bs).T, m_s), p, bs)
        xs = _moe(l, x1, h2, ei, ew, moe_w_gate, moe_w_up, moe_w_down, min(32, bs), bs)
        rows_s.append((ckv_s.reshape(bs, 1, 128), kr_s.reshape(bs, 1, MLA_ROPE), s_new.transpose(3, 0, 1, 2),
                       conv_s.reshape(bs, 3, QKV_DIM), v_s.reshape(bs, 1, BW)))
    p_out = [jnp.stack(a) for a in zip(*rows_p)]
    s_out = [jnp.stack(a) for a in zip(*rows_s)]
    return (xp.reshape(bp, tp, D_MODEL), xs.reshape(bs, 1, D_MODEL), *p_out, *s_out)
```

```python
import functools

import numpy as np
import jax
import jax.numpy as jnp
from jax import lax
from jax.experimental import pallas as pl
from jax.experimental.pallas import tpu as pltpu

F32 = jnp.float32
BF16 = jnp.bfloat16
HI = lax.Precision.HIGHEST
EPS = 1e-6
NEG = float("-inf")

D_MODEL = 1024
HEADS = 4
HD = 64
BW = 256
MLA_ROPE = 32
MLA_KV_RANK = 128
MLA_SCALE = 96.0 ** -0.5
ROPE_THETA = 10000.0
GM_CHUNK = 128
GDN_CHUNK = 64
QKV_DIM = 768
N_GROUPS = 4
EPG = 8
N_EXPERTS = 32
D_EXPERT = 256
PAGE = 128
VMEM_LIMIT = 56 * 1024 * 1024

NT = (((1,), (1,)), ((), ()))
TN = (((0,), (0,)), ((), ()))


def _cp(*sem):
    return pltpu.CompilerParams(dimension_semantics=sem, vmem_limit_bytes=VMEM_LIMIT)


def _rms(x, g):
    ms = jnp.sum(x * x, axis=-1, keepdims=True) * (1.0 / x.shape[-1])
    return x * lax.rsqrt(ms + EPS) * g


def _bdot(a, b):
    return jnp.dot(a.astype(BF16), b.astype(BF16), preferred_element_type=F32)


def _bdot_nt(a, b):
    return lax.dot_general(a.astype(BF16), b.astype(BF16), NT, preferred_element_type=F32)


def _hdot(a, b):
    return jnp.dot(a, b, precision=HI, preferred_element_type=F32)


def _split2(a):
    hi = a.astype(BF16)
    return hi, (a - hi.astype(F32)).astype(BF16)


def _dot3(a, b):
    d = lambda x, y: jnp.dot(x, y, preferred_element_type=F32)
    return d(a[0], b[0]) + (d(a[0], b[1]) + d(a[1], b[0]))


def _sel_dot(w01, x, left):
    x0 = x.astype(BF16)
    r1 = x - x0.astype(F32)
    x1 = r1.astype(BF16)
    x2 = (r1 - x1.astype(F32)).astype(BF16)
    d = (lambda p: jnp.dot(w01, p, preferred_element_type=F32)) if left else (lambda p: jnp.dot(p, w01, preferred_element_type=F32))
    return d(x0) + (d(x1) + d(x2))


def _silu(x):
    return x * jax.nn.sigmoid(x)


def _gelu(x):
    return 0.5 * x * (1.0 + lax.erf(x * 0.7071067811865476))


def _lane_head(width=BW):
    return lax.broadcasted_iota(jnp.int32, (1, width), 1) // HD


def _full(shape):
    n = len(shape)
    return pl.BlockSpec(shape, lambda *_: (0,) * n)


def _block_mean(width, segs):
    m = np.zeros((width, width), np.float32)
    for a, b in segs:
        m[a:b, a:b] = 1.0 / (b - a)
    return m


_BMEAN64 = _block_mean(BW, [(64 * h, 64 * h + 64) for h in range(HEADS)])
_BONES64 = _BMEAN64 * 64.0
_BQ = _block_mean(512, [(128 * h, 128 * h + 64) for h in range(HEADS)] + [(128 * h + 64, 128 * h + 96) for h in range(HEADS)])
_BK = _block_mean(512, [(128 * h, 128 * h + 64) for h in range(HEADS)])
_IND8 = np.zeros((8, BW), np.float32)
for _h in range(HEADS):
    _IND8[_h, 64 * _h:64 * _h + 64] = 1.0
_EXPB = np.zeros((128, BW), np.float32)
_EXPG = np.zeros((128, BW), np.float32)
for _h in range(HEADS):
    _EXPB[_h, 64 * _h:64 * _h + 64] = 1.0
    _EXPG[4 + _h, 64 * _h:64 * _h + 64] = 1.0
_BDMASK = (_BONES64 > 0).astype(np.float32)


def _in_proj_kernel(x_ref, g_ref, w_ref, wbat_ref, oa_ref, ob_ref, oc_ref, om_ref, oba_ref, obat_ref):
    hb = _rms(x_ref[...], g_ref[...]).astype(BF16)
    oa_ref[...] = jnp.dot(hb, w_ref[:, 0:512], preferred_element_type=F32)
    ob_ref[...] = jnp.dot(hb, w_ref[:, 512:1024], preferred_element_type=F32)
    oc_ref[...] = jnp.dot(hb, w_ref[:, 1024:2048], preferred_element_type=F32)
    om_ref[...] = jnp.dot(hb, w_ref[:, 2048:2304], preferred_element_type=F32)
    oba_ref[...] = jnp.dot(hb, w_ref[:, 2304:2432], preferred_element_type=F32)
    obat_ref[...] = lax.dot_general(wbat_ref[...], hb, NT, preferred_element_type=F32)


def _in_proj(x, g, w, wbat, tm):
    m = x.shape[0]
    widths = (512, 512, 1024, 256, 128)
    return pl.pallas_call(
        _in_proj_kernel,
        grid=(m // tm,),
        in_specs=[pl.BlockSpec((tm, D_MODEL), lambda i: (i, 0)), _full((1, D_MODEL)), _full(w.shape), _full(wbat.shape)],
        out_specs=[pl.BlockSpec((tm, n), lambda i: (i, 0)) for n in widths] + [pl.BlockSpec((8, tm), lambda i: (0, i))],
        out_shape=[jax.ShapeDtypeStruct((m, n), F32) for n in widths] + [jax.ShapeDtypeStruct((8, m), F32)],
        compiler_params=_cp("parallel"),
        name="in_proj",
    )(x, g, w, wbat)


def _gmlp_kernel(z_ref, g_ref, ws_ref, b_ref, o_ref, *, ta):
    row = lax.broadcasted_iota(jnp.int32, (GM_CHUNK, GM_CHUNK), 0)
    col = lax.broadcasted_iota(jnp.int32, (GM_CHUNK, GM_CHUNK), 1)
    tril = col <= row
    lh = _lane_head()
    wts = [jnp.where(tril, ws_ref[g], 0.0).astype(BF16) for g in range(HEADS)]
    for c in range(ta // GM_CHUNK):
        sl = slice(c * GM_CHUNK, (c + 1) * GM_CHUNK)
        ge = _gelu(z_ref[sl, :])
        u = ge[:, :BW]
        vb = _rms(ge[:, BW:], g_ref[...]).astype(BF16)
        s = b_ref[...]
        for g in range(HEADS):
            s = s + jnp.where(lh == g, jnp.dot(wts[g], vb, preferred_element_type=F32), 0.0)
        o_ref[sl, :] = u * s


def _gmlp(za, m, g, ws, bfull, ta):
    return pl.pallas_call(
        functools.partial(_gmlp_kernel, ta=ta),
        grid=(m // ta,),
        in_specs=[pl.BlockSpec((ta, 512), lambda i: (i, 0)), _full((1, BW)), _full(ws.shape), _full(bfull.shape)],
        out_specs=pl.BlockSpec((ta, BW), lambda i: (i, 0)),
        out_shape=jax.ShapeDtypeStruct((m, BW), F32),
        compiler_params=_cp("parallel"),
        name="gmlp",
    )(za, g, ws, bfull)


def _mla_pre_kernel(z_ref, c_ref, s1_ref, s2_ref, gcq_ref, wuq_ref, qg_ref, bq_ref, gckv_ref, gkr_ref, wuk_ref, kg_ref,
                    bk_ref, oq_ref, ok_ref, ockv_ref, okr_ref, ockvt_ref, oq32_ref, *, tb):
    cs, s1, s2 = c_ref[...], s1_ref[...], s2_ref[...]

    def rope(x):
        return x * cs + pltpu.roll(x, 112, 1) * s1 + pltpu.roll(x, 16, 1) * s2

    z = z_ref[...]
    cq = _rms(z[:, 0:256], gcq_ref[...])
    q = _bdot(cq, wuq_ref[...])
    qn = q * lax.rsqrt(_bdot(q * q, bq_ref[...]) + EPS) * qg_ref[...]
    ckv = _rms(z[:, 256:384], gckv_ref[...])
    ockv_ref[...] = ckv
    for c in range(ckv.shape[0] // tb):
        ockvt_ref[c] = ckv[c * tb:(c + 1) * tb, :].T.astype(BF16)
    krb = z[:, 384:512]
    kr = rope(krb * lax.rsqrt(jnp.sum(krb * krb, axis=-1, keepdims=True) * (1.0 / MLA_ROPE) + EPS) * gkr_ref[...])
    okr_ref[...] = kr[:, 64:96]
    k = _bdot(ckv, wuk_ref[...])
    kn = k * lax.rsqrt(_bdot(k * k, bk_ref[...]) + EPS) * kg_ref[...]
    for h in range(HEADS):
        sl = slice(128 * h, 128 * h + 128)
        qh = rope(qn[:, sl])
        oq_ref[h] = qh.astype(BF16)
        oq32_ref[:, sl] = qh
        ok_ref[h] = (kn[:, sl] + kr).astype(BF16)


def _mla_pre(zb, m, tabs, t_blocks, p, tm, tb):
    cs, s1, s2 = tabs
    tab_spec = pl.BlockSpec((tm, 128), lambda i: (i % t_blocks, 0))
    consts = [p["mla_cq_g"], p["w_uq"], p["qg"], jnp.asarray(_BQ, BF16), p["mla_ckv_g"], p["kr_g"], p["w_uk"], p["kg"],
              jnp.asarray(_BK, BF16)]
    return pl.pallas_call(
        functools.partial(_mla_pre_kernel, tb=tb),
        grid=(m // tm,),
        in_specs=[pl.BlockSpec((tm, 512), lambda i: (i, 0)), tab_spec, tab_spec, tab_spec] + [_full(c.shape) for c in consts],
        out_specs=[pl.BlockSpec((HEADS, tm, 128), lambda i: (0, i, 0)), pl.BlockSpec((HEADS, tm, 128), lambda i: (0, i, 0)),
                   pl.BlockSpec((tm, 128), lambda i: (i, 0)), pl.BlockSpec((tm, MLA_ROPE), lambda i: (i, 0)),
                   pl.BlockSpec((tm // tb, 128, tb), lambda i: (i, 0, 0)), pl.BlockSpec((tm, 512), lambda i: (i, 0))],
        out_shape=[jax.ShapeDtypeStruct((HEADS, m, 128), BF16), jax.ShapeDtypeStruct((HEADS, m, 128), BF16),
                   jax.ShapeDtypeStruct((m, 128), F32), jax.ShapeDtypeStruct((m, MLA_ROPE), F32),
                   jax.ShapeDtypeStruct((m // tb, 128, tb), BF16), jax.ShapeDtypeStruct((m, 512), F32)],
        compiler_params=_cp("parallel"),
        name="mla_pre",
    )(zb, cs, s1, s2, *consts)


def _mla_attn_kernel(q_ref, k_ref, v_ref, wuv_ref, o_ref, *, tq):
    i = pl.program_id(1)
    row = lax.broadcasted_iota(jnp.int32, (tq, tq), 0)
    col = lax.broadcasted_iota(jnp.int32, (tq, tq), 1)
    causal = row <= col

    def step(j, carry, mask):
        off = pl.multiple_of(j * tq, tq)
        vt = v_ref[j]
        ss = [lax.dot_general(k_ref[h, pl.ds(off, tq), :], q_ref[h], NT, preferred_element_type=F32) for h in range(HEADS)]
        stats = []
        for h in range(HEADS):
            m, l, _ = carry[h]
            s = ss[h] * MLA_SCALE
            if mask:
                s = jnp.where(causal, s, NEG)
            mn = jnp.maximum(m, jnp.max(s, axis=0, keepdims=True))
            pr = jnp.exp(s - mn)
            al = jnp.exp(m - mn)
            stats.append((mn, al * l + jnp.sum(pr, axis=0, keepdims=True), al, pr.astype(BF16)))
        pvs = [jnp.dot(vt, stats[h][3], preferred_element_type=F32) for h in range(HEADS)]
        return tuple((stats[h][0], stats[h][1], stats[h][2] * carry[h][2] + pvs[h]) for h in range(HEADS))

    init = tuple((jnp.full((1, tq), NEG, F32), jnp.zeros((1, tq), F32), jnp.zeros((MLA_KV_RANK, tq), F32)) for _ in range(HEADS))
    carry = lax.fori_loop(0, i, lambda j, c: step(j, c, False), init)
    carry = step(i, carry, True)
    out = jnp.zeros((tq, BW), F32)
    for h in range(HEADS):
        m, l, acc = carry[h]
        out = out + lax.dot_general((acc / l).astype(BF16), wuv_ref[h], TN, preferred_element_type=F32)
    o_ref[...] = out


def _mla_attn(q4, k4, ckvt, wuv, n, t, tq):
    nq = t // tq
    return pl.pallas_call(
        functools.partial(_mla_attn_kernel, tq=tq),
        grid=(n, nq),
        in_specs=[pl.BlockSpec((HEADS, tq, 128), lambda b, i: (0, b * nq + i, 0)),
                  pl.BlockSpec((HEADS, t, 128), lambda b, i: (0, b, 0)),
                  pl.BlockSpec((nq, MLA_KV_RANK, tq), lambda b, i: (b, 0, 0)), _full(wuv.shape)],
        out_specs=pl.BlockSpec((tq, BW), lambda b, i: (b * nq + i, 0)),
        out_shape=jax.ShapeDtypeStruct((n * t, BW), F32),
        compiler_params=_cp("parallel", "arbitrary"),
        name="mla_attn",
    )(q4, k4, ckvt, wuv)


def _mla_decode_kernel(pt_ref, qk_ref, qr_ref, cnew_ref, krnew_ref, kng_ref, wukt_ref, wuv_ref, ind_ref, ckv_hbm, krt_hbm, o_ref,
                       cbuf, kbuf, sem, *, layer, npages, pp):
    n = pl.program_id(0)
    ngroups = npages // pp

    def page_copies(sample, grp, slot):
        base = sample * npages + grp * pp
        out = []
        for i in range(pp):
            page = pt_ref[base + i]
            out.append(pltpu.make_async_copy(ckv_hbm.at[layer, page], cbuf.at[slot, pl.ds(i * PAGE, PAGE)], sem.at[slot]))
            out.append(pltpu.make_async_copy(krt_hbm.at[layer, page], kbuf.at[slot, :, pl.ds(i * PAGE, PAGE)], sem.at[slot]))
        return out

    @pl.when(n == 0)
    def _():
        for cp in page_copies(0, 0, 0):
            cp.start()

    ind = ind_ref[...]
    qbd = (ind * (qk_ref[...] * kng_ref[...])).astype(BF16)
    qr = qr_ref[...].astype(BF16)
    wukt = wukt_ref[...]
    qabs = jnp.dot(qbd, wukt, preferred_element_type=F32).astype(BF16)
    w2 = jnp.concatenate([wukt, qabs, jnp.zeros((8, MLA_KV_RANK), BF16)], axis=0)

    def head_ms(kk):
        rows = [jnp.sum(kk[HD * h:HD * (h + 1)], axis=0, keepdims=True) for h in range(HEADS)]
        return jnp.concatenate(rows + [jnp.ones((8 - HEADS, kk.shape[1]), F32)], axis=0) * (1.0 / HD)

    m = jnp.full((8, 1), NEG, F32)
    l = jnp.zeros((8, 1), F32)
    acc = jnp.zeros((8, MLA_KV_RANK), F32)
    for grp in range(ngroups):
        slot = (n * ngroups + grp) % 2
        if grp + 1 < ngroups:
            for cp in page_copies(n, grp + 1, 1 - slot):
                cp.start()
        else:
            @pl.when(n + 1 < pl.num_programs(0))
            def _():
                for cp in page_copies(n + 1, 0, 1 - slot):
                    cp.start()
        for cp in page_copies(n, grp, slot):
            cp.wait()
        cb = cbuf[slot].astype(BF16)
        krt = kbuf[slot].astype(BF16)
        kq = lax.dot_general(w2, cb, NT, preferred_element_type=F32)
        kt = kq[0:BW]
        num = kq[BW:BW + 8]
        s = (num * lax.rsqrt(head_ms(kt * kt) + EPS) + jnp.dot(qr, krt, preferred_element_type=F32)) * MLA_SCALE
        mn = jnp.maximum(m, jnp.max(s, axis=-1, keepdims=True))
        pb = jnp.exp(s - mn)
        al = jnp.exp(m - mn)
        l = al * l + jnp.sum(pb, axis=-1, keepdims=True)
        acc = al * acc + jnp.dot(pb.astype(BF16), cb, preferred_element_type=F32)
        m = mn

    cb = jnp.broadcast_to(cnew_ref[...], (8, MLA_KV_RANK)).astype(BF16)
    k1 = lax.dot_general(cb, wukt, NT, preferred_element_type=F32)
    krn = krnew_ref[...].astype(BF16).astype(F32)
    num1 = jnp.sum(qabs.astype(F32) * cb.astype(F32), axis=-1, keepdims=True)
    ms1 = jnp.sum(ind * (k1 * k1), axis=-1, keepdims=True) * (1.0 / HD)
    s1 = (num1 * lax.rsqrt(ms1 + EPS) + jnp.sum(qr.astype(F32) * krn, axis=-1, keepdims=True)) * MLA_SCALE
    mn1 = jnp.maximum(m, s1)
    p1 = jnp.exp(s1 - mn1)
    al1 = jnp.exp(m - mn1)
    lat = (al1 * acc + p1 * cb.astype(F32)) / (al1 * l + p1)
    o8 = jnp.dot(lat.astype(BF16), wuv_ref[...], preferred_element_type=F32)
    o_ref[...] = jnp.sum(o8 * ind, axis=0, keepdims=True)


def _mla_decode(layer, page_table, cache_ckv, cache_krt, qk, qr8, cnew, krnew, kng, wuk, wuv, pp):
    ns, npages = page_table.shape
    pt = page_table.reshape(-1)

    def per_sample(shape):
        return pl.BlockSpec((None,) + shape, lambda n, pt_ref: (n, 0, 0))

    def const(a):
        nd = a.ndim
        return pl.BlockSpec(a.shape, lambda n, pt_ref: (0,) * nd)

    ind = jnp.asarray(_IND8)
    hbm = pl.BlockSpec(memory_space=pl.ANY)
    grid_spec = pltpu.PrefetchScalarGridSpec(
        num_scalar_prefetch=1,
        grid=(ns,),
        in_specs=[per_sample((1, BW)), per_sample((8, MLA_ROPE)), per_sample((1, MLA_KV_RANK)), per_sample((1, MLA_ROPE)),
                  const(kng), const(wuk), const(wuv), const(ind), hbm, hbm],
        out_specs=per_sample((1, BW)),
        scratch_shapes=[pltpu.VMEM((2, pp * PAGE, MLA_KV_RANK), F32), pltpu.VMEM((2, MLA_ROPE, pp * PAGE), F32),
                        pltpu.SemaphoreType.DMA((2,))],
    )
    return pl.pallas_call(
        functools.partial(_mla_decode_kernel, layer=layer, npages=npages, pp=pp),
        grid_spec=grid_spec,
        out_shape=jax.ShapeDtypeStruct((ns, 1, BW), F32),
        compiler_params=_cp("arbitrary"),
        name="mla_decode",
    )(pt, qk, qr8, cnew, krnew, kng, wuk, wuv, ind, cache_ckv, cache_krt)


def _gdn_kernel(zc_ref, zba_ref, bat_ref, cw_ref, parr_ref, parc_ref, gout_ref, bones_ref, expb_ref, expg_ref, bdm_ref,
                o_ref, sfin_ref, xbuf, s_sc, *, tg):
    t = pl.program_id(1)
    c = GDN_CHUNK

    @pl.when(t == 0)
    def _():
        xbuf[0:8, :] = jnp.zeros((8, QKV_DIM), F32)
        s_sc[...] = jnp.zeros(s_sc.shape, F32)

    @pl.when(t > 0)
    def _():
        xbuf[5:8, :] = xbuf[tg + 5:tg + 8, :]

    xbuf[8:8 + tg, :] = zc_ref[:, 0:QKV_DIM]
    y = cw_ref[0:1, :] * xbuf[5:5 + tg, :]
    for i in range(1, 4):
        y = y + cw_ref[i:i + 1, :] * xbuf[5 + i:5 + i + tg, :]
    y = _silu(y)

    zba = zba_ref[...]
    beta_col = jax.nn.sigmoid(zba)
    g_col = -jnp.exp(parr_ref[0:1, :]) * jax.nn.softplus(zba + parr_ref[1:2, :])
    g_row = -jnp.exp(parc_ref[:, 0:1]) * jax.nn.softplus(bat_ref[...] + parc_ref[:, 1:2])

    row = lax.broadcasted_iota(jnp.int32, (c, c), 0)
    col = lax.broadcasted_iota(jnp.int32, (c, c), 1)
    incl = col <= row
    strict = col < row
    lt = incl.astype(BF16)
    ut = (col >= row).astype(BF16)
    bones = bones_ref[...]
    bdm = bdm_ref[...]
    expb, expg = expb_ref[...], expg_ref[...]
    brow = lax.broadcasted_iota(jnp.int32, (BW, BW), 0)
    bcl = lax.broadcasted_iota(jnp.int32, (BW, BW), 1)
    same_head = (brow // HD) == (bcl // HD)
    incl_bd = same_head & ((bcl % HD) <= (brow % HD))
    strict_bd = same_head & ((bcl % HD) < (brow % HD))
    eye_bd = (brow == bcl).astype(F32)

    def same_block(b):
        return (brow // b) == (bcl // b)
    same_head2 = jnp.concatenate([same_head, same_head], axis=1)

    def stack4(a):
        return jnp.concatenate([a, a, a, a], axis=0)

    def fold4(a):
        return (a[0:c] + a[c:2 * c]) + (a[2 * c:3 * c] + a[3 * c:4 * c])

    nchunk = tg // c
    pre = []
    for ci in range(nchunk):
        sl = slice(ci * c, (ci + 1) * c)
        q, k, v = y[sl, 0:256], y[sl, 256:512], y[sl, 512:768]
        qn = q * lax.rsqrt(_bdot(q * q, bones) + EPS) * (HD ** -0.5)
        kn = k * lax.rsqrt(_bdot(k * k, bones) + EPS)
        bcol = beta_col[sl, :]
        gcum_c = _sel_dot(lt, g_col[sl, :], True)
        gcum_r = _sel_dot(ut, g_row[:, sl], False)
        gx = _sel_dot(expg, gcum_c, False)
        bx = _sel_dot(expb, bcol, False)
        egx = jnp.exp(gx)
        rhs = _split2(jnp.concatenate([bx * v, bx * egx * kn], axis=1))
        ks = jnp.where(same_head, stack4(kn), 0.0).astype(BF16)
        qs = jnp.where(same_head, stack4(qn), 0.0).astype(BF16)
        gc_s = jnp.concatenate([gcum_c[:, 4 + h:5 + h] for h in range(HEADS)], axis=0)
        gr_s = jnp.concatenate([gcum_r[4 + h:5 + h, :] for h in range(HEADS)], axis=1)
        beta_s = jnp.concatenate([bcol[:, h:h + 1] for h in range(HEADS)], axis=0)
        dm = jnp.exp(jnp.where(incl_bd, gc_s - gr_s, NEG))
        a = jnp.where(strict_bd, beta_s * lax.dot_general(ks, ks, NT, preferred_element_type=F32) * dm, 0.0)
        qk = (lax.dot_general(qs, ks, NT, preferred_element_type=F32) * dm).astype(BF16)
        pre.append(dict(sl=sl, qn=qn, kn=kn, gx=gx, egx=egx, rhs=rhs, qk=qk, a=_split2(a),
                        tinv=eye_bd - jnp.where(same_block(2), a, 0.0)))

    b = 2
    while b < c:
        off = same_block(2 * b) & jnp.logical_not(same_block(b))
        for d in pre:
            d["ts"] = _split2(d["tinv"])
            d["w"] = _split2(_dot3((jnp.where(off, d["a"][0], 0.0), jnp.where(off, d["a"][1], 0.0)), d["ts"]))
        for d in pre:
            d["tinv"] = d["tinv"] - _dot3(d["ts"], d["w"])
        b *= 2
    for d in pre:
        rhs = d["rhs"]
        d["x"] = fold4(jnp.where(same_head2, _dot3(_split2(d["tinv"]), (stack4(rhs[0]), stack4(rhs[1]))), 0.0))

    for d in pre:
        sl, qn, kn, gx, egx, qk, x = d["sl"], d["qn"], d["kn"], d["gx"], d["egx"], d["qk"], d["x"]
        s = s_sc[...]
        sb = s.astype(BF16)
        u = x[:, :BW] - jnp.dot(x[:, BW:].astype(BF16), sb, preferred_element_type=F32)
        ub = u.astype(BF16)
        o = egx * jnp.dot(qn.astype(BF16), sb, preferred_element_type=F32)
        o = o + fold4(jnp.where(same_head, jnp.dot(qk, stack4(ub), preferred_element_type=F32), 0.0))
        glast = gx[c - 1:c, :]
        kf = (kn * jnp.exp(glast - gx)).astype(BF16)
        s_new = jnp.exp(glast) * s + lax.dot_general(kf, ub, TN, preferred_element_type=F32)
        s_sc[...] = s_new * bdm
        on = o * lax.rsqrt(_bdot(o * o, bones) * (1.0 / HD) + EPS) * gout_ref[...]
        o_ref[sl, :] = on * _silu(zc_ref[sl, QKV_DIM:QKV_DIM + BW])

    @pl.when(t == pl.num_programs(1) - 1)
    def _():
        sfin_ref[...] = s_sc[...]


def _gdn_prompt(zc, zba, bat, n, t, p, tg):
    nt = t // tg
    consts = [p["conv_w"], p["gdn_par_r"], p["gdn_par_c"], p["gdn_out_gx"], jnp.asarray(_BONES64, BF16), jnp.asarray(_EXPB, BF16),
              jnp.asarray(_EXPG, BF16), jnp.asarray(_BDMASK)]
    return pl.pallas_call(
        functools.partial(_gdn_kernel, tg=tg),
        grid=(n, nt),
        in_specs=[pl.BlockSpec((tg, 1024), lambda b, i: (b * nt + i, 0)), pl.BlockSpec((tg, 128), lambda b, i: (b * nt + i, 0)),
                  pl.BlockSpec((8, tg), lambda b, i: (0, b * nt + i))] + [_full(c.shape) for c in consts],
        out_specs=[pl.BlockSpec((tg, BW), lambda b, i: (b * nt + i, 0)), pl.BlockSpec((None, BW, BW), lambda b, i: (b, 0, 0))],
        out_shape=[jax.ShapeDtypeStruct((n * t, BW), F32), jax.ShapeDtypeStruct((n, BW, BW), F32)],
        scratch_shapes=[pltpu.VMEM((8 + tg, QKV_DIM), F32), pltpu.VMEM((BW, BW), F32)],
        compiler_params=_cp("parallel", "arbitrary"),
        name="gdn_prompt",
    )(zc, zba, bat, *consts)


def _sample_tok_kernel(za_ref, zc_ref, zba_ref, zm_ref, sconv_ref, cw_ref, gmg_ref, gmw_ref, gmb_ref, parr_ref, bones_ref,
                       memg_ref, bmean_ref, oa_ref, ov_ref, oconv_ref, oq_ref, ok_ref, ovv_ref, ozg_ref, obeta_ref, og_ref, omq_ref):
    ge = _gelu(za_ref[...])
    v = _rms(ge[:, BW:], gmg_ref[...])
    ov_ref[...] = v
    oa_ref[...] = ge[:, :BW] * (gmw_ref[...] * v + gmb_ref[...])
    sc = sconv_ref[...]
    x = zc_ref[:, 0:QKV_DIM]
    y = (cw_ref[0:1, :] * sc[:, 0:768] + cw_ref[1:2, :] * sc[:, 768:1536] + cw_ref[2:3, :] * sc[:, 1536:2304]
         + cw_ref[3:4, :] * x)
    oconv_ref[:, 0:1536] = sc[:, 768:2304]
    oconv_ref[:, 1536:2304] = x
    y = _silu(y)
    q, k = y[:, 0:256], y[:, 256:512]
    bones = bones_ref[...]
    oq_ref[...] = (q * lax.rsqrt(_bdot(q * q, bones) + EPS) * (HD ** -0.5)).T
    ok_ref[...] = (k * lax.rsqrt(_bdot(k * k, bones) + EPS)).T
    ovv_ref[...] = y[:, 512:768].T
    ozg_ref[...] = zc_ref[:, QKV_DIM:QKV_DIM + BW].T
    zba = zba_ref[...]
    obeta_ref[...] = jax.nn.sigmoid(zba).T
    og_ref[...] = (-jnp.exp(parr_ref[0:1, :]) * jax.nn.softplus(zba + parr_ref[1:2, :])).T
    mq = zm_ref[...]
    omq_ref[...] = mq * lax.rsqrt(_bdot(mq * mq, bmean_ref[...]) + EPS) * memg_ref[...]


def _sample_tok(za, zc, zba, zm, sconv, p):
    ns = za.shape[0]
    args = [za, zc, zba, zm, sconv, p["conv_w"], p["gm_norm_g"], p["gm_w0"], p["gm_b0"], p["gdn_par_r"],
            jnp.asarray(_BONES64, BF16), p["mem_qn_gx"], jnp.asarray(_BMEAN64, BF16)]
    shapes = [(ns, BW), (ns, BW), (ns, 2304), (BW, ns), (BW, ns), (BW, ns), (BW, ns), (128, ns), (128, ns), (ns, BW)]
    return pl.pallas_call(
        _sample_tok_kernel,
        in_specs=[_full(a.shape) for a in args],
        out_specs=[_full(s) for s in shapes],
        out_shape=[jax.ShapeDtypeStruct(s, F32) for s in shapes],
        grid=(1,),
        compiler_params=_cp("arbitrary"),
        name="sample_tok",
    )(*args)


def _gdn_step_kernel(s_ref, q_ref, k_ref, v_ref, beta_ref, g_ref, zg_ref, gout_ref, so_ref, o_ref, o_sc):
    q, k = q_ref[...], k_ref[...]
    eg = jnp.exp(g_ref[...])
    beta = beta_ref[...]
    qk = jnp.sum(q * k, axis=0, keepdims=True)
    ssq = jnp.zeros(qk.shape, F32)
    for v in range(HD):
        sv = s_ref[v]
        sk = jnp.sum(sv * k, axis=0, keepdims=True)
        sq = jnp.sum(sv * q, axis=0, keepdims=True)
        u = beta * (v_ref[v:v + 1, :] - eg * sk)
        o = eg * sq + qk * u
        so_ref[v] = eg * sv + u * k
        o_sc[v:v + 1, :] = o
        ssq = ssq + o * o
    o_ref[...] = o_sc[...] * lax.rsqrt(ssq * (1.0 / HD) + EPS) * gout_ref[...] * _silu(zg_ref[...])


def _gdn_step(layer, state_t, q, k, v, beta, g, zg, gout_col):
    ns = state_t.shape[-1]
    vec = pl.BlockSpec((None, HD, ns), lambda h: (h, 0, 0))
    sca = pl.BlockSpec((None, 1, ns), lambda h: (h, 0, 0))
    return pl.pallas_call(
        _gdn_step_kernel,
        grid=(HEADS,),
        in_specs=[pl.BlockSpec((None, None, HD, HD, ns), lambda h: (layer, h, 0, 0, 0)), vec, vec, vec, sca, sca, vec, _full((HD, 1))],
        out_specs=[pl.BlockSpec((None, HD, HD, ns), lambda h: (h, 0, 0, 0)), vec],
        out_shape=[jax.ShapeDtypeStruct((HEADS, HD, HD, ns), F32), jax.ShapeDtypeStruct((HEADS, HD, ns), F32)],
        scratch_shapes=[pltpu.VMEM((HD, ns), F32)],
        compiler_params=_cp("parallel"),
        name="gdn_step",
    )(state_t, q, k, v, beta, g, zg, gout_col)


def _mem_kv_kernel(x_ref, g_ref, w_ref, kg_ref, bmean_ref, ok_ref, ov_ref):
    kv = _bdot(_rms(x_ref[...], g_ref[...]), w_ref[...])
    k = kv[:, 0:BW]
    ok_ref[...] = k * lax.rsqrt(_bdot(k * k, bmean_ref[...]) + EPS) * kg_ref[...]
    ov_ref[...] = kv[:, BW:]


def _mem_kv(mem, g, w, kgx, tm):
    m = mem.shape[0]
    bmean = jnp.asarray(_BMEAN64, BF16)
    return pl.pallas_call(
        _mem_kv_kernel,
        grid=(m // tm,),
        in_specs=[pl.BlockSpec((tm, D_MODEL), lambda i: (i, 0)), _full((1, D_MODEL)), _full(w.shape), _full((1, BW)), _full((BW, BW))],
        out_specs=[pl.BlockSpec((tm, BW), lambda i: (i, 0))] * 2,
        out_shape=[jax.ShapeDtypeStruct((m, BW), F32)] * 2,
        compiler_params=_cp("parallel"),
        name="mem_kv",
    )(mem, g, w, kgx, bmean)


def _mem_attn_kernel(q_ref, k_ref, v_ref, gq_ref, bmean_ref, o_ref):
    q = q_ref[...]
    qn = q * lax.rsqrt(_bdot(q * q, bmean_ref[...]) + EPS) * gq_ref[...]
    kb = k_ref[...].astype(BF16)
    vb = v_ref[...].astype(BF16)
    lh = _lane_head()
    out = jnp.zeros(q.shape, F32)
    for h in range(HEADS):
        mh = lh == h
        s = lax.dot_general(jnp.where(mh, qn, 0.0).astype(BF16), kb, NT, preferred_element_type=F32) * (HD ** -0.5)
        e = jnp.exp(s - jnp.max(s, axis=-1, keepdims=True))
        pr = e / jnp.sum(e, axis=-1, keepdims=True)
        out = out + jnp.where(mh, jnp.dot(pr.astype(BF16), vb, preferred_element_type=F32), 0.0)
    o_ref[...] = out


def _mem_attn(zm, mk, mv, gqx, n, t, mt, tq):
    nq = t // tq
    bmean = jnp.asarray(_BMEAN64, BF16)
    return pl.pallas_call(
        _mem_attn_kernel,
        grid=(n, nq),
        in_specs=[pl.BlockSpec((tq, BW), lambda b, i: (b * nq + i, 0)), pl.BlockSpec((mt, BW), lambda b, i: (b, 0)),
                  pl.BlockSpec((mt, BW), lambda b, i: (b, 0)), _full((1, BW)), _full((BW, BW))],
        out_specs=pl.BlockSpec((tq, BW), lambda b, i: (b * nq + i, 0)),
        out_shape=jax.ShapeDtypeStruct((n * t, BW), F32),
        compiler_params=_cp("parallel", "parallel"),
        name="mem_attn",
    )(zm, mk, mv, gqx, bmean)


def _mem_attn_s_kernel(q_ref, k_ref, v_ref, ind_ref, o_ref, *, bn):
    ind = ind_ref[...]
    for i in range(bn):
        qbd = (ind * q_ref[i:i + 1, :]).astype(BF16)
        s = jnp.dot(qbd, k_ref[i].astype(BF16), preferred_element_type=F32) * (HD ** -0.5)
        e = jnp.exp(s - jnp.max(s, axis=-1, keepdims=True))
        pr = e / jnp.sum(e, axis=-1, keepdims=True)
        o8 = lax.dot_general(pr.astype(BF16), v_ref[i].astype(BF16), NT, preferred_element_type=F32)
        o_ref[i:i + 1, :] = jnp.sum(o8 * ind, axis=0, keepdims=True)


def _mem_attn_s(layer, mqn, cache_k, cache_v, bn):
    ns = mqn.shape[0]
    mt = cache_k.shape[3]
    kv_spec = pl.BlockSpec((None, bn, BW, mt), lambda i: (layer, i, 0, 0))
    return pl.pallas_call(
        functools.partial(_mem_attn_s_kernel, bn=bn),
        grid=(ns // bn,),
        in_specs=[pl.BlockSpec((bn, BW), lambda i: (i, 0)), kv_spec, kv_spec, _full((8, BW))],
        out_specs=pl.BlockSpec((bn, BW), lambda i: (i, 0)),
        out_shape=jax.ShapeDtypeStruct((ns, BW), F32),
        compiler_params=_cp("parallel"),
        name="mem_attn_s",
    )(mqn, cache_k, cache_v, jnp.asarray(_IND8))


def _merge_kernel(x_ref, a_ref, b_ref, c_ref, m_ref, g1_ref, wg_ref, wb_ref, wo_ref, g2_ref, wrh_ref, wrl_ref, br_ref,
                  x1_ref, h2_ref, ei_ref, ew_ref):
    x = x_ref[...]
    hb = _rms(x, g1_ref[...]).astype(BF16)
    acc = jnp.zeros(x.shape, F32)
    for b, br in enumerate((a_ref, b_ref, c_ref, m_ref)):
        gate = jax.nn.sigmoid(jnp.dot(hb, wg_ref[:, b * D_MODEL:(b + 1) * D_MODEL], preferred_element_type=F32))
        acc = acc + gate * jnp.dot(br[...].astype(BF16), wb_ref[b], preferred_element_type=F32)
    x1 = x + jnp.dot(acc.astype(BF16), wo_ref[...], preferred_element_type=F32)
    x1_ref[...] = x1
    h2 = _rms(x1, g2_ref[...])
    h2_ref[...] = h2
    logits = _dot3(_split2(h2), (wrh_ref[...], wrl_ref[...])) + br_ref[...]
    lane = lax.broadcasted_iota(jnp.int32, (1, 128), 1).astype(F32)
    big = 1e9
    lg = jnp.where(lane < N_GROUPS, logits, NEG)
    mg = jnp.max(lg, axis=-1, keepdims=True)
    g_w = 1.0 / jnp.sum(jnp.exp(lg - mg), axis=-1, keepdims=True)
    gi = jnp.min(jnp.where(lg == mg, lane, big), axis=-1, keepdims=True)
    sel = (lane >= N_GROUPS) & (lane < N_GROUPS + N_EXPERTS) & (jnp.floor((lane - N_GROUPS) * (1.0 / EPG)) == gi)
    le = jnp.where(sel, logits, NEG)
    m1 = jnp.max(le, axis=-1, keepdims=True)
    i1 = jnp.min(jnp.where(le == m1, lane, big), axis=-1, keepdims=True)
    le2 = jnp.where(lane == i1, NEG, le)
    m2 = jnp.max(le2, axis=-1, keepdims=True)
    i2 = jnp.min(jnp.where(le2 == m2, lane, big), axis=-1, keepdims=True)
    z = jnp.sum(jnp.exp(le - m1), axis=-1, keepdims=True)
    p1 = 1.0 / z
    p2 = jnp.exp(m2 - m1) / z
    w1 = p1 / (p1 + p2) * g_w
    w2 = p2 / (p1 + p2) * g_w
    ei_ref[...] = jnp.where(lane == 0, i1 - N_GROUPS, jnp.where(lane == 1, i2 - N_GROUPS, 0.0)).astype(jnp.int32)
    ew_ref[...] = jnp.where(lane == 0, w1, jnp.where(lane == 1, w2, 0.0))


def _merge(x, branches, p, tm):
    m = x.shape[0]
    consts = [p["norm1_g"], p["w_gate"], p["w_branch"], p["w_out"], p["norm2_g"], *_split2(p["w_router"]), p["b_router"]]
    tile = lambda w: pl.BlockSpec((tm, w), lambda i: (i, 0))
    return pl.pallas_call(
        _merge_kernel,
        grid=(m // tm,),
        in_specs=[tile(D_MODEL)] + [tile(BW)] * 4 + [_full(c.shape) for c in consts],
        out_specs=[tile(D_MODEL), tile(D_MODEL), tile(128), tile(128)],
        out_shape=[jax.ShapeDtypeStruct((m, D_MODEL), F32), jax.ShapeDtypeStruct((m, D_MODEL), F32),
                   jax.ShapeDtypeStruct((m, 128), jnp.int32), jax.ShapeDtypeStruct((m, 128), F32)],
        compiler_params=_cp("parallel"),
        name="merge",
    )(x, *branches, *consts)


def _dispatch_kernel(poff_ref, plen_ref, nv_ref, pos_ref, h_ref, xs_out, zbuf, sem, zsem, *, tmd, te):
    @pl.when(pl.program_id(0) == 0)
    def _():
        zbuf[...] = jnp.zeros(zbuf.shape, F32)

        def pad_copies(e):
            off = poff_ref[e]
            head = (-off) & 7
            body = plen_ref[e] - head
            out = [(i < head, pltpu.make_async_copy(zbuf.at[pl.ds(0, 1)], xs_out.at[pl.ds(off + i, 1)], zsem)) for i in range(7)]
            b = te // 2
            while b >= 8:
                start = pl.multiple_of(off + head + (body & ~(2 * b - 1)), 8)
                out.append(((body & b) != 0, pltpu.make_async_copy(zbuf.at[pl.ds(0, b)], xs_out.at[pl.ds(start, b)], zsem)))
                b //= 2
            return out

        def tail_copies(t):
            return [pltpu.make_async_copy(zbuf, xs_out.at[pl.ds(pl.multiple_of(t * te + k * zbuf.shape[0], 8), zbuf.shape[0])], zsem)
                    for k in range(te // zbuf.shape[0])]

        def start_pad(e, carry):
            for cond, cp in pad_copies(e):
                pl.when(cond)(cp.start)
            return carry

        def wait_pad(e, carry):
            for cond, cp in pad_copies(e):
                pl.when(cond)(cp.wait)
            return carry

        def start_tail(t, carry):
            for cp in tail_copies(t):
                cp.start()
            return carry

        def wait_tail(t, carry):
            for cp in tail_copies(t):
                cp.wait()
            return carry

        n_tiles = xs_out.shape[0] // te
        lax.fori_loop(0, N_EXPERTS, start_pad, 0)
        lax.fori_loop(nv_ref[0], n_tiles, start_tail, 0)
        lax.fori_loop(0, N_EXPERTS, wait_pad, 0)
        lax.fori_loop(nv_ref[0], n_tiles, wait_tail, 0)

    def issue(i, carry):
        for s in range(2):
            pltpu.make_async_copy(h_ref.at[pl.ds(i, 1)], xs_out.at[pl.ds(pos_ref[2 * i + s], 1)], sem).start()
        return carry

    lax.fori_loop(0, tmd, issue, 0, unroll=8)
    for s in range(2):
        pltpu.make_async_copy(h_ref, xs_out.at[pl.ds(0, tmd)], sem).wait()


def _dispatch(pad_off, pad_len, n_valid, pos, h2, rows, tmd, te):
    m = h2.shape[0]
    grid_spec = pltpu.PrefetchScalarGridSpec(
        num_scalar_prefetch=3,
        grid=(m // tmd,),
        in_specs=[pl.BlockSpec((2 * tmd,), lambda i, po, pn, nv: (i,), memory_space=pltpu.SMEM),
                  pl.BlockSpec((tmd, D_MODEL), lambda i, po, pn, nv: (i, 0))],
        out_specs=pl.BlockSpec(memory_space=pl.ANY),
        scratch_shapes=[pltpu.VMEM((max(te // 2, 8), D_MODEL), F32), pltpu.SemaphoreType.DMA(()), pltpu.SemaphoreType.DMA(())],
    )
    return pl.pallas_call(
        functools.partial(_dispatch_kernel, tmd=tmd, te=te),
        grid_spec=grid_spec,
        out_shape=jax.ShapeDtypeStruct((rows, D_MODEL), F32),
        compiler_params=_cp("arbitrary"),
        name="moe_dispatch",
    )(pad_off, pad_len, n_valid, pos, h2)


def _expert_kernel(te_ref, nv_ref, x_ref, wg_ref, wu_ref, wd_ref, o_ref):
    del te_ref

    @pl.when(pl.program_id(0) < nv_ref[0])
    def _():
        xb = x_ref[...].astype(BF16)
        gt = jnp.dot(xb, wg_ref[...].astype(BF16), preferred_element_type=F32)
        up = jnp.dot(xb, wu_ref[...].astype(BF16), preferred_element_type=F32)
        o_ref[...] = jnp.dot((_silu(gt) * up).astype(BF16), wd_ref[...].astype(BF16), preferred_element_type=F32)

    @pl.when(pl.program_id(0) >= nv_ref[0])
    def _():
        o_ref[...] = jnp.zeros(o_ref.shape, F32)


def _experts(layer, tile_expert, n_valid, xs, w_gate, w_up, w_down, te):
    rows = xs.shape[0]

    def xmap(i, te_ref, nv_ref):
        return (jnp.minimum(i, nv_ref[0] - 1), 0)

    def wmap(i, te_ref, nv_ref):
        return (layer, te_ref[i], 0, 0)

    grid_spec = pltpu.PrefetchScalarGridSpec(
        num_scalar_prefetch=2,
        grid=(rows // te,),
        in_specs=[pl.BlockSpec((te, D_MODEL), xmap), pl.BlockSpec((None, None, D_MODEL, D_EXPERT), wmap),
                  pl.BlockSpec((None, None, D_MODEL, D_EXPERT), wmap), pl.BlockSpec((None, None, D_EXPERT, D_MODEL), wmap)],
        out_specs=pl.BlockSpec((te, D_MODEL), lambda i, te_ref, nv_ref: (i, 0)),
    )
    return pl.pallas_call(
        _expert_kernel,
        grid_spec=grid_spec,
        out_shape=jax.ShapeDtypeStruct((rows, D_MODEL), F32),
        compiler_params=_cp("arbitrary"),
        name="moe_experts",
    )(tile_expert, n_valid, xs, w_gate, w_up, w_down)


def _combine_kernel(pos_ref, x1_ref, ew_ref, ys_hbm, o_ref, r0, r1, sem, *, tmc):
    bufs = (r0, r1)

    def issue(i, carry):
        for s in range(2):
            pltpu.make_async_copy(ys_hbm.at[pl.ds(pos_ref[2 * i + s], 1)], bufs[s].at[pl.ds(i, 1)], sem).start()
        return carry

    lax.fori_loop(0, tmc, issue, 0, unroll=8)
    for s in range(2):
        pltpu.make_async_copy(ys_hbm.at[pl.ds(0, tmc)], bufs[s], sem).wait()
    ew = ew_ref[...]
    o_ref[...] = x1_ref[...] + ew[:, 0:1] * r0[...] + ew[:, 1:2] * r1[...]


def _combine(pos, x1, ew, ys, tmc):
    m = x1.shape[0]
    return pl.pallas_call(
        functools.partial(_combine_kernel, tmc=tmc),
        grid=(m // tmc,),
        in_specs=[pl.BlockSpec((2 * tmc,), lambda i: (i,), memory_space=pltpu.SMEM),
                  pl.BlockSpec((tmc, D_MODEL), lambda i: (i, 0)), pl.BlockSpec((tmc, 128), lambda i: (i, 0)),
                  pl.BlockSpec(memory_space=pl.ANY)],
        out_specs=pl.BlockSpec((tmc, D_MODEL), lambda i: (i, 0)),
        out_shape=jax.ShapeDtypeStruct((m, D_MODEL), F32),
        scratch_shapes=[pltpu.VMEM((tmc, D_MODEL), F32), pltpu.VMEM((tmc, D_MODEL), F32), pltpu.SemaphoreType.DMA(())],
        compiler_params=_cp("arbitrary"),
        name="moe_combine",
    )(pos, x1, ew, ys)


def _moe(layer, x1, h2, ei, ew, w_gate, w_up, w_down, te, tmd):
    m = x1.shape[0]
    flat_e = ei[:, 0:2].reshape(-1)
    onehot = (flat_e[:, None] == jnp.arange(N_EXPERTS, dtype=jnp.int32)[None, :]).astype(jnp.int32)
    csum = jnp.cumsum(onehot, axis=0)
    rank = jnp.sum(csum * onehot, axis=1) - 1
    counts = csum[-1]
    padded = ((counts + te - 1) // te) * te
    pend = jnp.cumsum(padded)
    pstart = pend - padded
    pos = (jnp.sum(onehot * pstart[None, :], axis=1) + rank).astype(jnp.int32)
    rows = ((2 * m + N_EXPERTS * (te - 1)) // te) * te
    n_tiles = rows // te
    n_valid = (pend[-1] // te).astype(jnp.int32).reshape(1)
    tile_start = jnp.arange(n_tiles, dtype=jnp.int32) * te
    tile_expert = jnp.minimum(jnp.sum((tile_start[:, None] >= pend[None, :]).astype(jnp.int32), axis=1), N_EXPERTS - 1)
    last_e = jnp.take(tile_expert, jnp.maximum(n_valid[0] - 1, 0))
    tile_expert = jnp.where(jnp.arange(n_tiles) < n_valid[0], tile_expert, last_e).astype(jnp.int32)
    xs = _dispatch((pstart + counts).astype(jnp.int32), (padded - counts).astype(jnp.int32), n_valid, pos, h2, rows, tmd, te)
    ys = _experts(layer, tile_expert, n_valid, xs, w_gate, w_up, w_down, te)
    return _combine(pos, x1, ew, ys, tmd)


def _tile4(v):
    return jnp.tile(v, HEADS).reshape(1, BW)


def _prep_layer(l, w):
    w_in = w["w_in"][l]
    z = lambda n: jnp.zeros((D_MODEL, n), F32)
    b_al = w_in[:, 1952:1960]
    w_small = jnp.concatenate(
        [w_in[:, 0:512], w_in[:, 512:768], w_in[:, 768:896], z(64), w_in[:, 896:928], z(32), w_in[:, 928:1696],
         w_in[:, 1696:1952], w_in[:, 1960:2216], b_al, z(120)], axis=1).astype(BF16)
    uq = w["mla_w_uq"][l]
    w_uq = jnp.pad(uq, ((0, 0), (0, 0), (0, 32))).reshape(256, 512).astype(BF16)
    qg = jnp.tile(jnp.concatenate([w["mla_qn_g"][l], w["mla_qr_g"][l], jnp.zeros((32,), F32)]), HEADS).reshape(1, 512)
    uk = w["mla_w_uk"][l]
    w_uk_p = jnp.pad(uk, ((0, 0), (0, 0), (0, 64))).reshape(128, 512).astype(BF16)
    kg = jnp.tile(jnp.concatenate([w["mla_kn_g"][l], jnp.zeros((64,), F32)]), HEADS).reshape(1, 512)
    kr_g = jnp.concatenate([jnp.zeros((64,), F32), w["mla_kr_g"][l], jnp.zeros((32,), F32)]).reshape(1, 128)
    uv = w["mla_w_uv"][l]
    w_uv_p = jnp.stack([jnp.pad(uv[:, h, :], ((0, 0), (64 * h, BW - 64 * h - 64))) for h in range(HEADS)]).astype(BF16)
    par_r = jnp.zeros((8, 128), F32).at[0, 4:8].set(w["gdn_a_log"][l]).at[1, 4:8].set(w["gdn_dt_bias"][l])
    par_c = jnp.zeros((8, 128), F32).at[4:8, 0].set(w["gdn_a_log"][l]).at[4:8, 1].set(w["gdn_dt_bias"][l])
    w_router = jnp.concatenate([w["moe_wg"][l], w["moe_we"][l], jnp.zeros((D_MODEL, 128 - 36), F32)], axis=1)
    b_router = jnp.concatenate([w["moe_bg"][l], w["moe_be"][l], jnp.zeros((128 - 36,), F32)]).reshape(1, 128)
    return {
        "norm1_g": w["norm1_g"][l].reshape(1, D_MODEL), "w_small": w_small, "w_bat": b_al.T.astype(BF16),
        "w_gate": w_in[:, 2216:].astype(BF16),
        "gm_norm_g": w["gm_norm_g"][l].reshape(1, BW), "gm_ws": w["gm_ws"][l],
        "gm_bfull": jnp.repeat(w["gm_b"][l].T, HD, axis=1),
        "gm_w0": jnp.repeat(w["gm_ws"][l][:, 0, 0], HD).reshape(1, BW), "gm_b0": jnp.repeat(w["gm_b"][l][:, 0], HD).reshape(1, BW),
        "mla_cq_g": w["mla_cq_g"][l].reshape(1, 256), "w_uq": w_uq, "qg": qg, "mla_ckv_g": w["mla_ckv_g"][l].reshape(1, 128),
        "kr_g": kr_g, "w_uk": w_uk_p, "kg": kg, "w_uv": w_uv_p,
        "w_uk_c": uk.reshape(128, BW).T.astype(BF16), "w_uv_c": uv.reshape(128, BW).astype(BF16), "kn_gx": _tile4(w["mla_kn_g"][l]),
        "conv_w": w["gdn_conv_w"][l], "gdn_par_r": par_r, "gdn_par_c": par_c, "gdn_out_gx": _tile4(w["gdn_out_g"][l]),
        "gdn_out_gc": w["gdn_out_g"][l].reshape(HD, 1),
        "mem_norm_g": w["mem_norm_g"][l].reshape(1, D_MODEL), "mem_w_kv": w["mem_w_kv"][l].astype(BF16),
        "mem_qn_gx": _tile4(w["mem_qn_g"][l]), "mem_kn_gx": _tile4(w["mem_kn_g"][l]),
        "w_branch": w["w_branch"][l].astype(BF16), "w_out": w["w_out"][l].astype(BF16),
        "norm2_g": w["norm2_g"][l].reshape(1, D_MODEL), "w_router": w_router, "b_router": b_router,
    }


def _rope_tables(pos):
    half = MLA_ROPE // 2
    inv = ROPE_THETA ** (-jnp.arange(half, dtype=F32) / half)
    ang = pos.astype(F32)[:, None] * inv[None, :]
    cos, sin = jnp.cos(ang), jnp.sin(ang)
    t = pos.shape[0]
    one, zero = jnp.ones((t, 64), F32), jnp.zeros((t, 64), F32)
    z16, z32 = jnp.zeros((t, 16), F32), jnp.zeros((t, 32), F32)
    return (jnp.concatenate([one, cos, cos, jnp.ones((t, 32), F32)], axis=1),
            jnp.concatenate([zero, -sin, z16, z32], axis=1),
            jnp.concatenate([zero, z16, sin, z32], axis=1))


def kernel(x_prompt, mem_prompt, x_sample, cache_mla_ckv, cache_mla_kr, cache_mem_k, cache_mem_v, state_gdn, state_conv,
           page_table, norm1_g, w_in, gm_norm_g, gm_ws, gm_b, mla_cq_g, mla_w_uq, mla_qn_g, mla_qr_g, mla_ckv_g, mla_kr_g,
           mla_w_uk, mla_kn_g, mla_w_uv, gdn_conv_w, gdn_a_log, gdn_dt_bias, gdn_out_g, mem_norm_g, mem_w_kv, mem_qn_g,
           mem_kn_g, w_branch, w_out, norm2_g, moe_wg, moe_bg, moe_we, moe_be, moe_w_gate, moe_w_up, moe_w_down):
    w = dict(norm1_g=norm1_g, w_in=w_in, gm_norm_g=gm_norm_g, gm_ws=gm_ws, gm_b=gm_b, mla_cq_g=mla_cq_g, mla_w_uq=mla_w_uq,
             mla_qn_g=mla_qn_g, mla_qr_g=mla_qr_g, mla_ckv_g=mla_ckv_g, mla_kr_g=mla_kr_g, mla_w_uk=mla_w_uk, mla_kn_g=mla_kn_g,
             mla_w_uv=mla_w_uv, gdn_conv_w=gdn_conv_w, gdn_a_log=gdn_a_log, gdn_dt_bias=gdn_dt_bias, gdn_out_g=gdn_out_g,
             mem_norm_g=mem_norm_g, mem_w_kv=mem_w_kv, mem_qn_g=mem_qn_g, mem_kn_g=mem_kn_g, w_branch=w_branch, w_out=w_out,
             norm2_g=norm2_g, moe_wg=moe_wg, moe_bg=moe_bg, moe_we=moe_we, moe_be=moe_be)
    depth = w_in.shape[0]
    bp, tp, _ = x_prompt.shape
    bs = x_sample.shape[0]
    mt = mem_prompt.shape[1]
    n_pages = page_table.shape[1]
    past_len = n_pages * cache_mla_ckv.shape[2]
    mp = bp * tp

    tm_p = min(512, mp)
    tq = min(256, tp)
    ta = min(512, tp)
    tg = min(256, tp)
    pp = min(64, n_pages)
    cache_krt = jnp.swapaxes(cache_mla_kr, 2, 3)
    tabs_p = _rope_tables(jnp.arange(tp, dtype=jnp.int32))
    tabs_s = _rope_tables(jnp.full((bs,), past_len, jnp.int32))

    xp = x_prompt.reshape(mp, D_MODEL)
    xs = x_sample.reshape(bs, D_MODEL)
    mem = mem_prompt.reshape(bp * mt, D_MODEL)
    cache_k = cache_mem_k.transpose(0, 1, 3, 4, 2).reshape(depth, bs, BW, mt)
    cache_v = cache_mem_v.transpose(0, 1, 3, 4, 2).reshape(depth, bs, BW, mt)
    state_t = state_gdn.transpose(0, 2, 3, 4, 1)
    rows_p, rows_s = [], []
    for l in range(depth):
        p = _prep_layer(l, w)
        mk, mv = _mem_kv(mem, p["mem_norm_g"], p["mem_w_kv"], p["mem_kn_gx"], min(512, bp * mt))
        za, zb, zc, zm, zba, bat = _in_proj(xp, p["norm1_g"], p["w_small"], p["w_bat"], tm_p)
        a_out = _gmlp(za, mp, p["gm_norm_g"], p["gm_ws"], p["gm_bfull"], tm_p)
        q4, k4, ckv, kr, ckvt, _ = _mla_pre(zb, mp, tabs_p, tp // min(tm_p, tp), p, min(tm_p, tp), ta)
        b_out = _mla_attn(q4, k4, ckvt, p["w_uv"], bp, tp, ta)
        c_out, sfin = _gdn_prompt(zc, zba, bat, bp, tp, p, tg)
        m_out = _mem_attn(zm, mk, mv, p["mem_qn_gx"], bp, tp, mt, tq)
        x1, h2, ei, ew = _merge(xp, (a_out, b_out, c_out, m_out), p, tm_p)
        xp = _moe(l, x1, h2, ei, ew, moe_w_gate, moe_w_up, moe_w_down, min(256, mp), min(512, mp))
        s_p = jnp.stack([sfin[:, 64 * h:64 * h + 64, 64 * h:64 * h + 64] for h in range(HEADS)], axis=1).transpose(0, 1, 3, 2)
        conv_p = zc.reshape(bp, tp, 1024)[:, tp - 3:, 0:QKV_DIM]
        rows_p.append((ckv.reshape(bp, tp, 128), kr.reshape(bp, tp, MLA_ROPE), s_p, conv_p,
                       mk.reshape(bp, mt, HEADS, HD), mv.reshape(bp, mt, HEADS, HD)))
        za, zb, zc, zm, zba, _ = _in_proj(xs, p["norm1_g"], p["w_small"], p["w_bat"], bs)
        a_s, v_s, conv_s, gq, gk, gv, zg, beta, gdec, mqn = _sample_tok(za, zc, zba, zm, state_conv[l].reshape(bs, 3 * QKV_DIM), p)
        _, _, ckv_s, kr_s, _, q32 = _mla_pre(zb, bs, tabs_s, 1, p, bs, bs)
        q3 = q32.reshape(bs, HEADS, 128)
        qk = q3[:, :, 0:64].reshape(bs, 1, BW)
        qr8 = jnp.pad(q3[:, :, 64:96], ((0, 0), (0, 4), (0, 0)))
        b_s = _mla_decode(l, page_table, cache_mla_ckv, cache_krt, qk, qr8, ckv_s.reshape(bs, 1, 128),
                          kr_s.reshape(bs, 1, MLA_ROPE), p["kn_gx"], p["w_uk_c"], p["w_uv_c"], pp).reshape(bs, BW)
        hv = lambda a: a.reshape(HEADS, HD, bs)
        s_new, c_t = _gdn_step(l, state_t, hv(gq), hv(gk), hv(gv), beta[0:4].reshape(HEADS, 1, bs),
                               gdec[4:8].reshape(HEADS, 1, bs), hv(zg), p["gdn_out_gc"])
        m_s = _mem_attn_s(l, mqn, cache_k, cache_v, min(8, bs))
        x1, h2, ei, ew = _merge(xs, (a_s, b_s, c_t.reshape(BW, bs).T, m_s), p, bs)
        xs = _moe(l, x1, h2, ei, ew, moe_w_gate, moe_w_up, moe_w_down, min(32, bs), bs)
        rows_s.append((ckv_s.reshape(bs, 1, 128), kr_s.reshape(bs, 1, MLA_ROPE), s_new.transpose(3, 0, 1, 2),
                       conv_s.reshape(bs, 3, QKV_DIM), v_s.reshape(bs, 1, BW)))
    p_out = [jnp.stack(a) for a in zip(*rows_p)]
    s_out = [jnp.stack(a) for a in zip(*rows_s)]
    return (xp.reshape(bp, tp, D_MODEL), xs.reshape(bs, 1, D_MODEL), *p_out, *s_out)
```

```python
import functools

import numpy as np
import jax
import jax.numpy as jnp
from jax import lax
from jax.experimental import pallas as pl
from jax.experimental.pallas import tpu as pltpu

F32 = jnp.float32
BF16 = jnp.bfloat16
HI = lax.Precision.HIGHEST
EPS = 1e-6
NEG = float("-inf")

D_MODEL = 1024
HEADS = 4
HD = 64
BW = 256
MLA_ROPE = 32
MLA_KV_RANK = 128
MLA_SCALE = 96.0 ** -0.5
ROPE_THETA = 10000.0
GM_CHUNK = 128
GDN_CHUNK = 64
QKV_DIM = 768
N_GROUPS = 4
EPG = 8
N_EXPERTS = 32
D_EXPERT = 256
PAGE = 128
VMEM_LIMIT = 56 * 1024 * 1024

NT = (((1,), (1,)), ((), ()))
TN = (((0,), (0,)), ((), ()))


def _cp(*sem):
    return pltpu.CompilerParams(dimension_semantics=sem, vmem_limit_bytes=VMEM_LIMIT)


def _rms(x, g):
    ms = jnp.sum(x * x, axis=-1, keepdims=True) * (1.0 / x.shape[-1])
    return x * lax.rsqrt(ms + EPS) * g


def _bdot(a, b):
    return jnp.dot(a.astype(BF16), b.astype(BF16), preferred_element_type=F32)


def _bdot_nt(a, b):
    return lax.dot_general(a.astype(BF16), b.astype(BF16), NT, preferred_element_type=F32)


def _hdot(a, b):
    return jnp.dot(a, b, precision=HI, preferred_element_type=F32)


def _split2(a):
    hi = a.astype(BF16)
    return hi, (a - hi.astype(F32)).astype(BF16)


def _dot3(a, b):
    d = lambda x, y: jnp.dot(x, y, preferred_element_type=F32)
    return d(a[0], b[0]) + (d(a[0], b[1]) + d(a[1], b[0]))


def _sel_dot(w01, x, left):
    x0 = x.astype(BF16)
    r1 = x - x0.astype(F32)
    x1 = r1.astype(BF16)
    x2 = (r1 - x1.astype(F32)).astype(BF16)
    d = (lambda p: jnp.dot(w01, p, preferred_element_type=F32)) if left else (lambda p: jnp.dot(p, w01, preferred_element_type=F32))
    return d(x0) + (d(x1) + d(x2))


def _silu(x):
    return x * jax.nn.sigmoid(x)


def _gelu(x):
    return 0.5 * x * (1.0 + lax.erf(x * 0.7071067811865476))


def _lane_head(width=BW):
    return lax.broadcasted_iota(jnp.int32, (1, width), 1) // HD


def _full(shape):
    n = len(shape)
    return pl.BlockSpec(shape, lambda *_: (0,) * n)


def _block_mean(width, segs):
    m = np.zeros((width, width), np.float32)
    for a, b in segs:
        m[a:b, a:b] = 1.0 / (b - a)
    return m


_BMEAN64 = _block_mean(BW, [(64 * h, 64 * h + 64) for h in range(HEADS)])
_BONES64 = _BMEAN64 * 64.0
_BQ = _block_mean(512, [(128 * h, 128 * h + 64) for h in range(HEADS)] + [(128 * h + 64, 128 * h + 96) for h in range(HEADS)])
_BK = _block_mean(512, [(128 * h, 128 * h + 64) for h in range(HEADS)])
_IND8 = np.zeros((8, BW), np.float32)
for _h in range(HEADS):
    _IND8[_h, 64 * _h:64 * _h + 64] = 1.0
_EXPB = np.zeros((128, BW), np.float32)
_EXPG = np.zeros((128, BW), np.float32)
for _h in range(HEADS):
    _EXPB[_h, 64 * _h:64 * _h + 64] = 1.0
    _EXPG[4 + _h, 64 * _h:64 * _h + 64] = 1.0
_BDMASK = (_BONES64 > 0).astype(np.float32)


def _in_proj_kernel(x_ref, g_ref, w_ref, wbat_ref, oa_ref, ob_ref, oc_ref, om_ref, oba_ref, obat_ref):
    hb = _rms(x_ref[...], g_ref[...]).astype(BF16)
    oa_ref[...] = jnp.dot(hb, w_ref[:, 0:512], preferred_element_type=F32)
    ob_ref[...] = jnp.dot(hb, w_ref[:, 512:1024], preferred_element_type=F32)
    oc_ref[...] = jnp.dot(hb, w_ref[:, 1024:2048], preferred_element_type=F32)
    om_ref[...] = jnp.dot(hb, w_ref[:, 2048:2304], preferred_element_type=F32)
    oba_ref[...] = jnp.dot(hb, w_ref[:, 2304:2432], preferred_element_type=F32)
    obat_ref[...] = lax.dot_general(wbat_ref[...], hb, NT, preferred_element_type=F32)


def _in_proj(x, g, w, wbat, tm):
    m = x.shape[0]
    widths = (512, 512, 1024, 256, 128)
    return pl.pallas_call(
        _in_proj_kernel,
        grid=(m // tm,),
        in_specs=[pl.BlockSpec((tm, D_MODEL), lambda i: (i, 0)), _full((1, D_MODEL)), _full(w.shape), _full(wbat.shape)],
        out_specs=[pl.BlockSpec((tm, n), lambda i: (i, 0)) for n in widths] + [pl.BlockSpec((8, tm), lambda i: (0, i))],
        out_shape=[jax.ShapeDtypeStruct((m, n), F32) for n in widths] + [jax.ShapeDtypeStruct((8, m), F32)],
        compiler_params=_cp("parallel"),
        name="in_proj",
    )(x, g, w, wbat)


def _gmlp_body(za, g_ref, ws_ref, b_ref, o_ref):
    row = lax.broadcasted_iota(jnp.int32, (GM_CHUNK, GM_CHUNK), 0)
    col = lax.broadcasted_iota(jnp.int32, (GM_CHUNK, GM_CHUNK), 1)
    tril = col <= row
    lh = _lane_head()
    wts = [jnp.where(tril, ws_ref[g], 0.0).astype(BF16) for g in range(HEADS)]
    for c in range(za.shape[0] // GM_CHUNK):
        sl = slice(c * GM_CHUNK, (c + 1) * GM_CHUNK)
        ge = _gelu(za[sl, :])
        u = ge[:, :BW]
        vb = _rms(ge[:, BW:], g_ref[...]).astype(BF16)
        s = b_ref[...]
        for g in range(HEADS):
            s = s + jnp.where(lh == g, jnp.dot(wts[g], vb, preferred_element_type=F32), 0.0)
        o_ref[sl, :] = u * s


def _mla_pre_body(z, c_ref, s1_ref, s2_ref, gcq_ref, wuq_ref, qg_ref, bq_ref, gckv_ref, gkr_ref, wuk_ref, kg_ref, bk_ref,
                  oq_ref, ok_ref, ockv_ref, okr_ref, ockvt_ref, oq32_ref, tb):
    cs, s1, s2 = c_ref[...], s1_ref[...], s2_ref[...]

    def rope(x):
        return x * cs + pltpu.roll(x, 112, 1) * s1 + pltpu.roll(x, 16, 1) * s2

    cq = _rms(z[:, 0:256], gcq_ref[...])
    q = _bdot(cq, wuq_ref[...])
    qn = q * lax.rsqrt(_bdot(q * q, bq_ref[...]) + EPS) * qg_ref[...]
    ckv = _rms(z[:, 256:384], gckv_ref[...])
    ockv_ref[...] = ckv
    for c in range(ckv.shape[0] // tb):
        ockvt_ref[c] = ckv[c * tb:(c + 1) * tb, :].T.astype(BF16)
    krb = z[:, 384:512]
    kr = rope(krb * lax.rsqrt(jnp.sum(krb * krb, axis=-1, keepdims=True) * (1.0 / MLA_ROPE) + EPS) * gkr_ref[...])
    okr_ref[...] = kr[:, 64:96]
    k = _bdot(ckv, wuk_ref[...])
    kn = k * lax.rsqrt(_bdot(k * k, bk_ref[...]) + EPS) * kg_ref[...]
    for h in range(HEADS):
        sl = slice(128 * h, 128 * h + 128)
        qh = rope(qn[:, sl])
        oq_ref[h] = qh.astype(BF16)
        if oq32_ref is not None:
            oq32_ref[:, sl] = qh
        ok_ref[h] = (kn[:, sl] + kr).astype(BF16)


def _mla_pre_kernel(z_ref, *refs, tb):
    _mla_pre_body(z_ref[...], *refs, tb)


def _mla_consts(p):
    return [p["mla_cq_g"], p["w_uq"], p["qg"], jnp.asarray(_BQ, BF16), p["mla_ckv_g"], p["kr_g"], p["w_uk"], p["kg"],
            jnp.asarray(_BK, BF16)]


def _front_kernel(x_ref, g_ref, w_ref, wbat_ref, gmg_ref, gmws_ref, gmb_ref, *refs, tb):
    mla_in, (oa_ref, oq_ref, ok_ref, ockv_ref, okr_ref, ockvt_ref, oc_ref, om_ref, oba_ref, obat_ref) = refs[:12], refs[12:]
    hb = _rms(x_ref[...], g_ref[...]).astype(BF16)
    _gmlp_body(jnp.dot(hb, w_ref[:, 0:512], preferred_element_type=F32), gmg_ref, gmws_ref, gmb_ref, oa_ref)
    _mla_pre_body(jnp.dot(hb, w_ref[:, 512:1024], preferred_element_type=F32), *mla_in,
                  oq_ref, ok_ref, ockv_ref, okr_ref, ockvt_ref, None, tb)
    oc_ref[...] = jnp.dot(hb, w_ref[:, 1024:2048], preferred_element_type=F32)
    om_ref[...] = jnp.dot(hb, w_ref[:, 2048:2304], preferred_element_type=F32)
    oba_ref[...] = jnp.dot(hb, w_ref[:, 2304:2432], preferred_element_type=F32)
    obat_ref[...] = lax.dot_general(wbat_ref[...], hb, NT, preferred_element_type=F32)


def _front(x, tabs, t_blocks, p, tm, tb):
    m = x.shape[0]
    tab_spec = pl.BlockSpec((tm, 128), lambda i: (i % t_blocks, 0))
    consts = [p["norm1_g"], p["w_small"], p["w_bat"], p["gm_norm_g"], p["gm_ws"], p["gm_bfull"]]
    mla = _mla_consts(p)
    tile = lambda w, dt=F32: (pl.BlockSpec((tm, w), lambda i: (i, 0)), jax.ShapeDtypeStruct((m, w), dt))
    head = (pl.BlockSpec((HEADS, tm, 128), lambda i: (0, i, 0)), jax.ShapeDtypeStruct((HEADS, m, 128), BF16))
    outs = [tile(BW), head, head, tile(128), tile(MLA_ROPE),
            (pl.BlockSpec((tm // tb, 128, tb), lambda i: (i, 0, 0)), jax.ShapeDtypeStruct((m // tb, 128, tb), BF16)),
            tile(1024), tile(BW), tile(128), (pl.BlockSpec((8, tm), lambda i: (0, i)), jax.ShapeDtypeStruct((8, m), F32))]
    return pl.pallas_call(
        functools.partial(_front_kernel, tb=tb),
        grid=(m // tm,),
        in_specs=[pl.BlockSpec((tm, D_MODEL), lambda i: (i, 0))] + [_full(c.shape) for c in consts] + [tab_spec] * 3
        + [_full(c.shape) for c in mla],
        out_specs=[o[0] for o in outs],
        out_shape=[o[1] for o in outs],
        compiler_params=_cp("parallel"),
        name="front",
    )(x, *consts, *tabs, *mla)


def _mla_pre(zb, m, tabs, t_blocks, p, tm, tb):
    cs, s1, s2 = tabs
    tab_spec = pl.BlockSpec((tm, 128), lambda i: (i % t_blocks, 0))
    consts = _mla_consts(p)
    return pl.pallas_call(
        functools.partial(_mla_pre_kernel, tb=tb),
        grid=(m // tm,),
        in_specs=[pl.BlockSpec((tm, 512), lambda i: (i, 0)), tab_spec, tab_spec, tab_spec] + [_full(c.shape) for c in consts],
        out_specs=[pl.BlockSpec((HEADS, tm, 128), lambda i: (0, i, 0)), pl.BlockSpec((HEADS, tm, 128), lambda i: (0, i, 0)),
                   pl.BlockSpec((tm, 128), lambda i: (i, 0)), pl.BlockSpec((tm, MLA_ROPE), lambda i: (i, 0)),
                   pl.BlockSpec((tm // tb, 128, tb), lambda i: (i, 0, 0)), pl.BlockSpec((tm, 512), lambda i: (i, 0))],
        out_shape=[jax.ShapeDtypeStruct((HEADS, m, 128), BF16), jax.ShapeDtypeStruct((HEADS, m, 128), BF16),
                   jax.ShapeDtypeStruct((m, 128), F32), jax.ShapeDtypeStruct((m, MLA_ROPE), F32),
                   jax.ShapeDtypeStruct((m // tb, 128, tb), BF16), jax.ShapeDtypeStruct((m, 512), F32)],
        compiler_params=_cp("parallel"),
        name="mla_pre",
    )(zb, cs, s1, s2, *consts)


def _mla_attn_kernel(q_ref, k_ref, v_ref, wuv_ref, o_ref, *, tq):
    i = pl.program_id(1)
    row = lax.broadcasted_iota(jnp.int32, (tq, tq), 0)
    col = lax.broadcasted_iota(jnp.int32, (tq, tq), 1)
    causal = row <= col

    def step(j, carry, mask):
        off = pl.multiple_of(j * tq, tq)
        vt = v_ref[j]
        ss = [lax.dot_general(k_ref[h, pl.ds(off, tq), :], q_ref[h], NT, preferred_element_type=F32) for h in range(HEADS)]
        stats = []
        for h in range(HEADS):
            m, l, _ = carry[h]
            s = ss[h] * MLA_SCALE
            if mask:
                s = jnp.where(causal, s, NEG)
            mn = jnp.maximum(m, jnp.max(s, axis=0, keepdims=True))
            pr = jnp.exp(s - mn)
            al = jnp.exp(m - mn)
            stats.append((mn, al * l + jnp.sum(pr, axis=0, keepdims=True), al, pr.astype(BF16)))
        pvs = [jnp.dot(vt, stats[h][3], preferred_element_type=F32) for h in range(HEADS)]
        return tuple((stats[h][0], stats[h][1], stats[h][2] * carry[h][2] + pvs[h]) for h in range(HEADS))

    init = tuple((jnp.full((1, tq), NEG, F32), jnp.zeros((1, tq), F32), jnp.zeros((MLA_KV_RANK, tq), F32)) for _ in range(HEADS))
    carry = lax.fori_loop(0, i, lambda j, c: step(j, c, False), init)
    carry = step(i, carry, True)
    out = jnp.zeros((tq, BW), F32)
    for h in range(HEADS):
        m, l, acc = carry[h]
        out = out + lax.dot_general((acc / l).astype(BF16), wuv_ref[h], TN, preferred_element_type=F32)
    o_ref[...] = out


def _mla_attn(q4, k4, ckvt, wuv, n, t, tq):
    nq = t // tq
    return pl.pallas_call(
        functools.partial(_mla_attn_kernel, tq=tq),
        grid=(n, nq),
        in_specs=[pl.BlockSpec((HEADS, tq, 128), lambda b, i: (0, b * nq + i, 0)),
                  pl.BlockSpec((HEADS, t, 128), lambda b, i: (0, b, 0)),
                  pl.BlockSpec((nq, MLA_KV_RANK, tq), lambda b, i: (b, 0, 0)), _full(wuv.shape)],
        out_specs=pl.BlockSpec((tq, BW), lambda b, i: (b * nq + i, 0)),
        out_shape=jax.ShapeDtypeStruct((n * t, BW), F32),
        compiler_params=_cp("parallel", "arbitrary"),
        name="mla_attn",
    )(q4, k4, ckvt, wuv)


def _mla_decode_kernel(pt_ref, qk_ref, qr_ref, cnew_ref, krnew_ref, kng_ref, wukt_ref, wuv_ref, ind_ref, ckv_hbm, krt_hbm, o_ref,
                       cbuf, kbuf, sem, *, layer, npages, pp):
    n = pl.program_id(0)
    ngroups = npages // pp

    def page_copies(sample, grp, slot):
        base = sample * npages + grp * pp
        out = []
        for i in range(pp):
            page = pt_ref[base + i]
            out.append(pltpu.make_async_copy(ckv_hbm.at[layer, page], cbuf.at[slot, pl.ds(i * PAGE, PAGE)], sem.at[slot]))
            out.append(pltpu.make_async_copy(krt_hbm.at[layer, page], kbuf.at[slot, :, pl.ds(i * PAGE, PAGE)], sem.at[slot]))
        return out

    @pl.when(n == 0)
    def _():
        for cp in page_copies(0, 0, 0):
            cp.start()

    ind = ind_ref[...]
    qbd = (ind * (qk_ref[...] * kng_ref[...])).astype(BF16)
    qr = qr_ref[...].astype(BF16)
    wukt = wukt_ref[...]
    qabs = jnp.dot(qbd, wukt, preferred_element_type=F32).astype(BF16)
    w2 = jnp.concatenate([wukt, qabs, jnp.zeros((8, MLA_KV_RANK), BF16)], axis=0)

    def head_ms(kk):
        rows = [jnp.sum(kk[HD * h:HD * (h + 1)], axis=0, keepdims=True) for h in range(HEADS)]
        return jnp.concatenate(rows + [jnp.ones((8 - HEADS, kk.shape[1]), F32)], axis=0) * (1.0 / HD)

    m = jnp.full((8, 1), NEG, F32)
    l = jnp.zeros((8, 1), F32)
    acc = jnp.zeros((8, MLA_KV_RANK), F32)
    for grp in range(ngroups):
        slot = (n * ngroups + grp) % 2
        if grp + 1 < ngroups:
            for cp in page_copies(n, grp + 1, 1 - slot):
                cp.start()
        else:
            @pl.when(n + 1 < pl.num_programs(0))
            def _():
                for cp in page_copies(n + 1, 0, 1 - slot):
                    cp.start()
        for cp in page_copies(n, grp, slot):
            cp.wait()
        cb = cbuf[slot].astype(BF16)
        krt = kbuf[slot].astype(BF16)
        kq = lax.dot_general(w2, cb, NT, preferred_element_type=F32)
        kt = kq[0:BW]
        num = kq[BW:BW + 8]
        s = (num * lax.rsqrt(head_ms(kt * kt) + EPS) + jnp.dot(qr, krt, preferred_element_type=F32)) * MLA_SCALE
        mn = jnp.maximum(m, jnp.max(s, axis=-1, keepdims=True))
        pb = jnp.exp(s - mn)
        al = jnp.exp(m - mn)
        l = al * l + jnp.sum(pb, axis=-1, keepdims=True)
        acc = al * acc + jnp.dot(pb.astype(BF16), cb, preferred_element_type=F32)
        m = mn

    cb = jnp.broadcast_to(cnew_ref[...], (8, MLA_KV_RANK)).astype(BF16)
    k1 = lax.dot_general(cb, wukt, NT, preferred_element_type=F32)
    krn = krnew_ref[...].astype(BF16).astype(F32)
    num1 = jnp.sum(qabs.astype(F32) * cb.astype(F32), axis=-1, keepdims=True)
    ms1 = jnp.sum(ind * (k1 * k1), axis=-1, keepdims=True) * (1.0 / HD)
    s1 = (num1 * lax.rsqrt(ms1 + EPS) + jnp.sum(qr.astype(F32) * krn, axis=-1, keepdims=True)) * MLA_SCALE
    mn1 = jnp.maximum(m, s1)
    p1 = jnp.exp(s1 - mn1)
    al1 = jnp.exp(m - mn1)
    lat = (al1 * acc + p1 * cb.astype(F32)) / (al1 * l + p1)
    o8 = jnp.dot(lat.astype(BF16), wuv_ref[...], preferred_element_type=F32)
    o_ref[...] = jnp.sum(o8 * ind, axis=0, keepdims=True)


def _mla_decode(layer, page_table, cache_ckv, cache_krt, qk, qr8, cnew, krnew, kng, wuk, wuv, pp):
    ns, npages = page_table.shape
    pt = page_table.reshape(-1)

    def per_sample(shape):
        return pl.BlockSpec((None,) + shape, lambda n, pt_ref: (n, 0, 0))

    def const(a):
        nd = a.ndim
        return pl.BlockSpec(a.shape, lambda n, pt_ref: (0,) * nd)

    ind = jnp.asarray(_IND8)
    hbm = pl.BlockSpec(memory_space=pl.ANY)
    grid_spec = pltpu.PrefetchScalarGridSpec(
        num_scalar_prefetch=1,
        grid=(ns,),
        in_specs=[per_sample((1, BW)), per_sample((8, MLA_ROPE)), per_sample((1, MLA_KV_RANK)), per_sample((1, MLA_ROPE)),
                  const(kng), const(wuk), const(wuv), const(ind), hbm, hbm],
        out_specs=per_sample((1, BW)),
        scratch_shapes=[pltpu.VMEM((2, pp * PAGE, MLA_KV_RANK), F32), pltpu.VMEM((2, MLA_ROPE, pp * PAGE), F32),
                        pltpu.SemaphoreType.DMA((2,))],
    )
    return pl.pallas_call(
        functools.partial(_mla_decode_kernel, layer=layer, npages=npages, pp=pp),
        grid_spec=grid_spec,
        out_shape=jax.ShapeDtypeStruct((ns, 1, BW), F32),
        compiler_params=_cp("arbitrary"),
        name="mla_decode",
    )(pt, qk, qr8, cnew, krnew, kng, wuk, wuv, ind, cache_ckv, cache_krt)


def _gdn_kernel(zc_ref, zba_ref, bat_ref, cw_ref, parr_ref, parc_ref, gout_ref, bones_ref, expb_ref, expg_ref, bdm_ref,
                o_ref, sfin_ref, xbuf, s_sc, *, tg):
    t = pl.program_id(1)
    c = GDN_CHUNK

    @pl.when(t == 0)
    def _():
        xbuf[0:8, :] = jnp.zeros((8, QKV_DIM), F32)
        s_sc[...] = jnp.zeros(s_sc.shape, F32)

    @pl.when(t > 0)
    def _():
        xbuf[5:8, :] = xbuf[tg + 5:tg + 8, :]

    xbuf[8:8 + tg, :] = zc_ref[:, 0:QKV_DIM]
    y = cw_ref[0:1, :] * xbuf[5:5 + tg, :]
    for i in range(1, 4):
        y = y + cw_ref[i:i + 1, :] * xbuf[5 + i:5 + i + tg, :]
    y = _silu(y)

    zba = zba_ref[...]
    beta_col = jax.nn.sigmoid(zba)
    g_col = -jnp.exp(parr_ref[0:1, :]) * jax.nn.softplus(zba + parr_ref[1:2, :])
    g_row = -jnp.exp(parc_ref[:, 0:1]) * jax.nn.softplus(bat_ref[...] + parc_ref[:, 1:2])

    row = lax.broadcasted_iota(jnp.int32, (c, c), 0)
    col = lax.broadcasted_iota(jnp.int32, (c, c), 1)
    incl = col <= row
    strict = col < row
    lt = incl.astype(BF16)
    ut = (col >= row).astype(BF16)
    bones = bones_ref[...]
    bdm = bdm_ref[...]
    expb, expg = expb_ref[...], expg_ref[...]
    brow = lax.broadcasted_iota(jnp.int32, (BW, BW), 0)
    bcl = lax.broadcasted_iota(jnp.int32, (BW, BW), 1)
    same_head = (brow // HD) == (bcl // HD)
    incl_bd = same_head & ((bcl % HD) <= (brow % HD))
    strict_bd = same_head & ((bcl % HD) < (brow % HD))
    eye_bd = (brow == bcl).astype(F32)

    def same_block(b):
        return (brow // b) == (bcl // b)
    same_head2 = jnp.concatenate([same_head, same_head], axis=1)

    def stack4(a):
        return jnp.concatenate([a, a, a, a], axis=0)

    def fold4(a):
        return (a[0:c] + a[c:2 * c]) + (a[2 * c:3 * c] + a[3 * c:4 * c])

    nchunk = tg // c
    pre = []
    for ci in range(nchunk):
        sl = slice(ci * c, (ci + 1) * c)
        q, k, v = y[sl, 0:256], y[sl, 256:512], y[sl, 512:768]
        qn = q * lax.rsqrt(_bdot(q * q, bones) + EPS) * (HD ** -0.5)
        kn = k * lax.rsqrt(_bdot(k * k, bones) + EPS)
        bcol = beta_col[sl, :]
        gcum_c = _sel_dot(lt, g_col[sl, :], True)
        gcum_r = _sel_dot(ut, g_row[:, sl], False)
        gx = _sel_dot(expg, gcum_c, False)
        bx = _sel_dot(expb, bcol, False)
        egx = jnp.exp(gx)
        rhs = _split2(jnp.concatenate([bx * v, bx * egx * kn], axis=1))
        ks = jnp.where(same_head, stack4(kn), 0.0).astype(BF16)
        qs = jnp.where(same_head, stack4(qn), 0.0).astype(BF16)
        gc_s = jnp.concatenate([gcum_c[:, 4 + h:5 + h] for h in range(HEADS)], axis=0)
        gr_s = jnp.concatenate([gcum_r[4 + h:5 + h, :] for h in range(HEADS)], axis=1)
        beta_s = jnp.concatenate([bcol[:, h:h + 1] for h in range(HEADS)], axis=0)
        dm = jnp.exp(jnp.where(incl_bd, gc_s - gr_s, NEG))
        a = jnp.where(strict_bd, beta_s * lax.dot_general(ks, ks, NT, preferred_element_type=F32) * dm, 0.0)
        qk = (lax.dot_general(qs, ks, NT, preferred_element_type=F32) * dm).astype(BF16)
        pre.append(dict(sl=sl, qn=qn, kn=kn, gx=gx, egx=egx, rhs=rhs, qk=qk, a=_split2(a),
                        tinv=eye_bd - jnp.where(same_block(2), a, 0.0)))

    b = 2
    while b < c:
        off = same_block(2 * b) & jnp.logical_not(same_block(b))
        for d in pre:
            d["ts"] = _split2(d["tinv"])
            d["w"] = _split2(_dot3((jnp.where(off, d["a"][0], 0.0), jnp.where(off, d["a"][1], 0.0)), d["ts"]))
        for d in pre:
            d["tinv"] = d["tinv"] - _dot3(d["ts"], d["w"])
        b *= 2
    for d in pre:
        rhs = d["rhs"]
        d["x"] = fold4(jnp.where(same_head2, _dot3(_split2(d["tinv"]), (stack4(rhs[0]), stack4(rhs[1]))), 0.0))

    for d in pre:
        sl, qn, kn, gx, egx, qk, x = d["sl"], d["qn"], d["kn"], d["gx"], d["egx"], d["qk"], d["x"]
        s = s_sc[...]
        sb = s.astype(BF16)
        u = x[:, :BW] - jnp.dot(x[:, BW:].astype(BF16), sb, preferred_element_type=F32)
        ub = u.astype(BF16)
        o = egx * jnp.dot(qn.astype(BF16), sb, preferred_element_type=F32)
        o = o + fold4(jnp.where(same_head, jnp.dot(qk, stack4(ub), preferred_element_type=F32), 0.0))
        glast = gx[c - 1:c, :]
        kf = (kn * jnp.exp(glast - gx)).astype(BF16)
        s_new = jnp.exp(glast) * s + lax.dot_general(kf, ub, TN, preferred_element_type=F32)
        s_sc[...] = s_new * bdm
        on = o * lax.rsqrt(_bdot(o * o, bones) * (1.0 / HD) + EPS) * gout_ref[...]
        o_ref[sl, :] = on * _silu(zc_ref[sl, QKV_DIM:QKV_DIM + BW])

    @pl.when(t == pl.num_programs(1) - 1)
    def _():
        sfin_ref[...] = s_sc[...]


def _gdn_prompt(zc, zba, bat, n, t, p, tg):
    nt = t // tg
    consts = [p["conv_w"], p["gdn_par_r"], p["gdn_par_c"], p["gdn_out_gx"], jnp.asarray(_BONES64, BF16), jnp.asarray(_EXPB, BF16),
              jnp.asarray(_EXPG, BF16), jnp.asarray(_BDMASK)]
    return pl.pallas_call(
        functools.partial(_gdn_kernel, tg=tg),
        grid=(n, nt),
        in_specs=[pl.BlockSpec((tg, 1024), lambda b, i: (b * nt + i, 0)), pl.BlockSpec((tg, 128), lambda b, i: (b * nt + i, 0)),
                  pl.BlockSpec((8, tg), lambda b, i: (0, b * nt + i))] + [_full(c.shape) for c in consts],
        out_specs=[pl.BlockSpec((tg, BW), lambda b, i: (b * nt + i, 0)), pl.BlockSpec((None, BW, BW), lambda b, i: (b, 0, 0))],
        out_shape=[jax.ShapeDtypeStruct((n * t, BW), F32), jax.ShapeDtypeStruct((n, BW, BW), F32)],
        scratch_shapes=[pltpu.VMEM((8 + tg, QKV_DIM), F32), pltpu.VMEM((BW, BW), F32)],
        compiler_params=_cp("parallel", "arbitrary"),
        name="gdn_prompt",
    )(zc, zba, bat, *consts)


def _sample_tok_kernel(za_ref, zc_ref, zba_ref, zm_ref, sconv_ref, cw_ref, gmg_ref, gmw_ref, gmb_ref, parr_ref, bones_ref,
                       memg_ref, bmean_ref, oa_ref, ov_ref, oconv_ref, oq_ref, ok_ref, ovv_ref, ozg_ref, obeta_ref, og_ref, omq_ref):
    ge = _gelu(za_ref[...])
    v = _rms(ge[:, BW:], gmg_ref[...])
    ov_ref[...] = v
    oa_ref[...] = ge[:, :BW] * (gmw_ref[...] * v + gmb_ref[...])
    sc = sconv_ref[...]
    x = zc_ref[:, 0:QKV_DIM]
    y = (cw_ref[0:1, :] * sc[:, 0:768] + cw_ref[1:2, :] * sc[:, 768:1536] + cw_ref[2:3, :] * sc[:, 1536:2304]
         + cw_ref[3:4, :] * x)
    oconv_ref[:, 0:1536] = sc[:, 768:2304]
    oconv_ref[:, 1536:2304] = x
    y = _silu(y)
    q, k = y[:, 0:256], y[:, 256:512]
    bones = bones_ref[...]
    oq_ref[...] = (q * lax.rsqrt(_bdot(q * q, bones) + EPS) * (HD ** -0.5)).T
    ok_ref[...] = (k * lax.rsqrt(_bdot(k * k, bones) + EPS)).T
    ovv_ref[...] = y[:, 512:768].T
    ozg_ref[...] = zc_ref[:, QKV_DIM:QKV_DIM + BW].T
    zba = zba_ref[...]
    obeta_ref[...] = jax.nn.sigmoid(zba).T
    og_ref[...] = (-jnp.exp(parr_ref[0:1, :]) * jax.nn.softplus(zba + parr_ref[1:2, :])).T
    mq = zm_ref[...]
    omq_ref[...] = mq * lax.rsqrt(_bdot(mq * mq, bmean_ref[...]) + EPS) * memg_ref[...]


def _sample_tok(za, zc, zba, zm, sconv, p):
    ns = za.shape[0]
    args = [za, zc, zba, zm, sconv, p["conv_w"], p["gm_norm_g"], p["gm_w0"], p["gm_b0"], p["gdn_par_r"],
            jnp.asarray(_BONES64, BF16), p["mem_qn_gx"], jnp.asarray(_BMEAN64, BF16)]
    shapes = [(ns, BW), (ns, BW), (ns, 2304), (BW, ns), (BW, ns), (BW, ns), (BW, ns), (128, ns), (128, ns), (ns, BW)]
    return pl.pallas_call(
        _sample_tok_kernel,
        in_specs=[_full(a.shape) for a in args],
        out_specs=[_full(s) for s in shapes],
        out_shape=[jax.ShapeDtypeStruct(s, F32) for s in shapes],
        grid=(1,),
        compiler_params=_cp("arbitrary"),
        name="sample_tok",
    )(*args)


def _gdn_step_kernel(s_ref, q_ref, k_ref, v_ref, beta_ref, g_ref, zg_ref, gout_ref, so_ref, o_ref, o_sc):
    q, k = q_ref[...], k_ref[...]
    eg = jnp.exp(g_ref[...])
    beta = beta_ref[...]
    qk = jnp.sum(q * k, axis=0, keepdims=True)
    ssq = jnp.zeros(qk.shape, F32)
    for v in range(HD):
        sv = s_ref[v]
        sk = jnp.sum(sv * k, axis=0, keepdims=True)
        sq = jnp.sum(sv * q, axis=0, keepdims=True)
        u = beta * (v_ref[v:v + 1, :] - eg * sk)
        o = eg * sq + qk * u
        so_ref[v] = eg * sv + u * k
        o_sc[v:v + 1, :] = o
        ssq = ssq + o * o
    o_ref[...] = o_sc[...] * lax.rsqrt(ssq * (1.0 / HD) + EPS) * gout_ref[...] * _silu(zg_ref[...])


def _gdn_step(layer, state_t, q, k, v, beta, g, zg, gout_col):
    ns = state_t.shape[-1]
    vec = pl.BlockSpec((None, HD, ns), lambda h: (h, 0, 0))
    sca = pl.BlockSpec((None, 1, ns), lambda h: (h, 0, 0))
    return pl.pallas_call(
        _gdn_step_kernel,
        grid=(HEADS,),
        in_specs=[pl.BlockSpec((None, None, HD, HD, ns), lambda h: (layer, h, 0, 0, 0)), vec, vec, vec, sca, sca, vec, _full((HD, 1))],
        out_specs=[pl.BlockSpec((None, HD, HD, ns), lambda h: (h, 0, 0, 0)), vec],
        out_shape=[jax.ShapeDtypeStruct((HEADS, HD, HD, ns), F32), jax.ShapeDtypeStruct((HEADS, HD, ns), F32)],
        scratch_shapes=[pltpu.VMEM((HD, ns), F32)],
        compiler_params=_cp("parallel"),
        name="gdn_step",
    )(state_t, q, k, v, beta, g, zg, gout_col)


def _mem_kv_kernel(x_ref, g_ref, w_ref, kg_ref, bmean_ref, ok_ref, ov_ref):
    kv = _bdot(_rms(x_ref[...], g_ref[...]), w_ref[...])
    k = kv[:, 0:BW]
    ok_ref[...] = k * lax.rsqrt(_bdot(k * k, bmean_ref[...]) + EPS) * kg_ref[...]
    ov_ref[...] = kv[:, BW:]


def _mem_kv(mem, g, w, kgx, tm):
    m = mem.shape[0]
    bmean = jnp.asarray(_BMEAN64, BF16)
    return pl.pallas_call(
        _mem_kv_kernel,
        grid=(m // tm,),
        in_specs=[pl.BlockSpec((tm, D_MODEL), lambda i: (i, 0)), _full((1, D_MODEL)), _full(w.shape), _full((1, BW)), _full((BW, BW))],
        out_specs=[pl.BlockSpec((tm, BW), lambda i: (i, 0))] * 2,
        out_shape=[jax.ShapeDtypeStruct((m, BW), F32)] * 2,
        compiler_params=_cp("parallel"),
        name="mem_kv",
    )(mem, g, w, kgx, bmean)


def _mem_attn_kernel(q_ref, k_ref, v_ref, gq_ref, bmean_ref, o_ref):
    q = q_ref[...]
    qn = q * lax.rsqrt(_bdot(q * q, bmean_ref[...]) + EPS) * gq_ref[...]
    kb = k_ref[...].astype(BF16)
    vb = v_ref[...].astype(BF16)
    lh = _lane_head()
    out = jnp.zeros(q.shape, F32)
    for h in range(HEADS):
        mh = lh == h
        s = lax.dot_general(jnp.where(mh, qn, 0.0).astype(BF16), kb, NT, preferred_element_type=F32) * (HD ** -0.5)
        e = jnp.exp(s - jnp.max(s, axis=-1, keepdims=True))
        pr = e / jnp.sum(e, axis=-1, keepdims=True)
        out = out + jnp.where(mh, jnp.dot(pr.astype(BF16), vb, preferred_element_type=F32), 0.0)
    o_ref[...] = out


def _mem_attn(zm, mk, mv, gqx, n, t, mt, tq):
    nq = t // tq
    bmean = jnp.asarray(_BMEAN64, BF16)
    return pl.pallas_call(
        _mem_attn_kernel,
        grid=(n, nq),
        in_specs=[pl.BlockSpec((tq, BW), lambda b, i: (b * nq + i, 0)), pl.BlockSpec((mt, BW), lambda b, i: (b, 0)),
                  pl.BlockSpec((mt, BW), lambda b, i: (b, 0)), _full((1, BW)), _full((BW, BW))],
        out_specs=pl.BlockSpec((tq, BW), lambda b, i: (b * nq + i, 0)),
        out_shape=jax.ShapeDtypeStruct((n * t, BW), F32),
        compiler_params=_cp("parallel", "parallel"),
        name="mem_attn",
    )(zm, mk, mv, gqx, bmean)


def _mem_attn_s_kernel(q_ref, k_ref, v_ref, ind_ref, o_ref, *, bn):
    ind = ind_ref[...]
    for i in range(bn):
        qbd = (ind * q_ref[i:i + 1, :]).astype(BF16)
        s = jnp.dot(qbd, k_ref[i].astype(BF16), preferred_element_type=F32) * (HD ** -0.5)
        e = jnp.exp(s - jnp.max(s, axis=-1, keepdims=True))
        pr = e / jnp.sum(e, axis=-1, keepdims=True)
        o8 = lax.dot_general(pr.astype(BF16), v_ref[i].astype(BF16), NT, preferred_element_type=F32)
        o_ref[i:i + 1, :] = jnp.sum(o8 * ind, axis=0, keepdims=True)


def _mem_attn_s(layer, mqn, cache_k, cache_v, bn):
    ns = mqn.shape[0]
    mt = cache_k.shape[3]
    kv_spec = pl.BlockSpec((None, bn, BW, mt), lambda i: (layer, i, 0, 0))
    return pl.pallas_call(
        functools.partial(_mem_attn_s_kernel, bn=bn),
        grid=(ns // bn,),
        in_specs=[pl.BlockSpec((bn, BW), lambda i: (i, 0)), kv_spec, kv_spec, _full((8, BW))],
        out_specs=pl.BlockSpec((bn, BW), lambda i: (i, 0)),
        out_shape=jax.ShapeDtypeStruct((ns, BW), F32),
        compiler_params=_cp("parallel"),
        name="mem_attn_s",
    )(mqn, cache_k, cache_v, jnp.asarray(_IND8))


def _merge_kernel(x_ref, a_ref, b_ref, c_ref, m_ref, g1_ref, wg_ref, wb_ref, wo_ref, g2_ref, wrh_ref, wrl_ref, br_ref,
                  x1_ref, h2_ref, ei_ref, ew_ref):
    x = x_ref[...]
    hb = _rms(x, g1_ref[...]).astype(BF16)
    acc = jnp.zeros(x.shape, F32)
    for b, br in enumerate((a_ref, b_ref, c_ref, m_ref)):
        gate = jax.nn.sigmoid(jnp.dot(hb, wg_ref[:, b * D_MODEL:(b + 1) * D_MODEL], preferred_element_type=F32))
        acc = acc + gate * jnp.dot(br[...].astype(BF16), wb_ref[b], preferred_element_type=F32)
    x1 = x + jnp.dot(acc.astype(BF16), wo_ref[...], preferred_element_type=F32)
    x1_ref[...] = x1
    h2 = _rms(x1, g2_ref[...])
    h2_ref[...] = h2
    logits = _dot3(_split2(h2), (wrh_ref[...], wrl_ref[...])) + br_ref[...]
    lane = lax.broadcasted_iota(jnp.int32, (1, 128), 1).astype(F32)
    big = 1e9
    lg = jnp.where(lane < N_GROUPS, logits, NEG)
    mg = jnp.max(lg, axis=-1, keepdims=True)
    g_w = 1.0 / jnp.sum(jnp.exp(lg - mg), axis=-1, keepdims=True)
    gi = jnp.min(jnp.where(lg == mg, lane, big), axis=-1, keepdims=True)
    sel = (lane >= N_GROUPS) & (lane < N_GROUPS + N_EXPERTS) & (jnp.floor((lane - N_GROUPS) * (1.0 / EPG)) == gi)
    le = jnp.where(sel, logits, NEG)
    m1 = jnp.max(le, axis=-1, keepdims=True)
    i1 = jnp.min(jnp.where(le == m1, lane, big), axis=-1, keepdims=True)
    le2 = jnp.where(lane == i1, NEG, le)
    m2 = jnp.max(le2, axis=-1, keepdims=True)
    i2 = jnp.min(jnp.where(le2 == m2, lane, big), axis=-1, keepdims=True)
    z = jnp.sum(jnp.exp(le - m1), axis=-1, keepdims=True)
    p1 = 1.0 / z
    p2 = jnp.exp(m2 - m1) / z
    w1 = p1 / (p1 + p2) * g_w
    w2 = p2 / (p1 + p2) * g_w
    ei_ref[...] = jnp.where(lane == 0, i1 - N_GROUPS, jnp.where(lane == 1, i2 - N_GROUPS, 0.0)).astype(jnp.int32)
    ew_ref[...] = jnp.where(lane == 0, w1, jnp.where(lane == 1, w2, 0.0))


def _merge(x, branches, p, tm):
    m = x.shape[0]
    consts = [p["norm1_g"], p["w_gate"], p["w_branch"], p["w_out"], p["norm2_g"], *_split2(p["w_router"]), p["b_router"]]
    tile = lambda w: pl.BlockSpec((tm, w), lambda i: (i, 0))
    return pl.pallas_call(
        _merge_kernel,
        grid=(m // tm,),
        in_specs=[tile(D_MODEL)] + [tile(BW)] * 4 + [_full(c.shape) for c in consts],
        out_specs=[tile(D_MODEL), tile(D_MODEL), tile(128), tile(128)],
        out_shape=[jax.ShapeDtypeStruct((m, D_MODEL), F32), jax.ShapeDtypeStruct((m, D_MODEL), F32),
                   jax.ShapeDtypeStruct((m, 128), jnp.int32), jax.ShapeDtypeStruct((m, 128), F32)],
        compiler_params=_cp("parallel"),
        name="merge",
    )(x, *branches, *consts)


def _dispatch_kernel(poff_ref, plen_ref, nv_ref, pos_ref, h_ref, xs_out, zbuf, sem, zsem, *, tmd, te):
    @pl.when(pl.program_id(0) == 0)
    def _():
        zbuf[...] = jnp.zeros(zbuf.shape, F32)

        def pad_copies(e):
            off = poff_ref[e]
            head = (-off) & 7
            body = plen_ref[e] - head
            out = [(i < head, pltpu.make_async_copy(zbuf.at[pl.ds(0, 1)], xs_out.at[pl.ds(off + i, 1)], zsem)) for i in range(7)]
            b = te // 2
            while b >= 8:
                start = pl.multiple_of(off + head + (body & ~(2 * b - 1)), 8)
                out.append(((body & b) != 0, pltpu.make_async_copy(zbuf.at[pl.ds(0, b)], xs_out.at[pl.ds(start, b)], zsem)))
                b //= 2
            return out

        def tail_copies(t):
            return [pltpu.make_async_copy(zbuf, xs_out.at[pl.ds(pl.multiple_of(t * te + k * zbuf.shape[0], 8), zbuf.shape[0])], zsem)
                    for k in range(te // zbuf.shape[0])]

        def start_pad(e, carry):
            for cond, cp in pad_copies(e):
                pl.when(cond)(cp.start)
            return carry

        def wait_pad(e, carry):
            for cond, cp in pad_copies(e):
                pl.when(cond)(cp.wait)
            return carry

        def start_tail(t, carry):
            for cp in tail_copies(t):
                cp.start()
            return carry

        def wait_tail(t, carry):
            for cp in tail_copies(t):
                cp.wait()
            return carry

        n_tiles = xs_out.shape[0] // te
        lax.fori_loop(0, N_EXPERTS, start_pad, 0)
        lax.fori_loop(nv_ref[0], n_tiles, start_tail, 0)
        lax.fori_loop(0, N_EXPERTS, wait_pad, 0)
        lax.fori_loop(nv_ref[0], n_tiles, wait_tail, 0)

    def issue(i, carry):
        for s in range(2):
            pltpu.make_async_copy(h_ref.at[pl.ds(i, 1)], xs_out.at[pl.ds(pos_ref[2 * i + s], 1)], sem).start()
        return carry

    lax.fori_loop(0, tmd, issue, 0, unroll=8)
    for s in range(2):
        pltpu.make_async_copy(h_ref, xs_out.at[pl.ds(0, tmd)], sem).wait()


def _dispatch(pad_off, pad_len, n_valid, pos, h2, rows, tmd, te):
    m = h2.shape[0]
    grid_spec = pltpu.PrefetchScalarGridSpec(
        num_scalar_prefetch=3,
        grid=(m // tmd,),
        in_specs=[pl.BlockSpec((2 * tmd,), lambda i, po, pn, nv: (i,), memory_space=pltpu.SMEM),
                  pl.BlockSpec((tmd, D_MODEL), lambda i, po, pn, nv: (i, 0))],
        out_specs=pl.BlockSpec(memory_space=pl.ANY),
        scratch_shapes=[pltpu.VMEM((max(te // 2, 8), D_MODEL), F32), pltpu.SemaphoreType.DMA(()), pltpu.SemaphoreType.DMA(())],
    )
    return pl.pallas_call(
        functools.partial(_dispatch_kernel, tmd=tmd, te=te),
        grid_spec=grid_spec,
        out_shape=jax.ShapeDtypeStruct((rows, D_MODEL), F32),
        compiler_params=_cp("arbitrary"),
        name="moe_dispatch",
    )(pad_off, pad_len, n_valid, pos, h2)


def _expert_kernel(te_ref, nv_ref, x_ref, wg_ref, wu_ref, wd_ref, o_ref):
    del te_ref

    @pl.when(pl.program_id(0) < nv_ref[0])
    def _():
        xb = x_ref[...].astype(BF16)
        gt = jnp.dot(xb, wg_ref[...].astype(BF16), preferred_element_type=F32)
        up = jnp.dot(xb, wu_ref[...].astype(BF16), preferred_element_type=F32)
        o_ref[...] = jnp.dot((_silu(gt) * up).astype(BF16), wd_ref[...].astype(BF16), preferred_element_type=F32)

    @pl.when(pl.program_id(0) >= nv_ref[0])
    def _():
        o_ref[...] = jnp.zeros(o_ref.shape, F32)


def _experts(layer, tile_expert, n_valid, xs, w_gate, w_up, w_down, te):
    rows = xs.shape[0]

    def xmap(i, te_ref, nv_ref):
        return (jnp.minimum(i, nv_ref[0] - 1), 0)

    def wmap(i, te_ref, nv_ref):
        return (layer, te_ref[i], 0, 0)

    grid_spec = pltpu.PrefetchScalarGridSpec(
        num_scalar_prefetch=2,
        grid=(rows // te,),
        in_specs=[pl.BlockSpec((te, D_MODEL), xmap), pl.BlockSpec((None, None, D_MODEL, D_EXPERT), wmap),
                  pl.BlockSpec((None, None, D_MODEL, D_EXPERT), wmap), pl.BlockSpec((None, None, D_EXPERT, D_MODEL), wmap)],
        out_specs=pl.BlockSpec((te, D_MODEL), lambda i, te_ref, nv_ref: (i, 0)),
    )
    return pl.pallas_call(
        _expert_kernel,
        grid_spec=grid_spec,
        out_shape=jax.ShapeDtypeStruct((rows, D_MODEL), F32),
        compiler_params=_cp("arbitrary"),
        name="moe_experts",
    )(tile_expert, n_valid, xs, w_gate, w_up, w_down)


def _combine_kernel(pos_ref, posn_ref, x1_ref, ew_ref, ys_hbm, o_ref, rbuf, sem, *, tmc):
    i = pl.program_id(0)
    slot = i % 2

    def gather(p_ref, sl):
        def issue(r, carry):
            for s in range(2):
                pltpu.make_async_copy(ys_hbm.at[pl.ds(p_ref[2 * r + s], 1)], rbuf.at[sl, s, pl.ds(r, 1)], sem.at[sl]).start()
            return carry

        lax.fori_loop(0, tmc, issue, 0, unroll=8)

    @pl.when(i == 0)
    def _():
        gather(pos_ref, 0)

    @pl.when(i + 1 < pl.num_programs(0))
    def _():
        gather(posn_ref, 1 - slot)

    for s in range(2):
        pltpu.make_async_copy(ys_hbm.at[pl.ds(0, tmc)], rbuf.at[slot, s], sem.at[slot]).wait()
    ew = ew_ref[...]
    o_ref[...] = x1_ref[...] + ew[:, 0:1] * rbuf[slot, 0] + ew[:, 1:2] * rbuf[slot, 1]


def _combine(pos, x1, ew, ys, tmc):
    m = x1.shape[0]
    nt = m // tmc
    return pl.pallas_call(
        functools.partial(_combine_kernel, tmc=tmc),
        grid=(nt,),
        in_specs=[pl.BlockSpec((2 * tmc,), lambda i: (i,), memory_space=pltpu.SMEM),
                  pl.BlockSpec((2 * tmc,), lambda i: (jnp.minimum(i + 1, nt - 1),), memory_space=pltpu.SMEM),
                  pl.BlockSpec((tmc, D_MODEL), lambda i: (i, 0)), pl.BlockSpec((tmc, 128), lambda i: (i, 0)),
                  pl.BlockSpec(memory_space=pl.ANY)],
        out_specs=pl.BlockSpec((tmc, D_MODEL), lambda i: (i, 0)),
        out_shape=jax.ShapeDtypeStruct((m, D_MODEL), F32),
        scratch_shapes=[pltpu.VMEM((2, 2, tmc, D_MODEL), F32), pltpu.SemaphoreType.DMA((2,))],
        compiler_params=_cp("arbitrary"),
        name="moe_combine",
    )(pos, pos, x1, ew, ys)


def _moe(layer, x1, h2, ei, ew, w_gate, w_up, w_down, te, tmd):
    m = x1.shape[0]
    flat_e = ei[:, 0:2].reshape(-1)
    onehot = (flat_e[:, None] == jnp.arange(N_EXPERTS, dtype=jnp.int32)[None, :]).astype(jnp.int32)
    csum = jnp.cumsum(onehot, axis=0)
    rank = jnp.sum(csum * onehot, axis=1) - 1
    counts = csum[-1]
    padded = ((counts + te - 1) // te) * te
    pend = jnp.cumsum(padded)
    pstart = pend - padded
    pos = (jnp.sum(onehot * pstart[None, :], axis=1) + rank).astype(jnp.int32)
    rows = ((2 * m + N_EXPERTS * (te - 1)) // te) * te
    n_tiles = rows // te
    n_valid = (pend[-1] // te).astype(jnp.int32).reshape(1)
    tile_start = jnp.arange(n_tiles, dtype=jnp.int32) * te
    tile_expert = jnp.minimum(jnp.sum((tile_start[:, None] >= pend[None, :]).astype(jnp.int32), axis=1), N_EXPERTS - 1)
    last_e = jnp.take(tile_expert, jnp.maximum(n_valid[0] - 1, 0))
    tile_expert = jnp.where(jnp.arange(n_tiles) < n_valid[0], tile_expert, last_e).astype(jnp.int32)
    xs = _dispatch((pstart + counts).astype(jnp.int32), (padded - counts).astype(jnp.int32), n_valid, pos, h2, rows, tmd, te)
    ys = _experts(layer, tile_expert, n_valid, xs, w_gate, w_up, w_down, te)
    return _combine(pos, x1, ew, ys, tmd)


def _tile4(v):
    return jnp.tile(v, HEADS).reshape(1, BW)


def _prep_layer(l, w):
    w_in = w["w_in"][l]
    z = lambda n: jnp.zeros((D_MODEL, n), F32)
    b_al = w_in[:, 1952:1960]
    w_small = jnp.concatenate(
        [w_in[:, 0:512], w_in[:, 512:768], w_in[:, 768:896], z(64), w_in[:, 896:928], z(32), w_in[:, 928:1696],
         w_in[:, 1696:1952], w_in[:, 1960:2216], b_al, z(120)], axis=1).astype(BF16)
    uq = w["mla_w_uq"][l]
    w_uq = jnp.pad(uq, ((0, 0), (0, 0), (0, 32))).reshape(256, 512).astype(BF16)
    qg = jnp.tile(jnp.concatenate([w["mla_qn_g"][l], w["mla_qr_g"][l], jnp.zeros((32,), F32)]), HEADS).reshape(1, 512)
    uk = w["mla_w_uk"][l]
    w_uk_p = jnp.pad(uk, ((0, 0), (0, 0), (0, 64))).reshape(128, 512).astype(BF16)
    kg = jnp.tile(jnp.concatenate([w["mla_kn_g"][l], jnp.zeros((64,), F32)]), HEADS).reshape(1, 512)
    kr_g = jnp.concatenate([jnp.zeros((64,), F32), w["mla_kr_g"][l], jnp.zeros((32,), F32)]).reshape(1, 128)
    uv = w["mla_w_uv"][l]
    w_uv_p = jnp.stack([jnp.pad(uv[:, h, :], ((0, 0), (64 * h, BW - 64 * h - 64))) for h in range(HEADS)]).astype(BF16)
    par_r = jnp.zeros((8, 128), F32).at[0, 4:8].set(w["gdn_a_log"][l]).at[1, 4:8].set(w["gdn_dt_bias"][l])
    par_c = jnp.zeros((8, 128), F32).at[4:8, 0].set(w["gdn_a_log"][l]).at[4:8, 1].set(w["gdn_dt_bias"][l])
    w_router = jnp.concatenate([w["moe_wg"][l], w["moe_we"][l], jnp.zeros((D_MODEL, 128 - 36), F32)], axis=1)
    b_router = jnp.concatenate([w["moe_bg"][l], w["moe_be"][l], jnp.zeros((128 - 36,), F32)]).reshape(1, 128)
    return {
        "norm1_g": w["norm1_g"][l].reshape(1, D_MODEL), "w_small": w_small, "w_bat": b_al.T.astype(BF16),
        "w_gate": w_in[:, 2216:].astype(BF16),
        "gm_norm_g": w["gm_norm_g"][l].reshape(1, BW), "gm_ws": w["gm_ws"][l],
        "gm_bfull": jnp.repeat(w["gm_b"][l].T, HD, axis=1),
        "gm_w0": jnp.repeat(w["gm_ws"][l][:, 0, 0], HD).reshape(1, BW), "gm_b0": jnp.repeat(w["gm_b"][l][:, 0], HD).reshape(1, BW),
        "mla_cq_g": w["mla_cq_g"][l].reshape(1, 256), "w_uq": w_uq, "qg": qg, "mla_ckv_g": w["mla_ckv_g"][l].reshape(1, 128),
        "kr_g": kr_g, "w_uk": w_uk_p, "kg": kg, "w_uv": w_uv_p,
        "w_uk_c": uk.reshape(128, BW).T.astype(BF16), "w_uv_c": uv.reshape(128, BW).astype(BF16), "kn_gx": _tile4(w["mla_kn_g"][l]),
        "conv_w": w["gdn_conv_w"][l], "gdn_par_r": par_r, "gdn_par_c": par_c, "gdn_out_gx": _tile4(w["gdn_out_g"][l]),
        "gdn_out_gc": w["gdn_out_g"][l].reshape(HD, 1),
        "mem_norm_g": w["mem_norm_g"][l].reshape(1, D_MODEL), "mem_w_kv": w["mem_w_kv"][l].astype(BF16),
        "mem_qn_gx": _tile4(w["mem_qn_g"][l]), "mem_kn_gx": _tile4(w["mem_kn_g"][l]),
        "w_branch": w["w_branch"][l].astype(BF16), "w_out": w["w_out"][l].astype(BF16),
        "norm2_g": w["norm2_g"][l].reshape(1, D_MODEL), "w_router": w_router, "b_router": b_router,
    }


def _rope_tables(pos):
    half = MLA_ROPE // 2
    inv = ROPE_THETA ** (-jnp.arange(half, dtype=F32) / half)
    ang = pos.astype(F32)[:, None] * inv[None, :]
    cos, sin = jnp.cos(ang), jnp.sin(ang)
    t = pos.shape[0]
    one, zero = jnp.ones((t, 64), F32), jnp.zeros((t, 64), F32)
    z16, z32 = jnp.zeros((t, 16), F32), jnp.zeros((t, 32), F32)
    return (jnp.concatenate([one, cos, cos, jnp.ones((t, 32), F32)], axis=1),
            jnp.concatenate([zero, -sin, z16, z32], axis=1),
            jnp.concatenate([zero, z16, sin, z32], axis=1))


def kernel(x_prompt, mem_prompt, x_sample, cache_mla_ckv, cache_mla_kr, cache_mem_k, cache_mem_v, state_gdn, state_conv,
           page_table, norm1_g, w_in, gm_norm_g, gm_ws, gm_b, mla_cq_g, mla_w_uq, mla_qn_g, mla_qr_g, mla_ckv_g, mla_kr_g,
           mla_w_uk, mla_kn_g, mla_w_uv, gdn_conv_w, gdn_a_log, gdn_dt_bias, gdn_out_g, mem_norm_g, mem_w_kv, mem_qn_g,
           mem_kn_g, w_branch, w_out, norm2_g, moe_wg, moe_bg, moe_we, moe_be, moe_w_gate, moe_w_up, moe_w_down):
    w = dict(norm1_g=norm1_g, w_in=w_in, gm_norm_g=gm_norm_g, gm_ws=gm_ws, gm_b=gm_b, mla_cq_g=mla_cq_g, mla_w_uq=mla_w_uq,
             mla_qn_g=mla_qn_g, mla_qr_g=mla_qr_g, mla_ckv_g=mla_ckv_g, mla_kr_g=mla_kr_g, mla_w_uk=mla_w_uk, mla_kn_g=mla_kn_g,
             mla_w_uv=mla_w_uv, gdn_conv_w=gdn_conv_w, gdn_a_log=gdn_a_log, gdn_dt_bias=gdn_dt_bias, gdn_out_g=gdn_out_g,
             mem_norm_g=mem_norm_g, mem_w_kv=mem_w_kv, mem_qn_g=mem_qn_g, mem_kn_g=mem_kn_g, w_branch=w_branch, w_out=w_out,
             norm2_g=norm2_g, moe_wg=moe_wg, moe_bg=moe_bg, moe_we=moe_we, moe_be=moe_be)
    depth = w_in.shape[0]
    bp, tp, _ = x_prompt.shape
    bs = x_sample.shape[0]
    mt = mem_prompt.shape[1]
    n_pages = page_table.shape[1]
    past_len = n_pages * cache_mla_ckv.shape[2]
    mp = bp * tp

    tm_p = min(512, mp)
    tq = min(256, tp)
    ta = min(512, tp)
    tg = min(256, tp)
    pp = min(64, n_pages)
    cache_krt = jnp.swapaxes(cache_mla_kr, 2, 3)
    tabs_p = _rope_tables(jnp.arange(tp, dtype=jnp.int32))
    tabs_s = _rope_tables(jnp.full((bs,), past_len, jnp.int32))

    xp = x_prompt.reshape(mp, D_MODEL)
    xs = x_sample.reshape(bs, D_MODEL)
    mem = mem_prompt.reshape(bp * mt, D_MODEL)
    cache_k = cache_mem_k.transpose(0, 1, 3, 4, 2).reshape(depth, bs, BW, mt)
    cache_v = cache_mem_v.transpose(0, 1, 3, 4, 2).reshape(depth, bs, BW, mt)
    state_t = state_gdn.transpose(0, 2, 3, 4, 1)
    rows_p, rows_s = [], []
    for l in range(depth):
        p = _prep_layer(l, w)
        mk, mv = _mem_kv(mem, p["mem_norm_g"], p["mem_w_kv"], p["mem_kn_gx"], min(512, bp * mt))
        tm_f = min(tm_p, tp)
        a_out, q4, k4, ckv, kr, ckvt, zc, zm, zba, bat = _front(xp, tabs_p, tp // tm_f, p, tm_f, ta)
        b_out = _mla_attn(q4, k4, ckvt, p["w_uv"], bp, tp, ta)
        c_out, sfin = _gdn_prompt(zc, zba, bat, bp, tp, p, tg)
        m_out = _mem_attn(zm, mk, mv, p["mem_qn_gx"], bp, tp, mt, tq)
        x1, h2, ei, ew = _merge(xp, (a_out, b_out, c_out, m_out), p, tm_p)
        xp = _moe(l, x1, h2, ei, ew, moe_w_gate, moe_w_up, moe_w_down, min(256, mp), min(512, mp))
        s_p = jnp.stack([sfin[:, 64 * h:64 * h + 64, 64 * h:64 * h + 64] for h in range(HEADS)], axis=1).transpose(0, 1, 3, 2)
        conv_p = zc.reshape(bp, tp, 1024)[:, tp - 3:, 0:QKV_DIM]
        rows_p.append((ckv.reshape(bp, tp, 128), kr.reshape(bp, tp, MLA_ROPE), s_p, conv_p,
                       mk.reshape(bp, mt, HEADS, HD), mv.reshape(bp, mt, HEADS, HD)))
        za, zb, zc, zm, zba, _ = _in_proj(xs, p["norm1_g"], p["w_small"], p["w_bat"], bs)
        a_s, v_s, conv_s, gq, gk, gv, zg, beta, gdec, mqn = _sample_tok(za, zc, zba, zm, state_conv[l].reshape(bs, 3 * QKV_DIM), p)
        _, _, ckv_s, kr_s, _, q32 = _mla_pre(zb, bs, tabs_s, 1, p, bs, bs)
        q3 = q32.reshape(bs, HEADS, 128)
        qk = q3[:, :, 0:64].reshape(bs, 1, BW)
        qr8 = jnp.pad(q3[:, :, 64:96], ((0, 0), (0, 4), (0, 0)))
        b_s = _mla_decode(l, page_table, cache_mla_ckv, cache_krt, qk, qr8, ckv_s.reshape(bs, 1, 128),
                          kr_s.reshape(bs, 1, MLA_ROPE), p["kn_gx"], p["w_uk_c"], p["w_uv_c"], pp).reshape(bs, BW)
        hv = lambda a: a.reshape(HEADS, HD, bs)
        s_new, c_t = _gdn_step(l, state_t, hv(gq), hv(gk), hv(gv), beta[0:4].reshape(HEADS, 1, bs),
                               gdec[4:8].reshape(HEADS, 1, bs), hv(zg), p["gdn_out_gc"])
        m_s = _mem_attn_s(l, mqn, cache_k, cache_v, min(8, bs))
        x1, h2, ei, ew = _merge(xs, (a_s, b_s, c_t.reshape(BW, bs).T, m_s), p, bs)
        xs = _moe(l, x1, h2, ei, ew, moe_w_gate, moe_w_up, moe_w_down, min(32, bs), bs)
        rows_s.append((ckv_s.reshape(bs, 1, 128), kr_s.reshape(bs, 1, MLA_ROPE), s_new.transpose(3, 0, 1, 2),
                       conv_s.reshape(bs, 3, QKV_DIM), v_s.reshape(bs, 1, BW)))
    p_out = [jnp.stack(a) for a in zip(*rows_p)]
    s_out = [jnp.stack(a) for a in zip(*rows_s)]
    return (xp.reshape(bp, tp, D_MODEL), xs.reshape(bs, 1, D_MODEL), *p_out, *s_out)
```

```python
import functools

import numpy as np
import jax
import jax.numpy as jnp
from jax import lax
from jax.experimental import pallas as pl
from jax.experimental.pallas import tpu as pltpu

F32 = jnp.float32
BF16 = jnp.bfloat16
HI = lax.Precision.HIGHEST
EPS = 1e-6
NEG = float("-inf")

D_MODEL = 1024
HEADS = 4
HD = 64
BW = 256
MLA_ROPE = 32
MLA_KV_RANK = 128
MLA_SCALE = 96.0 ** -0.5
ROPE_THETA = 10000.0
GM_CHUNK = 128
GDN_CHUNK = 64
QKV_DIM = 768
N_GROUPS = 4
EPG = 8
N_EXPERTS = 32
D_EXPERT = 256
PAGE = 128
VMEM_LIMIT = 56 * 1024 * 1024

NT = (((1,), (1,)), ((), ()))
TN = (((0,), (0,)), ((), ()))


def _cp(*sem):
    return pltpu.CompilerParams(dimension_semantics=sem, vmem_limit_bytes=VMEM_LIMIT)


def _rms(x, g):
    ms = jnp.sum(x * x, axis=-1, keepdims=True) * (1.0 / x.shape[-1])
    return x * lax.rsqrt(ms + EPS) * g


def _bdot(a, b):
    return jnp.dot(a.astype(BF16), b.astype(BF16), preferred_element_type=F32)


def _bdot_nt(a, b):
    return lax.dot_general(a.astype(BF16), b.astype(BF16), NT, preferred_element_type=F32)


def _hdot(a, b):
    return jnp.dot(a, b, precision=HI, preferred_element_type=F32)


def _split2(a):
    hi = a.astype(BF16)
    return hi, (a - hi.astype(F32)).astype(BF16)


def _dot3(a, b):
    d = lambda x, y: jnp.dot(x, y, preferred_element_type=F32)
    return d(a[0], b[0]) + (d(a[0], b[1]) + d(a[1], b[0]))


def _sel_dot(w01, x, left):
    x0 = x.astype(BF16)
    r1 = x - x0.astype(F32)
    x1 = r1.astype(BF16)
    x2 = (r1 - x1.astype(F32)).astype(BF16)
    d = (lambda p: jnp.dot(w01, p, preferred_element_type=F32)) if left else (lambda p: jnp.dot(p, w01, preferred_element_type=F32))
    return d(x0) + (d(x1) + d(x2))


def _silu(x):
    return x * jax.nn.sigmoid(x)


def _gelu(x):
    return 0.5 * x * (1.0 + lax.erf(x * 0.7071067811865476))


def _lane_head(width=BW):
    return lax.broadcasted_iota(jnp.int32, (1, width), 1) // HD


def _full(shape):
    n = len(shape)
    return pl.BlockSpec(shape, lambda *_: (0,) * n)


def _block_mean(width, segs):
    m = np.zeros((width, width), np.float32)
    for a, b in segs:
        m[a:b, a:b] = 1.0 / (b - a)
    return m


_BMEAN64 = _block_mean(BW, [(64 * h, 64 * h + 64) for h in range(HEADS)])
_BONES64 = _BMEAN64 * 64.0
_BQ = _block_mean(512, [(128 * h, 128 * h + 64) for h in range(HEADS)] + [(128 * h + 64, 128 * h + 96) for h in range(HEADS)])
_BK = _block_mean(512, [(128 * h, 128 * h + 64) for h in range(HEADS)])
_IND8 = np.zeros((8, BW), np.float32)
for _h in range(HEADS):
    _IND8[_h, 64 * _h:64 * _h + 64] = 1.0
_EXPB = np.zeros((128, BW), np.float32)
_EXPG = np.zeros((128, BW), np.float32)
for _h in range(HEADS):
    _EXPB[_h, 64 * _h:64 * _h + 64] = 1.0
    _EXPG[4 + _h, 64 * _h:64 * _h + 64] = 1.0
_BDMASK = (_BONES64 > 0).astype(np.float32)


def _in_proj_kernel(x_ref, g_ref, w_ref, wbat_ref, oa_ref, ob_ref, oc_ref, om_ref, oba_ref, obat_ref):
    hb = _rms(x_ref[...], g_ref[...]).astype(BF16)
    oa_ref[...] = jnp.dot(hb, w_ref[:, 0:512], preferred_element_type=F32)
    ob_ref[...] = jnp.dot(hb, w_ref[:, 512:1024], preferred_element_type=F32)
    oc_ref[...] = jnp.dot(hb, w_ref[:, 1024:2048], preferred_element_type=F32)
    om_ref[...] = jnp.dot(hb, w_ref[:, 2048:2304], preferred_element_type=F32)
    oba_ref[...] = jnp.dot(hb, w_ref[:, 2304:2432], preferred_element_type=F32)
    obat_ref[...] = lax.dot_general(wbat_ref[...], hb, NT, preferred_element_type=F32)


def _in_proj(x, g, w, wbat, tm):
    m = x.shape[0]
    widths = (512, 512, 1024, 256, 128)
    return pl.pallas_call(
        _in_proj_kernel,
        grid=(m // tm,),
        in_specs=[pl.BlockSpec((tm, D_MODEL), lambda i: (i, 0)), _full((1, D_MODEL)), _full(w.shape), _full(wbat.shape)],
        out_specs=[pl.BlockSpec((tm, n), lambda i: (i, 0)) for n in widths] + [pl.BlockSpec((8, tm), lambda i: (0, i))],
        out_shape=[jax.ShapeDtypeStruct((m, n), F32) for n in widths] + [jax.ShapeDtypeStruct((8, m), F32)],
        compiler_params=_cp("parallel"),
        name="in_proj",
    )(x, g, w, wbat)


def _gmlp_body(za, g_ref, ws_ref, b_ref, o_ref):
    row = lax.broadcasted_iota(jnp.int32, (GM_CHUNK, GM_CHUNK), 0)
    col = lax.broadcasted_iota(jnp.int32, (GM_CHUNK, GM_CHUNK), 1)
    tril = col <= row
    lh = _lane_head()
    wts = [jnp.where(tril, ws_ref[g], 0.0).astype(BF16) for g in range(HEADS)]
    for c in range(za.shape[0] // GM_CHUNK):
        sl = slice(c * GM_CHUNK, (c + 1) * GM_CHUNK)
        ge = _gelu(za[sl, :])
        u = ge[:, :BW]
        vb = _rms(ge[:, BW:], g_ref[...]).astype(BF16)
        s = b_ref[...]
        for g in range(HEADS):
            s = s + jnp.where(lh == g, jnp.dot(wts[g], vb, preferred_element_type=F32), 0.0)
        o_ref[sl, :] = u * s


def _mla_pre_body(z, c_ref, s1_ref, s2_ref, gcq_ref, wuq_ref, qg_ref, bq_ref, gckv_ref, gkr_ref, wuk_ref, kg_ref, bk_ref,
                  oq_ref, ok_ref, ockv_ref, okr_ref, ockvt_ref, oq32_ref, tb):
    cs, s1, s2 = c_ref[...], s1_ref[...], s2_ref[...]

    def rope(x):
        return x * cs + pltpu.roll(x, 112, 1) * s1 + pltpu.roll(x, 16, 1) * s2

    cq = _rms(z[:, 0:256], gcq_ref[...])
    q = _bdot(cq, wuq_ref[...])
    qn = q * lax.rsqrt(_bdot(q * q, bq_ref[...]) + EPS) * qg_ref[...]
    ckv = _rms(z[:, 256:384], gckv_ref[...])
    ockv_ref[...] = ckv
    for c in range(ckv.shape[0] // tb):
        ockvt_ref[c] = ckv[c * tb:(c + 1) * tb, :].T.astype(BF16)
    krb = z[:, 384:512]
    kr = rope(krb * lax.rsqrt(jnp.sum(krb * krb, axis=-1, keepdims=True) * (1.0 / MLA_ROPE) + EPS) * gkr_ref[...])
    okr_ref[...] = kr[:, 64:96]
    k = _bdot(ckv, wuk_ref[...])
    kn = k * lax.rsqrt(_bdot(k * k, bk_ref[...]) + EPS) * kg_ref[...]
    for h in range(HEADS):
        sl = slice(128 * h, 128 * h + 128)
        qh = rope(qn[:, sl])
        oq_ref[h] = qh.astype(BF16)
        if oq32_ref is not None:
            oq32_ref[:, sl] = qh
        ok_ref[h] = (kn[:, sl] + kr).astype(BF16)


def _mla_pre_kernel(z_ref, *refs, tb):
    _mla_pre_body(z_ref[...], *refs, tb)


def _mla_consts(p):
    return [p["mla_cq_g"], p["w_uq"], p["qg"], jnp.asarray(_BQ, BF16), p["mla_ckv_g"], p["kr_g"], p["w_uk"], p["kg"],
            jnp.asarray(_BK, BF16)]


def _front_kernel(x_ref, g_ref, w_ref, wbat_ref, gmg_ref, gmws_ref, gmb_ref, *refs, tb):
    mla_in, (oa_ref, oq_ref, ok_ref, ockv_ref, okr_ref, ockvt_ref, oc_ref, om_ref, oba_ref, obat_ref) = refs[:12], refs[12:]
    hb = _rms(x_ref[...], g_ref[...]).astype(BF16)
    _gmlp_body(jnp.dot(hb, w_ref[:, 0:512], preferred_element_type=F32), gmg_ref, gmws_ref, gmb_ref, oa_ref)
    _mla_pre_body(jnp.dot(hb, w_ref[:, 512:1024], preferred_element_type=F32), *mla_in,
                  oq_ref, ok_ref, ockv_ref, okr_ref, ockvt_ref, None, tb)
    oc_ref[...] = jnp.dot(hb, w_ref[:, 1024:2048], preferred_element_type=F32)
    om_ref[...] = jnp.dot(hb, w_ref[:, 2048:2304], preferred_element_type=F32)
    oba_ref[...] = jnp.dot(hb, w_ref[:, 2304:2432], preferred_element_type=F32)
    obat_ref[...] = lax.dot_general(wbat_ref[...], hb, NT, preferred_element_type=F32)


def _front(x, tabs, t_blocks, p, tm, tb):
    m = x.shape[0]
    tab_spec = pl.BlockSpec((tm, 128), lambda i: (i % t_blocks, 0))
    consts = [p["norm1_g"], p["w_small"], p["w_bat"], p["gm_norm_g"], p["gm_ws"], p["gm_bfull"]]
    mla = _mla_consts(p)
    tile = lambda w, dt=F32: (pl.BlockSpec((tm, w), lambda i: (i, 0)), jax.ShapeDtypeStruct((m, w), dt))
    head = (pl.BlockSpec((HEADS, tm, 128), lambda i: (0, i, 0)), jax.ShapeDtypeStruct((HEADS, m, 128), BF16))
    outs = [tile(BW), head, head, tile(128), tile(MLA_ROPE),
            (pl.BlockSpec((tm // tb, 128, tb), lambda i: (i, 0, 0)), jax.ShapeDtypeStruct((m // tb, 128, tb), BF16)),
            tile(1024), tile(BW), tile(128), (pl.BlockSpec((8, tm), lambda i: (0, i)), jax.ShapeDtypeStruct((8, m), F32))]
    return pl.pallas_call(
        functools.partial(_front_kernel, tb=tb),
        grid=(m // tm,),
        in_specs=[pl.BlockSpec((tm, D_MODEL), lambda i: (i, 0))] + [_full(c.shape) for c in consts] + [tab_spec] * 3
        + [_full(c.shape) for c in mla],
        out_specs=[o[0] for o in outs],
        out_shape=[o[1] for o in outs],
        compiler_params=_cp("parallel"),
        name="front",
    )(x, *consts, *tabs, *mla)


def _mla_pre(zb, m, tabs, t_blocks, p, tm, tb):
    cs, s1, s2 = tabs
    tab_spec = pl.BlockSpec((tm, 128), lambda i: (i % t_blocks, 0))
    consts = _mla_consts(p)
    return pl.pallas_call(
        functools.partial(_mla_pre_kernel, tb=tb),
        grid=(m // tm,),
        in_specs=[pl.BlockSpec((tm, 512), lambda i: (i, 0)), tab_spec, tab_spec, tab_spec] + [_full(c.shape) for c in consts],
        out_specs=[pl.BlockSpec((HEADS, tm, 128), lambda i: (0, i, 0)), pl.BlockSpec((HEADS, tm, 128), lambda i: (0, i, 0)),
                   pl.BlockSpec((tm, 128), lambda i: (i, 0)), pl.BlockSpec((tm, MLA_ROPE), lambda i: (i, 0)),
                   pl.BlockSpec((tm // tb, 128, tb), lambda i: (i, 0, 0)), pl.BlockSpec((tm, 512), lambda i: (i, 0))],
        out_shape=[jax.ShapeDtypeStruct((HEADS, m, 128), BF16), jax.ShapeDtypeStruct((HEADS, m, 128), BF16),
                   jax.ShapeDtypeStruct((m, 128), F32), jax.ShapeDtypeStruct((m, MLA_ROPE), F32),
                   jax.ShapeDtypeStruct((m // tb, 128, tb), BF16), jax.ShapeDtypeStruct((m, 512), F32)],
        compiler_params=_cp("parallel"),
        name="mla_pre",
    )(zb, cs, s1, s2, *consts)


def _mla_attn_kernel(q_ref, k_ref, v_ref, wuv_ref, o_ref, *, tq):
    i = pl.program_id(1)
    row = lax.broadcasted_iota(jnp.int32, (tq, tq), 0)
    col = lax.broadcasted_iota(jnp.int32, (tq, tq), 1)
    causal = row <= col

    def step(j, carry, mask):
        off = pl.multiple_of(j * tq, tq)
        vt = v_ref[j]
        ss = [lax.dot_general(k_ref[h, pl.ds(off, tq), :], q_ref[h], NT, preferred_element_type=F32) for h in range(HEADS)]
        stats = []
        for h in range(HEADS):
            m, l, _ = carry[h]
            s = ss[h] * MLA_SCALE
            if mask:
                s = jnp.where(causal, s, NEG)
            mn = jnp.maximum(m, jnp.max(s, axis=0, keepdims=True))
            pr = jnp.exp(s - mn)
            al = jnp.exp(m - mn)
            stats.append((mn, al * l + jnp.sum(pr, axis=0, keepdims=True), al, pr.astype(BF16)))
        pvs = [jnp.dot(vt, stats[h][3], preferred_element_type=F32) for h in range(HEADS)]
        return tuple((stats[h][0], stats[h][1], stats[h][2] * carry[h][2] + pvs[h]) for h in range(HEADS))

    init = tuple((jnp.full((1, tq), NEG, F32), jnp.zeros((1, tq), F32), jnp.zeros((MLA_KV_RANK, tq), F32)) for _ in range(HEADS))
    carry = lax.fori_loop(0, i, lambda j, c: step(j, c, False), init)
    carry = step(i, carry, True)
    out = jnp.zeros((tq, BW), F32)
    for h in range(HEADS):
        m, l, acc = carry[h]
        out = out + lax.dot_general((acc / l).astype(BF16), wuv_ref[h], TN, preferred_element_type=F32)
    o_ref[...] = out


def _mla_attn(q4, k4, ckvt, wuv, n, t, tq):
    nq = t // tq
    return pl.pallas_call(
        functools.partial(_mla_attn_kernel, tq=tq),
        grid=(n, nq),
        in_specs=[pl.BlockSpec((HEADS, tq, 128), lambda b, i: (0, b * nq + i, 0)),
                  pl.BlockSpec((HEADS, t, 128), lambda b, i: (0, b, 0)),
                  pl.BlockSpec((nq, MLA_KV_RANK, tq), lambda b, i: (b, 0, 0)), _full(wuv.shape)],
        out_specs=pl.BlockSpec((tq, BW), lambda b, i: (b * nq + i, 0)),
        out_shape=jax.ShapeDtypeStruct((n * t, BW), F32),
        compiler_params=_cp("parallel", "arbitrary"),
        name="mla_attn",
    )(q4, k4, ckvt, wuv)


def _mla_decode_kernel(pt_ref, qk_ref, qr_ref, cnew_ref, krnew_ref, kng_ref, wukt_ref, wuv_ref, ind_ref, ckv_hbm, krt_hbm, o_ref,
                       cbuf, kbuf, sem, *, layer, npages, pp):
    n = pl.program_id(0)
    ngroups = npages // pp

    def page_copies(sample, grp, slot):
        base = sample * npages + grp * pp
        out = []
        for i in range(pp):
            page = pt_ref[base + i]
            out.append(pltpu.make_async_copy(ckv_hbm.at[layer, page], cbuf.at[slot, pl.ds(i * PAGE, PAGE)], sem.at[slot]))
            out.append(pltpu.make_async_copy(krt_hbm.at[layer, page], kbuf.at[slot, :, pl.ds(i * PAGE, PAGE)], sem.at[slot]))
        return out

    @pl.when(n == 0)
    def _():
        for cp in page_copies(0, 0, 0):
            cp.start()

    ind = ind_ref[...]
    qbd = (ind * (qk_ref[...] * kng_ref[...])).astype(BF16)
    qr = qr_ref[...].astype(BF16)
    wukt = wukt_ref[...]
    qabs = jnp.dot(qbd, wukt, preferred_element_type=F32).astype(BF16)
    w2 = jnp.concatenate([wukt, qabs, jnp.zeros((8, MLA_KV_RANK), BF16)], axis=0)

    def head_ms(kk):
        rows = [jnp.sum(kk[HD * h:HD * (h + 1)], axis=0, keepdims=True) for h in range(HEADS)]
        return jnp.concatenate(rows + [jnp.ones((8 - HEADS, kk.shape[1]), F32)], axis=0) * (1.0 / HD)

    m = jnp.full((8, 1), NEG, F32)
    l = jnp.zeros((8, 1), F32)
    acc = jnp.zeros((8, MLA_KV_RANK), F32)
    for grp in range(ngroups):
        slot = (n * ngroups + grp) % 2
        if grp + 1 < ngroups:
            for cp in page_copies(n, grp + 1, 1 - slot):
                cp.start()
        else:
            @pl.when(n + 1 < pl.num_programs(0))
            def _():
                for cp in page_copies(n + 1, 0, 1 - slot):
                    cp.start()
        for cp in page_copies(n, grp, slot):
            cp.wait()
        cb = cbuf[slot].astype(BF16)
        krt = kbuf[slot].astype(BF16)
        kq = lax.dot_general(w2, cb, NT, preferred_element_type=F32)
        kt = kq[0:BW]
        num = kq[BW:BW + 8]
        s = (num * lax.rsqrt(head_ms(kt * kt) + EPS) + jnp.dot(qr, krt, preferred_element_type=F32)) * MLA_SCALE
        mn = jnp.maximum(m, jnp.max(s, axis=-1, keepdims=True))
        pb = jnp.exp(s - mn)
        al = jnp.exp(m - mn)
        l = al * l + jnp.sum(pb, axis=-1, keepdims=True)
        acc = al * acc + jnp.dot(pb.astype(BF16), cb, preferred_element_type=F32)
        m = mn

    cb = jnp.broadcast_to(cnew_ref[...], (8, MLA_KV_RANK)).astype(BF16)
    k1 = lax.dot_general(cb, wukt, NT, preferred_element_type=F32)
    krn = krnew_ref[...].astype(BF16).astype(F32)
    num1 = jnp.sum(qabs.astype(F32) * cb.astype(F32), axis=-1, keepdims=True)
    ms1 = jnp.sum(ind * (k1 * k1), axis=-1, keepdims=True) * (1.0 / HD)
    s1 = (num1 * lax.rsqrt(ms1 + EPS) + jnp.sum(qr.astype(F32) * krn, axis=-1, keepdims=True)) * MLA_SCALE
    mn1 = jnp.maximum(m, s1)
    p1 = jnp.exp(s1 - mn1)
    al1 = jnp.exp(m - mn1)
    lat = (al1 * acc + p1 * cb.astype(F32)) / (al1 * l + p1)
    o8 = jnp.dot(lat.astype(BF16), wuv_ref[...], preferred_element_type=F32)
    o_ref[...] = jnp.sum(o8 * ind, axis=0, keepdims=True)


def _mla_decode(layer, page_table, cache_ckv, cache_krt, qk, qr8, cnew, krnew, kng, wuk, wuv, pp):
    ns, npages = page_table.shape
    pt = page_table.reshape(-1)

    def per_sample(shape):
        return pl.BlockSpec((None,) + shape, lambda n, pt_ref: (n, 0, 0))

    def const(a):
        nd = a.ndim
        return pl.BlockSpec(a.shape, lambda n, pt_ref: (0,) * nd)

    ind = jnp.asarray(_IND8)
    hbm = pl.BlockSpec(memory_space=pl.ANY)
    grid_spec = pltpu.PrefetchScalarGridSpec(
        num_scalar_prefetch=1,
        grid=(ns,),
        in_specs=[per_sample((1, BW)), per_sample((8, MLA_ROPE)), per_sample((1, MLA_KV_RANK)), per_sample((1, MLA_ROPE)),
                  const(kng), const(wuk), const(wuv), const(ind), hbm, hbm],
        out_specs=per_sample((1, BW)),
        scratch_shapes=[pltpu.VMEM((2, pp * PAGE, MLA_KV_RANK), F32), pltpu.VMEM((2, MLA_ROPE, pp * PAGE), F32),
                        pltpu.SemaphoreType.DMA((2,))],
    )
    return pl.pallas_call(
        functools.partial(_mla_decode_kernel, layer=layer, npages=npages, pp=pp),
        grid_spec=grid_spec,
        out_shape=jax.ShapeDtypeStruct((ns, 1, BW), F32),
        compiler_params=_cp("arbitrary"),
        name="mla_decode",
    )(pt, qk, qr8, cnew, krnew, kng, wuk, wuv, ind, cache_ckv, cache_krt)


def _gdn_kernel(zc_ref, zba_ref, bat_ref, cw_ref, parr_ref, parc_ref, gout_ref, bones_ref, expb_ref, expg_ref, bdm_ref,
                o_ref, sfin_ref, xbuf, s_sc, *, tg):
    t = pl.program_id(1)
    c = GDN_CHUNK

    @pl.when(t == 0)
    def _():
        xbuf[0:8, :] = jnp.zeros((8, QKV_DIM), F32)
        s_sc[...] = jnp.zeros(s_sc.shape, F32)

    @pl.when(t > 0)
    def _():
        xbuf[5:8, :] = xbuf[tg + 5:tg + 8, :]

    xbuf[8:8 + tg, :] = zc_ref[:, 0:QKV_DIM]
    y = cw_ref[0:1, :] * xbuf[5:5 + tg, :]
    for i in range(1, 4):
        y = y + cw_ref[i:i + 1, :] * xbuf[5 + i:5 + i + tg, :]
    y = _silu(y)

    zba = zba_ref[...]
    beta_col = jax.nn.sigmoid(zba)
    g_col = -jnp.exp(parr_ref[0:1, :]) * jax.nn.softplus(zba + parr_ref[1:2, :])
    g_row = -jnp.exp(parc_ref[:, 0:1]) * jax.nn.softplus(bat_ref[...] + parc_ref[:, 1:2])

    row = lax.broadcasted_iota(jnp.int32, (c, c), 0)
    col = lax.broadcasted_iota(jnp.int32, (c, c), 1)
    incl = col <= row
    strict = col < row
    lt = incl.astype(BF16)
    ut = (col >= row).astype(BF16)
    bones = bones_ref[...]
    bdm = bdm_ref[...]
    expb, expg = expb_ref[...], expg_ref[...]
    brow = lax.broadcasted_iota(jnp.int32, (BW, BW), 0)
    bcl = lax.broadcasted_iota(jnp.int32, (BW, BW), 1)
    same_head = (brow // HD) == (bcl // HD)
    incl_bd = same_head & ((bcl % HD) <= (brow % HD))
    strict_bd = same_head & ((bcl % HD) < (brow % HD))
    eye_bd = (brow == bcl).astype(F32)

    def same_block(b):
        return (brow // b) == (bcl // b)
    same_head2 = jnp.concatenate([same_head, same_head], axis=1)

    def stack4(a):
        return jnp.concatenate([a, a, a, a], axis=0)

    def fold4(a):
        return (a[0:c] + a[c:2 * c]) + (a[2 * c:3 * c] + a[3 * c:4 * c])

    nchunk = tg // c
    pre = []
    for ci in range(nchunk):
        sl = slice(ci * c, (ci + 1) * c)
        q, k, v = y[sl, 0:256], y[sl, 256:512], y[sl, 512:768]
        qn = q * lax.rsqrt(_bdot(q * q, bones) + EPS) * (HD ** -0.5)
        kn = k * lax.rsqrt(_bdot(k * k, bones) + EPS)
        bcol = beta_col[sl, :]
        gcum_c = _sel_dot(lt, g_col[sl, :], True)
        gcum_r = _sel_dot(ut, g_row[:, sl], False)
        gx = _sel_dot(expg, gcum_c, False)
        bx = _sel_dot(expb, bcol, False)
        egx = jnp.exp(gx)
        rhs = _split2(jnp.concatenate([bx * v, bx * egx * kn], axis=1))
        ks = jnp.where(same_head, stack4(kn), 0.0).astype(BF16)
        qs = jnp.where(same_head, stack4(qn), 0.0).astype(BF16)
        gc_s = jnp.concatenate([gcum_c[:, 4 + h:5 + h] for h in range(HEADS)], axis=0)
        gr_s = jnp.concatenate([gcum_r[4 + h:5 + h, :] for h in range(HEADS)], axis=1)
        beta_s = jnp.concatenate([bcol[:, h:h + 1] for h in range(HEADS)], axis=0)
        dm = jnp.exp(jnp.where(incl_bd, gc_s - gr_s, NEG))
        a = jnp.where(strict_bd, beta_s * lax.dot_general(ks, ks, NT, preferred_element_type=F32) * dm, 0.0)
        qk = (lax.dot_general(qs, ks, NT, preferred_element_type=F32) * dm).astype(BF16)
        pre.append(dict(sl=sl, qn=qn, kn=kn, gx=gx, egx=egx, rhs=rhs, qk=qk, a=_split2(a),
                        tinv=eye_bd - jnp.where(same_block(2), a, 0.0)))

    b = 2
    while b < c:
        off = same_block(2 * b) & jnp.logical_not(same_block(b))
        for d in pre:
            d["ts"] = _split2(d["tinv"])
            d["w"] = _split2(_dot3((jnp.where(off, d["a"][0], 0.0), jnp.where(off, d["a"][1], 0.0)), d["ts"]))
        for d in pre:
            d["tinv"] = d["tinv"] - _dot3(d["ts"], d["w"])
        b *= 2
    for d in pre:
        rhs = d["rhs"]
        d["x"] = fold4(jnp.where(same_head2, _dot3(_split2(d["tinv"]), (stack4(rhs[0]), stack4(rhs[1]))), 0.0))

    for d in pre:
        sl, qn, kn, gx, egx, qk, x = d["sl"], d["qn"], d["kn"], d["gx"], d["egx"], d["qk"], d["x"]
        s = s_sc[...]
        sb = s.astype(BF16)
        u = x[:, :BW] - jnp.dot(x[:, BW:].astype(BF16), sb, preferred_element_type=F32)
        ub = u.astype(BF16)
        o = egx * jnp.dot(qn.astype(BF16), sb, preferred_element_type=F32)
        o = o + fold4(jnp.where(same_head, jnp.dot(qk, stack4(ub), preferred_element_type=F32), 0.0))
        glast = gx[c - 1:c, :]
        kf = (kn * jnp.exp(glast - gx)).astype(BF16)
        s_new = jnp.exp(glast) * s + lax.dot_general(kf, ub, TN, preferred_element_type=F32)
        s_sc[...] = s_new * bdm
        on = o * lax.rsqrt(_bdot(o * o, bones) * (1.0 / HD) + EPS) * gout_ref[...]
        o_ref[sl, :] = on * _silu(zc_ref[sl, QKV_DIM:QKV_DIM + BW])

    @pl.when(t == pl.num_programs(1) - 1)
    def _():
        sfin_ref[...] = s_sc[...]


def _gdn_prompt(zc, zba, bat, n, t, p, tg):
    nt = t // tg
    consts = [p["conv_w"], p["gdn_par_r"], p["gdn_par_c"], p["gdn_out_gx"], jnp.asarray(_BONES64, BF16), jnp.asarray(_EXPB, BF16),
              jnp.asarray(_EXPG, BF16), jnp.asarray(_BDMASK)]
    return pl.pallas_call(
        functools.partial(_gdn_kernel, tg=tg),
        grid=(n, nt),
        in_specs=[pl.BlockSpec((tg, 1024), lambda b, i: (b * nt + i, 0)), pl.BlockSpec((tg, 128), lambda b, i: (b * nt + i, 0)),
                  pl.BlockSpec((8, tg), lambda b, i: (0, b * nt + i))] + [_full(c.shape) for c in consts],
        out_specs=[pl.BlockSpec((tg, BW), lambda b, i: (b * nt + i, 0)), pl.BlockSpec((None, BW, BW), lambda b, i: (b, 0, 0))],
        out_shape=[jax.ShapeDtypeStruct((n * t, BW), F32), jax.ShapeDtypeStruct((n, BW, BW), F32)],
        scratch_shapes=[pltpu.VMEM((8 + tg, QKV_DIM), F32), pltpu.VMEM((BW, BW), F32)],
        compiler_params=_cp("parallel", "arbitrary"),
        name="gdn_prompt",
    )(zc, zba, bat, *consts)


def _sample_tok_kernel(za_ref, zc_ref, zba_ref, zm_ref, sconv_ref, cw_ref, gmg_ref, gmw_ref, gmb_ref, parr_ref, bones_ref,
                       memg_ref, bmean_ref, oa_ref, ov_ref, oconv_ref, oq_ref, ok_ref, ovv_ref, ozg_ref, obeta_ref, og_ref, omq_ref):
    ge = _gelu(za_ref[...])
    v = _rms(ge[:, BW:], gmg_ref[...])
    ov_ref[...] = v
    oa_ref[...] = ge[:, :BW] * (gmw_ref[...] * v + gmb_ref[...])
    sc = sconv_ref[...]
    x = zc_ref[:, 0:QKV_DIM]
    y = (cw_ref[0:1, :] * sc[:, 0:768] + cw_ref[1:2, :] * sc[:, 768:1536] + cw_ref[2:3, :] * sc[:, 1536:2304]
         + cw_ref[3:4, :] * x)
    oconv_ref[:, 0:1536] = sc[:, 768:2304]
    oconv_ref[:, 1536:2304] = x
    y = _silu(y)
    q, k = y[:, 0:256], y[:, 256:512]
    bones = bones_ref[...]
    oq_ref[...] = (q * lax.rsqrt(_bdot(q * q, bones) + EPS) * (HD ** -0.5)).T
    ok_ref[...] = (k * lax.rsqrt(_bdot(k * k, bones) + EPS)).T
    ovv_ref[...] = y[:, 512:768].T
    ozg_ref[...] = zc_ref[:, QKV_DIM:QKV_DIM + BW].T
    zba = zba_ref[...]
    obeta_ref[...] = jax.nn.sigmoid(zba).T
    og_ref[...] = (-jnp.exp(parr_ref[0:1, :]) * jax.nn.softplus(zba + parr_ref[1:2, :])).T
    mq = zm_ref[...]
    omq_ref[...] = mq * lax.rsqrt(_bdot(mq * mq, bmean_ref[...]) + EPS) * memg_ref[...]


def _sample_tok(za, zc, zba, zm, sconv, p):
    ns = za.shape[0]
    args = [za, zc, zba, zm, sconv, p["conv_w"], p["gm_norm_g"], p["gm_w0"], p["gm_b0"], p["gdn_par_r"],
            jnp.asarray(_BONES64, BF16), p["mem_qn_gx"], jnp.asarray(_BMEAN64, BF16)]
    shapes = [(ns, BW), (ns, BW), (ns, 2304), (BW, ns), (BW, ns), (BW, ns), (BW, ns), (128, ns), (128, ns), (ns, BW)]
    return pl.pallas_call(
        _sample_tok_kernel,
        in_specs=[_full(a.shape) for a in args],
        out_specs=[_full(s) for s in shapes],
        out_shape=[jax.ShapeDtypeStruct(s, F32) for s in shapes],
        grid=(1,),
        compiler_params=_cp("arbitrary"),
        name="sample_tok",
    )(*args)


def _gdn_step_kernel(s_ref, q_ref, k_ref, v_ref, beta_ref, g_ref, zg_ref, gout_ref, so_ref, o_ref, o_sc):
    q, k = q_ref[...], k_ref[...]
    eg = jnp.exp(g_ref[...])
    beta = beta_ref[...]
    qk = jnp.sum(q * k, axis=0, keepdims=True)
    ssq = jnp.zeros(qk.shape, F32)
    for v in range(HD):
        sv = s_ref[v]
        sk = jnp.sum(sv * k, axis=0, keepdims=True)
        sq = jnp.sum(sv * q, axis=0, keepdims=True)
        u = beta * (v_ref[v:v + 1, :] - eg * sk)
        o = eg * sq + qk * u
        so_ref[v] = eg * sv + u * k
        o_sc[v:v + 1, :] = o
        ssq = ssq + o * o
    o_ref[...] = o_sc[...] * lax.rsqrt(ssq * (1.0 / HD) + EPS) * gout_ref[...] * _silu(zg_ref[...])


def _gdn_step(layer, state_t, q, k, v, beta, g, zg, gout_col):
    ns = state_t.shape[-1]
    vec = pl.BlockSpec((None, HD, ns), lambda h: (h, 0, 0))
    sca = pl.BlockSpec((None, 1, ns), lambda h: (h, 0, 0))
    return pl.pallas_call(
        _gdn_step_kernel,
        grid=(HEADS,),
        in_specs=[pl.BlockSpec((None, None, HD, HD, ns), lambda h: (layer, h, 0, 0, 0)), vec, vec, vec, sca, sca, vec, _full((HD, 1))],
        out_specs=[pl.BlockSpec((None, HD, HD, ns), lambda h: (h, 0, 0, 0)), vec],
        out_shape=[jax.ShapeDtypeStruct((HEADS, HD, HD, ns), F32), jax.ShapeDtypeStruct((HEADS, HD, ns), F32)],
        scratch_shapes=[pltpu.VMEM((HD, ns), F32)],
        compiler_params=_cp("parallel"),
        name="gdn_step",
    )(state_t, q, k, v, beta, g, zg, gout_col)


def _mem_kv_kernel(x_ref, g_ref, w_ref, kg_ref, bmean_ref, ok_ref, ov_ref):
    kv = _bdot(_rms(x_ref[...], g_ref[...]), w_ref[...])
    k = kv[:, 0:BW]
    ok_ref[...] = k * lax.rsqrt(_bdot(k * k, bmean_ref[...]) + EPS) * kg_ref[...]
    ov_ref[...] = kv[:, BW:]


def _mem_kv(mem, g, w, kgx, tm):
    m = mem.shape[0]
    bmean = jnp.asarray(_BMEAN64, BF16)
    return pl.pallas_call(
        _mem_kv_kernel,
        grid=(m // tm,),
        in_specs=[pl.BlockSpec((tm, D_MODEL), lambda i: (i, 0)), _full((1, D_MODEL)), _full(w.shape), _full((1, BW)), _full((BW, BW))],
        out_specs=[pl.BlockSpec((tm, BW), lambda i: (i, 0))] * 2,
        out_shape=[jax.ShapeDtypeStruct((m, BW), F32)] * 2,
        compiler_params=_cp("parallel"),
        name="mem_kv",
    )(mem, g, w, kgx, bmean)


def _mem_attn_kernel(q_ref, k_ref, v_ref, gq_ref, bmean_ref, o_ref):
    q = q_ref[...]
    qn = q * lax.rsqrt(_bdot(q * q, bmean_ref[...]) + EPS) * gq_ref[...]
    kb = k_ref[...].astype(BF16)
    vb = v_ref[...].astype(BF16)
    lh = _lane_head()
    out = jnp.zeros(q.shape, F32)
    for h in range(HEADS):
        mh = lh == h
        s = lax.dot_general(jnp.where(mh, qn, 0.0).astype(BF16), kb, NT, preferred_element_type=F32) * (HD ** -0.5)
        e = jnp.exp(s - jnp.max(s, axis=-1, keepdims=True))
        pr = e / jnp.sum(e, axis=-1, keepdims=True)
        out = out + jnp.where(mh, jnp.dot(pr.astype(BF16), vb, preferred_element_type=F32), 0.0)
    o_ref[...] = out


def _mem_attn(zm, mk, mv, gqx, n, t, mt, tq):
    nq = t // tq
    bmean = jnp.asarray(_BMEAN64, BF16)
    return pl.pallas_call(
        _mem_attn_kernel,
        grid=(n, nq),
        in_specs=[pl.BlockSpec((tq, BW), lambda b, i: (b * nq + i, 0)), pl.BlockSpec((mt, BW), lambda b, i: (b, 0)),
                  pl.BlockSpec((mt, BW), lambda b, i: (b, 0)), _full((1, BW)), _full((BW, BW))],
        out_specs=pl.BlockSpec((tq, BW), lambda b, i: (b * nq + i, 0)),
        out_shape=jax.ShapeDtypeStruct((n * t, BW), F32),
        compiler_params=_cp("parallel", "parallel"),
        name="mem_attn",
    )(zm, mk, mv, gqx, bmean)


def _mem_attn_s_kernel(q_ref, k_ref, v_ref, ind_ref, o_ref, *, bn):
    ind = ind_ref[...]
    for i in range(bn):
        qbd = (ind * q_ref[i:i + 1, :]).astype(BF16)
        s = jnp.dot(qbd, k_ref[i].astype(BF16), preferred_element_type=F32) * (HD ** -0.5)
        e = jnp.exp(s - jnp.max(s, axis=-1, keepdims=True))
        pr = e / jnp.sum(e, axis=-1, keepdims=True)
        o8 = lax.dot_general(pr.astype(BF16), v_ref[i].astype(BF16), NT, preferred_element_type=F32)
        o_ref[i:i + 1, :] = jnp.sum(o8 * ind, axis=0, keepdims=True)


def _mem_attn_s(layer, mqn, cache_k, cache_v, bn):
    ns = mqn.shape[0]
    mt = cache_k.shape[3]
    kv_spec = pl.BlockSpec((None, bn, BW, mt), lambda i: (layer, i, 0, 0))
    return pl.pallas_call(
        functools.partial(_mem_attn_s_kernel, bn=bn),
        grid=(ns // bn,),
        in_specs=[pl.BlockSpec((bn, BW), lambda i: (i, 0)), kv_spec, kv_spec, _full((8, BW))],
        out_specs=pl.BlockSpec((bn, BW), lambda i: (i, 0)),
        out_shape=jax.ShapeDtypeStruct((ns, BW), F32),
        compiler_params=_cp("parallel"),
        name="mem_attn_s",
    )(mqn, cache_k, cache_v, jnp.asarray(_IND8))


def _merge_kernel(x_ref, a_ref, b_ref, c_ref, m_ref, g1_ref, wg_ref, wb_ref, wo_ref, g2_ref, wrh_ref, wrl_ref, br_ref,
                  x1_ref, h2_ref, ei_ref, ew_ref):
    x = x_ref[...]
    hb = _rms(x, g1_ref[...]).astype(BF16)
    acc = jnp.zeros(x.shape, F32)
    for b, br in enumerate((a_ref, b_ref, c_ref, m_ref)):
        gate = jax.nn.sigmoid(jnp.dot(hb, wg_ref[:, b * D_MODEL:(b + 1) * D_MODEL], preferred_element_type=F32))
        acc = acc + gate * jnp.dot(br[...].astype(BF16), wb_ref[b], preferred_element_type=F32)
    x1 = x + jnp.dot(acc.astype(BF16), wo_ref[...], preferred_element_type=F32)
    x1_ref[...] = x1
    h2 = _rms(x1, g2_ref[...])
    h2_ref[...] = h2
    logits = _dot3(_split2(h2), (wrh_ref[...], wrl_ref[...])) + br_ref[...]
    lane = lax.broadcasted_iota(jnp.int32, (1, 128), 1).astype(F32)
    big = 1e9
    lg = jnp.where(lane < N_GROUPS, logits, NEG)
    mg = jnp.max(lg, axis=-1, keepdims=True)
    g_w = 1.0 / jnp.sum(jnp.exp(lg - mg), axis=-1, keepdims=True)
    gi = jnp.min(jnp.where(lg == mg, lane, big), axis=-1, keepdims=True)
    sel = (lane >= N_GROUPS) & (lane < N_GROUPS + N_EXPERTS) & (jnp.floor((lane - N_GROUPS) * (1.0 / EPG)) == gi)
    le = jnp.where(sel, logits, NEG)
    m1 = jnp.max(le, axis=-1, keepdims=True)
    i1 = jnp.min(jnp.where(le == m1, lane, big), axis=-1, keepdims=True)
    le2 = jnp.where(lane == i1, NEG, le)
    m2 = jnp.max(le2, axis=-1, keepdims=True)
    i2 = jnp.min(jnp.where(le2 == m2, lane, big), axis=-1, keepdims=True)
    z = jnp.sum(jnp.exp(le - m1), axis=-1, keepdims=True)
    p1 = 1.0 / z
    p2 = jnp.exp(m2 - m1) / z
    w1 = p1 / (p1 + p2) * g_w
    w2 = p2 / (p1 + p2) * g_w
    ei_ref[...] = jnp.where(lane == 0, i1 - N_GROUPS, jnp.where(lane == 1, i2 - N_GROUPS, 0.0)).astype(jnp.int32)
    ew_ref[...] = jnp.where(lane == 0, w1, jnp.where(lane == 1, w2, 0.0))


def _merge(x, branches, p, tm):
    m = x.shape[0]
    consts = [p["norm1_g"], p["w_gate"], p["w_branch"], p["w_out"], p["norm2_g"], *_split2(p["w_router"]), p["b_router"]]
    tile = lambda w: pl.BlockSpec((tm, w), lambda i: (i, 0))
    return pl.pallas_call(
        _merge_kernel,
        grid=(m // tm,),
        in_specs=[tile(D_MODEL)] + [tile(BW)] * 4 + [_full(c.shape) for c in consts],
        out_specs=[tile(D_MODEL), tile(D_MODEL), tile(128), tile(128)],
        out_shape=[jax.ShapeDtypeStruct((m, D_MODEL), F32), jax.ShapeDtypeStruct((m, D_MODEL), F32),
                   jax.ShapeDtypeStruct((m, 128), jnp.int32), jax.ShapeDtypeStruct((m, 128), F32)],
        compiler_params=_cp("parallel"),
        name="merge",
    )(x, *branches, *consts)


def _dispatch_kernel(poff_ref, plen_ref, nv_ref, pos_ref, h_ref, posb_ref, hb_ref, xs_out, zbuf, sem, zsem, *, tmd, te, n1):
    @pl.when(pl.program_id(0) == 0)
    def _():
        zbuf[...] = jnp.zeros(zbuf.shape, F32)

        def pad_copies(e):
            off = poff_ref[e]
            head = (-off) & 7
            body = plen_ref[e] - head
            out = [(i < head, pltpu.make_async_copy(zbuf.at[pl.ds(0, 1)], xs_out.at[pl.ds(off + i, 1)], zsem)) for i in range(7)]
            b = te // 2
            while b >= 8:
                start = pl.multiple_of(off + head + (body & ~(2 * b - 1)), 8)
                out.append(((body & b) != 0, pltpu.make_async_copy(zbuf.at[pl.ds(0, b)], xs_out.at[pl.ds(start, b)], zsem)))
                b //= 2
            return out

        def tail_copies(t):
            return [pltpu.make_async_copy(zbuf, xs_out.at[pl.ds(pl.multiple_of(t * te + k * zbuf.shape[0], 8), zbuf.shape[0])], zsem)
                    for k in range(te // zbuf.shape[0])]

        def start_pad(e, carry):
            for cond, cp in pad_copies(e):
                pl.when(cond)(cp.start)
            return carry

        def wait_pad(e, carry):
            for cond, cp in pad_copies(e):
                pl.when(cond)(cp.wait)
            return carry

        def start_tail(t, carry):
            for cp in tail_copies(t):
                cp.start()
            return carry

        def wait_tail(t, carry):
            for cp in tail_copies(t):
                cp.wait()
            return carry

        n_tiles = xs_out.shape[0] // te
        lax.fori_loop(0, N_EXPERTS, start_pad, 0)
        lax.fori_loop(nv_ref[0], n_tiles, start_tail, 0)
        lax.fori_loop(0, N_EXPERTS, wait_pad, 0)
        lax.fori_loop(nv_ref[0], n_tiles, wait_tail, 0)

    def scatter(p_ref, src_ref, nrows):
        def issue(i, carry):
            for s in range(2):
                pltpu.make_async_copy(src_ref.at[pl.ds(i, 1)], xs_out.at[pl.ds(p_ref[2 * i + s], 1)], sem).start()
            return carry

        lax.fori_loop(0, nrows, issue, 0, unroll=8)
        for s in range(2):
            pltpu.make_async_copy(src_ref, xs_out.at[pl.ds(0, nrows)], sem).wait()

    pl.when(pl.program_id(0) < n1)(lambda: scatter(pos_ref, h_ref, tmd))
    pl.when(pl.program_id(0) == n1)(lambda: scatter(posb_ref, hb_ref, hb_ref.shape[0]))


def _dispatch(pad_off, pad_len, n_valid, pos, h2, posb, h2b, rows, tmd, te):
    n1 = h2.shape[0] // tmd
    mb = h2b.shape[0]
    grid_spec = pltpu.PrefetchScalarGridSpec(
        num_scalar_prefetch=3,
        grid=(n1 + 1,),
        in_specs=[pl.BlockSpec((2 * tmd,), lambda i, po, pn, nv: (jnp.minimum(i, n1 - 1),), memory_space=pltpu.SMEM),
                  pl.BlockSpec((tmd, D_MODEL), lambda i, po, pn, nv: (jnp.minimum(i, n1 - 1), 0)),
                  pl.BlockSpec((2 * mb,), lambda i, po, pn, nv: (0,), memory_space=pltpu.SMEM),
                  pl.BlockSpec((mb, D_MODEL), lambda i, po, pn, nv: (0, 0))],
        out_specs=pl.BlockSpec(memory_space=pl.ANY),
        scratch_shapes=[pltpu.VMEM((max(te // 2, 8), D_MODEL), F32), pltpu.SemaphoreType.DMA(()), pltpu.SemaphoreType.DMA(())],
    )
    return pl.pallas_call(
        functools.partial(_dispatch_kernel, tmd=tmd, te=te, n1=n1),
        grid_spec=grid_spec,
        out_shape=jax.ShapeDtypeStruct((rows, D_MODEL), F32),
        compiler_params=_cp("arbitrary"),
        name="moe_dispatch",
    )(pad_off, pad_len, n_valid, pos, h2, posb, h2b)


def _expert_kernel(te_ref, nv_ref, x_ref, wg_ref, wu_ref, wd_ref, o_ref):
    del te_ref

    @pl.when(pl.program_id(0) < nv_ref[0])
    def _():
        xb = x_ref[...].astype(BF16)
        gt = jnp.dot(xb, wg_ref[...].astype(BF16), preferred_element_type=F32)
        up = jnp.dot(xb, wu_ref[...].astype(BF16), preferred_element_type=F32)
        o_ref[...] = jnp.dot((_silu(gt) * up).astype(BF16), wd_ref[...].astype(BF16), preferred_element_type=F32)

    @pl.when(pl.program_id(0) >= nv_ref[0])
    def _():
        o_ref[...] = jnp.zeros(o_ref.shape, F32)


def _experts(layer, tile_expert, n_valid, xs, w_gate, w_up, w_down, te):
    rows = xs.shape[0]

    def xmap(i, te_ref, nv_ref):
        return (jnp.minimum(i, nv_ref[0] - 1), 0)

    def wmap(i, te_ref, nv_ref):
        return (layer, te_ref[i], 0, 0)

    grid_spec = pltpu.PrefetchScalarGridSpec(
        num_scalar_prefetch=2,
        grid=(rows // te,),
        in_specs=[pl.BlockSpec((te, D_MODEL), xmap), pl.BlockSpec((None, None, D_MODEL, D_EXPERT), wmap),
                  pl.BlockSpec((None, None, D_MODEL, D_EXPERT), wmap), pl.BlockSpec((None, None, D_EXPERT, D_MODEL), wmap)],
        out_specs=pl.BlockSpec((te, D_MODEL), lambda i, te_ref, nv_ref: (i, 0)),
    )
    return pl.pallas_call(
        _expert_kernel,
        grid_spec=grid_spec,
        out_shape=jax.ShapeDtypeStruct((rows, D_MODEL), F32),
        compiler_params=_cp("arbitrary"),
        name="moe_experts",
    )(tile_expert, n_valid, xs, w_gate, w_up, w_down)


def _combine_kernel(pos_ref, posn_ref, x1_ref, ew_ref, ys_hbm, o_ref, rbuf, sem, *, tmc):
    i = pl.program_id(0)
    slot = i % 2

    def gather(p_ref, sl):
        def issue(r, carry):
            for s in range(2):
                pltpu.make_async_copy(ys_hbm.at[pl.ds(p_ref[2 * r + s], 1)], rbuf.at[sl, s, pl.ds(r, 1)], sem.at[sl]).start()
            return carry

        lax.fori_loop(0, tmc, issue, 0, unroll=8)

    @pl.when(i == 0)
    def _():
        gather(pos_ref, 0)

    @pl.when(i + 1 < pl.num_programs(0))
    def _():
        gather(posn_ref, 1 - slot)

    for s in range(2):
        pltpu.make_async_copy(ys_hbm.at[pl.ds(0, tmc)], rbuf.at[slot, s], sem.at[slot]).wait()
    ew = ew_ref[...]
    o_ref[...] = x1_ref[...] + ew[:, 0:1] * rbuf[slot, 0] + ew[:, 1:2] * rbuf[slot, 1]


def _combine(pos, x1, ew, ys, tmc):
    m = x1.shape[0]
    nt = m // tmc
    return pl.pallas_call(
        functools.partial(_combine_kernel, tmc=tmc),
        grid=(nt,),
        in_specs=[pl.BlockSpec((2 * tmc,), lambda i: (i,), memory_space=pltpu.SMEM),
                  pl.BlockSpec((2 * tmc,), lambda i: (jnp.minimum(i + 1, nt - 1),), memory_space=pltpu.SMEM),
                  pl.BlockSpec((tmc, D_MODEL), lambda i: (i, 0)), pl.BlockSpec((tmc, 128), lambda i: (i, 0)),
                  pl.BlockSpec(memory_space=pl.ANY)],
        out_specs=pl.BlockSpec((tmc, D_MODEL), lambda i: (i, 0)),
        out_shape=jax.ShapeDtypeStruct((m, D_MODEL), F32),
        scratch_shapes=[pltpu.VMEM((2, 2, tmc, D_MODEL), F32), pltpu.SemaphoreType.DMA((2,))],
        compiler_params=_cp("arbitrary"),
        name="moe_combine",
    )(pos, pos, x1, ew, ys)


def _moe(layer, groups, w_gate, w_up, w_down, te):
    assert len(groups) == 2 and groups[1][0].shape[0] == groups[1][4]
    sizes = [g[0].shape[0] for g in groups]
    m = sum(sizes)
    flat_e = jnp.concatenate([g[2][:, 0:2].reshape(-1) for g in groups])
    onehot = (flat_e[:, None] == jnp.arange(N_EXPERTS, dtype=jnp.int32)[None, :]).astype(jnp.int32)
    csum = jnp.cumsum(onehot, axis=0)
    rank = jnp.sum(csum * onehot, axis=1) - 1
    counts = csum[-1]
    padded = ((counts + te - 1) // te) * te
    pend = jnp.cumsum(padded)
    pstart = pend - padded
    pos = (jnp.sum(onehot * pstart[None, :], axis=1) + rank).astype(jnp.int32)
    rows = ((2 * m + N_EXPERTS * (te - 1)) // te) * te
    n_tiles = rows // te
    n_valid = (pend[-1] // te).astype(jnp.int32).reshape(1)
    tile_start = jnp.arange(n_tiles, dtype=jnp.int32) * te
    tile_expert = jnp.minimum(jnp.sum((tile_start[:, None] >= pend[None, :]).astype(jnp.int32), axis=1), N_EXPERTS - 1)
    last_e = jnp.take(tile_expert, jnp.maximum(n_valid[0] - 1, 0))
    tile_expert = jnp.where(jnp.arange(n_tiles) < n_valid[0], tile_expert, last_e).astype(jnp.int32)
    starts = np.cumsum([0] + sizes)
    gpos = [pos[2 * starts[i]:2 * starts[i + 1]] for i in range(len(groups))]
    xs = _dispatch((pstart + counts).astype(jnp.int32), (padded - counts).astype(jnp.int32), n_valid, gpos[0], groups[0][1],
                   gpos[1], groups[1][1], rows, groups[0][4], te)
    ys = _experts(layer, tile_expert, n_valid, xs, w_gate, w_up, w_down, te)
    return [_combine(p, g[0], g[3], ys, g[4]) for g, p in zip(groups, gpos)]


def _tile4(v):
    return jnp.tile(v, HEADS).reshape(1, BW)


def _prep_layer(l, w):
    w_in = w["w_in"][l]
    z = lambda n: jnp.zeros((D_MODEL, n), F32)
    b_al = w_in[:, 1952:1960]
    w_small = jnp.concatenate(
        [w_in[:, 0:512], w_in[:, 512:768], w_in[:, 768:896], z(64), w_in[:, 896:928], z(32), w_in[:, 928:1696],
         w_in[:, 1696:1952], w_in[:, 1960:2216], b_al, z(120)], axis=1).astype(BF16)
    uq = w["mla_w_uq"][l]
    w_uq = jnp.pad(uq, ((0, 0), (0, 0), (0, 32))).reshape(256, 512).astype(BF16)
    qg = jnp.tile(jnp.concatenate([w["mla_qn_g"][l], w["mla_qr_g"][l], jnp.zeros((32,), F32)]), HEADS).reshape(1, 512)
    uk = w["mla_w_uk"][l]
    w_uk_p = jnp.pad(uk, ((0, 0), (0, 0), (0, 64))).reshape(128, 512).astype(BF16)
    kg = jnp.tile(jnp.concatenate([w["mla_kn_g"][l], jnp.zeros((64,), F32)]), HEADS).reshape(1, 512)
    kr_g = jnp.concatenate([jnp.zeros((64,), F32), w["mla_kr_g"][l], jnp.zeros((32,), F32)]).reshape(1, 128)
    uv = w["mla_w_uv"][l]
    w_uv_p = jnp.stack([jnp.pad(uv[:, h, :], ((0, 0), (64 * h, BW - 64 * h - 64))) for h in range(HEADS)]).astype(BF16)
    par_r = jnp.zeros((8, 128), F32).at[0, 4:8].set(w["gdn_a_log"][l]).at[1, 4:8].set(w["gdn_dt_bias"][l])
    par_c = jnp.zeros((8, 128), F32).at[4:8, 0].set(w["gdn_a_log"][l]).at[4:8, 1].set(w["gdn_dt_bias"][l])
    w_router = jnp.concatenate([w["moe_wg"][l], w["moe_we"][l], jnp.zeros((D_MODEL, 128 - 36), F32)], axis=1)
    b_router = jnp.concatenate([w["moe_bg"][l], w["moe_be"][l], jnp.zeros((128 - 36,), F32)]).reshape(1, 128)
    return {
        "norm1_g": w["norm1_g"][l].reshape(1, D_MODEL), "w_small": w_small, "w_bat": b_al.T.astype(BF16),
        "w_gate": w_in[:, 2216:].astype(BF16),
        "gm_norm_g": w["gm_norm_g"][l].reshape(1, BW), "gm_ws": w["gm_ws"][l],
        "gm_bfull": jnp.repeat(w["gm_b"][l].T, HD, axis=1),
        "gm_w0": jnp.repeat(w["gm_ws"][l][:, 0, 0], HD).reshape(1, BW), "gm_b0": jnp.repeat(w["gm_b"][l][:, 0], HD).reshape(1, BW),
        "mla_cq_g": w["mla_cq_g"][l].reshape(1, 256), "w_uq": w_uq, "qg": qg, "mla_ckv_g": w["mla_ckv_g"][l].reshape(1, 128),
        "kr_g": kr_g, "w_uk": w_uk_p, "kg": kg, "w_uv": w_uv_p,
        "w_uk_c": uk.reshape(128, BW).T.astype(BF16), "w_uv_c": uv.reshape(128, BW).astype(BF16), "kn_gx": _tile4(w["mla_kn_g"][l]),
        "conv_w": w["gdn_conv_w"][l], "gdn_par_r": par_r, "gdn_par_c": par_c, "gdn_out_gx": _tile4(w["gdn_out_g"][l]),
        "gdn_out_gc": w["gdn_out_g"][l].reshape(HD, 1),
        "mem_norm_g": w["mem_norm_g"][l].reshape(1, D_MODEL), "mem_w_kv": w["mem_w_kv"][l].astype(BF16),
        "mem_qn_gx": _tile4(w["mem_qn_g"][l]), "mem_kn_gx": _tile4(w["mem_kn_g"][l]),
        "w_branch": w["w_branch"][l].astype(BF16), "w_out": w["w_out"][l].astype(BF16),
        "norm2_g": w["norm2_g"][l].reshape(1, D_MODEL), "w_router": w_router, "b_router": b_router,
    }


def _rope_tables(pos):
    half = MLA_ROPE // 2
    inv = ROPE_THETA ** (-jnp.arange(half, dtype=F32) / half)
    ang = pos.astype(F32)[:, None] * inv[None, :]
    cos, sin = jnp.cos(ang), jnp.sin(ang)
    t = pos.shape[0]
    one, zero = jnp.ones((t, 64), F32), jnp.zeros((t, 64), F32)
    z16, z32 = jnp.zeros((t, 16), F32), jnp.zeros((t, 32), F32)
    return (jnp.concatenate([one, cos, cos, jnp.ones((t, 32), F32)], axis=1),
            jnp.concatenate([zero, -sin, z16, z32], axis=1),
            jnp.concatenate([zero, z16, sin, z32], axis=1))


def kernel(x_prompt, mem_prompt, x_sample, cache_mla_ckv, cache_mla_kr, cache_mem_k, cache_mem_v, state_gdn, state_conv,
           page_table, norm1_g, w_in, gm_norm_g, gm_ws, gm_b, mla_cq_g, mla_w_uq, mla_qn_g, mla_qr_g, mla_ckv_g, mla_kr_g,
           mla_w_uk, mla_kn_g, mla_w_uv, gdn_conv_w, gdn_a_log, gdn_dt_bias, gdn_out_g, mem_norm_g, mem_w_kv, mem_qn_g,
           mem_kn_g, w_branch, w_out, norm2_g, moe_wg, moe_bg, moe_we, moe_be, moe_w_gate, moe_w_up, moe_w_down):
    w = dict(norm1_g=norm1_g, w_in=w_in, gm_norm_g=gm_norm_g, gm_ws=gm_ws, gm_b=gm_b, mla_cq_g=mla_cq_g, mla_w_uq=mla_w_uq,
             mla_qn_g=mla_qn_g, mla_qr_g=mla_qr_g, mla_ckv_g=mla_ckv_g, mla_kr_g=mla_kr_g, mla_w_uk=mla_w_uk, mla_kn_g=mla_kn_g,
             mla_w_uv=mla_w_uv, gdn_conv_w=gdn_conv_w, gdn_a_log=gdn_a_log, gdn_dt_bias=gdn_dt_bias, gdn_out_g=gdn_out_g,
             mem_norm_g=mem_norm_g, mem_w_kv=mem_w_kv, mem_qn_g=mem_qn_g, mem_kn_g=mem_kn_g, w_branch=w_branch, w_out=w_out,
             norm2_g=norm2_g, moe_wg=moe_wg, moe_bg=moe_bg, moe_we=moe_we, moe_be=moe_be)
    depth = w_in.shape[0]
    bp, tp, _ = x_prompt.shape
    bs = x_sample.shape[0]
    mt = mem_prompt.shape[1]
    n_pages = page_table.shape[1]
    past_len = n_pages * cache_mla_ckv.shape[2]
    mp = bp * tp

    tm_p = min(512, mp)
    tq = min(256, tp)
    ta = min(512, tp)
    tg = min(256, tp)
    pp = min(64, n_pages)
    cache_krt = jnp.swapaxes(cache_mla_kr, 2, 3)
    tabs_p = _rope_tables(jnp.arange(tp, dtype=jnp.int32))
    tabs_s = _rope_tables(jnp.full((bs,), past_len, jnp.int32))

    xp = x_prompt.reshape(mp, D_MODEL)
    xs = x_sample.reshape(bs, D_MODEL)
    mem = mem_prompt.reshape(bp * mt, D_MODEL)
    cache_k = cache_mem_k.transpose(0, 1, 3, 4, 2).reshape(depth, bs, BW, mt)
    cache_v = cache_mem_v.transpose(0, 1, 3, 4, 2).reshape(depth, bs, BW, mt)
    state_t = state_gdn.transpose(0, 2, 3, 4, 1)
    rows_p, rows_s = [], []
    for l in range(depth):
        p = _prep_layer(l, w)
        mk, mv = _mem_kv(mem, p["mem_norm_g"], p["mem_w_kv"], p["mem_kn_gx"], min(512, bp * mt))
        tm_f = min(tm_p, tp)
        a_out, q4, k4, ckv, kr, ckvt, zc, zm, zba, bat = _front(xp, tabs_p, tp // tm_f, p, tm_f, ta)
        b_out = _mla_attn(q4, k4, ckvt, p["w_uv"], bp, tp, ta)
        c_out, sfin = _gdn_prompt(zc, zba, bat, bp, tp, p, tg)
        m_out = _mem_attn(zm, mk, mv, p["mem_qn_gx"], bp, tp, mt, tq)
        group_p = list(_merge(xp, (a_out, b_out, c_out, m_out), p, tm_p)) + [tm_p]
        s_p =jnp.stack([sfin[:, 64 * h:64 * h + 64, 64 * h:64 * h + 64] for h in range(HEADS)], axis=1).transpose(0, 1, 3, 2)
        conv_p = zc.reshape(bp, tp, 1024)[:, tp - 3:, 0:QKV_DIM]
        rows_p.append((ckv.reshape(bp, tp, 128), kr.reshape(bp, tp, MLA_ROPE), s_p, conv_p,
                       mk.reshape(bp, mt, HEADS, HD), mv.reshape(bp, mt, HEADS, HD)))
        za, zb, zc, zm, zba, _ = _in_proj(xs, p["norm1_g"], p["w_small"], p["w_bat"], bs)
        a_s, v_s, conv_s, gq, gk, gv, zg, beta, gdec, mqn = _sample_tok(za, zc, zba, zm, state_conv[l].reshape(bs, 3 * QKV_DIM), p)
        _, _, ckv_s, kr_s, _, q32 = _mla_pre(zb, bs, tabs_s, 1, p, bs, bs)
        q3 = q32.reshape(bs, HEADS, 128)
        qk = q3[:, :, 0:64].reshape(bs, 1, BW)
        qr8 = jnp.pad(q3[:, :, 64:96], ((0, 0), (0, 4), (0, 0)))
        b_s = _mla_decode(l, page_table, cache_mla_ckv, cache_krt, qk, qr8, ckv_s.reshape(bs, 1, 128),
                          kr_s.reshape(bs, 1, MLA_ROPE), p["kn_gx"], p["w_uk_c"], p["w_uv_c"], pp).reshape(bs, BW)
        hv = lambda a: a.reshape(HEADS, HD, bs)
        s_new, c_t = _gdn_step(l, state_t, hv(gq), hv(gk), hv(gv), beta[0:4].reshape(HEADS, 1, bs),
                               gdec[4:8].reshape(HEADS, 1, bs), hv(zg), p["gdn_out_gc"])
        m_s = _mem_attn_s(l, mqn, cache_k, cache_v, min(8, bs))
        group_s = list(_merge(xs, (a_s, b_s, c_t.reshape(BW, bs).T, m_s), p, bs)) + [bs]
        xp, xs = _moe(l, [group_p, group_s], moe_w_gate, moe_w_up, moe_w_down, min(256, mp))
        rows_s.append((ckv_s.reshape(bs, 1, 128), kr_s.reshape(bs, 1, MLA_ROPE), s_new.transpose(3, 0, 1, 2),
                       conv_s.reshape(bs, 3, QKV_DIM), v_s.reshape(bs, 1, BW)))
    p_out = [jnp.stack(a) for a in zip(*rows_p)]
    s_out = [jnp.stack(a) for a in zip(*rows_s)]
    return (xp.reshape(bp, tp, D_MODEL), xs.reshape(bs, 1, D_MODEL), *p_out, *s_out)
```

```python
import functools

import numpy as np
import jax
import jax.numpy as jnp
from jax import lax
from jax.experimental import pallas as pl
from jax.experimental.pallas import tpu as pltpu

F32 = jnp.float32
BF16 = jnp.bfloat16
EPS = 1e-6
NEG = float("-inf")

D_MODEL = 1024
HEADS = 4
HD = 64
BW = 256
MLA_ROPE = 32
MLA_KV_RANK = 128
MLA_SCALE = 96.0 ** -0.5
ROPE_THETA = 10000.0
GM_CHUNK = 128
GDN_CHUNK = 64
QKV_DIM = 768
N_GROUPS = 4
EPG = 8
N_EXPERTS = 32
D_EXPERT = 256
PAGE = 128
VMEM_LIMIT = 56 * 1024 * 1024

NT = (((1,), (1,)), ((), ()))
TN = (((0,), (0,)), ((), ()))


def _cp(*sem):
    return pltpu.CompilerParams(dimension_semantics=sem, vmem_limit_bytes=VMEM_LIMIT)


def _rms(x, g):
    ms = jnp.sum(x * x, axis=-1, keepdims=True) * (1.0 / x.shape[-1])
    return x * lax.rsqrt(ms + EPS) * g


def _bdot(a, b):
    return jnp.dot(a.astype(BF16), b.astype(BF16), preferred_element_type=F32)


def _bdot_nt(a, b):
    return lax.dot_general(a.astype(BF16), b.astype(BF16), NT, preferred_element_type=F32)


def _split2(a):
    hi = a.astype(BF16)
    return hi, (a - hi.astype(F32)).astype(BF16)


def _dot3(a, b):
    d = lambda x, y: jnp.dot(x, y, preferred_element_type=F32)
    return d(a[0], b[0]) + (d(a[0], b[1]) + d(a[1], b[0]))


def _sel_dot(w01, x, left):
    x0 = x.astype(BF16)
    r1 = x - x0.astype(F32)
    x1 = r1.astype(BF16)
    x2 = (r1 - x1.astype(F32)).astype(BF16)
    d = (lambda p: jnp.dot(w01, p, preferred_element_type=F32)) if left else (lambda p: jnp.dot(p, w01, preferred_element_type=F32))
    return d(x0) + (d(x1) + d(x2))


def _silu(x):
    return x * jax.nn.sigmoid(x)


def _gelu(x):
    return 0.5 * x * (1.0 + lax.erf(x * 0.7071067811865476))


def _lane_head(width=BW):
    return lax.broadcasted_iota(jnp.int32, (1, width), 1) // HD


def _full(shape):
    n = len(shape)
    return pl.BlockSpec(shape, lambda *_: (0,) * n)


def _block_mean(width, segs):
    m = np.zeros((width, width), np.float32)
    for a, b in segs:
        m[a:b, a:b] = 1.0 / (b - a)
    return m


_BMEAN64 = _block_mean(BW, [(64 * h, 64 * h + 64) for h in range(HEADS)])
_BONES64 = _BMEAN64 * 64.0
_BQ = _block_mean(512, [(128 * h, 128 * h + 64) for h in range(HEADS)] + [(128 * h + 64, 128 * h + 96) for h in range(HEADS)])
_BK = _block_mean(512, [(128 * h, 128 * h + 64) for h in range(HEADS)])
_IND8 = np.zeros((8, BW), np.float32)
for _h in range(HEADS):
    _IND8[_h, 64 * _h:64 * _h + 64] = 1.0
_EXPB = np.zeros((128, BW), np.float32)
_EXPG = np.zeros((128, BW), np.float32)
for _h in range(HEADS):
    _EXPB[_h, 64 * _h:64 * _h + 64] = 1.0
    _EXPG[4 + _h, 64 * _h:64 * _h + 64] = 1.0
_BDMASK = (_BONES64 > 0).astype(np.float32)


def _in_proj_kernel(x_ref, g_ref, w_ref, wbat_ref, oa_ref, ob_ref, oc_ref, om_ref, oba_ref, obat_ref):
    hb = _rms(x_ref[...], g_ref[...]).astype(BF16)
    oa_ref[...] = jnp.dot(hb, w_ref[:, 0:512], preferred_element_type=F32)
    ob_ref[...] = jnp.dot(hb, w_ref[:, 512:1024], preferred_element_type=F32)
    oc_ref[...] = jnp.dot(hb, w_ref[:, 1024:2048], preferred_element_type=F32)
    om_ref[...] = jnp.dot(hb, w_ref[:, 2048:2304], preferred_element_type=F32)
    oba_ref[...] = jnp.dot(hb, w_ref[:, 2304:2432], preferred_element_type=F32)
    obat_ref[...] = lax.dot_general(wbat_ref[...], hb, NT, preferred_element_type=F32)


def _in_proj(x, g, w, wbat, tm):
    m = x.shape[0]
    widths = (512, 512, 1024, 256, 128)
    return pl.pallas_call(
        _in_proj_kernel,
        grid=(m // tm,),
        in_specs=[pl.BlockSpec((tm, D_MODEL), lambda i: (i, 0)), _full((1, D_MODEL)), _full(w.shape), _full(wbat.shape)],
        out_specs=[pl.BlockSpec((tm, n), lambda i: (i, 0)) for n in widths] + [pl.BlockSpec((8, tm), lambda i: (0, i))],
        out_shape=[jax.ShapeDtypeStruct((m, n), F32) for n in widths] + [jax.ShapeDtypeStruct((8, m), F32)],
        compiler_params=_cp("parallel"),
        name="in_proj",
    )(x, g, w, wbat)


def _gmlp_body(za, g_ref, ws_ref, b_ref, o_ref):
    row = lax.broadcasted_iota(jnp.int32, (GM_CHUNK, GM_CHUNK), 0)
    col = lax.broadcasted_iota(jnp.int32, (GM_CHUNK, GM_CHUNK), 1)
    tril = col <= row
    lh = _lane_head()
    wts = [jnp.where(tril, ws_ref[g], 0.0).astype(BF16) for g in range(HEADS)]
    for c in range(za.shape[0] // GM_CHUNK):
        sl = slice(c * GM_CHUNK, (c + 1) * GM_CHUNK)
        ge = _gelu(za[sl, :])
        u = ge[:, :BW]
        vb = _rms(ge[:, BW:], g_ref[...]).astype(BF16)
        s = b_ref[...]
        for g in range(HEADS):
            s = s + jnp.where(lh == g, jnp.dot(wts[g], vb, preferred_element_type=F32), 0.0)
        o_ref[sl, :] = u * s


def _mla_pre_body(z, c_ref, s1_ref, s2_ref, gcq_ref, wuq_ref, qg_ref, bq_ref, gckv_ref, gkr_ref, wuk_ref, kg_ref, bk_ref,
                  oq_ref, ok_ref, ockv_ref, okr_ref, ockvt_ref, oq32_ref, tb):
    cs, s1, s2 = c_ref[...], s1_ref[...], s2_ref[...]

    def rope(x):
        return x * cs + pltpu.roll(x, 112, 1) * s1 + pltpu.roll(x, 16, 1) * s2

    cq = _rms(z[:, 0:256], gcq_ref[...])
    q = _bdot(cq, wuq_ref[...])
    qn = q * lax.rsqrt(_bdot(q * q, bq_ref[...]) + EPS) * qg_ref[...]
    ckv = _rms(z[:, 256:384], gckv_ref[...])
    ockv_ref[...] = ckv
    for c in range(ckv.shape[0] // tb):
        ockvt_ref[c] = ckv[c * tb:(c + 1) * tb, :].T.astype(BF16)
    krb = z[:, 384:512]
    kr = rope(krb * lax.rsqrt(jnp.sum(krb * krb, axis=-1, keepdims=True) * (1.0 / MLA_ROPE) + EPS) * gkr_ref[...])
    okr_ref[...] = kr[:, 64:96]
    k = _bdot(ckv, wuk_ref[...])
    kn = k * lax.rsqrt(_bdot(k * k, bk_ref[...]) + EPS) * kg_ref[...]
    for h in range(HEADS):
        sl = slice(128 * h, 128 * h + 128)
        qh = rope(qn[:, sl])
        oq_ref[h] = qh.astype(BF16)
        if oq32_ref is not None:
            oq32_ref[:, sl] = qh
        ok_ref[h] = (kn[:, sl] + kr).astype(BF16)


def _mla_pre_kernel(z_ref, *refs, tb):
    _mla_pre_body(z_ref[...], *refs, tb)


def _mla_consts(p):
    return [p["mla_cq_g"], p["w_uq"], p["qg"], jnp.asarray(_BQ, BF16), p["mla_ckv_g"], p["kr_g"], p["w_uk"], p["kg"],
            jnp.asarray(_BK, BF16)]


def _front_kernel(x_ref, g_ref, w_ref, wbat_ref, gmg_ref, gmws_ref, gmb_ref, *refs, tb):
    mla_in, (oa_ref, oq_ref, ok_ref, ockv_ref, okr_ref, ockvt_ref, oc_ref, om_ref, oba_ref, obat_ref) = refs[:12], refs[12:]
    hb = _rms(x_ref[...], g_ref[...]).astype(BF16)
    _gmlp_body(jnp.dot(hb, w_ref[:, 0:512], preferred_element_type=F32), gmg_ref, gmws_ref, gmb_ref, oa_ref)
    _mla_pre_body(jnp.dot(hb, w_ref[:, 512:1024], preferred_element_type=F32), *mla_in,
                  oq_ref, ok_ref, ockv_ref, okr_ref, ockvt_ref, None, tb)
    oc_ref[...] = jnp.dot(hb, w_ref[:, 1024:2048], preferred_element_type=F32)
    om_ref[...] = jnp.dot(hb, w_ref[:, 2048:2304], preferred_element_type=F32)
    oba_ref[...] = jnp.dot(hb, w_ref[:, 2304:2432], preferred_element_type=F32)
    obat_ref[...] = lax.dot_general(wbat_ref[...], hb, NT, preferred_element_type=F32)


def _front(x, tabs, t_blocks, p, tm, tb):
    m = x.shape[0]
    tab_spec = pl.BlockSpec((tm, 128), lambda i: (i % t_blocks, 0))
    consts = [p["norm1_g"], p["w_small"], p["w_bat"], p["gm_norm_g"], p["gm_ws"], p["gm_bfull"]]
    mla = _mla_consts(p)
    tile = lambda w, dt=F32: (pl.BlockSpec((tm, w), lambda i: (i, 0)), jax.ShapeDtypeStruct((m, w), dt))
    head = (pl.BlockSpec((HEADS, tm, 128), lambda i: (0, i, 0)), jax.ShapeDtypeStruct((HEADS, m, 128), BF16))
    outs = [tile(BW), head, head, tile(128), tile(MLA_ROPE),
            (pl.BlockSpec((tm // tb, 128, tb), lambda i: (i, 0, 0)), jax.ShapeDtypeStruct((m // tb, 128, tb), BF16)),
            tile(1024), tile(BW), tile(128), (pl.BlockSpec((8, tm), lambda i: (0, i)), jax.ShapeDtypeStruct((8, m), F32))]
    return pl.pallas_call(
        functools.partial(_front_kernel, tb=tb),
        grid=(m // tm,),
        in_specs=[pl.BlockSpec((tm, D_MODEL), lambda i: (i, 0))] + [_full(c.shape) for c in consts] + [tab_spec] * 3
        + [_full(c.shape) for c in mla],
        out_specs=[o[0] for o in outs],
        out_shape=[o[1] for o in outs],
        compiler_params=_cp("parallel"),
        name="front",
    )(x, *consts, *tabs, *mla)


def _mla_pre(zb, m, tabs, t_blocks, p, tm, tb):
    cs, s1, s2 = tabs
    tab_spec = pl.BlockSpec((tm, 128), lambda i: (i % t_blocks, 0))
    consts = _mla_consts(p)
    return pl.pallas_call(
        functools.partial(_mla_pre_kernel, tb=tb),
        grid=(m // tm,),
        in_specs=[pl.BlockSpec((tm, 512), lambda i: (i, 0)), tab_spec, tab_spec, tab_spec] + [_full(c.shape) for c in consts],
        out_specs=[pl.BlockSpec((HEADS, tm, 128), lambda i: (0, i, 0)), pl.BlockSpec((HEADS, tm, 128), lambda i: (0, i, 0)),
                   pl.BlockSpec((tm, 128), lambda i: (i, 0)), pl.BlockSpec((tm, MLA_ROPE), lambda i: (i, 0)),
                   pl.BlockSpec((tm // tb, 128, tb), lambda i: (i, 0, 0)), pl.BlockSpec((tm, 512), lambda i: (i, 0))],
        out_shape=[jax.ShapeDtypeStruct((HEADS, m, 128), BF16), jax.ShapeDtypeStruct((HEADS, m, 128), BF16),
                   jax.ShapeDtypeStruct((m, 128), F32), jax.ShapeDtypeStruct((m, MLA_ROPE), F32),
                   jax.ShapeDtypeStruct((m // tb, 128, tb), BF16), jax.ShapeDtypeStruct((m, 512), F32)],
        compiler_params=_cp("parallel"),
        name="mla_pre",
    )(zb, cs, s1, s2, *consts)


def _mla_attn_kernel(q_ref, k_ref, v_ref, wuv_ref, o_ref, *, tq):
    i = pl.program_id(1)
    row = lax.broadcasted_iota(jnp.int32, (tq, tq), 0)
    col = lax.broadcasted_iota(jnp.int32, (tq, tq), 1)
    causal = row <= col

    def step(j, carry, mask):
        off = pl.multiple_of(j * tq, tq)
        vt = v_ref[j]
        ss = [lax.dot_general(k_ref[h, pl.ds(off, tq), :], q_ref[h], NT, preferred_element_type=F32) for h in range(HEADS)]
        stats = []
        for h in range(HEADS):
            m, l, _ = carry[h]
            s = ss[h] * MLA_SCALE
            if mask:
                s = jnp.where(causal, s, NEG)
            mn = jnp.maximum(m, jnp.max(s, axis=0, keepdims=True))
            pr = jnp.exp(s - mn)
            al = jnp.exp(m - mn)
            stats.append((mn, al * l + jnp.sum(pr, axis=0, keepdims=True), al, pr.astype(BF16)))
        pvs = [jnp.dot(vt, stats[h][3], preferred_element_type=F32) for h in range(HEADS)]
        return tuple((stats[h][0], stats[h][1], stats[h][2] * carry[h][2] + pvs[h]) for h in range(HEADS))

    init = tuple((jnp.full((1, tq), NEG, F32), jnp.zeros((1, tq), F32), jnp.zeros((MLA_KV_RANK, tq), F32)) for _ in range(HEADS))
    carry = lax.fori_loop(0, i, lambda j, c: step(j, c, False), init)
    carry = step(i, carry, True)
    out = jnp.zeros((tq, BW), F32)
    for h in range(HEADS):
        m, l, acc = carry[h]
        out = out + lax.dot_general((acc / l).astype(BF16), wuv_ref[h], TN, preferred_element_type=F32)
    o_ref[...] = out


def _mla_attn(q4, k4, ckvt, wuv, n, t, tq):
    nq = t // tq
    return pl.pallas_call(
        functools.partial(_mla_attn_kernel, tq=tq),
        grid=(n, nq),
        in_specs=[pl.BlockSpec((HEADS, tq, 128), lambda b, i: (0, b * nq + i, 0)),
                  pl.BlockSpec((HEADS, t, 128), lambda b, i: (0, b, 0)),
                  pl.BlockSpec((nq, MLA_KV_RANK, tq), lambda b, i: (b, 0, 0)), _full(wuv.shape)],
        out_specs=pl.BlockSpec((tq, BW), lambda b, i: (b * nq + i, 0)),
        out_shape=jax.ShapeDtypeStruct((n * t, BW), F32),
        compiler_params=_cp("parallel", "arbitrary"),
        name="mla_attn",
    )(q4, k4, ckvt, wuv)


def _mla_decode_kernel(pt_ref, qk_ref, qr_ref, cnew_ref, krnew_ref, kng_ref, wukt_ref, wuv_ref, ind_ref, ckv_hbm, krt_hbm, o_ref,
                       cbuf, kbuf, sem, *, layer, npages, pp):
    n = pl.program_id(0)
    ngroups = npages // pp

    def page_copies(sample, grp, slot):
        base = sample * npages + grp * pp
        out = []
        for i in range(pp):
            page = pt_ref[base + i]
            out.append(pltpu.make_async_copy(ckv_hbm.at[layer, page], cbuf.at[slot, pl.ds(i * PAGE, PAGE)], sem.at[slot]))
            out.append(pltpu.make_async_copy(krt_hbm.at[layer, page], kbuf.at[slot, :, pl.ds(i * PAGE, PAGE)], sem.at[slot]))
        return out

    @pl.when(n == 0)
    def _():
        for cp in page_copies(0, 0, 0):
            cp.start()

    ind = ind_ref[...]
    qbd = (ind * (qk_ref[...] * kng_ref[...])).astype(BF16)
    qr = qr_ref[...].astype(BF16)
    wukt = wukt_ref[...]
    qabs = jnp.dot(qbd, wukt, preferred_element_type=F32).astype(BF16)
    w2 = jnp.concatenate([wukt, qabs, jnp.zeros((8, MLA_KV_RANK), BF16)], axis=0)

    def head_ms(kk):
        rows = [jnp.sum(kk[HD * h:HD * (h + 1)], axis=0, keepdims=True) for h in range(HEADS)]
        return jnp.concatenate(rows + [jnp.ones((8 - HEADS, kk.shape[1]), F32)], axis=0) * (1.0 / HD)

    m = jnp.full((8, 1), NEG, F32)
    l = jnp.zeros((8, 1), F32)
    acc = jnp.zeros((8, MLA_KV_RANK), F32)
    for grp in range(ngroups):
        slot = (n * ngroups + grp) % 2
        if grp + 1 < ngroups:
            for cp in page_copies(n, grp + 1, 1 - slot):
                cp.start()
        else:
            @pl.when(n + 1 < pl.num_programs(0))
            def _():
                for cp in page_copies(n + 1, 0, 1 - slot):
                    cp.start()
        for cp in page_copies(n, grp, slot):
            cp.wait()
        cb = cbuf[slot].astype(BF16)
        krt = kbuf[slot].astype(BF16)
        kq = lax.dot_general(w2, cb, NT, preferred_element_type=F32)
        kt = kq[0:BW]
        num = kq[BW:BW + 8]
        s = (num * lax.rsqrt(head_ms(kt * kt) + EPS) + jnp.dot(qr, krt, preferred_element_type=F32)) * MLA_SCALE
        mn = jnp.maximum(m, jnp.max(s, axis=-1, keepdims=True))
        pb = jnp.exp(s - mn)
        al = jnp.exp(m - mn)
        l = al * l + jnp.sum(pb, axis=-1, keepdims=True)
        acc = al * acc + jnp.dot(pb.astype(BF16), cb, preferred_element_type=F32)
        m = mn

    cb = jnp.broadcast_to(cnew_ref[...], (8, MLA_KV_RANK)).astype(BF16)
    k1 = lax.dot_general(cb, wukt, NT, preferred_element_type=F32)
    krn = krnew_ref[...].astype(BF16).astype(F32)
    num1 = jnp.sum(qabs.astype(F32) * cb.astype(F32), axis=-1, keepdims=True)
    ms1 = jnp.sum(ind * (k1 * k1), axis=-1, keepdims=True) * (1.0 / HD)
    s1 = (num1 * lax.rsqrt(ms1 + EPS) + jnp.sum(qr.astype(F32) * krn, axis=-1, keepdims=True)) * MLA_SCALE
    mn1 = jnp.maximum(m, s1)
    p1 = jnp.exp(s1 - mn1)
    al1 = jnp.exp(m - mn1)
    lat = (al1 * acc + p1 * cb.astype(F32)) / (al1 * l + p1)
    o8 = jnp.dot(lat.astype(BF16), wuv_ref[...], preferred_element_type=F32)
    o_ref[...] = jnp.sum(o8 * ind, axis=0, keepdims=True)


def _mla_decode(layer, page_table, cache_ckv, cache_krt, qk, qr8, cnew, krnew, kng, wuk, wuv, pp):
    ns, npages = page_table.shape
    pt = page_table.reshape(-1)

    def per_sample(shape):
        return pl.BlockSpec((None,) + shape, lambda n, pt_ref: (n, 0, 0))

    def const(a):
        nd = a.ndim
        return pl.BlockSpec(a.shape, lambda n, pt_ref: (0,) * nd)

    ind = jnp.asarray(_IND8)
    hbm = pl.BlockSpec(memory_space=pl.ANY)
    grid_spec = pltpu.PrefetchScalarGridSpec(
        num_scalar_prefetch=1,
        grid=(ns,),
        in_specs=[per_sample((1, BW)), per_sample((8, MLA_ROPE)), per_sample((1, MLA_KV_RANK)), per_sample((1, MLA_ROPE)),
                  const(kng), const(wuk), const(wuv), const(ind), hbm, hbm],
        out_specs=per_sample((1, BW)),
        scratch_shapes=[pltpu.VMEM((2, pp * PAGE, MLA_KV_RANK), F32), pltpu.VMEM((2, MLA_ROPE, pp * PAGE), F32),
                        pltpu.SemaphoreType.DMA((2,))],
    )
    return pl.pallas_call(
        functools.partial(_mla_decode_kernel, layer=layer, npages=npages, pp=pp),
        grid_spec=grid_spec,
        out_shape=jax.ShapeDtypeStruct((ns, 1, BW), F32),
        compiler_params=_cp("arbitrary"),
        name="mla_decode",
    )(pt, qk, qr8, cnew, krnew, kng, wuk, wuv, ind, cache_ckv, cache_krt)


def _gdn_kernel(zc_ref, zba_ref, bat_ref, cw_ref, parr_ref, parc_ref, gout_ref, bones_ref, expb_ref, expg_ref, bdm_ref,
                o_ref, sfin_ref, xbuf, s_sc, *, tg):
    t = pl.program_id(1)
    c = GDN_CHUNK

    @pl.when(t == 0)
    def _():
        xbuf[0:8, :] = jnp.zeros((8, QKV_DIM), F32)
        s_sc[...] = jnp.zeros(s_sc.shape, F32)

    @pl.when(t > 0)
    def _():
        xbuf[5:8, :] = xbuf[tg + 5:tg + 8, :]

    xbuf[8:8 + tg, :] = zc_ref[:, 0:QKV_DIM]
    y = cw_ref[0:1, :] * xbuf[5:5 + tg, :]
    for i in range(1, 4):
        y = y + cw_ref[i:i + 1, :] * xbuf[5 + i:5 + i + tg, :]
    y = _silu(y)

    zba = zba_ref[...]
    beta_col = jax.nn.sigmoid(zba)
    g_col = -jnp.exp(parr_ref[0:1, :]) * jax.nn.softplus(zba + parr_ref[1:2, :])
    g_row = -jnp.exp(parc_ref[:, 0:1]) * jax.nn.softplus(bat_ref[...] + parc_ref[:, 1:2])

    row = lax.broadcasted_iota(jnp.int32, (c, c), 0)
    col = lax.broadcasted_iota(jnp.int32, (c, c), 1)
    incl = col <= row
    strict = col < row
    lt = incl.astype(BF16)
    ut = (col >= row).astype(BF16)
    bones = bones_ref[...]
    bdm = bdm_ref[...]
    expb, expg = expb_ref[...], expg_ref[...]
    brow = lax.broadcasted_iota(jnp.int32, (BW, BW), 0)
    bcl = lax.broadcasted_iota(jnp.int32, (BW, BW), 1)
    same_head = (brow // HD) == (bcl // HD)
    incl_bd = same_head & ((bcl % HD) <= (brow % HD))
    strict_bd = same_head & ((bcl % HD) < (brow % HD))
    eye_bd = (brow == bcl).astype(F32)

    def same_block(b):
        return (brow // b) == (bcl // b)
    same_head2 = jnp.concatenate([same_head, same_head], axis=1)

    def stack4(a):
        return jnp.concatenate([a, a, a, a], axis=0)

    def fold4(a):
        return (a[0:c] + a[c:2 * c]) + (a[2 * c:3 * c] + a[3 * c:4 * c])

    nchunk = tg // c
    pre = []
    for ci in range(nchunk):
        sl = slice(ci * c, (ci + 1) * c)
        q, k, v = y[sl, 0:256], y[sl, 256:512], y[sl, 512:768]
        qn = q * lax.rsqrt(_bdot(q * q, bones) + EPS) * (HD ** -0.5)
        kn = k * lax.rsqrt(_bdot(k * k, bones) + EPS)
        bcol = beta_col[sl, :]
        gcum_c = _sel_dot(lt, g_col[sl, :], True)
        gcum_r = _sel_dot(ut, g_row[:, sl], False)
        gx = _sel_dot(expg, gcum_c, False)
        bx = _sel_dot(expb, bcol, False)
        egx = jnp.exp(gx)
        rhs = _split2(jnp.concatenate([bx * v, bx * egx * kn], axis=1))
        ks = jnp.where(same_head, stack4(kn), 0.0).astype(BF16)
        qs = jnp.where(same_head, stack4(qn), 0.0).astype(BF16)
        gc_s = jnp.concatenate([gcum_c[:, 4 + h:5 + h] for h in range(HEADS)], axis=0)
        gr_s = jnp.concatenate([gcum_r[4 + h:5 + h, :] for h in range(HEADS)], axis=1)
        beta_s = jnp.concatenate([bcol[:, h:h + 1] for h in range(HEADS)], axis=0)
        dm = jnp.exp(jnp.where(incl_bd, gc_s - gr_s, NEG))
        a = jnp.where(strict_bd, beta_s * lax.dot_general(ks, ks, NT, preferred_element_type=F32) * dm, 0.0)
        qk = (lax.dot_general(qs, ks, NT, preferred_element_type=F32) * dm).astype(BF16)
        pre.append(dict(sl=sl, qn=qn, kn=kn, gx=gx, egx=egx, rhs=rhs, qk=qk, a=_split2(a),
                        tinv=eye_bd - jnp.where(same_block(2), a, 0.0)))

    b = 2
    while b < c:
        off = same_block(2 * b) & jnp.logical_not(same_block(b))
        for d in pre:
            d["ts"] = _split2(d["tinv"])
            d["w"] = _split2(_dot3((jnp.where(off, d["a"][0], 0.0), jnp.where(off, d["a"][1], 0.0)), d["ts"]))
        for d in pre:
            d["tinv"] = d["tinv"] - _dot3(d["ts"], d["w"])
        b *= 2
    for d in pre:
        rhs = d["rhs"]
        d["x"] = fold4(jnp.where(same_head2, _dot3(_split2(d["tinv"]), (stack4(rhs[0]), stack4(rhs[1]))), 0.0))

    for d in pre:
        sl, qn, kn, gx, egx, qk, x = d["sl"], d["qn"], d["kn"], d["gx"], d["egx"], d["qk"], d["x"]
        s = s_sc[...]
        sb = s.astype(BF16)
        u = x[:, :BW] - jnp.dot(x[:, BW:].astype(BF16), sb, preferred_element_type=F32)
        ub = u.astype(BF16)
        o = egx * jnp.dot(qn.astype(BF16), sb, preferred_element_type=F32)
        o = o + fold4(jnp.where(same_head, jnp.dot(qk, stack4(ub), preferred_element_type=F32), 0.0))
        glast = gx[c - 1:c, :]
        kf = (kn * jnp.exp(glast - gx)).astype(BF16)
        s_new = jnp.exp(glast) * s + lax.dot_general(kf, ub, TN, preferred_element_type=F32)
        s_sc[...] = s_new * bdm
        on = o * lax.rsqrt(_bdot(o * o, bones) * (1.0 / HD) + EPS) * gout_ref[...]
        o_ref[sl, :] = on * _silu(zc_ref[sl, QKV_DIM:QKV_DIM + BW])

    @pl.when(t == pl.num_programs(1) - 1)
    def _():
        sfin_ref[...] = s_sc[...]


def _gdn_prompt(zc, zba, bat, n, t, p, tg):
    nt = t // tg
    consts = [p["conv_w"], p["gdn_par_r"], p["gdn_par_c"], p["gdn_out_gx"], jnp.asarray(_BONES64, BF16), jnp.asarray(_EXPB, BF16),
              jnp.asarray(_EXPG, BF16), jnp.asarray(_BDMASK)]
    return pl.pallas_call(
        functools.partial(_gdn_kernel, tg=tg),
        grid=(n, nt),
        in_specs=[pl.BlockSpec((tg, 1024), lambda b, i: (b * nt + i, 0)), pl.BlockSpec((tg, 128), lambda b, i: (b * nt + i, 0)),
                  pl.BlockSpec((8, tg), lambda b, i: (0, b * nt + i))] + [_full(c.shape) for c in consts],
        out_specs=[pl.BlockSpec((tg, BW), lambda b, i: (b * nt + i, 0)), pl.BlockSpec((None, BW, BW), lambda b, i: (b, 0, 0))],
        out_shape=[jax.ShapeDtypeStruct((n * t, BW), F32), jax.ShapeDtypeStruct((n, BW, BW), F32)],
        scratch_shapes=[pltpu.VMEM((8 + tg, QKV_DIM), F32), pltpu.VMEM((BW, BW), F32)],
        compiler_params=_cp("parallel", "arbitrary"),
        name="gdn_prompt",
    )(zc, zba, bat, *consts)


def _sample_tok_kernel(za_ref, zc_ref, zba_ref, zm_ref, sconv_ref, cw_ref, gmg_ref, gmw_ref, gmb_ref, parr_ref, bones_ref,
                       memg_ref, bmean_ref, oa_ref, ov_ref, oconv_ref, oq_ref, ok_ref, ovv_ref, ozg_ref, obeta_ref, og_ref, omq_ref):
    ge = _gelu(za_ref[...])
    v = _rms(ge[:, BW:], gmg_ref[...])
    ov_ref[...] = v
    oa_ref[...] = ge[:, :BW] * (gmw_ref[...] * v + gmb_ref[...])
    sc = sconv_ref[...]
    x = zc_ref[:, 0:QKV_DIM]
    y = (cw_ref[0:1, :] * sc[:, 0:768] + cw_ref[1:2, :] * sc[:, 768:1536] + cw_ref[2:3, :] * sc[:, 1536:2304]
         + cw_ref[3:4, :] * x)
    oconv_ref[:, 0:1536] = sc[:, 768:2304]
    oconv_ref[:, 1536:2304] = x
    y = _silu(y)
    q, k = y[:, 0:256], y[:, 256:512]
    bones = bones_ref[...]
    oq_ref[...] = (q * lax.rsqrt(_bdot(q * q, bones) + EPS) * (HD ** -0.5)).T
    ok_ref[...] = (k * lax.rsqrt(_bdot(k * k, bones) + EPS)).T
    ovv_ref[...] = y[:, 512:768].T
    ozg_ref[...] = zc_ref[:, QKV_DIM:QKV_DIM + BW].T
    zba = zba_ref[...]
    obeta_ref[...] = jax.nn.sigmoid(zba).T
    og_ref[...] = (-jnp.exp(parr_ref[0:1, :]) * jax.nn.softplus(zba + parr_ref[1:2, :])).T
    mq = zm_ref[...]
    omq_ref[...] = mq * lax.rsqrt(_bdot(mq * mq, bmean_ref[...]) + EPS) * memg_ref[...]


def _sample_tok(za, zc, zba, zm, sconv, p):
    ns = za.shape[0]
    args = [za, zc, zba, zm, sconv, p["conv_w"], p["gm_norm_g"], p["gm_w0"], p["gm_b0"], p["gdn_par_r"],
            jnp.asarray(_BONES64, BF16), p["mem_qn_gx"], jnp.asarray(_BMEAN64, BF16)]
    shapes = [(ns, BW), (ns, BW), (ns, 2304), (BW, ns), (BW, ns), (BW, ns), (BW, ns), (128, ns), (128, ns), (ns, BW)]
    return pl.pallas_call(
        _sample_tok_kernel,
        in_specs=[_full(a.shape) for a in args],
        out_specs=[_full(s) for s in shapes],
        out_shape=[jax.ShapeDtypeStruct(s, F32) for s in shapes],
        grid=(1,),
        compiler_params=_cp("arbitrary"),
        name="sample_tok",
    )(*args)


def _gdn_step_kernel(s_ref, q_ref, k_ref, v_ref, beta_ref, g_ref, zg_ref, gout_ref, so_ref, o_ref, o_sc):
    q, k = q_ref[...], k_ref[...]
    eg = jnp.exp(g_ref[...])
    beta = beta_ref[...]
    qk = jnp.sum(q * k, axis=0, keepdims=True)
    ssq = jnp.zeros(qk.shape, F32)
    for v in range(HD):
        sv = s_ref[v]
        sk = jnp.sum(sv * k, axis=0, keepdims=True)
        sq = jnp.sum(sv * q, axis=0, keepdims=True)
        u = beta * (v_ref[v:v + 1, :] - eg * sk)
        o = eg * sq + qk * u
        so_ref[v] = eg * sv + u * k
        o_sc[v:v + 1, :] = o
        ssq = ssq + o * o
    o_ref[...] = o_sc[...] * lax.rsqrt(ssq * (1.0 / HD) + EPS) * gout_ref[...] * _silu(zg_ref[...])


def _gdn_step(layer, state_t, q, k, v, beta, g, zg, gout_col):
    ns = state_t.shape[-1]
    vec = pl.BlockSpec((None, HD, ns), lambda h: (h, 0, 0))
    sca = pl.BlockSpec((None, 1, ns), lambda h: (h, 0, 0))
    return pl.pallas_call(
        _gdn_step_kernel,
        grid=(HEADS,),
        in_specs=[pl.BlockSpec((None, None, HD, HD, ns), lambda h: (layer, h, 0, 0, 0)), vec, vec, vec, sca, sca, vec, _full((HD, 1))],
        out_specs=[pl.BlockSpec((None, HD, HD, ns), lambda h: (h, 0, 0, 0)), vec],
        out_shape=[jax.ShapeDtypeStruct((HEADS, HD, HD, ns), F32), jax.ShapeDtypeStruct((HEADS, HD, ns), F32)],
        scratch_shapes=[pltpu.VMEM((HD, ns), F32)],
        compiler_params=_cp("parallel"),
        name="gdn_step",
    )(state_t, q, k, v, beta, g, zg, gout_col)


def _mem_kv_kernel(x_ref, g_ref, w_ref, kg_ref, bmean_ref, ok_ref, ov_ref):
    kv = _bdot(_rms(x_ref[...], g_ref[...]), w_ref[...])
    k = kv[:, 0:BW]
    ok_ref[...] = k * lax.rsqrt(_bdot(k * k, bmean_ref[...]) + EPS) * kg_ref[...]
    ov_ref[...] = kv[:, BW:]


def _mem_kv(mem, g, w, kgx, tm):
    m = mem.shape[0]
    bmean = jnp.asarray(_BMEAN64, BF16)
    return pl.pallas_call(
        _mem_kv_kernel,
        grid=(m // tm,),
        in_specs=[pl.BlockSpec((tm, D_MODEL), lambda i: (i, 0)), _full((1, D_MODEL)), _full(w.shape), _full((1, BW)), _full((BW, BW))],
        out_specs=[pl.BlockSpec((tm, BW), lambda i: (i, 0))] * 2,
        out_shape=[jax.ShapeDtypeStruct((m, BW), F32)] * 2,
        compiler_params=_cp("parallel"),
        name="mem_kv",
    )(mem, g, w, kgx, bmean)


def _mem_attn_kernel(q_ref, k_ref, v_ref, gq_ref, bmean_ref, o_ref):
    q = q_ref[...]
    qn = q * lax.rsqrt(_bdot(q * q, bmean_ref[...]) + EPS) * gq_ref[...]
    kb = k_ref[...].astype(BF16)
    vb = v_ref[...].astype(BF16)
    lh = _lane_head()
    out = jnp.zeros(q.shape, F32)
    for h in range(HEADS):
        mh = lh == h
        s = lax.dot_general(jnp.where(mh, qn, 0.0).astype(BF16), kb, NT, preferred_element_type=F32) * (HD ** -0.5)
        e = jnp.exp(s - jnp.max(s, axis=-1, keepdims=True))
        pr = e / jnp.sum(e, axis=-1, keepdims=True)
        out = out + jnp.where(mh, jnp.dot(pr.astype(BF16), vb, preferred_element_type=F32), 0.0)
    o_ref[...] = out


def _mem_attn(zm, mk, mv, gqx, n, t, mt, tq):
    nq = t // tq
    bmean = jnp.asarray(_BMEAN64, BF16)
    return pl.pallas_call(
        _mem_attn_kernel,
        grid=(n, nq),
        in_specs=[pl.BlockSpec((tq, BW), lambda b, i: (b * nq + i, 0)), pl.BlockSpec((mt, BW), lambda b, i: (b, 0)),
                  pl.BlockSpec((mt, BW), lambda b, i: (b, 0)), _full((1, BW)), _full((BW, BW))],
        out_specs=pl.BlockSpec((tq, BW), lambda b, i: (b * nq + i, 0)),
        out_shape=jax.ShapeDtypeStruct((n * t, BW), F32),
        compiler_params=_cp("parallel", "parallel"),
        name="mem_attn",
    )(zm, mk, mv, gqx, bmean)


def _mem_attn_s_kernel(q_ref, k_ref, v_ref, ind_ref, o_ref, *, bn):
    ind = ind_ref[...]
    for i in range(bn):
        qbd = (ind * q_ref[i:i + 1, :]).astype(BF16)
        s = jnp.dot(qbd, k_ref[i].astype(BF16), preferred_element_type=F32) * (HD ** -0.5)
        e = jnp.exp(s - jnp.max(s, axis=-1, keepdims=True))
        pr = e / jnp.sum(e, axis=-1, keepdims=True)
        o8 = lax.dot_general(pr.astype(BF16), v_ref[i].astype(BF16), NT, preferred_element_type=F32)
        o_ref[i:i + 1, :] = jnp.sum(o8 * ind, axis=0, keepdims=True)


def _mem_attn_s(layer, mqn, cache_k, cache_v, bn):
    ns = mqn.shape[0]
    mt = cache_k.shape[3]
    kv_spec = pl.BlockSpec((None, bn, BW, mt), lambda i: (layer, i, 0, 0))
    return pl.pallas_call(
        functools.partial(_mem_attn_s_kernel, bn=bn),
        grid=(ns // bn,),
        in_specs=[pl.BlockSpec((bn, BW), lambda i: (i, 0)), kv_spec, kv_spec, _full((8, BW))],
        out_specs=pl.BlockSpec((bn, BW), lambda i: (i, 0)),
        out_shape=jax.ShapeDtypeStruct((ns, BW), F32),
        compiler_params=_cp("parallel"),
        name="mem_attn_s",
    )(mqn, cache_k, cache_v, jnp.asarray(_IND8))


def _merge_kernel(x_ref, a_ref, b_ref, c_ref, m_ref, g1_ref, wg_ref, wb_ref, wo_ref, g2_ref, wr_ref, br_ref,
                  x1_ref, h2_ref, ei_ref, ew_ref):
    x = x_ref[...]
    hb = _rms(x, g1_ref[...]).astype(BF16)
    acc = jnp.zeros(x.shape, F32)
    for b, br in enumerate((a_ref, b_ref, c_ref, m_ref)):
        gate = jax.nn.sigmoid(jnp.dot(hb, wg_ref[:, b * D_MODEL:(b + 1) * D_MODEL], preferred_element_type=F32))
        acc = acc + gate * jnp.dot(br[...].astype(BF16), wb_ref[b], preferred_element_type=F32)
    x1 = x + jnp.dot(acc.astype(BF16), wo_ref[...], preferred_element_type=F32)
    x1_ref[...] = x1
    h2 = _rms(x1, g2_ref[...])
    h2_ref[...] = h2
    h2h, h2l = _split2(h2)
    r = jnp.dot(h2h, wr_ref[...], preferred_element_type=F32)
    logits = r[:, 0:128] + (r[:, 128:256] + jnp.dot(h2l, wr_ref[:, 0:128], preferred_element_type=F32)) + br_ref[...]
    lane = lax.broadcasted_iota(jnp.int32, (1, 128), 1).astype(F32)
    big = 1e9
    lg = jnp.where(lane < N_GROUPS, logits, NEG)
    mg = jnp.max(lg, axis=-1, keepdims=True)
    g_w = 1.0 / jnp.sum(jnp.exp(lg - mg), axis=-1, keepdims=True)
    gi = jnp.min(jnp.where(lg == mg, lane, big), axis=-1, keepdims=True)
    sel = (lane >= N_GROUPS) & (lane < N_GROUPS + N_EXPERTS) & (jnp.floor((lane - N_GROUPS) * (1.0 / EPG)) == gi)
    le = jnp.where(sel, logits, NEG)
    m1 = jnp.max(le, axis=-1, keepdims=True)
    i1 = jnp.min(jnp.where(le == m1, lane, big), axis=-1, keepdims=True)
    le2 = jnp.where(lane == i1, NEG, le)
    m2 = jnp.max(le2, axis=-1, keepdims=True)
    i2 = jnp.min(jnp.where(le2 == m2, lane, big), axis=-1, keepdims=True)
    z = jnp.sum(jnp.exp(le - m1), axis=-1, keepdims=True)
    p1 = 1.0 / z
    p2 = jnp.exp(m2 - m1) / z
    w1 = p1 / (p1 + p2) * g_w
    w2 = p2 / (p1 + p2) * g_w
    ei_ref[...] = jnp.where(lane == 0, i1 - N_GROUPS, jnp.where(lane == 1, i2 - N_GROUPS, 0.0)).astype(jnp.int32)
    ew_ref[...] = jnp.where(lane == 0, w1, jnp.where(lane == 1, w2, 0.0))


def _merge(x, branches, p, tm):
    m = x.shape[0]
    consts = [p["norm1_g"], p["w_gate"], p["w_branch"], p["w_out"], p["norm2_g"], jnp.concatenate(_split2(p["w_router"]), axis=1), p["b_router"]]
    tile = lambda w: pl.BlockSpec((tm, w), lambda i: (i, 0))
    return pl.pallas_call(
        _merge_kernel,
        grid=(m // tm,),
        in_specs=[tile(D_MODEL)] + [tile(BW)] * 4 + [_full(c.shape) for c in consts],
        out_specs=[tile(D_MODEL), tile(D_MODEL), tile(128), tile(128)],
        out_shape=[jax.ShapeDtypeStruct((m, D_MODEL), F32), jax.ShapeDtypeStruct((m, D_MODEL), F32),
                   jax.ShapeDtypeStruct((m, 128), jnp.int32), jax.ShapeDtypeStruct((m, 128), F32)],
        compiler_params=_cp("parallel"),
        name="merge",
    )(x, *branches, *consts)


def _dispatch_kernel(poff_ref, plen_ref, nv_ref, pos_ref, h_ref, posb_ref, hb_ref, xs_out, zbuf, sem, zsem, *, tmd, te, n1):
    @pl.when(pl.program_id(0) == 0)
    def _():
        zbuf[...] = jnp.zeros(zbuf.shape, F32)

        def pad_copies(e):
            off = poff_ref[e]
            head = (-off) & 7
            body = plen_ref[e] - head
            out = [(i < head, pltpu.make_async_copy(zbuf.at[pl.ds(0, 1)], xs_out.at[pl.ds(off + i, 1)], zsem)) for i in range(7)]
            b = te // 2
            while b >= 8:
                start = pl.multiple_of(off + head + (body & ~(2 * b - 1)), 8)
                out.append(((body & b) != 0, pltpu.make_async_copy(zbuf.at[pl.ds(0, b)], xs_out.at[pl.ds(start, b)], zsem)))
                b //= 2
            return out

        def tail_copies(t):
            return [pltpu.make_async_copy(zbuf, xs_out.at[pl.ds(pl.multiple_of(t * te + k * zbuf.shape[0], 8), zbuf.shape[0])], zsem)
                    for k in range(te // zbuf.shape[0])]

        def start_pad(e, carry):
            for cond, cp in pad_copies(e):
                pl.when(cond)(cp.start)
            return carry

        def wait_pad(e, carry):
            for cond, cp in pad_copies(e):
                pl.when(cond)(cp.wait)
            return carry

        def start_tail(t, carry):
            for cp in tail_copies(t):
                cp.start()
            return carry

        def wait_tail(t, carry):
            for cp in tail_copies(t):
                cp.wait()
            return carry

        n_tiles = xs_out.shape[0] // te
        lax.fori_loop(0, N_EXPERTS, start_pad, 0)
        lax.fori_loop(nv_ref[0], n_tiles, start_tail, 0)
        lax.fori_loop(0, N_EXPERTS, wait_pad, 0)
        lax.fori_loop(nv_ref[0], n_tiles, wait_tail, 0)

    def scatter(p_ref, src_ref, nrows):
        def issue(i, carry):
            for s in range(2):
                pltpu.make_async_copy(src_ref.at[pl.ds(i, 1)], xs_out.at[pl.ds(p_ref[2 * i + s], 1)], sem).start()
            return carry

        lax.fori_loop(0, nrows, issue, 0, unroll=16)
        for s in range(2):
            pltpu.make_async_copy(src_ref, xs_out.at[pl.ds(0, nrows)], sem).wait()

    pl.when(pl.program_id(0) < n1)(lambda: scatter(pos_ref, h_ref, tmd))
    pl.when(pl.program_id(0) == n1)(lambda: scatter(posb_ref, hb_ref, hb_ref.shape[0]))


def _dispatch(pad_off, pad_len, n_valid, pos, h2, posb, h2b, rows, tmd, te):
    n1 = h2.shape[0] // tmd
    mb = h2b.shape[0]
    grid_spec = pltpu.PrefetchScalarGridSpec(
        num_scalar_prefetch=3,
        grid=(n1 + 1,),
        in_specs=[pl.BlockSpec((2 * tmd,), lambda i, po, pn, nv: (jnp.minimum(i, n1 - 1),), memory_space=pltpu.SMEM),
                  pl.BlockSpec((tmd, D_MODEL), lambda i, po, pn, nv: (jnp.minimum(i, n1 - 1), 0)),
                  pl.BlockSpec((2 * mb,), lambda i, po, pn, nv: (0,), memory_space=pltpu.SMEM),
                  pl.BlockSpec((mb, D_MODEL), lambda i, po, pn, nv: (0, 0))],
        out_specs=pl.BlockSpec(memory_space=pl.ANY),
        scratch_shapes=[pltpu.VMEM((max(te // 2, 8), D_MODEL), F32), pltpu.SemaphoreType.DMA(()), pltpu.SemaphoreType.DMA(())],
    )
    return pl.pallas_call(
        functools.partial(_dispatch_kernel, tmd=tmd, te=te, n1=n1),
        grid_spec=grid_spec,
        out_shape=jax.ShapeDtypeStruct((rows, D_MODEL), F32),
        compiler_params=_cp("arbitrary"),
        name="moe_dispatch",
    )(pad_off, pad_len, n_valid, pos, h2, posb, h2b)


def _expert_kernel(te_ref, nv_ref, x_ref, wg_ref, wu_ref, wd_ref, o_ref):
    del te_ref

    @pl.when(pl.program_id(0) < nv_ref[0])
    def _():
        xb = x_ref[...].astype(BF16)
        gt = jnp.dot(xb, wg_ref[...].astype(BF16), preferred_element_type=F32)
        up = jnp.dot(xb, wu_ref[...].astype(BF16), preferred_element_type=F32)
        o_ref[...] = jnp.dot((_silu(gt) * up).astype(BF16), wd_ref[...].astype(BF16), preferred_element_type=F32)

    @pl.when(pl.program_id(0) >= nv_ref[0])
    def _():
        o_ref[...] = jnp.zeros(o_ref.shape, F32)


def _experts(layer, tile_expert, n_valid, xs, w_gate, w_up, w_down, te):
    rows = xs.shape[0]

    def xmap(i, te_ref, nv_ref):
        return (jnp.minimum(i, nv_ref[0] - 1), 0)

    def wmap(i, te_ref, nv_ref):
        return (layer, te_ref[i], 0, 0)

    grid_spec = pltpu.PrefetchScalarGridSpec(
        num_scalar_prefetch=2,
        grid=(rows // te,),
        in_specs=[pl.BlockSpec((te, D_MODEL), xmap), pl.BlockSpec((None, None, D_MODEL, D_EXPERT), wmap),
                  pl.BlockSpec((None, None, D_MODEL, D_EXPERT), wmap), pl.BlockSpec((None, None, D_EXPERT, D_MODEL), wmap)],
        out_specs=pl.BlockSpec((te, D_MODEL), lambda i, te_ref, nv_ref: (i, 0)),
    )
    return pl.pallas_call(
        _expert_kernel,
        grid_spec=grid_spec,
        out_shape=jax.ShapeDtypeStruct((rows, D_MODEL), F32),
        compiler_params=_cp("arbitrary"),
        name="moe_experts",
    )(tile_expert, n_valid, xs, w_gate, w_up, w_down)


def _combine_kernel(pos_ref, posn_ref, x1_ref, ew_ref, ys_hbm, o_ref, rbuf, sem, *, tmc):
    i = pl.program_id(0)
    slot = i % 2

    def gather(p_ref, sl):
        def issue(r, carry):
            for s in range(2):
                pltpu.make_async_copy(ys_hbm.at[pl.ds(p_ref[2 * r + s], 1)], rbuf.at[sl, s, pl.ds(r, 1)], sem.at[sl]).start()
            return carry

        lax.fori_loop(0, tmc, issue, 0, unroll=16)

    @pl.when(i == 0)
    def _():
        gather(pos_ref, 0)

    @pl.when(i + 1 < pl.num_programs(0))
    def _():
        gather(posn_ref, 1 - slot)

    for s in range(2):
        pltpu.make_async_copy(ys_hbm.at[pl.ds(0, tmc)], rbuf.at[slot, s], sem.at[slot]).wait()
    ew = ew_ref[...]
    o_ref[...] = x1_ref[...] + ew[:, 0:1] * rbuf[slot, 0] + ew[:, 1:2] * rbuf[slot, 1]


def _combine(pos, x1, ew, ys, tmc):
    m = x1.shape[0]
    nt = m // tmc
    return pl.pallas_call(
        functools.partial(_combine_kernel, tmc=tmc),
        grid=(nt,),
        in_specs=[pl.BlockSpec((2 * tmc,), lambda i: (i,), memory_space=pltpu.SMEM),
                  pl.BlockSpec((2 * tmc,), lambda i: (jnp.minimum(i + 1, nt - 1),), memory_space=pltpu.SMEM),
                  pl.BlockSpec((tmc, D_MODEL), lambda i: (i, 0)), pl.BlockSpec((tmc, 128), lambda i: (i, 0)),
                  pl.BlockSpec(memory_space=pl.ANY)],
        out_specs=pl.BlockSpec((tmc, D_MODEL), lambda i: (i, 0)),
        out_shape=jax.ShapeDtypeStruct((m, D_MODEL), F32),
        scratch_shapes=[pltpu.VMEM((2, 2, tmc, D_MODEL), F32), pltpu.SemaphoreType.DMA((2,))],
        compiler_params=_cp("arbitrary"),
        name="moe_combine",
    )(pos, pos, x1, ew, ys)


def _moe(layer, groups, w_gate, w_up, w_down, te):
    assert len(groups) == 2 and groups[1][0].shape[0] == groups[1][4]
    sizes = [g[0].shape[0] for g in groups]
    m = sum(sizes)
    flat_e = jnp.concatenate([g[2][:, 0:2].reshape(-1) for g in groups])
    onehot = (flat_e[:, None] == jnp.arange(N_EXPERTS, dtype=jnp.int32)[None, :]).astype(jnp.int32)
    csum = jnp.cumsum(onehot, axis=0)
    rank = jnp.sum(csum * onehot, axis=1) - 1
    counts = csum[-1]
    padded = ((counts + te - 1) // te) * te
    pend = jnp.cumsum(padded)
    pstart = pend - padded
    pos = (jnp.sum(onehot * pstart[None, :], axis=1) + rank).astype(jnp.int32)
    rows = ((2 * m + N_EXPERTS * (te - 1)) // te) * te
    n_tiles = rows // te
    n_valid = (pend[-1] // te).astype(jnp.int32).reshape(1)
    tile_start = jnp.arange(n_tiles, dtype=jnp.int32) * te
    tile_expert = jnp.minimum(jnp.sum((tile_start[:, None] >= pend[None, :]).astype(jnp.int32), axis=1), N_EXPERTS - 1)
    last_e = jnp.take(tile_expert, jnp.maximum(n_valid[0] - 1, 0))
    tile_expert = jnp.where(jnp.arange(n_tiles) < n_valid[0], tile_expert, last_e).astype(jnp.int32)
    starts = np.cumsum([0] + sizes)
    gpos = [pos[2 * starts[i]:2 * starts[i + 1]] for i in range(len(groups))]
    xs = _dispatch((pstart + counts).astype(jnp.int32), (padded - counts).astype(jnp.int32), n_valid, gpos[0], groups[0][1],
                   gpos[1], groups[1][1], rows, groups[0][4], te)
    ys = _experts(layer, tile_expert, n_valid, xs, w_gate, w_up, w_down, te)
    return [_combine(p, g[0], g[3], ys, g[4]) for g, p in zip(groups, gpos)]


def _tile4(v):
    return jnp.tile(v, HEADS).reshape(1, BW)


def _prep_layer(l, w):
    w_in = w["w_in"][l]
    z = lambda n: jnp.zeros((D_MODEL, n), F32)
    b_al = w_in[:, 1952:1960]
    w_small = jnp.concatenate(
        [w_in[:, 0:512], w_in[:, 512:768], w_in[:, 768:896], z(64), w_in[:, 896:928], z(32), w_in[:, 928:1696],
         w_in[:, 1696:1952], w_in[:, 1960:2216], b_al, z(120)], axis=1).astype(BF16)
    uq = w["mla_w_uq"][l]
    w_uq = jnp.pad(uq, ((0, 0), (0, 0), (0, 32))).reshape(256, 512).astype(BF16)
    qg = jnp.tile(jnp.concatenate([w["mla_qn_g"][l], w["mla_qr_g"][l], jnp.zeros((32,), F32)]), HEADS).reshape(1, 512)
    uk = w["mla_w_uk"][l]
    w_uk_p = jnp.pad(uk, ((0, 0), (0, 0), (0, 64))).reshape(128, 512).astype(BF16)
    kg = jnp.tile(jnp.concatenate([w["mla_kn_g"][l], jnp.zeros((64,), F32)]), HEADS).reshape(1, 512)
    kr_g = jnp.concatenate([jnp.zeros((64,), F32), w["mla_kr_g"][l], jnp.zeros((32,), F32)]).reshape(1, 128)
    uv = w["mla_w_uv"][l]
    w_uv_p = jnp.stack([jnp.pad(uv[:, h, :], ((0, 0), (64 * h, BW - 64 * h - 64))) for h in range(HEADS)]).astype(BF16)
    par_r = jnp.zeros((8, 128), F32).at[0, 4:8].set(w["gdn_a_log"][l]).at[1, 4:8].set(w["gdn_dt_bias"][l])
    par_c = jnp.zeros((8, 128), F32).at[4:8, 0].set(w["gdn_a_log"][l]).at[4:8, 1].set(w["gdn_dt_bias"][l])
    w_router = jnp.concatenate([w["moe_wg"][l], w["moe_we"][l], jnp.zeros((D_MODEL, 128 - 36), F32)], axis=1)
    b_router = jnp.concatenate([w["moe_bg"][l], w["moe_be"][l], jnp.zeros((128 - 36,), F32)]).reshape(1, 128)
    return {
        "norm1_g": w["norm1_g"][l].reshape(1, D_MODEL), "w_small": w_small, "w_bat": b_al.T.astype(BF16),
        "w_gate": w_in[:, 2216:].astype(BF16),
        "gm_norm_g": w["gm_norm_g"][l].reshape(1, BW), "gm_ws": w["gm_ws"][l],
        "gm_bfull": jnp.repeat(w["gm_b"][l].T, HD, axis=1),
        "gm_w0": jnp.repeat(w["gm_ws"][l][:, 0, 0], HD).reshape(1, BW), "gm_b0": jnp.repeat(w["gm_b"][l][:, 0], HD).reshape(1, BW),
        "mla_cq_g": w["mla_cq_g"][l].reshape(1, 256), "w_uq": w_uq, "qg": qg, "mla_ckv_g": w["mla_ckv_g"][l].reshape(1, 128),
        "kr_g": kr_g, "w_uk": w_uk_p, "kg": kg, "w_uv": w_uv_p,
        "w_uk_c": uk.reshape(128, BW).T.astype(BF16), "w_uv_c": uv.reshape(128, BW).astype(BF16), "kn_gx": _tile4(w["mla_kn_g"][l]),
        "conv_w": w["gdn_conv_w"][l], "gdn_par_r": par_r, "gdn_par_c": par_c, "gdn_out_gx": _tile4(w["gdn_out_g"][l]),
        "gdn_out_gc": w["gdn_out_g"][l].reshape(HD, 1),
        "mem_norm_g": w["mem_norm_g"][l].reshape(1, D_MODEL), "mem_w_kv": w["mem_w_kv"][l].astype(BF16),
        "mem_qn_gx": _tile4(w["mem_qn_g"][l]), "mem_kn_gx": _tile4(w["mem_kn_g"][l]),
        "w_branch": w["w_branch"][l].astype(BF16), "w_out": w["w_out"][l].astype(BF16),
        "norm2_g": w["norm2_g"][l].reshape(1, D_MODEL), "w_router": w_router, "b_router": b_router,
    }


def _rope_tables(pos):
    half = MLA_ROPE // 2
    inv = ROPE_THETA ** (-jnp.arange(half, dtype=F32) / half)
    ang = pos.astype(F32)[:, None] * inv[None, :]
    cos, sin = jnp.cos(ang), jnp.sin(ang)
    t = pos.shape[0]
    one, zero = jnp.ones((t, 64), F32), jnp.zeros((t, 64), F32)
    z16, z32 = jnp.zeros((t, 16), F32), jnp.zeros((t, 32), F32)
    return (jnp.concatenate([one, cos, cos, jnp.ones((t, 32), F32)], axis=1),
            jnp.concatenate([zero, -sin, z16, z32], axis=1),
            jnp.concatenate([zero, z16, sin, z32], axis=1))


def kernel(x_prompt, mem_prompt, x_sample, cache_mla_ckv, cache_mla_kr, cache_mem_k, cache_mem_v, state_gdn, state_conv,
           page_table, norm1_g, w_in, gm_norm_g, gm_ws, gm_b, mla_cq_g, mla_w_uq, mla_qn_g, mla_qr_g, mla_ckv_g, mla_kr_g,
           mla_w_uk, mla_kn_g, mla_w_uv, gdn_conv_w, gdn_a_log, gdn_dt_bias, gdn_out_g, mem_norm_g, mem_w_kv, mem_qn_g,
           mem_kn_g, w_branch, w_out, norm2_g, moe_wg, moe_bg, moe_we, moe_be, moe_w_gate, moe_w_up, moe_w_down):
    w = dict(norm1_g=norm1_g, w_in=w_in, gm_norm_g=gm_norm_g, gm_ws=gm_ws, gm_b=gm_b, mla_cq_g=mla_cq_g, mla_w_uq=mla_w_uq,
             mla_qn_g=mla_qn_g, mla_qr_g=mla_qr_g, mla_ckv_g=mla_ckv_g, mla_kr_g=mla_kr_g, mla_w_uk=mla_w_uk, mla_kn_g=mla_kn_g,
             mla_w_uv=mla_w_uv, gdn_conv_w=gdn_conv_w, gdn_a_log=gdn_a_log, gdn_dt_bias=gdn_dt_bias, gdn_out_g=gdn_out_g,
             mem_norm_g=mem_norm_g, mem_w_kv=mem_w_kv, mem_qn_g=mem_qn_g, mem_kn_g=mem_kn_g, w_branch=w_branch, w_out=w_out,
             norm2_g=norm2_g, moe_wg=moe_wg, moe_bg=moe_bg, moe_we=moe_we, moe_be=moe_be)
    depth = w_in.shape[0]
    bp, tp, _ = x_prompt.shape
    bs = x_sample.shape[0]
    mt = mem_prompt.shape[1]
    n_pages = page_table.shape[1]
    past_len = n_pages * cache_mla_ckv.shape[2]
    mp = bp * tp

    tm_p = min(512, mp)
    tq = min(256, tp)
    ta = min(512, tp)
    tg = min(256, tp)
    pp = min(64, n_pages)
    cache_krt = jnp.swapaxes(cache_mla_kr, 2, 3)
    tabs_p = _rope_tables(jnp.arange(tp, dtype=jnp.int32))
    tabs_s = _rope_tables(jnp.full((bs,), past_len, jnp.int32))

    xp = x_prompt.reshape(mp, D_MODEL)
    xs = x_sample.reshape(bs, D_MODEL)
    mem = mem_prompt.reshape(bp * mt, D_MODEL)
    cache_k = cache_mem_k.transpose(0, 1, 3, 4, 2).reshape(depth, bs, BW, mt)
    cache_v = cache_mem_v.transpose(0, 1, 3, 4, 2).reshape(depth, bs, BW, mt)
    state_t = state_gdn.transpose(0, 2, 3, 4, 1)
    rows_p, rows_s = [], []
    for l in range(depth):
        p = _prep_layer(l, w)
        mk, mv = _mem_kv(mem, p["mem_norm_g"], p["mem_w_kv"], p["mem_kn_gx"], min(512, bp * mt))
        tm_f = min(tm_p, tp)
        a_out, q4, k4, ckv, kr, ckvt, zc, zm, zba, bat = _front(xp, tabs_p, tp // tm_f, p, tm_f, ta)
        b_out = _mla_attn(q4, k4, ckvt, p["w_uv"], bp, tp, ta)
        c_out, sfin = _gdn_prompt(zc, zba, bat, bp, tp, p, tg)
        m_out = _mem_attn(zm, mk, mv, p["mem_qn_gx"], bp, tp, mt, tq)
        group_p = list(_merge(xp, (a_out, b_out, c_out, m_out), p, tm_p)) + [tm_p]
        s_p =jnp.stack([sfin[:, 64 * h:64 * h + 64, 64 * h:64 * h + 64] for h in range(HEADS)], axis=1).transpose(0, 1, 3, 2)
        conv_p = zc.reshape(bp, tp, 1024)[:, tp - 3:, 0:QKV_DIM]
        rows_p.append((ckv.reshape(bp, tp, 128), kr.reshape(bp, tp, MLA_ROPE), s_p, conv_p,
                       mk.reshape(bp, mt, HEADS, HD), mv.reshape(bp, mt, HEADS, HD)))
        za, zb, zc, zm, zba, _ = _in_proj(xs, p["norm1_g"], p["w_small"], p["w_bat"], bs)
        a_s, v_s, conv_s, gq, gk, gv, zg, beta, gdec, mqn = _sample_tok(za, zc, zba, zm, state_conv[l].reshape(bs, 3 * QKV_DIM), p)
        _, _, ckv_s, kr_s, _, q32 = _mla_pre(zb, bs, tabs_s, 1, p, bs, bs)
        q3 = q32.reshape(bs, HEADS, 128)
        qk = q3[:, :, 0:64].reshape(bs, 1, BW)
        qr8 = jnp.pad(q3[:, :, 64:96], ((0, 0), (0, 4), (0, 0)))
        b_s = _mla_decode(l, page_table, cache_mla_ckv, cache_krt, qk, qr8, ckv_s.reshape(bs, 1, 128),
                          kr_s.reshape(bs, 1, MLA_ROPE), p["kn_gx"], p["w_uk_c"], p["w_uv_c"], pp).reshape(bs, BW)
        hv = lambda a: a.reshape(HEADS, HD, bs)
        s_new, c_t = _gdn_step(l, state_t, hv(gq), hv(gk), hv(gv), beta[0:4].reshape(HEADS, 1, bs),
                               gdec[4:8].reshape(HEADS, 1, bs), hv(zg), p["gdn_out_gc"])
        m_s = _mem_attn_s(l, mqn, cache_k, cache_v, min(8, bs))
        group_s = list(_merge(xs, (a_s, b_s, c_t.reshape(BW, bs).T, m_s), p, bs)) + [bs]
        xp, xs = _moe(l, [group_p, group_s], moe_w_gate, moe_w_up, moe_w_down, min(256, mp))
        rows_s.append((ckv_s.reshape(bs, 1, 128), kr_s.reshape(bs, 1, MLA_ROPE), s_new.transpose(3, 0, 1, 2),
                       conv_s.reshape(bs, 3, QKV_DIM), v_s.reshape(bs, 1, BW)))
    p_out = [jnp.stack(a) for a in zip(*rows_p)]
    s_out = [jnp.stack(a) for a in zip(*rows_s)]
    return (xp.reshape(bp, tp, D_MODEL), xs.reshape(bs, 1, D_MODEL), *p_out, *s_out)
```

```python
import functools

import numpy as np
import jax
import jax.numpy as jnp
from jax import lax
from jax.experimental import pallas as pl
from jax.experimental.pallas import tpu as pltpu

F32 = jnp.float32
BF16 = jnp.bfloat16
EPS = 1e-6
NEG = float("-inf")

D_MODEL = 1024
HEADS = 4
HD = 64
BW = 256
MLA_ROPE = 32
MLA_KV_RANK = 128
MLA_SCALE = 96.0 ** -0.5
ROPE_THETA = 10000.0
GM_CHUNK = 128
GDN_CHUNK = 64
QKV_DIM = 768
N_GROUPS = 4
EPG = 8
N_EXPERTS = 32
D_EXPERT = 256
PAGE = 128
VMEM_LIMIT = 56 * 1024 * 1024

NT = (((1,), (1,)), ((), ()))
TN = (((0,), (0,)), ((), ()))


def _cp(*sem):
    return pltpu.CompilerParams(dimension_semantics=sem, vmem_limit_bytes=VMEM_LIMIT)


def _rms(x, g):
    ms = jnp.sum(x * x, axis=-1, keepdims=True) * (1.0 / x.shape[-1])
    return x * lax.rsqrt(ms + EPS) * g


def _bdot(a, b):
    return jnp.dot(a.astype(BF16), b.astype(BF16), preferred_element_type=F32)


def _bdot_nt(a, b):
    return lax.dot_general(a.astype(BF16), b.astype(BF16), NT, preferred_element_type=F32)


def _split2(a):
    hi = a.astype(BF16)
    return hi, (a - hi.astype(F32)).astype(BF16)


def _dot3(a, b):
    d = lambda x, y: jnp.dot(x, y, preferred_element_type=F32)
    return d(a[0], b[0]) + (d(a[0], b[1]) + d(a[1], b[0]))


def _sel_dot(w01, x, left):
    x0 = x.astype(BF16)
    r1 = x - x0.astype(F32)
    x1 = r1.astype(BF16)
    x2 = (r1 - x1.astype(F32)).astype(BF16)
    d = (lambda p: jnp.dot(w01, p, preferred_element_type=F32)) if left else (lambda p: jnp.dot(p, w01, preferred_element_type=F32))
    return d(x0) + (d(x1) + d(x2))


def _silu(x):
    return x * jax.nn.sigmoid(x)


def _gelu(x):
    return 0.5 * x * (1.0 + lax.erf(x * 0.7071067811865476))


def _lane_head(width=BW):
    return lax.broadcasted_iota(jnp.int32, (1, width), 1) // HD


def _full(shape):
    n = len(shape)
    return pl.BlockSpec(shape, lambda *_: (0,) * n)


def _block_mean(width, segs):
    m = np.zeros((width, width), np.float32)
    for a, b in segs:
        m[a:b, a:b] = 1.0 / (b - a)
    return m


_BMEAN64 = _block_mean(BW, [(64 * h, 64 * h + 64) for h in range(HEADS)])
_BONES64 = _BMEAN64 * 64.0
_BQ = _block_mean(512, [(128 * h, 128 * h + 64) for h in range(HEADS)] + [(128 * h + 64, 128 * h + 96) for h in range(HEADS)])
_BK = _block_mean(512, [(128 * h, 128 * h + 64) for h in range(HEADS)])
_IND8 = np.zeros((8, BW), np.float32)
for _h in range(HEADS):
    _IND8[_h, 64 * _h:64 * _h + 64] = 1.0
_EXPB = np.zeros((128, BW), np.float32)
_EXPG = np.zeros((128, BW), np.float32)
for _h in range(HEADS):
    _EXPB[_h, 64 * _h:64 * _h + 64] = 1.0
    _EXPG[4 + _h, 64 * _h:64 * _h + 64] = 1.0
_BDMASK = (_BONES64 > 0).astype(np.float32)


def _in_proj_kernel(x_ref, g_ref, w_ref, wbat_ref, oa_ref, ob_ref, oc_ref, om_ref, oba_ref, obat_ref):
    hb = _rms(x_ref[...], g_ref[...]).astype(BF16)
    oa_ref[...] = jnp.dot(hb, w_ref[:, 0:512], preferred_element_type=F32)
    ob_ref[...] = jnp.dot(hb, w_ref[:, 512:1024], preferred_element_type=F32)
    oc_ref[...] = jnp.dot(hb, w_ref[:, 1024:2048], preferred_element_type=F32)
    om_ref[...] = jnp.dot(hb, w_ref[:, 2048:2304], preferred_element_type=F32)
    oba_ref[...] = jnp.dot(hb, w_ref[:, 2304:2432], preferred_element_type=F32)
    obat_ref[...] = lax.dot_general(wbat_ref[...], hb, NT, preferred_element_type=F32)


def _in_proj(x, g, w, wbat, tm):
    m = x.shape[0]
    widths = (512, 512, 1024, 256, 128)
    return pl.pallas_call(
        _in_proj_kernel,
        grid=(m // tm,),
        in_specs=[pl.BlockSpec((tm, D_MODEL), lambda i: (i, 0)), _full((1, D_MODEL)), _full(w.shape), _full(wbat.shape)],
        out_specs=[pl.BlockSpec((tm, n), lambda i: (i, 0)) for n in widths] + [pl.BlockSpec((8, tm), lambda i: (0, i))],
        out_shape=[jax.ShapeDtypeStruct((m, n), F32) for n in widths] + [jax.ShapeDtypeStruct((8, m), F32)],
        compiler_params=_cp("parallel"),
        name="in_proj",
    )(x, g, w, wbat)


def _gmlp_body(za, g_ref, ws_ref, b_ref, o_ref):
    row = lax.broadcasted_iota(jnp.int32, (GM_CHUNK, GM_CHUNK), 0)
    col = lax.broadcasted_iota(jnp.int32, (GM_CHUNK, GM_CHUNK), 1)
    tril = col <= row
    lh = _lane_head()
    wts = [jnp.where(tril, ws_ref[g], 0.0).astype(BF16) for g in range(HEADS)]
    for c in range(za.shape[0] // GM_CHUNK):
        sl = slice(c * GM_CHUNK, (c + 1) * GM_CHUNK)
        ge = _gelu(za[sl, :])
        u = ge[:, :BW]
        vb = _rms(ge[:, BW:], g_ref[...]).astype(BF16)
        s = b_ref[...]
        for g in range(HEADS):
            s = s + jnp.where(lh == g, jnp.dot(wts[g], vb, preferred_element_type=F32), 0.0)
        o_ref[sl, :] = u * s


def _mla_pre_body(z, c_ref, s1_ref, s2_ref, gcq_ref, wuq_ref, qg_ref, bq_ref, gckv_ref, gkr_ref, wuk_ref, kg_ref, bk_ref,
                  oq_ref, ok_ref, ockv_ref, okr_ref, ockvt_ref, oq32_ref, tb):
    cs, s1, s2 = c_ref[...], s1_ref[...], s2_ref[...]

    def rope(x):
        return x * cs + pltpu.roll(x, 112, 1) * s1 + pltpu.roll(x, 16, 1) * s2

    cq = _rms(z[:, 0:256], gcq_ref[...])
    q = _bdot(cq, wuq_ref[...])
    qn = q * lax.rsqrt(_bdot(q * q, bq_ref[...]) + EPS) * qg_ref[...]
    ckv = _rms(z[:, 256:384], gckv_ref[...])
    ockv_ref[...] = ckv
    for c in range(ckv.shape[0] // tb):
        ockvt_ref[c] = ckv[c * tb:(c + 1) * tb, :].T.astype(BF16)
    krb = z[:, 384:512]
    kr = rope(krb * lax.rsqrt(jnp.sum(krb * krb, axis=-1, keepdims=True) * (1.0 / MLA_ROPE) + EPS) * gkr_ref[...])
    okr_ref[...] = kr[:, 64:96]
    k = _bdot(ckv, wuk_ref[...])
    kn = k * lax.rsqrt(_bdot(k * k, bk_ref[...]) + EPS) * kg_ref[...]
    for h in range(HEADS):
        sl = slice(128 * h, 128 * h + 128)
        qh = rope(qn[:, sl])
        oq_ref[h] = qh.astype(BF16)
        if oq32_ref is not None:
            oq32_ref[:, sl] = qh
        ok_ref[h] = (kn[:, sl] + kr).astype(BF16)


def _mla_pre_kernel(z_ref, *refs, tb):
    _mla_pre_body(z_ref[...], *refs, tb)


def _mla_consts(p):
    return [p["mla_cq_g"], p["w_uq"], p["qg"], jnp.asarray(_BQ, BF16), p["mla_ckv_g"], p["kr_g"], p["w_uk"], p["kg"],
            jnp.asarray(_BK, BF16)]


def _front_kernel(x_ref, g_ref, w_ref, wbat_ref, gmg_ref, gmws_ref, gmb_ref, *refs, tb):
    mla_in, (oa_ref, oq_ref, ok_ref, ockv_ref, okr_ref, ockvt_ref, oc_ref, om_ref, oba_ref, obat_ref) = refs[:12], refs[12:]
    hb = _rms(x_ref[...], g_ref[...]).astype(BF16)
    _gmlp_body(jnp.dot(hb, w_ref[:, 0:512], preferred_element_type=F32), gmg_ref, gmws_ref, gmb_ref, oa_ref)
    _mla_pre_body(jnp.dot(hb, w_ref[:, 512:1024], preferred_element_type=F32), *mla_in,
                  oq_ref, ok_ref, ockv_ref, okr_ref, ockvt_ref, None, tb)
    oc_ref[...] = jnp.dot(hb, w_ref[:, 1024:2048], preferred_element_type=F32)
    om_ref[...] = jnp.dot(hb, w_ref[:, 2048:2304], preferred_element_type=F32)
    oba_ref[...] = jnp.dot(hb, w_ref[:, 2304:2432], preferred_element_type=F32)
    obat_ref[...] = lax.dot_general(wbat_ref[...], hb, NT, preferred_element_type=F32)


def _front(x, tabs, t_blocks, p, tm, tb):
    m = x.shape[0]
    tab_spec = pl.BlockSpec((tm, 128), lambda i: (i % t_blocks, 0))
    consts = [p["norm1_g"], p["w_small"], p["w_bat"], p["gm_norm_g"], p["gm_ws"], p["gm_bfull"]]
    mla = _mla_consts(p)
    tile = lambda w, dt=F32: (pl.BlockSpec((tm, w), lambda i: (i, 0)), jax.ShapeDtypeStruct((m, w), dt))
    head = (pl.BlockSpec((HEADS, tm, 128), lambda i: (0, i, 0)), jax.ShapeDtypeStruct((HEADS, m, 128), BF16))
    outs = [tile(BW), head, head, tile(128), tile(MLA_ROPE),
            (pl.BlockSpec((tm // tb, 128, tb), lambda i: (i, 0, 0)), jax.ShapeDtypeStruct((m // tb, 128, tb), BF16)),
            tile(1024), tile(BW), tile(128), (pl.BlockSpec((8, tm), lambda i: (0, i)), jax.ShapeDtypeStruct((8, m), F32))]
    return pl.pallas_call(
        functools.partial(_front_kernel, tb=tb),
        grid=(m // tm,),
        in_specs=[pl.BlockSpec((tm, D_MODEL), lambda i: (i, 0))] + [_full(c.shape) for c in consts] + [tab_spec] * 3
        + [_full(c.shape) for c in mla],
        out_specs=[o[0] for o in outs],
        out_shape=[o[1] for o in outs],
        compiler_params=_cp("parallel"),
        name="front",
    )(x, *consts, *tabs, *mla)


def _mla_pre(zb, m, tabs, t_blocks, p, tm, tb):
    cs, s1, s2 = tabs
    tab_spec = pl.BlockSpec((tm, 128), lambda i: (i % t_blocks, 0))
    consts = _mla_consts(p)
    return pl.pallas_call(
        functools.partial(_mla_pre_kernel, tb=tb),
        grid=(m // tm,),
        in_specs=[pl.BlockSpec((tm, 512), lambda i: (i, 0)), tab_spec, tab_spec, tab_spec] + [_full(c.shape) for c in consts],
        out_specs=[pl.BlockSpec((HEADS, tm, 128), lambda i: (0, i, 0)), pl.BlockSpec((HEADS, tm, 128), lambda i: (0, i, 0)),
                   pl.BlockSpec((tm, 128), lambda i: (i, 0)), pl.BlockSpec((tm, MLA_ROPE), lambda i: (i, 0)),
                   pl.BlockSpec((tm // tb, 128, tb), lambda i: (i, 0, 0)), pl.BlockSpec((tm, 512), lambda i: (i, 0))],
        out_shape=[jax.ShapeDtypeStruct((HEADS, m, 128), BF16), jax.ShapeDtypeStruct((HEADS, m, 128), BF16),
                   jax.ShapeDtypeStruct((m, 128), F32), jax.ShapeDtypeStruct((m, MLA_ROPE), F32),
                   jax.ShapeDtypeStruct((m // tb, 128, tb), BF16), jax.ShapeDtypeStruct((m, 512), F32)],
        compiler_params=_cp("parallel"),
        name="mla_pre",
    )(zb, cs, s1, s2, *consts)


def _mla_attn_kernel(q_ref, k_ref, v_ref, wuv_ref, o_ref, *, tq):
    i = pl.program_id(1)
    row = lax.broadcasted_iota(jnp.int32, (tq, tq), 0)
    col = lax.broadcasted_iota(jnp.int32, (tq, tq), 1)
    causal = row <= col

    def step(j, carry, mask):
        off = pl.multiple_of(j * tq, tq)
        vt = v_ref[j]
        ss = [lax.dot_general(k_ref[h, pl.ds(off, tq), :], q_ref[h], NT, preferred_element_type=F32) for h in range(HEADS)]
        stats = []
        for h in range(HEADS):
            m, l, _ = carry[h]
            s = ss[h] * MLA_SCALE
            if mask:
                s = jnp.where(causal, s, NEG)
            mn = jnp.maximum(m, jnp.max(s, axis=0, keepdims=True))
            pr = jnp.exp(s - mn)
            al = jnp.exp(m - mn)
            stats.append((mn, al * l + jnp.sum(pr, axis=0, keepdims=True), al, pr.astype(BF16)))
        pvs = [jnp.dot(vt, stats[h][3], preferred_element_type=F32) for h in range(HEADS)]
        return tuple((stats[h][0], stats[h][1], stats[h][2] * carry[h][2] + pvs[h]) for h in range(HEADS))

    init = tuple((jnp.full((1, tq), NEG, F32), jnp.zeros((1, tq), F32), jnp.zeros((MLA_KV_RANK, tq), F32)) for _ in range(HEADS))
    carry = lax.fori_loop(0, i, lambda j, c: step(j, c, False), init)
    carry = step(i, carry, True)
    out = jnp.zeros((tq, BW), F32)
    for h in range(HEADS):
        m, l, acc = carry[h]
        out = out + lax.dot_general((acc / l).astype(BF16), wuv_ref[h], TN, preferred_element_type=F32)
    o_ref[...] = out


def _mla_attn(q4, k4, ckvt, wuv, n, t, tq):
    nq = t // tq
    return pl.pallas_call(
        functools.partial(_mla_attn_kernel, tq=tq),
        grid=(n, nq),
        in_specs=[pl.BlockSpec((HEADS, tq, 128), lambda b, i: (0, b * nq + i, 0)),
                  pl.BlockSpec((HEADS, t, 128), lambda b, i: (0, b, 0)),
                  pl.BlockSpec((nq, MLA_KV_RANK, tq), lambda b, i: (b, 0, 0)), _full(wuv.shape)],
        out_specs=pl.BlockSpec((tq, BW), lambda b, i: (b * nq + i, 0)),
        out_shape=jax.ShapeDtypeStruct((n * t, BW), F32),
        compiler_params=_cp("parallel", "arbitrary"),
        name="mla_attn",
    )(q4, k4, ckvt, wuv)


def _mla_decode_kernel(pt_ref, qk_ref, qr_ref, cnew_ref, krnew_ref, kng_ref, wukt_ref, wuv_ref, ind_ref, ckv_hbm, krt_hbm, o_ref,
                       cbuf, kbuf, sem, *, layer, npages, pp):
    n = pl.program_id(0)
    ngroups = npages // pp

    def page_copies(sample, grp, slot):
        base = sample * npages + grp * pp
        out = []
        for i in range(pp):
            page = pt_ref[base + i]
            out.append(pltpu.make_async_copy(ckv_hbm.at[layer, page], cbuf.at[slot, pl.ds(i * PAGE, PAGE)], sem.at[slot]))
            out.append(pltpu.make_async_copy(krt_hbm.at[layer, page], kbuf.at[slot, :, pl.ds(i * PAGE, PAGE)], sem.at[slot]))
        return out

    @pl.when(n == 0)
    def _():
        for cp in page_copies(0, 0, 0):
            cp.start()

    ind = ind_ref[...]
    qbd = (ind * (qk_ref[...] * kng_ref[...])).astype(BF16)
    qr = qr_ref[...].astype(BF16)
    wukt = wukt_ref[...]
    qabs = jnp.dot(qbd, wukt, preferred_element_type=F32).astype(BF16)
    w2 = jnp.concatenate([wukt, qabs, jnp.zeros((8, MLA_KV_RANK), BF16)], axis=0)

    def head_ms(kk):
        rows = [jnp.sum(kk[HD * h:HD * (h + 1)], axis=0, keepdims=True) for h in range(HEADS)]
        return jnp.concatenate(rows + [jnp.ones((8 - HEADS, kk.shape[1]), F32)], axis=0) * (1.0 / HD)

    m = jnp.full((8, 1), NEG, F32)
    l = jnp.zeros((8, 1), F32)
    acc = jnp.zeros((8, MLA_KV_RANK), F32)
    for grp in range(ngroups):
        slot = (n * ngroups + grp) % 2
        if grp + 1 < ngroups:
            for cp in page_copies(n, grp + 1, 1 - slot):
                cp.start()
        else:
            @pl.when(n + 1 < pl.num_programs(0))
            def _():
                for cp in page_copies(n + 1, 0, 1 - slot):
                    cp.start()
        for cp in page_copies(n, grp, slot):
            cp.wait()
        cb = cbuf[slot].astype(BF16)
        krt = kbuf[slot].astype(BF16)
        kq = lax.dot_general(w2, cb, NT, preferred_element_type=F32)
        kt = kq[0:BW]
        num = kq[BW:BW + 8]
        s = (num * lax.rsqrt(head_ms(kt * kt) + EPS) + jnp.dot(qr, krt, preferred_element_type=F32)) * MLA_SCALE
        mn = jnp.maximum(m, jnp.max(s, axis=-1, keepdims=True))
        pb = jnp.exp(s - mn)
        al = jnp.exp(m - mn)
        l = al * l + jnp.sum(pb, axis=-1, keepdims=True)
        acc = al * acc + jnp.dot(pb.astype(BF16), cb, preferred_element_type=F32)
        m = mn

    cb = jnp.broadcast_to(cnew_ref[...], (8, MLA_KV_RANK)).astype(BF16)
    k1 = lax.dot_general(cb, wukt, NT, preferred_element_type=F32)
    krn = krnew_ref[...].astype(BF16).astype(F32)
    num1 = jnp.sum(qabs.astype(F32) * cb.astype(F32), axis=-1, keepdims=True)
    ms1 = jnp.sum(ind * (k1 * k1), axis=-1, keepdims=True) * (1.0 / HD)
    s1 = (num1 * lax.rsqrt(ms1 + EPS) + jnp.sum(qr.astype(F32) * krn, axis=-1, keepdims=True)) * MLA_SCALE
    mn1 = jnp.maximum(m, s1)
    p1 = jnp.exp(s1 - mn1)
    al1 = jnp.exp(m - mn1)
    lat = (al1 * acc + p1 * cb.astype(F32)) / (al1 * l + p1)
    o8 = jnp.dot(lat.astype(BF16), wuv_ref[...], preferred_element_type=F32)
    o_ref[...] = jnp.sum(o8 * ind, axis=0, keepdims=True)


def _mla_decode(layer, page_table, cache_ckv, cache_krt, qk, qr8, cnew, krnew, kng, wuk, wuv, pp):
    ns, npages = page_table.shape
    pt = page_table.reshape(-1)

    def per_sample(shape):
        return pl.BlockSpec((None,) + shape, lambda n, pt_ref: (n, 0, 0))

    def const(a):
        nd = a.ndim
        return pl.BlockSpec(a.shape, lambda n, pt_ref: (0,) * nd)

    ind = jnp.asarray(_IND8)
    hbm = pl.BlockSpec(memory_space=pl.ANY)
    grid_spec = pltpu.PrefetchScalarGridSpec(
        num_scalar_prefetch=1,
        grid=(ns,),
        in_specs=[per_sample((1, BW)), per_sample((8, MLA_ROPE)), per_sample((1, MLA_KV_RANK)), per_sample((1, MLA_ROPE)),
                  const(kng), const(wuk), const(wuv), const(ind), hbm, hbm],
        out_specs=per_sample((1, BW)),
        scratch_shapes=[pltpu.VMEM((2, pp * PAGE, MLA_KV_RANK), F32), pltpu.VMEM((2, MLA_ROPE, pp * PAGE), F32),
                        pltpu.SemaphoreType.DMA((2,))],
    )
    return pl.pallas_call(
        functools.partial(_mla_decode_kernel, layer=layer, npages=npages, pp=pp),
        grid_spec=grid_spec,
        out_shape=jax.ShapeDtypeStruct((ns, 1, BW), F32),
        compiler_params=_cp("arbitrary"),
        name="mla_decode",
    )(pt, qk, qr8, cnew, krnew, kng, wuk, wuv, ind, cache_ckv, cache_krt)


def _gdn_kernel(zc_ref, zba_ref, bat_ref, cw_ref, parr_ref, parc_ref, gout_ref, bones_ref, expb_ref, expg_ref, bdm_ref,
                o_ref, sfin_ref, xbuf, s_sc, *, tg):
    t = pl.program_id(1)
    c = GDN_CHUNK

    @pl.when(t == 0)
    def _():
        xbuf[0:8, :] = jnp.zeros((8, QKV_DIM), F32)
        s_sc[...] = jnp.zeros(s_sc.shape, F32)

    @pl.when(t > 0)
    def _():
        xbuf[5:8, :] = xbuf[tg + 5:tg + 8, :]

    xbuf[8:8 + tg, :] = zc_ref[:, 0:QKV_DIM]
    y = cw_ref[0:1, :] * xbuf[5:5 + tg, :]
    for i in range(1, 4):
        y = y + cw_ref[i:i + 1, :] * xbuf[5 + i:5 + i + tg, :]
    y = _silu(y)

    zba = zba_ref[...]
    beta_col = jax.nn.sigmoid(zba)
    g_col = -jnp.exp(parr_ref[0:1, :]) * jax.nn.softplus(zba + parr_ref[1:2, :])
    g_row = -jnp.exp(parc_ref[:, 0:1]) * jax.nn.softplus(bat_ref[...] + parc_ref[:, 1:2])

    row = lax.broadcasted_iota(jnp.int32, (c, c), 0)
    col = lax.broadcasted_iota(jnp.int32, (c, c), 1)
    incl = col <= row
    strict = col < row
    lt = incl.astype(BF16)
    ut = (col >= row).astype(BF16)
    bones = bones_ref[...]
    bdm = bdm_ref[...]
    expb, expg = expb_ref[...], expg_ref[...]
    brow = lax.broadcasted_iota(jnp.int32, (BW, BW), 0)
    bcl = lax.broadcasted_iota(jnp.int32, (BW, BW), 1)
    same_head = (brow // HD) == (bcl // HD)
    incl_bd = same_head & ((bcl % HD) <= (brow % HD))
    strict_bd = same_head & ((bcl % HD) < (brow % HD))
    eye_bd = (brow == bcl).astype(F32)

    def same_block(b):
        return (brow // b) == (bcl // b)
    same_head2 = jnp.concatenate([same_head, same_head], axis=1)

    def stack4(a):
        return jnp.concatenate([a, a, a, a], axis=0)

    def fold4(a):
        return (a[0:c] + a[c:2 * c]) + (a[2 * c:3 * c] + a[3 * c:4 * c])

    nchunk = tg // c
    pre = []
    for ci in range(nchunk):
        sl = slice(ci * c, (ci + 1) * c)
        q, k, v = y[sl, 0:256], y[sl, 256:512], y[sl, 512:768]
        qn = q * lax.rsqrt(_bdot(q * q, bones) + EPS) * (HD ** -0.5)
        kn = k * lax.rsqrt(_bdot(k * k, bones) + EPS)
        bcol = beta_col[sl, :]
        gcum_c = _sel_dot(lt, g_col[sl, :], True)
        gcum_r = _sel_dot(ut, g_row[:, sl], False)
        gx = _sel_dot(expg, gcum_c, False)
        bx = _sel_dot(expb, bcol, False)
        egx = jnp.exp(gx)
        rhs = _split2(jnp.concatenate([bx * v, bx * egx * kn], axis=1))
        ks = jnp.where(same_head, stack4(kn), 0.0).astype(BF16)
        qs = jnp.where(same_head, stack4(qn), 0.0).astype(BF16)
        gc_s = jnp.concatenate([gcum_c[:, 4 + h:5 + h] for h in range(HEADS)], axis=0)
        gr_s = jnp.concatenate([gcum_r[4 + h:5 + h, :] for h in range(HEADS)], axis=1)
        beta_s = jnp.concatenate([bcol[:, h:h + 1] for h in range(HEADS)], axis=0)
        dm = jnp.exp(jnp.where(incl_bd, gc_s - gr_s, NEG))
        a = jnp.where(strict_bd, beta_s * lax.dot_general(ks, ks, NT, preferred_element_type=F32) * dm, 0.0)
        qk = (lax.dot_general(qs, ks, NT, preferred_element_type=F32) * dm).astype(BF16)
        pre.append(dict(sl=sl, qn=qn, kn=kn, gx=gx, egx=egx, rhs=rhs, qk=qk, a=_split2(a),
                        tinv=eye_bd - jnp.where(same_block(2), a, 0.0)))

    b = 2
    while b < c:
        off = same_block(2 * b) & jnp.logical_not(same_block(b))
        for d in pre:
            d["ts"] = _split2(d["tinv"])
            d["w"] = _split2(_dot3((jnp.where(off, d["a"][0], 0.0), jnp.where(off, d["a"][1], 0.0)), d["ts"]))
        for d in pre:
            d["tinv"] = d["tinv"] - _dot3(d["ts"], d["w"])
        b *= 2
    for d in pre:
        rhs = d["rhs"]
        d["x"] = fold4(jnp.where(same_head2, _dot3(_split2(d["tinv"]), (stack4(rhs[0]), stack4(rhs[1]))), 0.0))

    for d in pre:
        sl, qn, kn, gx, egx, qk, x = d["sl"], d["qn"], d["kn"], d["gx"], d["egx"], d["qk"], d["x"]
        s = s_sc[...]
        sb = s.astype(BF16)
        u = x[:, :BW] - jnp.dot(x[:, BW:].astype(BF16), sb, preferred_element_type=F32)
        ub = u.astype(BF16)
        o = egx * jnp.dot(qn.astype(BF16), sb, preferred_element_type=F32)
        o = o + fold4(jnp.where(same_head, jnp.dot(qk, stack4(ub), preferred_element_type=F32), 0.0))
        glast = gx[c - 1:c, :]
        kf = (kn * jnp.exp(glast - gx)).astype(BF16)
        s_new = jnp.exp(glast) * s + lax.dot_general(kf, ub, TN, preferred_element_type=F32)
        s_sc[...] = s_new * bdm
        on = o * lax.rsqrt(_bdot(o * o, bones) * (1.0 / HD) + EPS) * gout_ref[...]
        o_ref[sl, :] = on * _silu(zc_ref[sl, QKV_DIM:QKV_DIM + BW])

    @pl.when(t == pl.num_programs(1) - 1)
    def _():
        sfin_ref[...] = s_sc[...]


def _gdn_prompt(zc, zba, bat, n, t, p, tg):
    nt = t // tg
    consts = [p["conv_w"], p["gdn_par_r"], p["gdn_par_c"], p["gdn_out_gx"], jnp.asarray(_BONES64, BF16), jnp.asarray(_EXPB, BF16),
              jnp.asarray(_EXPG, BF16), jnp.asarray(_BDMASK)]
    return pl.pallas_call(
        functools.partial(_gdn_kernel, tg=tg),
        grid=(n, nt),
        in_specs=[pl.BlockSpec((tg, 1024), lambda b, i: (b * nt + i, 0)), pl.BlockSpec((tg, 128), lambda b, i: (b * nt + i, 0)),
                  pl.BlockSpec((8, tg), lambda b, i: (0, b * nt + i))] + [_full(c.shape) for c in consts],
        out_specs=[pl.BlockSpec((tg, BW), lambda b, i: (b * nt + i, 0)), pl.BlockSpec((None, BW, BW), lambda b, i: (b, 0, 0))],
        out_shape=[jax.ShapeDtypeStruct((n * t, BW), F32), jax.ShapeDtypeStruct((n, BW, BW), F32)],
        scratch_shapes=[pltpu.VMEM((8 + tg, QKV_DIM), F32), pltpu.VMEM((BW, BW), F32)],
        compiler_params=_cp("parallel", "arbitrary"),
        name="gdn_prompt",
    )(zc, zba, bat, *consts)


def _sample_tok_kernel(za_ref, zc_ref, zba_ref, zm_ref, sconv_ref, cw_ref, gmg_ref, gmw_ref, gmb_ref, parr_ref, bones_ref,
                       memg_ref, bmean_ref, oa_ref, ov_ref, oconv_ref, oq_ref, ok_ref, ovv_ref, ozg_ref, obeta_ref, og_ref, omq_ref):
    ge = _gelu(za_ref[...])
    v = _rms(ge[:, BW:], gmg_ref[...])
    ov_ref[...] = v
    oa_ref[...] = ge[:, :BW] * (gmw_ref[...] * v + gmb_ref[...])
    sc = sconv_ref[...]
    x = zc_ref[:, 0:QKV_DIM]
    y = (cw_ref[0:1, :] * sc[:, 0:768] + cw_ref[1:2, :] * sc[:, 768:1536] + cw_ref[2:3, :] * sc[:, 1536:2304]
         + cw_ref[3:4, :] * x)
    oconv_ref[:, 0:1536] = sc[:, 768:2304]
    oconv_ref[:, 1536:2304] = x
    y = _silu(y)
    q, k = y[:, 0:256], y[:, 256:512]
    bones = bones_ref[...]
    oq_ref[...] = (q * lax.rsqrt(_bdot(q * q, bones) + EPS) * (HD ** -0.5)).T
    ok_ref[...] = (k * lax.rsqrt(_bdot(k * k, bones) + EPS)).T
    ovv_ref[...] = y[:, 512:768].T
    ozg_ref[...] = zc_ref[:, QKV_DIM:QKV_DIM + BW].T
    zba = zba_ref[...]
    obeta_ref[...] = jax.nn.sigmoid(zba).T
    og_ref[...] = (-jnp.exp(parr_ref[0:1, :]) * jax.nn.softplus(zba + parr_ref[1:2, :])).T
    mq = zm_ref[...]
    omq_ref[...] = mq * lax.rsqrt(_bdot(mq * mq, bmean_ref[...]) + EPS) * memg_ref[...]


def _sample_tok(za, zc, zba, zm, sconv, p):
    ns = za.shape[0]
    args = [za, zc, zba, zm, sconv, p["conv_w"], p["gm_norm_g"], p["gm_w0"], p["gm_b0"], p["gdn_par_r"],
            jnp.asarray(_BONES64, BF16), p["mem_qn_gx"], jnp.asarray(_BMEAN64, BF16)]
    shapes = [(ns, BW), (ns, BW), (ns, 2304), (BW, ns), (BW, ns), (BW, ns), (BW, ns), (128, ns), (128, ns), (ns, BW)]
    return pl.pallas_call(
        _sample_tok_kernel,
        in_specs=[_full(a.shape) for a in args],
        out_specs=[_full(s) for s in shapes],
        out_shape=[jax.ShapeDtypeStruct(s, F32) for s in shapes],
        grid=(1,),
        compiler_params=_cp("arbitrary"),
        name="sample_tok",
    )(*args)


def _gdn_step_kernel(s_ref, q_ref, k_ref, v_ref, beta_ref, g_ref, zg_ref, gout_ref, so_ref, o_ref, o_sc):
    q, k = q_ref[...], k_ref[...]
    eg = jnp.exp(g_ref[...])
    beta = beta_ref[...]
    qk = jnp.sum(q * k, axis=0, keepdims=True)
    ssq = jnp.zeros(qk.shape, F32)
    for v in range(HD):
        sv = s_ref[v]
        sk = jnp.sum(sv * k, axis=0, keepdims=True)
        sq = jnp.sum(sv * q, axis=0, keepdims=True)
        u = beta * (v_ref[v:v + 1, :] - eg * sk)
        o = eg * sq + qk * u
        so_ref[v] = eg * sv + u * k
        o_sc[v:v + 1, :] = o
        ssq = ssq + o * o
    o_ref[...] = o_sc[...] * lax.rsqrt(ssq * (1.0 / HD) + EPS) * gout_ref[...] * _silu(zg_ref[...])


def _gdn_step(layer, state_t, q, k, v, beta, g, zg, gout_col):
    ns = state_t.shape[-1]
    vec = pl.BlockSpec((None, HD, ns), lambda h: (h, 0, 0))
    sca = pl.BlockSpec((None, 1, ns), lambda h: (h, 0, 0))
    return pl.pallas_call(
        _gdn_step_kernel,
        grid=(HEADS,),
        in_specs=[pl.BlockSpec((None, None, HD, HD, ns), lambda h: (layer, h, 0, 0, 0)), vec, vec, vec, sca, sca, vec, _full((HD, 1))],
        out_specs=[pl.BlockSpec((None, HD, HD, ns), lambda h: (h, 0, 0, 0)), vec],
        out_shape=[jax.ShapeDtypeStruct((HEADS, HD, HD, ns), F32), jax.ShapeDtypeStruct((HEADS, HD, ns), F32)],
        scratch_shapes=[pltpu.VMEM((HD, ns), F32)],
        compiler_params=_cp("parallel"),
        name="gdn_step",
    )(state_t, q, k, v, beta, g, zg, gout_col)


def _mem_kv_kernel(x_ref, g_ref, w_ref, kg_ref, bmean_ref, ok_ref, ov_ref):
    kv = _bdot(_rms(x_ref[...], g_ref[...]), w_ref[...])
    k = kv[:, 0:BW]
    ok_ref[...] = k * lax.rsqrt(_bdot(k * k, bmean_ref[...]) + EPS) * kg_ref[...]
    ov_ref[...] = kv[:, BW:]


def _mem_kv(mem, g, w, kgx, tm):
    m = mem.shape[0]
    bmean = jnp.asarray(_BMEAN64, BF16)
    return pl.pallas_call(
        _mem_kv_kernel,
        grid=(m // tm,),
        in_specs=[pl.BlockSpec((tm, D_MODEL), lambda i: (i, 0)), _full((1, D_MODEL)), _full(w.shape), _full((1, BW)), _full((BW, BW))],
        out_specs=[pl.BlockSpec((tm, BW), lambda i: (i, 0))] * 2,
        out_shape=[jax.ShapeDtypeStruct((m, BW), F32)] * 2,
        compiler_params=_cp("parallel"),
        name="mem_kv",
    )(mem, g, w, kgx, bmean)


def _mem_attn_kernel(q_ref, k_ref, v_ref, gq_ref, bmean_ref, o_ref):
    q = q_ref[...]
    qn = q * lax.rsqrt(_bdot(q * q, bmean_ref[...]) + EPS) * gq_ref[...]
    kb = k_ref[...].astype(BF16)
    vb = v_ref[...].astype(BF16)
    lh = _lane_head()
    out = jnp.zeros(q.shape, F32)
    for h in range(HEADS):
        mh = lh == h
        s = lax.dot_general(jnp.where(mh, qn, 0.0).astype(BF16), kb, NT, preferred_element_type=F32) * (HD ** -0.5)
        e = jnp.exp(s - jnp.max(s, axis=-1, keepdims=True))
        pr = e / jnp.sum(e, axis=-1, keepdims=True)
        out = out + jnp.where(mh, jnp.dot(pr.astype(BF16), vb, preferred_element_type=F32), 0.0)
    o_ref[...] = out


def _mem_attn(zm, mk, mv, gqx, n, t, mt, tq):
    nq = t // tq
    bmean = jnp.asarray(_BMEAN64, BF16)
    return pl.pallas_call(
        _mem_attn_kernel,
        grid=(n, nq),
        in_specs=[pl.BlockSpec((tq, BW), lambda b, i: (b * nq + i, 0)), pl.BlockSpec((mt, BW), lambda b, i: (b, 0)),
                  pl.BlockSpec((mt, BW), lambda b, i: (b, 0)), _full((1, BW)), _full((BW, BW))],
        out_specs=pl.BlockSpec((tq, BW), lambda b, i: (b * nq + i, 0)),
        out_shape=jax.ShapeDtypeStruct((n * t, BW), F32),
        compiler_params=_cp("parallel", "parallel"),
        name="mem_attn",
    )(zm, mk, mv, gqx, bmean)


def _mem_attn_s_kernel(q_ref, k_ref, v_ref, ind_ref, o_ref, *, bn):
    ind = ind_ref[...]
    for i in range(bn):
        qbd = (ind * q_ref[i:i + 1, :]).astype(BF16)
        s = jnp.dot(qbd, k_ref[i].astype(BF16), preferred_element_type=F32) * (HD ** -0.5)
        e = jnp.exp(s - jnp.max(s, axis=-1, keepdims=True))
        pr = e / jnp.sum(e, axis=-1, keepdims=True)
        o8 = lax.dot_general(pr.astype(BF16), v_ref[i].astype(BF16), NT, preferred_element_type=F32)
        o_ref[i:i + 1, :] = jnp.sum(o8 * ind, axis=0, keepdims=True)


def _mem_attn_s(layer, mqn, cache_k, cache_v, bn):
    ns = mqn.shape[0]
    mt = cache_k.shape[3]
    kv_spec = pl.BlockSpec((None, bn, BW, mt), lambda i: (layer, i, 0, 0))
    return pl.pallas_call(
        functools.partial(_mem_attn_s_kernel, bn=bn),
        grid=(ns // bn,),
        in_specs=[pl.BlockSpec((bn, BW), lambda i: (i, 0)), kv_spec, kv_spec, _full((8, BW))],
        out_specs=pl.BlockSpec((bn, BW), lambda i: (i, 0)),
        out_shape=jax.ShapeDtypeStruct((ns, BW), F32),
        compiler_params=_cp("parallel"),
        name="mem_attn_s",
    )(mqn, cache_k, cache_v, jnp.asarray(_IND8))


def _merge_kernel(x_ref, a_ref, b_ref, c_ref, m_ref, g1_ref, wg_ref, wb_ref, wo_ref, g2_ref, wr_ref, br_ref,
                  x1_ref, h2_ref, ei_ref, ew_ref):
    x = x_ref[...]
    hb = _rms(x, g1_ref[...]).astype(BF16)
    acc = jnp.zeros(x.shape, F32)
    for b, br in enumerate((a_ref, b_ref, c_ref, m_ref)):
        gate = jax.nn.sigmoid(jnp.dot(hb, wg_ref[:, b * D_MODEL:(b + 1) * D_MODEL], preferred_element_type=F32))
        acc = acc + gate * jnp.dot(br[...].astype(BF16), wb_ref[b], preferred_element_type=F32)
    x1 = x + jnp.dot(acc.astype(BF16), wo_ref[...], preferred_element_type=F32)
    x1_ref[...] = x1
    h2 = _rms(x1, g2_ref[...])
    h2_ref[...] = h2
    h2h, h2l = _split2(h2)
    r = jnp.dot(h2h, wr_ref[...], preferred_element_type=F32)
    logits = r[:, 0:128] + (r[:, 128:256] + jnp.dot(h2l, wr_ref[:, 0:128], preferred_element_type=F32)) + br_ref[...]
    lane = lax.broadcasted_iota(jnp.int32, (1, 128), 1).astype(F32)
    big = 1e9
    lg = jnp.where(lane < N_GROUPS, logits, NEG)
    mg = jnp.max(lg, axis=-1, keepdims=True)
    g_w = 1.0 / jnp.sum(jnp.exp(lg - mg), axis=-1, keepdims=True)
    gi = jnp.min(jnp.where(lg == mg, lane, big), axis=-1, keepdims=True)
    sel = (lane >= N_GROUPS) & (lane < N_GROUPS + N_EXPERTS) & (jnp.floor((lane - N_GROUPS) * (1.0 / EPG)) == gi)
    le = jnp.where(sel, logits, NEG)
    m1 = jnp.max(le, axis=-1, keepdims=True)
    i1 = jnp.min(jnp.where(le == m1, lane, big), axis=-1, keepdims=True)
    le2 = jnp.where(lane == i1, NEG, le)
    m2 = jnp.max(le2, axis=-1, keepdims=True)
    i2 = jnp.min(jnp.where(le2 == m2, lane, big), axis=-1, keepdims=True)
    z = jnp.sum(jnp.exp(le - m1), axis=-1, keepdims=True)
    p1 = 1.0 / z
    p2 = jnp.exp(m2 - m1) / z
    w1 = p1 / (p1 + p2) * g_w
    w2 = p2 / (p1 + p2) * g_w
    ei_ref[...] = jnp.where(lane == 0, i1 - N_GROUPS, jnp.where(lane == 1, i2 - N_GROUPS, 0.0)).astype(jnp.int32)
    ew_ref[...] = jnp.where(lane == 0, w1, jnp.where(lane == 1, w2, 0.0))


def _merge(x, branches, p, tm):
    m = x.shape[0]
    consts = [p["norm1_g"], p["w_gate"], p["w_branch"], p["w_out"], p["norm2_g"], jnp.concatenate(_split2(p["w_router"]), axis=1), p["b_router"]]
    tile = lambda w: pl.BlockSpec((tm, w), lambda i: (i, 0))
    return pl.pallas_call(
        _merge_kernel,
        grid=(m // tm,),
        in_specs=[tile(D_MODEL)] + [tile(BW)] * 4 + [_full(c.shape) for c in consts],
        out_specs=[tile(D_MODEL), tile(D_MODEL), tile(128), tile(128)],
        out_shape=[jax.ShapeDtypeStruct((m, D_MODEL), F32), jax.ShapeDtypeStruct((m, D_MODEL), F32),
                   jax.ShapeDtypeStruct((m, 128), jnp.int32), jax.ShapeDtypeStruct((m, 128), F32)],
        compiler_params=_cp("parallel"),
        name="merge",
    )(x, *branches, *consts)


def _dispatch_kernel(poff_ref, plen_ref, nv_ref, pos_ref, h_ref, posb_ref, hb_ref, xs_out, zbuf, sem, zsem, *, tmd, te, n1):
    @pl.when(pl.program_id(0) == 0)
    def _():
        zbuf[...] = jnp.zeros(zbuf.shape, F32)

        def pad_copies(e):
            off = poff_ref[e]
            head = (-off) & 7
            body = plen_ref[e] - head
            out = [(i < head, pltpu.make_async_copy(zbuf.at[pl.ds(0, 1)], xs_out.at[pl.ds(off + i, 1)], zsem)) for i in range(7)]
            b = te // 2
            while b >= 8:
                start = pl.multiple_of(off + head + (body & ~(2 * b - 1)), 8)
                out.append(((body & b) != 0, pltpu.make_async_copy(zbuf.at[pl.ds(0, b)], xs_out.at[pl.ds(start, b)], zsem)))
                b //= 2
            return out

        def tail_copies(t):
            return [pltpu.make_async_copy(zbuf, xs_out.at[pl.ds(pl.multiple_of(t * te + k * zbuf.shape[0], 8), zbuf.shape[0])], zsem)
                    for k in range(te // zbuf.shape[0])]

        def start_pad(e, carry):
            for cond, cp in pad_copies(e):
                pl.when(cond)(cp.start)
            return carry

        def wait_pad(e, carry):
            for cond, cp in pad_copies(e):
                pl.when(cond)(cp.wait)
            return carry

        def start_tail(t, carry):
            for cp in tail_copies(t):
                cp.start()
            return carry

        def wait_tail(t, carry):
            for cp in tail_copies(t):
                cp.wait()
            return carry

        n_tiles = xs_out.shape[0] // te
        lax.fori_loop(0, N_EXPERTS, start_pad, 0)
        lax.fori_loop(nv_ref[0], n_tiles, start_tail, 0)
        lax.fori_loop(0, N_EXPERTS, wait_pad, 0)
        lax.fori_loop(nv_ref[0], n_tiles, wait_tail, 0)

    def scatter(p_ref, src_ref, nrows):
        def issue(i, carry):
            for s in range(2):
                pltpu.make_async_copy(src_ref.at[pl.ds(i, 1)], xs_out.at[pl.ds(p_ref[2 * i + s], 1)], sem).start()
            return carry

        lax.fori_loop(0, nrows, issue, 0, unroll=16)
        for s in range(2):
            pltpu.make_async_copy(src_ref, xs_out.at[pl.ds(0, nrows)], sem).wait()

    pl.when(pl.program_id(0) < n1)(lambda: scatter(pos_ref, h_ref, tmd))
    pl.when(pl.program_id(0) == n1)(lambda: scatter(posb_ref, hb_ref, hb_ref.shape[0]))


def _dispatch(pad_off, pad_len, n_valid, pos, h2, posb, h2b, rows, tmd, te):
    n1 = h2.shape[0] // tmd
    mb = h2b.shape[0]
    grid_spec = pltpu.PrefetchScalarGridSpec(
        num_scalar_prefetch=3,
        grid=(n1 + 1,),
        in_specs=[pl.BlockSpec((2 * tmd,), lambda i, po, pn, nv: (jnp.minimum(i, n1 - 1),), memory_space=pltpu.SMEM),
                  pl.BlockSpec((tmd, D_MODEL), lambda i, po, pn, nv: (jnp.minimum(i, n1 - 1), 0)),
                  pl.BlockSpec((2 * mb,), lambda i, po, pn, nv: (0,), memory_space=pltpu.SMEM),
                  pl.BlockSpec((mb, D_MODEL), lambda i, po, pn, nv: (0, 0))],
        out_specs=pl.BlockSpec(memory_space=pl.ANY),
        scratch_shapes=[pltpu.VMEM((max(te // 2, 8), D_MODEL), F32), pltpu.SemaphoreType.DMA(()), pltpu.SemaphoreType.DMA(())],
    )
    return pl.pallas_call(
        functools.partial(_dispatch_kernel, tmd=tmd, te=te, n1=n1),
        grid_spec=grid_spec,
        out_shape=jax.ShapeDtypeStruct((rows, D_MODEL), F32),
        compiler_params=_cp("arbitrary"),
        name="moe_dispatch",
    )(pad_off, pad_len, n_valid, pos, h2, posb, h2b)


def _expert_kernel(te_ref, nv_ref, x_ref, wg_ref, wu_ref, wd_ref, o_ref):
    del te_ref

    @pl.when(pl.program_id(0) < nv_ref[0])
    def _():
        xb = x_ref[...].astype(BF16)
        gt = jnp.dot(xb, wg_ref[...].astype(BF16), preferred_element_type=F32)
        up = jnp.dot(xb, wu_ref[...].astype(BF16), preferred_element_type=F32)
        o_ref[...] = jnp.dot((_silu(gt) * up).astype(BF16), wd_ref[...].astype(BF16), preferred_element_type=F32)

    @pl.when(pl.program_id(0) >= nv_ref[0])
    def _():
        o_ref[...] = jnp.zeros(o_ref.shape, F32)


def _experts(layer, tile_expert, n_valid, xs, w_gate, w_up, w_down, te):
    rows = xs.shape[0]

    def xmap(i, te_ref, nv_ref):
        return (jnp.minimum(i, nv_ref[0] - 1), 0)

    def wmap(i, te_ref, nv_ref):
        return (layer, te_ref[i], 0, 0)

    grid_spec = pltpu.PrefetchScalarGridSpec(
        num_scalar_prefetch=2,
        grid=(rows // te,),
        in_specs=[pl.BlockSpec((te, D_MODEL), xmap), pl.BlockSpec((None, None, D_MODEL, D_EXPERT), wmap),
                  pl.BlockSpec((None, None, D_MODEL, D_EXPERT), wmap), pl.BlockSpec((None, None, D_EXPERT, D_MODEL), wmap)],
        out_specs=pl.BlockSpec((te, D_MODEL), lambda i, te_ref, nv_ref: (i, 0)),
    )
    return pl.pallas_call(
        _expert_kernel,
        grid_spec=grid_spec,
        out_shape=jax.ShapeDtypeStruct((rows, D_MODEL), F32),
        compiler_params=_cp("arbitrary"),
        name="moe_experts",
    )(tile_expert, n_valid, xs, w_gate, w_up, w_down)


def _combine_kernel(pos_ref, posn_ref, x1_ref, ew_ref, ys_hbm, o_ref, rbuf, sem, *, tmc):
    i = pl.program_id(0)
    slot = i % 2

    def gather(p_ref, sl):
        def issue(r, carry):
            for s in range(2):
                pltpu.make_async_copy(ys_hbm.at[pl.ds(p_ref[2 * r + s], 1)], rbuf.at[sl, s, pl.ds(r, 1)], sem.at[sl]).start()
            return carry

        lax.fori_loop(0, tmc, issue, 0, unroll=16)

    @pl.when(i == 0)
    def _():
        gather(pos_ref, 0)

    @pl.when(i + 1 < pl.num_programs(0))
    def _():
        gather(posn_ref, 1 - slot)

    for s in range(2):
        pltpu.make_async_copy(ys_hbm.at[pl.ds(0, tmc)], rbuf.at[slot, s], sem.at[slot]).wait()
    ew = ew_ref[...]
    o_ref[...] = x1_ref[...] + ew[:, 0:1] * rbuf[slot, 0] + ew[:, 1:2] * rbuf[slot, 1]


def _combine(pos, x1, ew, ys, tmc):
    m = x1.shape[0]
    nt = m // tmc
    return pl.pallas_call(
        functools.partial(_combine_kernel, tmc=tmc),
        grid=(nt,),
        in_specs=[pl.BlockSpec((2 * tmc,), lambda i: (i,), memory_space=pltpu.SMEM),
                  pl.BlockSpec((2 * tmc,), lambda i: (jnp.minimum(i + 1, nt - 1),), memory_space=pltpu.SMEM),
                  pl.BlockSpec((tmc, D_MODEL), lambda i: (i, 0)), pl.BlockSpec((tmc, 128), lambda i: (i, 0)),
                  pl.BlockSpec(memory_space=pl.ANY)],
        out_specs=pl.BlockSpec((tmc, D_MODEL), lambda i: (i, 0)),
        out_shape=jax.ShapeDtypeStruct((m, D_MODEL), F32),
        scratch_shapes=[pltpu.VMEM((2, 2, tmc, D_MODEL), F32), pltpu.SemaphoreType.DMA((2,))],
        compiler_params=_cp("arbitrary"),
        name="moe_combine",
    )(pos, pos, x1, ew, ys)


def _moe(layer, groups, w_gate, w_up, w_down, te):
    assert len(groups) == 2 and groups[1][0].shape[0] == groups[1][4]
    sizes = [g[0].shape[0] for g in groups]
    m = sum(sizes)
    flat_e = jnp.concatenate([g[2][:, 0:2].reshape(-1) for g in groups])
    onehot = (flat_e[:, None] == jnp.arange(N_EXPERTS, dtype=jnp.int32)[None, :]).astype(jnp.int32)
    csum = jnp.cumsum(onehot, axis=0)
    rank = jnp.sum(csum * onehot, axis=1) - 1
    counts = csum[-1]
    padded = ((counts + te - 1) // te) * te
    pend = jnp.cumsum(padded)
    pstart = pend - padded
    pos = (jnp.sum(onehot * pstart[None, :], axis=1) + rank).astype(jnp.int32)
    rows = ((2 * m + N_EXPERTS * (te - 1)) // te) * te
    n_tiles = rows // te
    n_valid = (pend[-1] // te).astype(jnp.int32).reshape(1)
    tile_start = jnp.arange(n_tiles, dtype=jnp.int32) * te
    tile_expert = jnp.minimum(jnp.sum((tile_start[:, None] >= pend[None, :]).astype(jnp.int32), axis=1), N_EXPERTS - 1)
    last_e = jnp.take(tile_expert, jnp.maximum(n_valid[0] - 1, 0))
    tile_expert = jnp.where(jnp.arange(n_tiles) < n_valid[0], tile_expert, last_e).astype(jnp.int32)
    starts = np.cumsum([0] + sizes)
    gpos = [pos[2 * starts[i]:2 * starts[i + 1]] for i in range(len(groups))]
    xs = _dispatch((pstart + counts).astype(jnp.int32), (padded - counts).astype(jnp.int32), n_valid, gpos[0], groups[0][1],
                   gpos[1], groups[1][1], rows, groups[0][4], te)
    ys = _experts(layer, tile_expert, n_valid, xs, w_gate, w_up, w_down, te)
    return [_combine(p, g[0], g[3], ys, g[4]) for g, p in zip(groups, gpos)]


def _tile4(v):
    return jnp.tile(v, HEADS).reshape(1, BW)


def _prep_layer(l, w):
    w_in = w["w_in"][l]
    z = lambda n: jnp.zeros((D_MODEL, n), F32)
    b_al = w_in[:, 1952:1960]
    w_small = jnp.concatenate(
        [w_in[:, 0:512], w_in[:, 512:768], w_in[:, 768:896], z(64), w_in[:, 896:928], z(32), w_in[:, 928:1696],
         w_in[:, 1696:1952], w_in[:, 1960:2216], b_al, z(120)], axis=1).astype(BF16)
    uq = w["mla_w_uq"][l]
    w_uq = jnp.pad(uq, ((0, 0), (0, 0), (0, 32))).reshape(256, 512).astype(BF16)
    qg = jnp.tile(jnp.concatenate([w["mla_qn_g"][l], w["mla_qr_g"][l], jnp.zeros((32,), F32)]), HEADS).reshape(1, 512)
    uk = w["mla_w_uk"][l]
    w_uk_p = jnp.pad(uk, ((0, 0), (0, 0), (0, 64))).reshape(128, 512).astype(BF16)
    kg = jnp.tile(jnp.concatenate([w["mla_kn_g"][l], jnp.zeros((64,), F32)]), HEADS).reshape(1, 512)
    kr_g = jnp.concatenate([jnp.zeros((64,), F32), w["mla_kr_g"][l], jnp.zeros((32,), F32)]).reshape(1, 128)
    uv = w["mla_w_uv"][l]
    w_uv_p = jnp.stack([jnp.pad(uv[:, h, :], ((0, 0), (64 * h, BW - 64 * h - 64))) for h in range(HEADS)]).astype(BF16)
    par_r = jnp.zeros((8, 128), F32).at[0, 4:8].set(w["gdn_a_log"][l]).at[1, 4:8].set(w["gdn_dt_bias"][l])
    par_c = jnp.zeros((8, 128), F32).at[4:8, 0].set(w["gdn_a_log"][l]).at[4:8, 1].set(w["gdn_dt_bias"][l])
    w_router = jnp.concatenate([w["moe_wg"][l], w["moe_we"][l], jnp.zeros((D_MODEL, 128 - 36), F32)], axis=1)
    b_router = jnp.concatenate([w["moe_bg"][l], w["moe_be"][l], jnp.zeros((128 - 36,), F32)]).reshape(1, 128)
    return {
        "norm1_g": w["norm1_g"][l].reshape(1, D_MODEL), "w_small": w_small, "w_bat": b_al.T.astype(BF16),
        "w_gate": w_in[:, 2216:].astype(BF16),
        "gm_norm_g": w["gm_norm_g"][l].reshape(1, BW), "gm_ws": w["gm_ws"][l],
        "gm_bfull": jnp.repeat(w["gm_b"][l].T, HD, axis=1),
        "gm_w0": jnp.repeat(w["gm_ws"][l][:, 0, 0], HD).reshape(1, BW), "gm_b0": jnp.repeat(w["gm_b"][l][:, 0], HD).reshape(1, BW),
        "mla_cq_g": w["mla_cq_g"][l].reshape(1, 256), "w_uq": w_uq, "qg": qg, "mla_ckv_g": w["mla_ckv_g"][l].reshape(1, 128),
        "kr_g": kr_g, "w_uk": w_uk_p, "kg": kg, "w_uv": w_uv_p,
        "w_uk_c": uk.reshape(128, BW).T.astype(BF16), "w_uv_c": uv.reshape(128, BW).astype(BF16), "kn_gx": _tile4(w["mla_kn_g"][l]),
        "conv_w": w["gdn_conv_w"][l], "gdn_par_r": par_r, "gdn_par_c": par_c, "gdn_out_gx": _tile4(w["gdn_out_g"][l]),
        "gdn_out_gc": w["gdn_out_g"][l].reshape(HD, 1),
        "mem_norm_g": w["mem_norm_g"][l].reshape(1, D_MODEL), "mem_w_kv": w["mem_w_kv"][l].astype(BF16),
        "mem_qn_gx": _tile4(w["mem_qn_g"][l]), "mem_kn_gx": _tile4(w["mem_kn_g"][l]),
        "w_branch": w["w_branch"][l].astype(BF16), "w_out": w["w_out"][l].astype(BF16),
        "norm2_g": w["norm2_g"][l].reshape(1, D_MODEL), "w_router": w_router, "b_router": b_router,
    }


def _rope_tables(pos):
    half = MLA_ROPE // 2
    inv = ROPE_THETA ** (-jnp.arange(half, dtype=F32) / half)
    ang = pos.astype(F32)[:, None] * inv[None, :]
    cos, sin = jnp.cos(ang), jnp.sin(ang)
    t = pos.shape[0]
    one, zero = jnp.ones((t, 64), F32), jnp.zeros((t, 64), F32)
    z16, z32 = jnp.zeros((t, 16), F32), jnp.zeros((t, 32), F32)
    return (jnp.concatenate([one, cos, cos, jnp.ones((t, 32), F32)], axis=1),
            jnp.concatenate([zero, -sin, z16, z32], axis=1),
            jnp.concatenate([zero, z16, sin, z32], axis=1))


def kernel(x_prompt, mem_prompt, x_sample, cache_mla_ckv, cache_mla_kr, cache_mem_k, cache_mem_v, state_gdn, state_conv,
           page_table, norm1_g, w_in, gm_norm_g, gm_ws, gm_b, mla_cq_g, mla_w_uq, mla_qn_g, mla_qr_g, mla_ckv_g, mla_kr_g,
           mla_w_uk, mla_kn_g, mla_w_uv, gdn_conv_w, gdn_a_log, gdn_dt_bias, gdn_out_g, mem_norm_g, mem_w_kv, mem_qn_g,
           mem_kn_g, w_branch, w_out, norm2_g, moe_wg, moe_bg, moe_we, moe_be, moe_w_gate, moe_w_up, moe_w_down):
    w = dict(norm1_g=norm1_g, w_in=w_in, gm_norm_g=gm_norm_g, gm_ws=gm_ws, gm_b=gm_b, mla_cq_g=mla_cq_g, mla_w_uq=mla_w_uq,
             mla_qn_g=mla_qn_g, mla_qr_g=mla_qr_g, mla_ckv_g=mla_ckv_g, mla_kr_g=mla_kr_g, mla_w_uk=mla_w_uk, mla_kn_g=mla_kn_g,
             mla_w_uv=mla_w_uv, gdn_conv_w=gdn_conv_w, gdn_a_log=gdn_a_log, gdn_dt_bias=gdn_dt_bias, gdn_out_g=gdn_out_g,
             mem_norm_g=mem_norm_g, mem_w_kv=mem_w_kv, mem_qn_g=mem_qn_g, mem_kn_g=mem_kn_g, w_branch=w_branch, w_out=w_out,
             norm2_g=norm2_g, moe_wg=moe_wg, moe_bg=moe_bg, moe_we=moe_we, moe_be=moe_be)
    depth = w_in.shape[0]
    bp, tp, _ = x_prompt.shape
    bs = x_sample.shape[0]
    mt = mem_prompt.shape[1]
    n_pages = page_table.shape[1]
    past_len = n_pages * cache_mla_ckv.shape[2]
    mp = bp * tp

    tm_p = min(512, mp)
    tq = min(512, tp)
    ta = min(512, tp)
    tg = min(256, tp)
    pp = min(64, n_pages)
    cache_krt = jnp.swapaxes(cache_mla_kr, 2, 3)
    tabs_p = _rope_tables(jnp.arange(tp, dtype=jnp.int32))
    tabs_s = _rope_tables(jnp.full((bs,), past_len, jnp.int32))

    xp = x_prompt.reshape(mp, D_MODEL)
    xs = x_sample.reshape(bs, D_MODEL)
    mem = mem_prompt.reshape(bp * mt, D_MODEL)
    cache_k = cache_mem_k.transpose(0, 1, 3, 4, 2).reshape(depth, bs, BW, mt)
    cache_v = cache_mem_v.transpose(0, 1, 3, 4, 2).reshape(depth, bs, BW, mt)
    state_t = state_gdn.transpose(0, 2, 3, 4, 1)
    rows_p, rows_s = [], []
    for l in range(depth):
        p = _prep_layer(l, w)
        mk, mv = _mem_kv(mem, p["mem_norm_g"], p["mem_w_kv"], p["mem_kn_gx"], min(512, bp * mt))
        tm_f = min(tm_p, tp)
        a_out, q4, k4, ckv, kr, ckvt, zc, zm, zba, bat = _front(xp, tabs_p, tp // tm_f, p, tm_f, ta)
        b_out = _mla_attn(q4, k4, ckvt, p["w_uv"], bp, tp, ta)
        c_out, sfin = _gdn_prompt(zc, zba, bat, bp, tp, p, tg)
        m_out = _mem_attn(zm, mk, mv, p["mem_qn_gx"], bp, tp, mt, tq)
        group_p = list(_merge(xp, (a_out, b_out, c_out, m_out), p, tm_p)) + [tm_p]
        s_p =jnp.stack([sfin[:, 64 * h:64 * h + 64, 64 * h:64 * h + 64] for h in range(HEADS)], axis=1).transpose(0, 1, 3, 2)
        conv_p = zc.reshape(bp, tp, 1024)[:, tp - 3:, 0:QKV_DIM]
        rows_p.append((ckv.reshape(bp, tp, 128), kr.reshape(bp, tp, MLA_ROPE), s_p, conv_p,
                       mk.reshape(bp, mt, HEADS, HD), mv.reshape(bp, mt, HEADS, HD)))
        za, zb, zc, zm, zba, _ = _in_proj(xs, p["norm1_g"], p["w_small"], p["w_bat"], bs)
        a_s, v_s, conv_s, gq, gk, gv, zg, beta, gdec, mqn = _sample_tok(za, zc, zba, zm, state_conv[l].reshape(bs, 3 * QKV_DIM), p)
        _, _, ckv_s, kr_s, _, q32 = _mla_pre(zb, bs, tabs_s, 1, p, bs, bs)
        q3 = q32.reshape(bs, HEADS, 128)
        qk = q3[:, :, 0:64].reshape(bs, 1, BW)
        qr8 = jnp.pad(q3[:, :, 64:96], ((0, 0), (0, 4), (0, 0)))
        b_s = _mla_decode(l, page_table, cache_mla_ckv, cache_krt, qk, qr8, ckv_s.reshape(bs, 1, 128),
                          kr_s.reshape(bs, 1, MLA_ROPE), p["kn_gx"], p["w_uk_c"], p["w_uv_c"], pp).reshape(bs, BW)
        hv = lambda a: a.reshape(HEADS, HD, bs)
        s_new, c_t = _gdn_step(l, state_t, hv(gq), hv(gk), hv(gv), beta[0:4].reshape(HEADS, 1, bs),
                               gdec[4:8].reshape(HEADS, 1, bs), hv(zg), p["gdn_out_gc"])
        m_s = _mem_attn_s(l, mqn, cache_k, cache_v, min(8, bs))
        group_s = list(_merge(xs, (a_s, b_s, c_t.reshape(BW, bs).T, m_s), p, bs)) + [bs]
        xp, xs = _moe(l, [group_p, group_s], moe_w_gate, moe_w_up, moe_w_down, min(512, mp))
        rows_s.append((ckv_s.reshape(bs, 1, 128), kr_s.reshape(bs, 1, MLA_ROPE), s_new.transpose(3, 0, 1, 2),
                       conv_s.reshape(bs, 3, QKV_DIM), v_s.reshape(bs, 1, BW)))
    p_out = [jnp.stack(a) for a in zip(*rows_p)]
    s_out = [jnp.stack(a) for a in zip(*rows_s)]
    return (xp.reshape(bp, tp, D_MODEL), xs.reshape(bs, 1, D_MODEL), *p_out, *s_out)
```

```python
import functools

import numpy as np
import jax
import jax.numpy as jnp
from jax import lax
from jax.experimental import pallas as pl
from jax.experimental.pallas import tpu as pltpu

F32 = jnp.float32
BF16 = jnp.bfloat16
EPS = 1e-6
NEG = float("-inf")

D_MODEL = 1024
HEADS = 4
HD = 64
BW = 256
MLA_ROPE = 32
MLA_KV_RANK = 128
MLA_SCALE = 96.0 ** -0.5
LOG2E = 1.4426950408889634
ROPE_THETA = 10000.0
GM_CHUNK = 128
GDN_CHUNK = 64
QKV_DIM = 768
N_GROUPS = 4
EPG = 8
N_EXPERTS = 32
D_EXPERT = 256
PAGE = 128
VMEM_LIMIT = 56 * 1024 * 1024

NT = (((1,), (1,)), ((), ()))
TN = (((0,), (0,)), ((), ()))


def _cp(*sem):
    return pltpu.CompilerParams(dimension_semantics=sem, vmem_limit_bytes=VMEM_LIMIT)


def _rms(x, g):
    ms = jnp.sum(x * x, axis=-1, keepdims=True) * (1.0 / x.shape[-1])
    return x * lax.rsqrt(ms + EPS) * g


def _bdot(a, b):
    return jnp.dot(a.astype(BF16), b.astype(BF16), preferred_element_type=F32)


def _bdot_nt(a, b):
    return lax.dot_general(a.astype(BF16), b.astype(BF16), NT, preferred_element_type=F32)


def _split2(a):
    hi = a.astype(BF16)
    return hi, (a - hi.astype(F32)).astype(BF16)


def _dot3(a, b):
    d = lambda x, y: jnp.dot(x, y, preferred_element_type=F32)
    return d(a[0], b[0]) + (d(a[0], b[1]) + d(a[1], b[0]))


def _sel_dot(w01, x, left):
    x0 = x.astype(BF16)
    r1 = x - x0.astype(F32)
    x1 = r1.astype(BF16)
    x2 = (r1 - x1.astype(F32)).astype(BF16)
    d = (lambda p: jnp.dot(w01, p, preferred_element_type=F32)) if left else (lambda p: jnp.dot(p, w01, preferred_element_type=F32))
    return d(x0) + (d(x1) + d(x2))


def _silu(x):
    return x * jax.nn.sigmoid(x)


def _gelu(x):
    return 0.5 * x * (1.0 + lax.erf(x * 0.7071067811865476))


def _lane_head(width=BW):
    return lax.broadcasted_iota(jnp.int32, (1, width), 1) // HD


def _full(shape):
    n = len(shape)
    return pl.BlockSpec(shape, lambda *_: (0,) * n)


def _block_mean(width, segs):
    m = np.zeros((width, width), np.float32)
    for a, b in segs:
        m[a:b, a:b] = 1.0 / (b - a)
    return m


_BMEAN64 = _block_mean(BW, [(64 * h, 64 * h + 64) for h in range(HEADS)])
_BONES64 = _BMEAN64 * 64.0
_BQ = _block_mean(512, [(128 * h, 128 * h + 64) for h in range(HEADS)] + [(128 * h + 64, 128 * h + 96) for h in range(HEADS)])
_BK = _block_mean(512, [(128 * h, 128 * h + 64) for h in range(HEADS)])
_IND8 = np.zeros((8, BW), np.float32)
for _h in range(HEADS):
    _IND8[_h, 64 * _h:64 * _h + 64] = 1.0
_EXPB = np.zeros((128, BW), np.float32)
_EXPG = np.zeros((128, BW), np.float32)
for _h in range(HEADS):
    _EXPB[_h, 64 * _h:64 * _h + 64] = 1.0
    _EXPG[4 + _h, 64 * _h:64 * _h + 64] = 1.0
_BDMASK = (_BONES64 > 0).astype(np.float32)


def _in_proj_kernel(x_ref, g_ref, w_ref, wbat_ref, oa_ref, ob_ref, oc_ref, om_ref, oba_ref, obat_ref):
    hb = _rms(x_ref[...], g_ref[...]).astype(BF16)
    oa_ref[...] = jnp.dot(hb, w_ref[:, 0:512], preferred_element_type=F32)
    ob_ref[...] = jnp.dot(hb, w_ref[:, 512:1024], preferred_element_type=F32)
    oc_ref[...] = jnp.dot(hb, w_ref[:, 1024:2048], preferred_element_type=F32)
    om_ref[...] = jnp.dot(hb, w_ref[:, 2048:2304], preferred_element_type=F32)
    oba_ref[...] = jnp.dot(hb, w_ref[:, 2304:2432], preferred_element_type=F32)
    obat_ref[...] = lax.dot_general(wbat_ref[...], hb, NT, preferred_element_type=F32)


def _in_proj(x, g, w, wbat, tm):
    m = x.shape[0]
    widths = (512, 512, 1024, 256, 128)
    return pl.pallas_call(
        _in_proj_kernel,
        grid=(m // tm,),
        in_specs=[pl.BlockSpec((tm, D_MODEL), lambda i: (i, 0)), _full((1, D_MODEL)), _full(w.shape), _full(wbat.shape)],
        out_specs=[pl.BlockSpec((tm, n), lambda i: (i, 0)) for n in widths] + [pl.BlockSpec((8, tm), lambda i: (0, i))],
        out_shape=[jax.ShapeDtypeStruct((m, n), F32) for n in widths] + [jax.ShapeDtypeStruct((8, m), F32)],
        compiler_params=_cp("parallel"),
        name="in_proj",
    )(x, g, w, wbat)


def _gmlp_body(za, g_ref, ws_ref, b_ref, o_ref):
    row = lax.broadcasted_iota(jnp.int32, (GM_CHUNK, GM_CHUNK), 0)
    col = lax.broadcasted_iota(jnp.int32, (GM_CHUNK, GM_CHUNK), 1)
    tril = col <= row
    lh = _lane_head()
    wts = [jnp.where(tril, ws_ref[g], 0.0).astype(BF16) for g in range(HEADS)]
    for c in range(za.shape[0] // GM_CHUNK):
        sl = slice(c * GM_CHUNK, (c + 1) * GM_CHUNK)
        ge = _gelu(za[sl, :])
        u = ge[:, :BW]
        vb = _rms(ge[:, BW:], g_ref[...]).astype(BF16)
        s = b_ref[...]
        for g in range(HEADS):
            s = s + jnp.where(lh == g, jnp.dot(wts[g], vb, preferred_element_type=F32), 0.0)
        o_ref[sl, :] = u * s


def _mla_pre_body(z, c_ref, s1_ref, s2_ref, gcq_ref, wuq_ref, qg_ref, bq_ref, gckv_ref, gkr_ref, wuk_ref, kg_ref, bk_ref,
                  oq_ref, ok_ref, ockv_ref, okr_ref, ockvt_ref, oq32_ref, tb):
    cs, s1, s2 = c_ref[...], s1_ref[...], s2_ref[...]

    def rope(x):
        return x * cs + pltpu.roll(x, 112, 1) * s1 + pltpu.roll(x, 16, 1) * s2

    cq = _rms(z[:, 0:256], gcq_ref[...])
    q = _bdot(cq, wuq_ref[...])
    qn = q * lax.rsqrt(_bdot(q * q, bq_ref[...]) + EPS) * qg_ref[...]
    ckv = _rms(z[:, 256:384], gckv_ref[...])
    ockv_ref[...] = ckv
    for c in range(ckv.shape[0] // tb):
        ockvt_ref[c] = ckv[c * tb:(c + 1) * tb, :].T.astype(BF16)
    krb = z[:, 384:512]
    kr = rope(krb * lax.rsqrt(jnp.sum(krb * krb, axis=-1, keepdims=True) * (1.0 / MLA_ROPE) + EPS) * gkr_ref[...])
    okr_ref[...] = kr[:, 64:96]
    k = _bdot(ckv, wuk_ref[...])
    kn = k * lax.rsqrt(_bdot(k * k, bk_ref[...]) + EPS) * kg_ref[...]
    for h in range(HEADS):
        sl = slice(128 * h, 128 * h + 128)
        qh = rope(qn[:, sl])
        oq_ref[h] = (qh * (MLA_SCALE * LOG2E)).astype(BF16)
        if oq32_ref is not None:
            oq32_ref[:, sl] = qh
        ok_ref[h] = (kn[:, sl] + kr).astype(BF16)


def _mla_pre_kernel(z_ref, *refs, tb):
    _mla_pre_body(z_ref[...], *refs, tb)


def _mla_consts(p):
    return [p["mla_cq_g"], p["w_uq"], p["qg"], jnp.asarray(_BQ, BF16), p["mla_ckv_g"], p["kr_g"], p["w_uk"], p["kg"],
            jnp.asarray(_BK, BF16)]


def _front_kernel(x_ref, g_ref, w_ref, wbat_ref, gmg_ref, gmws_ref, gmb_ref, *refs, tb):
    mla_in, (oa_ref, oq_ref, ok_ref, ockv_ref, okr_ref, ockvt_ref, oc_ref, om_ref, oba_ref, obat_ref) = refs[:12], refs[12:]
    hb = _rms(x_ref[...], g_ref[...]).astype(BF16)
    _gmlp_body(jnp.dot(hb, w_ref[:, 0:512], preferred_element_type=F32), gmg_ref, gmws_ref, gmb_ref, oa_ref)
    _mla_pre_body(jnp.dot(hb, w_ref[:, 512:1024], preferred_element_type=F32), *mla_in,
                  oq_ref, ok_ref, ockv_ref, okr_ref, ockvt_ref, None, tb)
    oc_ref[...] = jnp.dot(hb, w_ref[:, 1024:2048], preferred_element_type=F32)
    om_ref[...] = jnp.dot(hb, w_ref[:, 2048:2304], preferred_element_type=F32)
    oba_ref[...] = jnp.dot(hb, w_ref[:, 2304:2432], preferred_element_type=F32)
    obat_ref[...] = lax.dot_general(wbat_ref[...], hb, NT, preferred_element_type=F32)


def _front(x, tabs, t_blocks, p, tm, tb):
    m = x.shape[0]
    tab_spec = pl.BlockSpec((tm, 128), lambda i: (i % t_blocks, 0))
    consts = [p["norm1_g"], p["w_small"], p["w_bat"], p["gm_norm_g"], p["gm_ws"], p["gm_bfull"]]
    mla = _mla_consts(p)
    tile = lambda w, dt=F32: (pl.BlockSpec((tm, w), lambda i: (i, 0)), jax.ShapeDtypeStruct((m, w), dt))
    head = (pl.BlockSpec((HEADS, tm, 128), lambda i: (0, i, 0)), jax.ShapeDtypeStruct((HEADS, m, 128), BF16))
    outs = [tile(BW), head, head, tile(128), tile(MLA_ROPE),
            (pl.BlockSpec((tm // tb, 128, tb), lambda i: (i, 0, 0)), jax.ShapeDtypeStruct((m // tb, 128, tb), BF16)),
            tile(1024), tile(BW), tile(128), (pl.BlockSpec((8, tm), lambda i: (0, i)), jax.ShapeDtypeStruct((8, m), F32))]
    return pl.pallas_call(
        functools.partial(_front_kernel, tb=tb),
        grid=(m // tm,),
        in_specs=[pl.BlockSpec((tm, D_MODEL), lambda i: (i, 0))] + [_full(c.shape) for c in consts] + [tab_spec] * 3
        + [_full(c.shape) for c in mla],
        out_specs=[o[0] for o in outs],
        out_shape=[o[1] for o in outs],
        compiler_params=_cp("parallel"),
        name="front",
    )(x, *consts, *tabs, *mla)


def _mla_pre(zb, m, tabs, t_blocks, p, tm, tb):
    cs, s1, s2 = tabs
    tab_spec = pl.BlockSpec((tm, 128), lambda i: (i % t_blocks, 0))
    consts = _mla_consts(p)
    return pl.pallas_call(
        functools.partial(_mla_pre_kernel, tb=tb),
        grid=(m // tm,),
        in_specs=[pl.BlockSpec((tm, 512), lambda i: (i, 0)), tab_spec, tab_spec, tab_spec] + [_full(c.shape) for c in consts],
        out_specs=[pl.BlockSpec((HEADS, tm, 128), lambda i: (0, i, 0)), pl.BlockSpec((HEADS, tm, 128), lambda i: (0, i, 0)),
                   pl.BlockSpec((tm, 128), lambda i: (i, 0)), pl.BlockSpec((tm, MLA_ROPE), lambda i: (i, 0)),
                   pl.BlockSpec((tm // tb, 128, tb), lambda i: (i, 0, 0)), pl.BlockSpec((tm, 512), lambda i: (i, 0))],
        out_shape=[jax.ShapeDtypeStruct((HEADS, m, 128), BF16), jax.ShapeDtypeStruct((HEADS, m, 128), BF16),
                   jax.ShapeDtypeStruct((m, 128), F32), jax.ShapeDtypeStruct((m, MLA_ROPE), F32),
                   jax.ShapeDtypeStruct((m // tb, 128, tb), BF16), jax.ShapeDtypeStruct((m, 512), F32)],
        compiler_params=_cp("parallel"),
        name="mla_pre",
    )(zb, cs, s1, s2, *consts)


def _mla_attn_kernel(q_ref, k_ref, v_ref, wuv_ref, o_ref, *, tq):
    i = pl.program_id(1)
    row = lax.broadcasted_iota(jnp.int32, (tq, tq), 0)
    col = lax.broadcasted_iota(jnp.int32, (tq, tq), 1)
    causal = row <= col

    def step(j, carry, mask):
        off = pl.multiple_of(j * tq, tq)
        vt = v_ref[j]
        ss = [lax.dot_general(k_ref[h, pl.ds(off, tq), :], q_ref[h], NT, preferred_element_type=F32) for h in range(HEADS)]
        stats = []
        for h in range(HEADS):
            m, l, _ = carry[h]
            s = ss[h]
            if mask:
                s = jnp.where(causal, s, NEG)
            mn = jnp.maximum(m, jnp.max(s, axis=0, keepdims=True))
            pr = jnp.exp2(s - mn)
            al = jnp.exp2(m - mn)
            stats.append((mn, al * l + jnp.sum(pr, axis=0, keepdims=True), al, pr.astype(BF16)))
        pvs = [jnp.dot(vt, stats[h][3], preferred_element_type=F32) for h in range(HEADS)]
        return tuple((stats[h][0], stats[h][1], stats[h][2] * carry[h][2] + pvs[h]) for h in range(HEADS))

    init = tuple((jnp.full((1, tq), NEG, F32), jnp.zeros((1, tq), F32), jnp.zeros((MLA_KV_RANK, tq), F32)) for _ in range(HEADS))
    carry = lax.fori_loop(0, i, lambda j, c: step(j, c, False), init)
    carry = step(i, carry, True)
    out = jnp.zeros((tq, BW), F32)
    for h in range(HEADS):
        m, l, acc = carry[h]
        out = out + lax.dot_general((acc / l).astype(BF16), wuv_ref[h], TN, preferred_element_type=F32)
    o_ref[...] = out


def _mla_attn(q4, k4, ckvt, wuv, n, t, tq):
    nq = t // tq
    return pl.pallas_call(
        functools.partial(_mla_attn_kernel, tq=tq),
        grid=(n, nq),
        in_specs=[pl.BlockSpec((HEADS, tq, 128), lambda b, i: (0, b * nq + i, 0)),
                  pl.BlockSpec((HEADS, t, 128), lambda b, i: (0, b, 0)),
                  pl.BlockSpec((nq, MLA_KV_RANK, tq), lambda b, i: (b, 0, 0)), _full(wuv.shape)],
        out_specs=pl.BlockSpec((tq, BW), lambda b, i: (b * nq + i, 0)),
        out_shape=jax.ShapeDtypeStruct((n * t, BW), F32),
        compiler_params=_cp("parallel", "arbitrary"),
        name="mla_attn",
    )(q4, k4, ckvt, wuv)


def _mla_decode_kernel(pt_ref, qk_ref, qr_ref, cnew_ref, krnew_ref, kng_ref, wukt_ref, wuv_ref, ind_ref, ckv_hbm, krt_hbm, o_ref,
                       cbuf, kbuf, sem, *, layer, npages, pp):
    n = pl.program_id(0)
    ngroups = npages // pp

    def page_copies(sample, grp, slot):
        base = sample * npages + grp * pp
        out = []
        for i in range(pp):
            page = pt_ref[base + i]
            out.append(pltpu.make_async_copy(ckv_hbm.at[layer, page], cbuf.at[slot, pl.ds(i * PAGE, PAGE)], sem.at[slot]))
            out.append(pltpu.make_async_copy(krt_hbm.at[layer, page], kbuf.at[slot, :, pl.ds(i * PAGE, PAGE)], sem.at[slot]))
        return out

    @pl.when(n == 0)
    def _():
        for cp in page_copies(0, 0, 0):
            cp.start()

    ind = ind_ref[...]
    qbd = (ind * (qk_ref[...] * kng_ref[...])).astype(BF16)
    qr = qr_ref[...].astype(BF16)
    wukt = wukt_ref[...]
    qabs = jnp.dot(qbd, wukt, preferred_element_type=F32).astype(BF16)
    w2 = jnp.concatenate([wukt, qabs, jnp.zeros((8, MLA_KV_RANK), BF16)], axis=0)

    def head_ms(kk):
        rows = [jnp.sum(kk[HD * h:HD * (h + 1)], axis=0, keepdims=True) for h in range(HEADS)]
        return jnp.concatenate(rows + [jnp.ones((8 - HEADS, kk.shape[1]), F32)], axis=0) * (1.0 / HD)

    m = jnp.full((8, 1), NEG, F32)
    l = jnp.zeros((8, 1), F32)
    acc = jnp.zeros((8, MLA_KV_RANK), F32)
    for grp in range(ngroups):
        slot = (n * ngroups + grp) % 2
        if grp + 1 < ngroups:
            for cp in page_copies(n, grp + 1, 1 - slot):
                cp.start()
        else:
            @pl.when(n + 1 < pl.num_programs(0))
            def _():
                for cp in page_copies(n + 1, 0, 1 - slot):
                    cp.start()
        for cp in page_copies(n, grp, slot):
            cp.wait()
        cb = cbuf[slot].astype(BF16)
        krt = kbuf[slot].astype(BF16)
        kq = lax.dot_general(w2, cb, NT, preferred_element_type=F32)
        kt = kq[0:BW]
        num = kq[BW:BW + 8]
        s = (num * lax.rsqrt(head_ms(kt * kt) + EPS) + jnp.dot(qr, krt, preferred_element_type=F32)) * MLA_SCALE
        mn = jnp.maximum(m, jnp.max(s, axis=-1, keepdims=True))
        pb = jnp.exp(s - mn)
        al = jnp.exp(m - mn)
        l = al * l + jnp.sum(pb, axis=-1, keepdims=True)
        acc = al * acc + jnp.dot(pb.astype(BF16), cb, preferred_element_type=F32)
        m = mn

    cb = jnp.broadcast_to(cnew_ref[...], (8, MLA_KV_RANK)).astype(BF16)
    k1 = lax.dot_general(cb, wukt, NT, preferred_element_type=F32)
    krn = krnew_ref[...].astype(BF16).astype(F32)
    num1 = jnp.sum(qabs.astype(F32) * cb.astype(F32), axis=-1, keepdims=True)
    ms1 = jnp.sum(ind * (k1 * k1), axis=-1, keepdims=True) * (1.0 / HD)
    s1 = (num1 * lax.rsqrt(ms1 + EPS) + jnp.sum(qr.astype(F32) * krn, axis=-1, keepdims=True)) * MLA_SCALE
    mn1 = jnp.maximum(m, s1)
    p1 = jnp.exp(s1 - mn1)
    al1 = jnp.exp(m - mn1)
    lat = (al1 * acc + p1 * cb.astype(F32)) / (al1 * l + p1)
    o8 = jnp.dot(lat.astype(BF16), wuv_ref[...], preferred_element_type=F32)
    o_ref[...] = jnp.sum(o8 * ind, axis=0, keepdims=True)


def _mla_decode(layer, page_table, cache_ckv, cache_krt, qk, qr8, cnew, krnew, kng, wuk, wuv, pp):
    ns, npages = page_table.shape
    pt = page_table.reshape(-1)

    def per_sample(shape):
        return pl.BlockSpec((None,) + shape, lambda n, pt_ref: (n, 0, 0))

    def const(a):
        nd = a.ndim
        return pl.BlockSpec(a.shape, lambda n, pt_ref: (0,) * nd)

    ind = jnp.asarray(_IND8)
    hbm = pl.BlockSpec(memory_space=pl.ANY)
    grid_spec = pltpu.PrefetchScalarGridSpec(
        num_scalar_prefetch=1,
        grid=(ns,),
        in_specs=[per_sample((1, BW)), per_sample((8, MLA_ROPE)), per_sample((1, MLA_KV_RANK)), per_sample((1, MLA_ROPE)),
                  const(kng), const(wuk), const(wuv), const(ind), hbm, hbm],
        out_specs=per_sample((1, BW)),
        scratch_shapes=[pltpu.VMEM((2, pp * PAGE, MLA_KV_RANK), F32), pltpu.VMEM((2, MLA_ROPE, pp * PAGE), F32),
                        pltpu.SemaphoreType.DMA((2,))],
    )
    return pl.pallas_call(
        functools.partial(_mla_decode_kernel, layer=layer, npages=npages, pp=pp),
        grid_spec=grid_spec,
        out_shape=jax.ShapeDtypeStruct((ns, 1, BW), F32),
        compiler_params=_cp("arbitrary"),
        name="mla_decode",
    )(pt, qk, qr8, cnew, krnew, kng, wuk, wuv, ind, cache_ckv, cache_krt)


def _gdn_kernel(zc_ref, zba_ref, bat_ref, cw_ref, parr_ref, parc_ref, gout_ref, bones_ref, expb_ref, expg_ref, bdm_ref,
                o_ref, sfin_ref, xbuf, s_sc, *, tg):
    t = pl.program_id(1)
    c = GDN_CHUNK

    @pl.when(t == 0)
    def _():
        xbuf[0:8, :] = jnp.zeros((8, QKV_DIM), F32)
        s_sc[...] = jnp.zeros(s_sc.shape, F32)

    @pl.when(t > 0)
    def _():
        xbuf[5:8, :] = xbuf[tg + 5:tg + 8, :]

    xbuf[8:8 + tg, :] = zc_ref[:, 0:QKV_DIM]
    y = cw_ref[0:1, :] * xbuf[5:5 + tg, :]
    for i in range(1, 4):
        y = y + cw_ref[i:i + 1, :] * xbuf[5 + i:5 + i + tg, :]
    y = _silu(y)

    zba = zba_ref[...]
    beta_col = jax.nn.sigmoid(zba)
    g_col = -jnp.exp(parr_ref[0:1, :]) * jax.nn.softplus(zba + parr_ref[1:2, :])
    g_row = -jnp.exp(parc_ref[:, 0:1]) * jax.nn.softplus(bat_ref[...] + parc_ref[:, 1:2])

    row = lax.broadcasted_iota(jnp.int32, (c, c), 0)
    col = lax.broadcasted_iota(jnp.int32, (c, c), 1)
    incl = col <= row
    strict = col < row
    lt = incl.astype(BF16)
    ut = (col >= row).astype(BF16)
    bones = bones_ref[...]
    bdm = bdm_ref[...]
    expb, expg = expb_ref[...], expg_ref[...]
    brow = lax.broadcasted_iota(jnp.int32, (BW, BW), 0)
    bcl = lax.broadcasted_iota(jnp.int32, (BW, BW), 1)
    same_head = (brow // HD) == (bcl // HD)
    incl_bd = same_head & ((bcl % HD) <= (brow % HD))
    strict_bd = same_head & ((bcl % HD) < (brow % HD))
    eye_bd = (brow == bcl).astype(F32)

    def same_block(b):
        return (brow // b) == (bcl // b)
    same_head2 = jnp.concatenate([same_head, same_head], axis=1)

    def stack4(a):
        return jnp.concatenate([a, a, a, a], axis=0)

    def fold4(a):
        return (a[0:c] + a[c:2 * c]) + (a[2 * c:3 * c] + a[3 * c:4 * c])

    nchunk = tg // c
    pre = []
    for ci in range(nchunk):
        sl = slice(ci * c, (ci + 1) * c)
        q, k, v = y[sl, 0:256], y[sl, 256:512], y[sl, 512:768]
        qn = q * lax.rsqrt(_bdot(q * q, bones) + EPS) * (HD ** -0.5)
        kn = k * lax.rsqrt(_bdot(k * k, bones) + EPS)
        bcol = beta_col[sl, :]
        gcum_c = _sel_dot(lt, g_col[sl, :], True)
        gcum_r = _sel_dot(ut, g_row[:, sl], False)
        gx = _sel_dot(expg, gcum_c, False)
        bx = _sel_dot(expb, bcol, False)
        egx = jnp.exp(gx)
        rhs = _split2(jnp.concatenate([bx * v, bx * egx * kn], axis=1))
        ks = jnp.where(same_head, stack4(kn), 0.0).astype(BF16)
        qs = jnp.where(same_head, stack4(qn), 0.0).astype(BF16)
        gc_s = jnp.concatenate([gcum_c[:, 4 + h:5 + h] for h in range(HEADS)], axis=0)
        gr_s = jnp.concatenate([gcum_r[4 + h:5 + h, :] for h in range(HEADS)], axis=1)
        beta_s = jnp.concatenate([bcol[:, h:h + 1] for h in range(HEADS)], axis=0)
        dm = jnp.exp(jnp.where(incl_bd, gc_s - gr_s, NEG))
        a = jnp.where(strict_bd, beta_s * lax.dot_general(ks, ks, NT, preferred_element_type=F32) * dm, 0.0)
        qk = (lax.dot_general(qs, ks, NT, preferred_element_type=F32) * dm).astype(BF16)
        pre.append(dict(sl=sl, qn=qn, kn=kn, gx=gx, egx=egx, rhs=rhs, qk=qk, a=_split2(a),
                        tinv=eye_bd - jnp.where(same_block(2), a, 0.0)))

    b = 2
    while b < c:
        off = same_block(2 * b) & jnp.logical_not(same_block(b))
        for d in pre:
            d["ts"] = _split2(d["tinv"])
            d["w"] = _split2(_dot3((jnp.where(off, d["a"][0], 0.0), jnp.where(off, d["a"][1], 0.0)), d["ts"]))
        for d in pre:
            d["tinv"] = d["tinv"] - _dot3(d["ts"], d["w"])
        b *= 2
    for d in pre:
        rhs = d["rhs"]
        d["x"] = fold4(jnp.where(same_head2, _dot3(_split2(d["tinv"]), (stack4(rhs[0]), stack4(rhs[1]))), 0.0))

    for d in pre:
        sl, qn, kn, gx, egx, qk, x = d["sl"], d["qn"], d["kn"], d["gx"], d["egx"], d["qk"], d["x"]
        s = s_sc[...]
        sb = s.astype(BF16)
        u = x[:, :BW] - jnp.dot(x[:, BW:].astype(BF16), sb, preferred_element_type=F32)
        ub = u.astype(BF16)
        o = egx * jnp.dot(qn.astype(BF16), sb, preferred_element_type=F32)
        o = o + fold4(jnp.where(same_head, jnp.dot(qk, stack4(ub), preferred_element_type=F32), 0.0))
        glast = gx[c - 1:c, :]
        kf = (kn * jnp.exp(glast - gx)).astype(BF16)
        s_new = jnp.exp(glast) * s + lax.dot_general(kf, ub, TN, preferred_element_type=F32)
        s_sc[...] = s_new * bdm
        on = o * lax.rsqrt(_bdot(o * o, bones) * (1.0 / HD) + EPS) * gout_ref[...]
        o_ref[sl, :] = on * _silu(zc_ref[sl, QKV_DIM:QKV_DIM + BW])

    @pl.when(t == pl.num_programs(1) - 1)
    def _():
        sfin_ref[...] = s_sc[...]


def _gdn_prompt(zc, zba, bat, n, t, p, tg):
    nt = t // tg
    consts = [p["conv_w"], p["gdn_par_r"], p["gdn_par_c"], p["gdn_out_gx"], jnp.asarray(_BONES64, BF16), jnp.asarray(_EXPB, BF16),
              jnp.asarray(_EXPG, BF16), jnp.asarray(_BDMASK)]
    return pl.pallas_call(
        functools.partial(_gdn_kernel, tg=tg),
        grid=(n, nt),
        in_specs=[pl.BlockSpec((tg, 1024), lambda b, i: (b * nt + i, 0)), pl.BlockSpec((tg, 128), lambda b, i: (b * nt + i, 0)),
                  pl.BlockSpec((8, tg), lambda b, i: (0, b * nt + i))] + [_full(c.shape) for c in consts],
        out_specs=[pl.BlockSpec((tg, BW), lambda b, i: (b * nt + i, 0)), pl.BlockSpec((None, BW, BW), lambda b, i: (b, 0, 0))],
        out_shape=[jax.ShapeDtypeStruct((n * t, BW), F32), jax.ShapeDtypeStruct((n, BW, BW), F32)],
        scratch_shapes=[pltpu.VMEM((8 + tg, QKV_DIM), F32), pltpu.VMEM((BW, BW), F32)],
        compiler_params=_cp("parallel", "arbitrary"),
        name="gdn_prompt",
    )(zc, zba, bat, *consts)


def _sample_tok_kernel(za_ref, zc_ref, zba_ref, zm_ref, sconv_ref, cw_ref, gmg_ref, gmw_ref, gmb_ref, parr_ref, bones_ref,
                       memg_ref, bmean_ref, oa_ref, ov_ref, oconv_ref, oq_ref, ok_ref, ovv_ref, ozg_ref, obeta_ref, og_ref, omq_ref):
    ge = _gelu(za_ref[...])
    v = _rms(ge[:, BW:], gmg_ref[...])
    ov_ref[...] = v
    oa_ref[...] = ge[:, :BW] * (gmw_ref[...] * v + gmb_ref[...])
    sc = sconv_ref[...]
    x = zc_ref[:, 0:QKV_DIM]
    y = (cw_ref[0:1, :] * sc[:, 0:768] + cw_ref[1:2, :] * sc[:, 768:1536] + cw_ref[2:3, :] * sc[:, 1536:2304]
         + cw_ref[3:4, :] * x)
    oconv_ref[:, 0:1536] = sc[:, 768:2304]
    oconv_ref[:, 1536:2304] = x
    y = _silu(y)
    q, k = y[:, 0:256], y[:, 256:512]
    bones = bones_ref[...]
    oq_ref[...] = (q * lax.rsqrt(_bdot(q * q, bones) + EPS) * (HD ** -0.5)).T
    ok_ref[...] = (k * lax.rsqrt(_bdot(k * k, bones) + EPS)).T
    ovv_ref[...] = y[:, 512:768].T
    ozg_ref[...] = zc_ref[:, QKV_DIM:QKV_DIM + BW].T
    zba = zba_ref[...]
    obeta_ref[...] = jax.nn.sigmoid(zba).T
    og_ref[...] = (-jnp.exp(parr_ref[0:1, :]) * jax.nn.softplus(zba + parr_ref[1:2, :])).T
    mq = zm_ref[...]
    omq_ref[...] = mq * lax.rsqrt(_bdot(mq * mq, bmean_ref[...]) + EPS) * memg_ref[...]


def _sample_tok(za, zc, zba, zm, sconv, p):
    ns = za.shape[0]
    args = [za, zc, zba, zm, sconv, p["conv_w"], p["gm_norm_g"], p["gm_w0"], p["gm_b0"], p["gdn_par_r"],
            jnp.asarray(_BONES64, BF16), p["mem_qn_gx"], jnp.asarray(_BMEAN64, BF16)]
    shapes = [(ns, BW), (ns, BW), (ns, 2304), (BW, ns), (BW, ns), (BW, ns), (BW, ns), (128, ns), (128, ns), (ns, BW)]
    return pl.pallas_call(
        _sample_tok_kernel,
        in_specs=[_full(a.shape) for a in args],
        out_specs=[_full(s) for s in shapes],
        out_shape=[jax.ShapeDtypeStruct(s, F32) for s in shapes],
        grid=(1,),
        compiler_params=_cp("arbitrary"),
        name="sample_tok",
    )(*args)


def _gdn_step_kernel(s_ref, q_ref, k_ref, v_ref, beta_ref, g_ref, zg_ref, gout_ref, so_ref, o_ref, o_sc):
    q, k = q_ref[...], k_ref[...]
    eg = jnp.exp(g_ref[...])
    beta = beta_ref[...]
    qk = jnp.sum(q * k, axis=0, keepdims=True)
    ssq = jnp.zeros(qk.shape, F32)
    for v in range(HD):
        sv = s_ref[v]
        sk = jnp.sum(sv * k, axis=0, keepdims=True)
        sq = jnp.sum(sv * q, axis=0, keepdims=True)
        u = beta * (v_ref[v:v + 1, :] - eg * sk)
        o = eg * sq + qk * u
        so_ref[v] = eg * sv + u * k
        o_sc[v:v + 1, :] = o
        ssq = ssq + o * o
    o_ref[...] = o_sc[...] * lax.rsqrt(ssq * (1.0 / HD) + EPS) * gout_ref[...] * _silu(zg_ref[...])


def _gdn_step(layer, state_t, q, k, v, beta, g, zg, gout_col):
    ns = state_t.shape[-1]
    vec = pl.BlockSpec((None, HD, ns), lambda h: (h, 0, 0))
    sca = pl.BlockSpec((None, 1, ns), lambda h: (h, 0, 0))
    return pl.pallas_call(
        _gdn_step_kernel,
        grid=(HEADS,),
        in_specs=[pl.BlockSpec((None, None, HD, HD, ns), lambda h: (layer, h, 0, 0, 0)), vec, vec, vec, sca, sca, vec, _full((HD, 1))],
        out_specs=[pl.BlockSpec((None, HD, HD, ns), lambda h: (h, 0, 0, 0)), vec],
        out_shape=[jax.ShapeDtypeStruct((HEADS, HD, HD, ns), F32), jax.ShapeDtypeStruct((HEADS, HD, ns), F32)],
        scratch_shapes=[pltpu.VMEM((HD, ns), F32)],
        compiler_params=_cp("parallel"),
        name="gdn_step",
    )(state_t, q, k, v, beta, g, zg, gout_col)


def _mem_kv_kernel(x_ref, g_ref, w_ref, kg_ref, bmean_ref, ok_ref, ov_ref):
    kv = _bdot(_rms(x_ref[...], g_ref[...]), w_ref[...])
    k = kv[:, 0:BW]
    ok_ref[...] = k * lax.rsqrt(_bdot(k * k, bmean_ref[...]) + EPS) * kg_ref[...]
    ov_ref[...] = kv[:, BW:]


def _mem_kv(mem, g, w, kgx, tm):
    m = mem.shape[0]
    bmean = jnp.asarray(_BMEAN64, BF16)
    return pl.pallas_call(
        _mem_kv_kernel,
        grid=(m // tm,),
        in_specs=[pl.BlockSpec((tm, D_MODEL), lambda i: (i, 0)), _full((1, D_MODEL)), _full(w.shape), _full((1, BW)), _full((BW, BW))],
        out_specs=[pl.BlockSpec((tm, BW), lambda i: (i, 0))] * 2,
        out_shape=[jax.ShapeDtypeStruct((m, BW), F32)] * 2,
        compiler_params=_cp("parallel"),
        name="mem_kv",
    )(mem, g, w, kgx, bmean)


def _mem_attn_kernel(q_ref, k_ref, v_ref, gq_ref, bmean_ref, o_ref):
    q = q_ref[...]
    qn = q * lax.rsqrt(_bdot(q * q, bmean_ref[...]) + EPS) * gq_ref[...]
    kb = k_ref[...].astype(BF16)
    vb = v_ref[...].astype(BF16)
    lh = _lane_head()
    out = jnp.zeros(q.shape, F32)
    for h in range(HEADS):
        mh = lh == h
        s = lax.dot_general(jnp.where(mh, qn, 0.0).astype(BF16), kb, NT, preferred_element_type=F32) * (HD ** -0.5)
        e = jnp.exp(s - jnp.max(s, axis=-1, keepdims=True))
        pr = e / jnp.sum(e, axis=-1, keepdims=True)
        out = out + jnp.where(mh, jnp.dot(pr.astype(BF16), vb, preferred_element_type=F32), 0.0)
    o_ref[...] = out


def _mem_attn(zm, mk, mv, gqx, n, t, mt, tq):
    nq = t // tq
    bmean = jnp.asarray(_BMEAN64, BF16)
    return pl.pallas_call(
        _mem_attn_kernel,
        grid=(n, nq),
        in_specs=[pl.BlockSpec((tq, BW), lambda b, i: (b * nq + i, 0)), pl.BlockSpec((mt, BW), lambda b, i: (b, 0)),
                  pl.BlockSpec((mt, BW), lambda b, i: (b, 0)), _full((1, BW)), _full((BW, BW))],
        out_specs=pl.BlockSpec((tq, BW), lambda b, i: (b * nq + i, 0)),
        out_shape=jax.ShapeDtypeStruct((n * t, BW), F32),
        compiler_params=_cp("parallel", "parallel"),
        name="mem_attn",
    )(zm, mk, mv, gqx, bmean)


def _mem_attn_s_kernel(q_ref, k_ref, v_ref, ind_ref, o_ref, *, bn):
    ind = ind_ref[...]
    for i in range(bn):
        qbd = (ind * q_ref[i:i + 1, :]).astype(BF16)
        s = jnp.dot(qbd, k_ref[i].astype(BF16), preferred_element_type=F32) * (HD ** -0.5)
        e = jnp.exp(s - jnp.max(s, axis=-1, keepdims=True))
        pr = e / jnp.sum(e, axis=-1, keepdims=True)
        o8 = lax.dot_general(pr.astype(BF16), v_ref[i].astype(BF16), NT, preferred_element_type=F32)
        o_ref[i:i + 1, :] = jnp.sum(o8 * ind, axis=0, keepdims=True)


def _mem_attn_s(layer, mqn, cache_k, cache_v, bn):
    ns = mqn.shape[0]
    mt = cache_k.shape[3]
    kv_spec = pl.BlockSpec((None, bn, BW, mt), lambda i: (layer, i, 0, 0))
    return pl.pallas_call(
        functools.partial(_mem_attn_s_kernel, bn=bn),
        grid=(ns // bn,),
        in_specs=[pl.BlockSpec((bn, BW), lambda i: (i, 0)), kv_spec, kv_spec, _full((8, BW))],
        out_specs=pl.BlockSpec((bn, BW), lambda i: (i, 0)),
        out_shape=jax.ShapeDtypeStruct((ns, BW), F32),
        compiler_params=_cp("parallel"),
        name="mem_attn_s",
    )(mqn, cache_k, cache_v, jnp.asarray(_IND8))


def _merge_kernel(x_ref, a_ref, b_ref, c_ref, m_ref, g1_ref, wg_ref, wb_ref, wo_ref, g2_ref, wr_ref, br_ref,
                  x1_ref, h2_ref, ei_ref, ew_ref):
    x = x_ref[...]
    hb = _rms(x, g1_ref[...]).astype(BF16)
    acc = jnp.zeros(x.shape, F32)
    for b, br in enumerate((a_ref, b_ref, c_ref, m_ref)):
        gate = jax.nn.sigmoid(jnp.dot(hb, wg_ref[:, b * D_MODEL:(b + 1) * D_MODEL], preferred_element_type=F32))
        acc = acc + gate * jnp.dot(br[...].astype(BF16), wb_ref[b], preferred_element_type=F32)
    x1 = x + jnp.dot(acc.astype(BF16), wo_ref[...], preferred_element_type=F32)
    x1_ref[...] = x1
    h2 = _rms(x1, g2_ref[...])
    h2_ref[...] = h2
    h2h, h2l = _split2(h2)
    r = jnp.dot(h2h, wr_ref[...], preferred_element_type=F32)
    logits = r[:, 0:128] + (r[:, 128:256] + jnp.dot(h2l, wr_ref[:, 0:128], preferred_element_type=F32)) + br_ref[...]
    lane = lax.broadcasted_iota(jnp.int32, (1, 128), 1).astype(F32)
    big = 1e9
    lg = jnp.where(lane < N_GROUPS, logits, NEG)
    mg = jnp.max(lg, axis=-1, keepdims=True)
    g_w = 1.0 / jnp.sum(jnp.exp(lg - mg), axis=-1, keepdims=True)
    gi = jnp.min(jnp.where(lg == mg, lane, big), axis=-1, keepdims=True)
    sel = (lane >= N_GROUPS) & (lane < N_GROUPS + N_EXPERTS) & (jnp.floor((lane - N_GROUPS) * (1.0 / EPG)) == gi)
    le = jnp.where(sel, logits, NEG)
    m1 = jnp.max(le, axis=-1, keepdims=True)
    i1 = jnp.min(jnp.where(le == m1, lane, big), axis=-1, keepdims=True)
    le2 = jnp.where(lane == i1, NEG, le)
    m2 = jnp.max(le2, axis=-1, keepdims=True)
    i2 = jnp.min(jnp.where(le2 == m2, lane, big), axis=-1, keepdims=True)
    z = jnp.sum(jnp.exp(le - m1), axis=-1, keepdims=True)
    p1 = 1.0 / z
    p2 = jnp.exp(m2 - m1) / z
    w1 = p1 / (p1 + p2) * g_w
    w2 = p2 / (p1 + p2) * g_w
    ei_ref[...] = jnp.where(lane == 0, i1 - N_GROUPS, jnp.where(lane == 1, i2 - N_GROUPS, 0.0)).astype(jnp.int32)
    ew_ref[...] = jnp.where(lane == 0, w1, jnp.where(lane == 1, w2, 0.0))


def _merge(x, branches, p, tm):
    m = x.shape[0]
    consts = [p["norm1_g"], p["w_gate"], p["w_branch"], p["w_out"], p["norm2_g"], jnp.concatenate(_split2(p["w_router"]), axis=1), p["b_router"]]
    tile = lambda w: pl.BlockSpec((tm, w), lambda i: (i, 0))
    return pl.pallas_call(
        _merge_kernel,
        grid=(m // tm,),
        in_specs=[tile(D_MODEL)] + [tile(BW)] * 4 + [_full(c.shape) for c in consts],
        out_specs=[tile(D_MODEL), tile(D_MODEL), tile(128), tile(128)],
        out_shape=[jax.ShapeDtypeStruct((m, D_MODEL), F32), jax.ShapeDtypeStruct((m, D_MODEL), F32),
                   jax.ShapeDtypeStruct((m, 128), jnp.int32), jax.ShapeDtypeStruct((m, 128), F32)],
        compiler_params=_cp("parallel"),
        name="merge",
    )(x, *branches, *consts)


def _dispatch_kernel(poff_ref, plen_ref, nv_ref, pos_ref, h_ref, posb_ref, hb_ref, xs_out, zbuf, sem, zsem, *, tmd, te, n1):
    @pl.when(pl.program_id(0) == 0)
    def _():
        zbuf[...] = jnp.zeros(zbuf.shape, F32)

        def pad_copies(e):
            off = poff_ref[e]
            head = (-off) & 7
            body = plen_ref[e] - head
            out = [(i < head, pltpu.make_async_copy(zbuf.at[pl.ds(0, 1)], xs_out.at[pl.ds(off + i, 1)], zsem)) for i in range(7)]
            b = te // 2
            while b >= 8:
                start = pl.multiple_of(off + head + (body & ~(2 * b - 1)), 8)
                out.append(((body & b) != 0, pltpu.make_async_copy(zbuf.at[pl.ds(0, b)], xs_out.at[pl.ds(start, b)], zsem)))
                b //= 2
            return out

        def tail_copies(t):
            return [pltpu.make_async_copy(zbuf, xs_out.at[pl.ds(pl.multiple_of(t * te + k * zbuf.shape[0], 8), zbuf.shape[0])], zsem)
                    for k in range(te // zbuf.shape[0])]

        def start_pad(e, carry):
            for cond, cp in pad_copies(e):
                pl.when(cond)(cp.start)
            return carry

        def wait_pad(e, carry):
            for cond, cp in pad_copies(e):
                pl.when(cond)(cp.wait)
            return carry

        def start_tail(t, carry):
            for cp in tail_copies(t):
                cp.start()
            return carry

        def wait_tail(t, carry):
            for cp in tail_copies(t):
                cp.wait()
            return carry

        n_tiles = xs_out.shape[0] // te
        lax.fori_loop(0, N_EXPERTS, start_pad, 0)
        lax.fori_loop(nv_ref[0], n_tiles, start_tail, 0)
        lax.fori_loop(0, N_EXPERTS, wait_pad, 0)
        lax.fori_loop(nv_ref[0], n_tiles, wait_tail, 0)

    def scatter(p_ref, src_ref, nrows):
        def issue(i, carry):
            for s in range(2):
                pltpu.make_async_copy(src_ref.at[pl.ds(i, 1)], xs_out.at[pl.ds(p_ref[2 * i + s], 1)], sem).start()
            return carry

        lax.fori_loop(0, nrows, issue, 0, unroll=16)
        for s in range(2):
            pltpu.make_async_copy(src_ref, xs_out.at[pl.ds(0, nrows)], sem).wait()

    pl.when(pl.program_id(0) < n1)(lambda: scatter(pos_ref, h_ref, tmd))
    pl.when(pl.program_id(0) == n1)(lambda: scatter(posb_ref, hb_ref, hb_ref.shape[0]))


def _dispatch(pad_off, pad_len, n_valid, pos, h2, posb, h2b, rows, tmd, te):
    n1 = h2.shape[0] // tmd
    mb = h2b.shape[0]
    grid_spec = pltpu.PrefetchScalarGridSpec(
        num_scalar_prefetch=3,
        grid=(n1 + 1,),
        in_specs=[pl.BlockSpec((2 * tmd,), lambda i, po, pn, nv: (jnp.minimum(i, n1 - 1),), memory_space=pltpu.SMEM),
                  pl.BlockSpec((tmd, D_MODEL), lambda i, po, pn, nv: (jnp.minimum(i, n1 - 1), 0)),
                  pl.BlockSpec((2 * mb,), lambda i, po, pn, nv: (0,), memory_space=pltpu.SMEM),
                  pl.BlockSpec((mb, D_MODEL), lambda i, po, pn, nv: (0, 0))],
        out_specs=pl.BlockSpec(memory_space=pl.ANY),
        scratch_shapes=[pltpu.VMEM((max(te // 2, 8), D_MODEL), F32), pltpu.SemaphoreType.DMA(()), pltpu.SemaphoreType.DMA(())],
    )
    return pl.pallas_call(
        functools.partial(_dispatch_kernel, tmd=tmd, te=te, n1=n1),
        grid_spec=grid_spec,
        out_shape=jax.ShapeDtypeStruct((rows, D_MODEL), F32),
        compiler_params=_cp("arbitrary"),
        name="moe_dispatch",
    )(pad_off, pad_len, n_valid, pos, h2, posb, h2b)


def _expert_kernel(te_ref, nv_ref, x_ref, wg_ref, wu_ref, wd_ref, o_ref):
    del te_ref

    @pl.when(pl.program_id(0) < nv_ref[0])
    def _():
        xb = x_ref[...].astype(BF16)
        gt = jnp.dot(xb, wg_ref[...].astype(BF16), preferred_element_type=F32)
        up = jnp.dot(xb, wu_ref[...].astype(BF16), preferred_element_type=F32)
        o_ref[...] = jnp.dot((_silu(gt) * up).astype(BF16), wd_ref[...].astype(BF16), preferred_element_type=F32)

    @pl.when(pl.program_id(0) >= nv_ref[0])
    def _():
        o_ref[...] = jnp.zeros(o_ref.shape, F32)


def _experts(layer, tile_expert, n_valid, xs, w_gate, w_up, w_down, te):
    rows = xs.shape[0]

    def xmap(i, te_ref, nv_ref):
        return (jnp.minimum(i, nv_ref[0] - 1), 0)

    def wmap(i, te_ref, nv_ref):
        return (layer, te_ref[i], 0, 0)

    grid_spec = pltpu.PrefetchScalarGridSpec(
        num_scalar_prefetch=2,
        grid=(rows // te,),
        in_specs=[pl.BlockSpec((te, D_MODEL), xmap), pl.BlockSpec((None, None, D_MODEL, D_EXPERT), wmap),
                  pl.BlockSpec((None, None, D_MODEL, D_EXPERT), wmap), pl.BlockSpec((None, None, D_EXPERT, D_MODEL), wmap)],
        out_specs=pl.BlockSpec((te, D_MODEL), lambda i, te_ref, nv_ref: (i, 0)),
    )
    return pl.pallas_call(
        _expert_kernel,
        grid_spec=grid_spec,
        out_shape=jax.ShapeDtypeStruct((rows, D_MODEL), F32),
        compiler_params=_cp("arbitrary"),
        name="moe_experts",
    )(tile_expert, n_valid, xs, w_gate, w_up, w_down)


def _combine_kernel(pos_ref, posn_ref, x1_ref, ew_ref, ys_hbm, o_ref, rbuf, sem, *, tmc):
    i = pl.program_id(0)
    slot = i % 2

    def gather(p_ref, sl):
        def issue(r, carry):
            for s in range(2):
                pltpu.make_async_copy(ys_hbm.at[pl.ds(p_ref[2 * r + s], 1)], rbuf.at[sl, s, pl.ds(r, 1)], sem.at[sl]).start()
            return carry

        lax.fori_loop(0, tmc, issue, 0, unroll=16)

    @pl.when(i == 0)
    def _():
        gather(pos_ref, 0)

    @pl.when(i + 1 < pl.num_programs(0))
    def _():
        gather(posn_ref, 1 - slot)

    for s in range(2):
        pltpu.make_async_copy(ys_hbm.at[pl.ds(0, tmc)], rbuf.at[slot, s], sem.at[slot]).wait()
    ew = ew_ref[...]
    o_ref[...] = x1_ref[...] + ew[:, 0:1] * rbuf[slot, 0] + ew[:, 1:2] * rbuf[slot, 1]


def _combine(pos, x1, ew, ys, tmc):
    m = x1.shape[0]
    nt = m // tmc
    return pl.pallas_call(
        functools.partial(_combine_kernel, tmc=tmc),
        grid=(nt,),
        in_specs=[pl.BlockSpec((2 * tmc,), lambda i: (i,), memory_space=pltpu.SMEM),
                  pl.BlockSpec((2 * tmc,), lambda i: (jnp.minimum(i + 1, nt - 1),), memory_space=pltpu.SMEM),
                  pl.BlockSpec((tmc, D_MODEL), lambda i: (i, 0)), pl.BlockSpec((tmc, 128), lambda i: (i, 0)),
                  pl.BlockSpec(memory_space=pl.ANY)],
        out_specs=pl.BlockSpec((tmc, D_MODEL), lambda i: (i, 0)),
        out_shape=jax.ShapeDtypeStruct((m, D_MODEL), F32),
        scratch_shapes=[pltpu.VMEM((2, 2, tmc, D_MODEL), F32), pltpu.SemaphoreType.DMA((2,))],
        compiler_params=_cp("arbitrary"),
        name="moe_combine",
    )(pos, pos, x1, ew, ys)


def _moe(layer, groups, w_gate, w_up, w_down, te):
    assert len(groups) == 2 and groups[1][0].shape[0] == groups[1][4]
    sizes = [g[0].shape[0] for g in groups]
    m = sum(sizes)
    flat_e = jnp.concatenate([g[2][:, 0:2].reshape(-1) for g in groups])
    onehot = (flat_e[:, None] == jnp.arange(N_EXPERTS, dtype=jnp.int32)[None, :]).astype(jnp.int32)
    csum = jnp.cumsum(onehot, axis=0)
    rank = jnp.sum(csum * onehot, axis=1) - 1
    counts = csum[-1]
    padded = ((counts + te - 1) // te) * te
    pend = jnp.cumsum(padded)
    pstart = pend - padded
    pos = (jnp.sum(onehot * pstart[None, :], axis=1) + rank).astype(jnp.int32)
    rows = ((2 * m + N_EXPERTS * (te - 1)) // te) * te
    n_tiles = rows // te
    n_valid = (pend[-1] // te).astype(jnp.int32).reshape(1)
    tile_start = jnp.arange(n_tiles, dtype=jnp.int32) * te
    tile_expert = jnp.minimum(jnp.sum((tile_start[:, None] >= pend[None, :]).astype(jnp.int32), axis=1), N_EXPERTS - 1)
    last_e = jnp.take(tile_expert, jnp.maximum(n_valid[0] - 1, 0))
    tile_expert = jnp.where(jnp.arange(n_tiles) < n_valid[0], tile_expert, last_e).astype(jnp.int32)
    starts = np.cumsum([0] + sizes)
    gpos = [pos[2 * starts[i]:2 * starts[i + 1]] for i in range(len(groups))]
    xs = _dispatch((pstart + counts).astype(jnp.int32), (padded - counts).astype(jnp.int32), n_valid, gpos[0], groups[0][1],
                   gpos[1], groups[1][1], rows, groups[0][4], te)
    ys = _experts(layer, tile_expert, n_valid, xs, w_gate, w_up, w_down, te)
    return [_combine(p, g[0], g[3], ys, g[4]) for g, p in zip(groups, gpos)]


def _tile4(v):
    return jnp.tile(v, HEADS).reshape(1, BW)


def _prep_layer(l, w):
    w_in = w["w_in"][l]
    z = lambda n: jnp.zeros((D_MODEL, n), F32)
    b_al = w_in[:, 1952:1960]
    w_small = jnp.concatenate(
        [w_in[:, 0:512], w_in[:, 512:768], w_in[:, 768:896], z(64), w_in[:, 896:928], z(32), w_in[:, 928:1696],
         w_in[:, 1696:1952], w_in[:, 1960:2216], b_al, z(120)], axis=1).astype(BF16)
    uq = w["mla_w_uq"][l]
    w_uq = jnp.pad(uq, ((0, 0), (0, 0), (0, 32))).reshape(256, 512).astype(BF16)
    qg = jnp.tile(jnp.concatenate([w["mla_qn_g"][l], w["mla_qr_g"][l], jnp.zeros((32,), F32)]), HEADS).reshape(1, 512)
    uk = w["mla_w_uk"][l]
    w_uk_p = jnp.pad(uk, ((0, 0), (0, 0), (0, 64))).reshape(128, 512).astype(BF16)
    kg = jnp.tile(jnp.concatenate([w["mla_kn_g"][l], jnp.zeros((64,), F32)]), HEADS).reshape(1, 512)
    kr_g = jnp.concatenate([jnp.zeros((64,), F32), w["mla_kr_g"][l], jnp.zeros((32,), F32)]).reshape(1, 128)
    uv = w["mla_w_uv"][l]
    w_uv_p = jnp.stack([jnp.pad(uv[:, h, :], ((0, 0), (64 * h, BW - 64 * h - 64))) for h in range(HEADS)]).astype(BF16)
    par_r = jnp.zeros((8, 128), F32).at[0, 4:8].set(w["gdn_a_log"][l]).at[1, 4:8].set(w["gdn_dt_bias"][l])
    par_c = jnp.zeros((8, 128), F32).at[4:8, 0].set(w["gdn_a_log"][l]).at[4:8, 1].set(w["gdn_dt_bias"][l])
    w_router = jnp.concatenate([w["moe_wg"][l], w["moe_we"][l], jnp.zeros((D_MODEL, 128 - 36), F32)], axis=1)
    b_router = jnp.concatenate([w["moe_bg"][l], w["moe_be"][l], jnp.zeros((128 - 36,), F32)]).reshape(1, 128)
    return {
        "norm1_g": w["norm1_g"][l].reshape(1, D_MODEL), "w_small": w_small, "w_bat": b_al.T.astype(BF16),
        "w_gate": w_in[:, 2216:].astype(BF16),
        "gm_norm_g": w["gm_norm_g"][l].reshape(1, BW), "gm_ws": w["gm_ws"][l],
        "gm_bfull": jnp.repeat(w["gm_b"][l].T, HD, axis=1),
        "gm_w0": jnp.repeat(w["gm_ws"][l][:, 0, 0], HD).reshape(1, BW), "gm_b0": jnp.repeat(w["gm_b"][l][:, 0], HD).reshape(1, BW),
        "mla_cq_g": w["mla_cq_g"][l].reshape(1, 256), "w_uq": w_uq, "qg": qg, "mla_ckv_g": w["mla_ckv_g"][l].reshape(1, 128),
        "kr_g": kr_g, "w_uk": w_uk_p, "kg": kg, "w_uv": w_uv_p,
        "w_uk_c": uk.reshape(128, BW).T.astype(BF16), "w_uv_c": uv.reshape(128, BW).astype(BF16), "kn_gx": _tile4(w["mla_kn_g"][l]),
        "conv_w": w["gdn_conv_w"][l], "gdn_par_r": par_r, "gdn_par_c": par_c, "gdn_out_gx": _tile4(w["gdn_out_g"][l]),
        "gdn_out_gc": w["gdn_out_g"][l].reshape(HD, 1),
        "mem_norm_g": w["mem_norm_g"][l].reshape(1, D_MODEL), "mem_w_kv": w["mem_w_kv"][l].astype(BF16),
        "mem_qn_gx": _tile4(w["mem_qn_g"][l]), "mem_kn_gx": _tile4(w["mem_kn_g"][l]),
        "w_branch": w["w_branch"][l].astype(BF16), "w_out": w["w_out"][l].astype(BF16),
        "norm2_g": w["norm2_g"][l].reshape(1, D_MODEL), "w_router": w_router, "b_router": b_router,
    }


def _rope_tables(pos):
    half = MLA_ROPE // 2
    inv = ROPE_THETA ** (-jnp.arange(half, dtype=F32) / half)
    ang = pos.astype(F32)[:, None] * inv[None, :]
    cos, sin = jnp.cos(ang), jnp.sin(ang)
    t = pos.shape[0]
    one, zero = jnp.ones((t, 64), F32), jnp.zeros((t, 64), F32)
    z16, z32 = jnp.zeros((t, 16), F32), jnp.zeros((t, 32), F32)
    return (jnp.concatenate([one, cos, cos, jnp.ones((t, 32), F32)], axis=1),
            jnp.concatenate([zero, -sin, z16, z32], axis=1),
            jnp.concatenate([zero, z16, sin, z32], axis=1))


def kernel(x_prompt, mem_prompt, x_sample, cache_mla_ckv, cache_mla_kr, cache_mem_k, cache_mem_v, state_gdn, state_conv,
           page_table, norm1_g, w_in, gm_norm_g, gm_ws, gm_b, mla_cq_g, mla_w_uq, mla_qn_g, mla_qr_g, mla_ckv_g, mla_kr_g,
           mla_w_uk, mla_kn_g, mla_w_uv, gdn_conv_w, gdn_a_log, gdn_dt_bias, gdn_out_g, mem_norm_g, mem_w_kv, mem_qn_g,
           mem_kn_g, w_branch, w_out, norm2_g, moe_wg, moe_bg, moe_we, moe_be, moe_w_gate, moe_w_up, moe_w_down):
    w = dict(norm1_g=norm1_g, w_in=w_in, gm_norm_g=gm_norm_g, gm_ws=gm_ws, gm_b=gm_b, mla_cq_g=mla_cq_g, mla_w_uq=mla_w_uq,
             mla_qn_g=mla_qn_g, mla_qr_g=mla_qr_g, mla_ckv_g=mla_ckv_g, mla_kr_g=mla_kr_g, mla_w_uk=mla_w_uk, mla_kn_g=mla_kn_g,
             mla_w_uv=mla_w_uv, gdn_conv_w=gdn_conv_w, gdn_a_log=gdn_a_log, gdn_dt_bias=gdn_dt_bias, gdn_out_g=gdn_out_g,
             mem_norm_g=mem_norm_g, mem_w_kv=mem_w_kv, mem_qn_g=mem_qn_g, mem_kn_g=mem_kn_g, w_branch=w_branch, w_out=w_out,
             norm2_g=norm2_g, moe_wg=moe_wg, moe_bg=moe_bg, moe_we=moe_we, moe_be=moe_be)
    depth = w_in.shape[0]
    bp, tp, _ = x_prompt.shape
    bs = x_sample.shape[0]
    mt = mem_prompt.shape[1]
    n_pages = page_table.shape[1]
    past_len = n_pages * cache_mla_ckv.shape[2]
    mp = bp * tp

    tm_p = min(512, mp)
    tq = min(512, tp)
    ta = min(512, tp)
    tg = min(256, tp)
    pp = min(64, n_pages)
    cache_krt = jnp.swapaxes(cache_mla_kr, 2, 3)
    tabs_p = _rope_tables(jnp.arange(tp, dtype=jnp.int32))
    tabs_s = _rope_tables(jnp.full((bs,), past_len, jnp.int32))

    xp = x_prompt.reshape(mp, D_MODEL)
    xs = x_sample.reshape(bs, D_MODEL)
    mem = mem_prompt.reshape(bp * mt, D_MODEL)
    cache_k = cache_mem_k.transpose(0, 1, 3, 4, 2).reshape(depth, bs, BW, mt)
    cache_v = cache_mem_v.transpose(0, 1, 3, 4, 2).reshape(depth, bs, BW, mt)
    state_t = state_gdn.transpose(0, 2, 3, 4, 1)
    rows_p, rows_s = [], []
    for l in range(depth):
        p = _prep_layer(l, w)
        mk, mv = _mem_kv(mem, p["mem_norm_g"], p["mem_w_kv"], p["mem_kn_gx"], min(512, bp * mt))
        tm_f = min(tm_p, tp)
        a_out, q4, k4, ckv, kr, ckvt, zc, zm, zba, bat = _front(xp, tabs_p, tp // tm_f, p, tm_f, ta)
        b_out = _mla_attn(q4, k4, ckvt, p["w_uv"], bp, tp, ta)
        c_out, sfin = _gdn_prompt(zc, zba, bat, bp, tp, p, tg)
        m_out = _mem_attn(zm, mk, mv, p["mem_qn_gx"], bp, tp, mt, tq)
        group_p = list(_merge(xp, (a_out, b_out, c_out, m_out), p, tm_p)) + [tm_p]
        s_p =jnp.stack([sfin[:, 64 * h:64 * h + 64, 64 * h:64 * h + 64] for h in range(HEADS)], axis=1).transpose(0, 1, 3, 2)
        conv_p = zc.reshape(bp, tp, 1024)[:, tp - 3:, 0:QKV_DIM]
        rows_p.append((ckv.reshape(bp, tp, 128), kr.reshape(bp, tp, MLA_ROPE), s_p, conv_p,
                       mk.reshape(bp, mt, HEADS, HD), mv.reshape(bp, mt, HEADS, HD)))
        za, zb, zc, zm, zba, _ = _in_proj(xs, p["norm1_g"], p["w_small"], p["w_bat"], bs)
        a_s, v_s, conv_s, gq, gk, gv, zg, beta, gdec, mqn = _sample_tok(za, zc, zba, zm, state_conv[l].reshape(bs, 3 * QKV_DIM), p)
        _, _, ckv_s, kr_s, _, q32 = _mla_pre(zb, bs, tabs_s, 1, p, bs, bs)
        q3 = q32.reshape(bs, HEADS, 128)
        qk = q3[:, :, 0:64].reshape(bs, 1, BW)
        qr8 = jnp.pad(q3[:, :, 64:96], ((0, 0), (0, 4), (0, 0)))
        b_s = _mla_decode(l, page_table, cache_mla_ckv, cache_krt, qk, qr8, ckv_s.reshape(bs, 1, 128),
                          kr_s.reshape(bs, 1, MLA_ROPE), p["kn_gx"], p["w_uk_c"], p["w_uv_c"], pp).reshape(bs, BW)
        hv = lambda a: a.reshape(HEADS, HD, bs)
        s_new, c_t = _gdn_step(l, state_t, hv(gq), hv(gk), hv(gv), beta[0:4].reshape(HEADS, 1, bs),
                               gdec[4:8].reshape(HEADS, 1, bs), hv(zg), p["gdn_out_gc"])
        m_s = _mem_attn_s(l, mqn, cache_k, cache_v, min(8, bs))
        group_s = list(_merge(xs, (a_s, b_s, c_t.reshape(BW, bs).T, m_s), p, bs)) + [bs]
        xp, xs = _moe(l, [group_p, group_s], moe_w_gate, moe_w_up, moe_w_down, min(512, mp))
        rows_s.append((ckv_s.reshape(bs, 1, 128), kr_s.reshape(bs, 1, MLA_ROPE), s_new.transpose(3, 0, 1, 2),
                       conv_s.reshape(bs, 3, QKV_DIM), v_s.reshape(bs, 1, BW)))
    p_out = [jnp.stack(a) for a in zip(*rows_p)]
    s_out = [jnp.stack(a) for a in zip(*rows_s)]
    return (xp.reshape(bp, tp, D_MODEL), xs.reshape(bs, 1, D_MODEL), *p_out, *s_out)
```
